```python
import jax, jax.numpy as jnp
from jax import lax
import numpy as np

D_MODEL = 2048
BATCH = 2
SEQ = 4096
DEPTH = 2
DEC_BATCH = 16
DEC_SEQ = 16
PAST_LEN = 1024

CHUNK = 64
A_HEADS = 8
A_HEAD_DIM = 128
D_A = A_HEADS * A_HEAD_DIM
IDX_HEADS = 8
IDX_DIM = 64
TOPK_MAX = 256
Q_BLOCK = 64
POOL_WINDOWS = (2, 4, 8, 16)
POOL_GROUP_DIM = 128
D_B = len(POOL_WINDOWS) * POOL_GROUP_DIM
POOL_STATE = max(POOL_WINDOWS) - 1
C_HEADS = 4
C_HEAD_DIM = 128
C_V_DIM = 128
D_C = C_HEADS * C_HEAD_DIM
HGRN_BLOCK = 16
N_BRANCH = 3
D_FF = 5632
N_EXPERTS = 8
TOP_K = 2
N_DENSE = (DEPTH + 1) // 2
N_MOE = DEPTH // 2
IN_SIZES = (D_A, D_A, D_A, IDX_HEADS * IDX_DIM, IDX_DIM, IDX_HEADS,
            D_B, D_C, D_C, C_HEADS * C_V_DIM, C_HEADS * C_V_DIM, N_BRANCH * D_MODEL)
D_IN = sum(IN_SIZES)
EPS = 1e-6
NEG = -1e30
LB_FLOOR = 1e-30

kernel_name = "hybrid_streaming_dsa_pool_hgrn2_step"

f32 = jnp.float32


def rms_norm(x, g):
    xf = x.astype(f32)
    y = xf * lax.rsqrt(jnp.mean(xf * xf, axis=-1, keepdims=True) + EPS)
    return (y * g.astype(f32)).astype(x.dtype)


def split_columns(h):
    idx = np.cumsum(IN_SIZES)[:-1].tolist()
    return jnp.split(h, idx, axis=-1)


def indexer_scores(q_idx, k_idx, w_idx):
    dots = jnp.einsum('bqhd,bsd->bqhs', q_idx, k_idx).astype(f32)
    w = w_idx.astype(f32) * (IDX_HEADS ** -0.5 * IDX_DIM ** -0.5)
    return jnp.einsum('bqh,bqhs->bqs', w, jax.nn.relu(dots))


def sparse_attend(q, k, v, sel, valid):
    gather = jax.vmap(lambda a, i: a[i])
    k_sel = gather(k, sel)
    v_sel = gather(v, sel)
    s = jnp.einsum('bqhd,bqkhd->bqhk', q, k_sel).astype(f32) * (A_HEAD_DIM ** -0.5)
    if valid is not None:
        s = jnp.where(valid[:, :, None, :], s, NEG)
    p = jax.nn.softmax(s, axis=-1).astype(v.dtype)
    o = jnp.einsum('bqhk,bqkhd->bqhd', p, v_sel)
    return o.reshape(o.shape[0], o.shape[1], D_A)


def dsa_prompt(q, k, v, q_idx, k_idx, w_idx):
    B, T = q.shape[0], q.shape[1]
    topk = min(TOPK_MAX, T // 4)
    nblk = T // Q_BLOCK
    key_chunk = jnp.arange(T) // CHUNK

    def to_blocks(a):
        return jnp.swapaxes(a.reshape((B, nblk, Q_BLOCK) + a.shape[2:]), 0, 1)

    def block(args):
        qb, qib, wb, blk = args
        q_chunk = (blk * Q_BLOCK + jnp.arange(Q_BLOCK)) // CHUNK
        admissible = key_chunk[None, :] <= q_chunk[:, None]
        scores = jnp.where(admissible[None], indexer_scores(qib, k_idx, wb), NEG)
        _, sel = lax.top_k(scores, topk)
        valid = key_chunk[sel] <= q_chunk[None, :, None]
        return sparse_attend(qb, k, v, sel, valid)

    out = lax.map(block, (to_blocks(q), to_blocks(q_idx), to_blocks(w_idx), jnp.arange(nblk)))
    return jnp.swapaxes(out, 0, 1).reshape(B, T, D_A)


def dsa_sample(q, k_all, v_all, q_idx, kidx_all, w_idx):
    L = k_all.shape[1]
    topk = min(TOPK_MAX, L // 4)
    _, sel = lax.top_k(indexer_scores(q_idx, kidx_all, w_idx), topk)
    return sparse_attend(q, k_all, v_all, sel, None)


def pool_mixer(u_ext, n_new, first_pos, pool_w, pool_scale):
    uf = u_ext.astype(f32)
    csum = jnp.cumsum(uf, axis=1)
    pos = first_pos + jnp.arange(n_new)
    means = []
    for gi, w in enumerate(POOL_WINDOWS):
        c = csum[:, :, gi * POOL_GROUP_DIM:(gi + 1) * POOL_GROUP_DIM]
        c_prev = jnp.pad(c, ((0, 0), (w, 0), (0, 0)))[:, :c.shape[1]]
        count = jnp.minimum(pos + 1, w).astype(f32)
        means.append((c - c_prev)[:, -n_new:] / count[None, :, None])
    d = jnp.concatenate(means, axis=-1) - uf[:, -n_new:]
    B = d.shape[0]
    d = d.reshape(B, n_new, len(POOL_WINDOWS), POOL_GROUP_DIM)
    o = jnp.einsum('btgc,gcd->btgd', d, pool_w.astype(f32)).reshape(B, n_new, D_B)
    return (o * pool_scale.astype(f32)).astype(u_ext.dtype)


def hgrn2_scan(q, k, i, g, s0, block):
    B, T, H, DK = q.shape
    n = T // block

    def rs(a):
        return a.reshape(B, n, block, H, a.shape[-1]).transpose(1, 0, 3, 2, 4)

    causal = jnp.tril(jnp.ones((block, block), bool))[:, :, None]

    def step(S, xs):
        qc, kc, ic, gc = xs
        b = jnp.cumsum(gc, axis=2)
        o_inter = jnp.einsum('bhtd,bhde->bhte', qc * jnp.exp(b), S)
        diff = b[:, :, :, None, :] - b[:, :, None, :, :]
        decay = jnp.where(causal, jnp.exp(jnp.where(causal, diff, 0.0)), 0.0)
        A = jnp.einsum('bhtd,bhsd,bhtsd->bhts', qc, kc, decay)
        o = o_inter + jnp.einsum('bhts,bhse->bhte', A, ic)
        b_last = b[:, :, -1:, :]
        S_new = jnp.exp(b_last[:, :, 0, :])[..., None] * S + jnp.einsum(
            'bhsd,bhse->bhde', kc * jnp.exp(b_last - b), ic)
        return S_new, o

    S_T, o = lax.scan(step, s0, (rs(q), rs(k), rs(i), rs(g)))
    o = o.transpose(1, 0, 3, 2, 4).reshape(B, T, H, i.shape[-1])
    return o, S_T


def hgrn2_mixer(cq, cf, ci, cg, lb, norm_g, s0, block):
    B, T = cq.shape[0], cq.shape[1]
    heads = lambda a: a.astype(f32).reshape(B, T, C_HEADS, -1)
    q = jax.nn.silu(heads(cq))
    g = heads(jnp.logaddexp(jnp.log(jnp.maximum(lb, LB_FLOOR)),
                            jnp.log1p(-lb) + jax.nn.log_sigmoid(cf.astype(f32))))
    k = -jnp.expm1(g)
    o, S = hgrn2_scan(q, k, heads(ci), g, s0.astype(f32), block)
    o = rms_norm(o, norm_g).reshape(B, T, C_HEADS * C_V_DIM) * jax.nn.silu(cg.astype(f32))
    return o.astype(cq.dtype), S


def token_mixing(xn, w_in, w_proj_a, w_proj_b, w_proj_c, w_out, pool_w, pool_scale, lb, c_norm_g,
                 past_k, past_v, past_kidx, pool_prefix, s0, first_pos, hgrn_block):
    B, T = xn.shape[0], xn.shape[1]
    aq, ak, av, aqi, aki, aw, bu, cq, cf, ci, cg, gates = split_columns(xn @ w_in)
    q = aq.reshape(B, T, A_HEADS, A_HEAD_DIM)
    k = ak.reshape(B, T, A_HEADS, A_HEAD_DIM)
    v = av.reshape(B, T, A_HEADS, A_HEAD_DIM)
    q_idx = aqi.reshape(B, T, IDX_HEADS, IDX_DIM)
    if past_k is None:
        o_a = dsa_prompt(q, k, v, q_idx, aki, aw)
        u_ext = bu
    else:
        k_all = jnp.concatenate([past_k.astype(k.dtype), k], axis=1)
        v_all = jnp.concatenate([past_v.astype(v.dtype), v], axis=1)
        kidx_all = jnp.concatenate([past_kidx.astype(aki.dtype), aki], axis=1)
        o_a = dsa_sample(q, k_all, v_all, q_idx, kidx_all, aw)
        u_ext = jnp.concatenate([pool_prefix.astype(bu.dtype), bu], axis=1)
    o_b = pool_mixer(u_ext, T, first_pos, pool_w, pool_scale)
    o_c, s_new = hgrn2_mixer(cq, cf, ci, cg, lb, c_norm_g, s0, hgrn_block)
    g_a, g_b, g_c = jnp.split(jax.nn.sigmoid(gates.astype(f32)), N_BRANCH, axis=-1)
    merged = g_a * (o_a @ w_proj_a) + g_b * (o_b @ w_proj_b) + g_c * (o_c @ w_proj_c)
    y = merged.astype(xn.dtype) @ w_out
    return y, k, v, aki, u_ext[:, -POOL_STATE:], s_new


def swiglu(x, wg, wu, wd):
    return (jax.nn.silu(x @ wg) * (x @ wu)) @ wd


def moe_swiglu(x, router, wg, wu, wd):
    logits = (x @ router).astype(f32)
    top_vals, top_idx = lax.top_k(logits, TOP_K)
    gates = jax.nn.softmax(top_vals, axis=-1)
    combine = jnp.sum(jax.nn.one_hot(top_idx, N_EXPERTS, dtype=f32) * gates[..., None], axis=-2)
    out = jnp.zeros(x.shape, f32)
    for e in range(N_EXPERTS):
        out = out + combine[..., e:e + 1] * swiglu(x, wg[e], wu[e], wd[e]).astype(f32)
    return out.astype(x.dtype)


def setup_inputs(seed: int = 0) -> dict:
    key = jax.random.key(seed)
    ks = iter(jax.random.split(key, 32))
    nrm = lambda shape, scale: jax.random.normal(next(ks), shape, f32) * scale
    gain = lambda shape: 1.0 + 0.1 * jax.random.normal(next(ks), shape, f32)
    return {
        "x_prompt": nrm((BATCH, SEQ, D_MODEL), 1.0),
        "x_sample": nrm((DEC_BATCH, DEC_SEQ, D_MODEL), 1.0),
        "cache_k": nrm((DEPTH, DEC_BATCH, PAST_LEN, A_HEADS, A_HEAD_DIM), 1.0),
        "cache_v": nrm((DEPTH, DEC_BATCH, PAST_LEN, A_HEADS, A_HEAD_DIM), 1.0),
        "cache_kidx": nrm((DEPTH, DEC_BATCH, PAST_LEN, IDX_DIM), 1.0),
        "cache_pool": nrm((DEPTH, DEC_BATCH, POOL_STATE, D_B), 1.0),
        "state_hgrn": nrm((DEPTH, DEC_BATCH, C_HEADS, C_HEAD_DIM, C_V_DIM), 0.5),
        "norm_mix_g": gain((DEPTH, D_MODEL)),
        "norm_ffn_g": gain((DEPTH, D_MODEL)),
        "final_norm_g": gain((D_MODEL,)),
        "w_in": nrm((DEPTH, D_MODEL, D_IN), D_MODEL ** -0.5),
        "w_proj_a": nrm((DEPTH, D_A, D_MODEL), D_A ** -0.5),
        "w_proj_b": nrm((DEPTH, D_B, D_MODEL), D_B ** -0.5),
        "w_proj_c": nrm((DEPTH, D_C, D_MODEL), D_C ** -0.5),
        "w_out": nrm((DEPTH, D_MODEL, D_MODEL), D_MODEL ** -0.5),
        "pool_w": nrm((DEPTH, len(POOL_WINDOWS), POOL_GROUP_DIM, POOL_GROUP_DIM), POOL_GROUP_DIM ** -0.5),
        "pool_scale": gain((DEPTH, D_B)),
        "hgrn_lb_logits": nrm((DEPTH, D_C), 0.5),
        "hgrn_norm_g": gain((DEPTH, C_V_DIM)),
        "ffn_w_gate": nrm((N_DENSE, D_MODEL, D_FF), D_MODEL ** -0.5),
        "ffn_w_up": nrm((N_DENSE, D_MODEL, D_FF), D_MODEL ** -0.5),
        "ffn_w_down": nrm((N_DENSE, D_FF, D_MODEL), D_FF ** -0.5),
        "moe_router": nrm((N_MOE, D_MODEL, N_EXPERTS), D_MODEL ** -0.5),
        "moe_w_gate": nrm((N_MOE, N_EXPERTS, D_MODEL, D_FF), D_MODEL ** -0.5),
        "moe_w_up": nrm((N_MOE, N_EXPERTS, D_MODEL, D_FF), D_MODEL ** -0.5),
        "moe_w_down": nrm((N_MOE, N_EXPERTS, D_FF, D_MODEL), D_FF ** -0.5),
    }


def reference(x_prompt, x_sample, cache_k, cache_v, cache_kidx, cache_pool, state_hgrn,
              norm_mix_g, norm_ffn_g, final_norm_g, w_in, w_proj_a, w_proj_b, w_proj_c, w_out,
              pool_w, pool_scale, hgrn_lb_logits, hgrn_norm_g,
              ffn_w_gate, ffn_w_up, ffn_w_down, moe_router, moe_w_gate, moe_w_up, moe_w_down):
    lb_soft = jax.nn.softmax(hgrn_lb_logits.astype(f32), axis=0)
    lb_all = jnp.cumsum(lb_soft, axis=0) - lb_soft[0:1]
    past_len = cache_k.shape[2]
    dec_seq = x_sample.shape[1]
    s0_prompt = jnp.zeros((x_prompt.shape[0], C_HEADS, C_HEAD_DIM, C_V_DIM), f32)
    hp, hs = x_prompt, x_sample
    pk, pv, pki, pp, ps = [], [], [], [], []
    sk, sv, ski, sp, ss = [], [], [], [], []
    for l in range(DEPTH):
        shared = (w_in[l], w_proj_a[l], w_proj_b[l], w_proj_c[l], w_out[l], pool_w[l], pool_scale[l],
                  lb_all[l], hgrn_norm_g[l])
        y, k_new, v_new, kidx_new, pool_new, s_new = token_mixing(
            rms_norm(hp, norm_mix_g[l]), *shared, None, None, None, None, s0_prompt, 0, HGRN_BLOCK)
        hp = hp + y
        pk.append(k_new); pv.append(v_new); pki.append(kidx_new); pp.append(pool_new); ps.append(s_new)
        y, k_new, v_new, kidx_new, pool_new, s_new = token_mixing(
            rms_norm(hs, norm_mix_g[l]), *shared, cache_k[l], cache_v[l], cache_kidx[l], cache_pool[l],
            state_hgrn[l], past_len, dec_seq)
        hs = hs + y
        sk.append(k_new); sv.append(v_new); ski.append(kidx_new); sp.append(pool_new); ss.append(s_new)
        j = l // 2
        if l % 2 == 0:
            hp = hp + swiglu(rms_norm(hp, norm_ffn_g[l]), ffn_w_gate[j], ffn_w_up[j], ffn_w_down[j])
            hs = hs + swiglu(rms_norm(hs, norm_ffn_g[l]), ffn_w_gate[j], ffn_w_up[j], ffn_w_down[j])
        else:
            hp = hp + moe_swiglu(rms_norm(hp, norm_ffn_g[l]), moe_router[j], moe_w_gate[j], moe_w_up[j], moe_w_down[j])
            hs = hs + moe_swiglu(rms_norm(hs, norm_ffn_g[l]), moe_router[j], moe_w_gate[j], moe_w_up[j], moe_w_down[j])
    y_prompt = rms_norm(hp, final_norm_g)
    y_sample = rms_norm(hs, final_norm_g)
    return (y_prompt, y_sample,
            jnp.stack(pk), jnp.stack(pv), jnp.stack(pki), jnp.stack(pp), jnp.stack(ps),
            jnp.stack(sk), jnp.stack(sv), jnp.stack(ski), jnp.stack(sp), jnp.stack(ss))
```

```python
import functools

import numpy as np
import jax
import jax.numpy as jnp
from jax import lax
from jax.experimental import pallas as pl
from jax.experimental.pallas import tpu as pltpu

f32 = jnp.float32
bf16 = jnp.bfloat16

D_MODEL = 2048
CHUNK = 64
A_HEADS = 8
A_HEAD_DIM = 128
D_A = A_HEADS * A_HEAD_DIM
IDX_HEADS = 8
IDX_DIM = 64
TOPK_MAX = 256
POOL_WINDOWS = (2, 4, 8, 16)
POOL_GROUP_DIM = 128
D_B = len(POOL_WINDOWS) * POOL_GROUP_DIM
POOL_STATE = max(POOL_WINDOWS) - 1
C_HEADS = 4
C_HEAD_DIM = 128
C_V_DIM = 128
D_C = C_HEADS * C_HEAD_DIM
HGRN_BLOCK = 16
N_BRANCH = 3
N_EXPERTS = 8
IN_SIZES = (D_A, D_A, D_A, IDX_HEADS * IDX_DIM, IDX_DIM, IDX_HEADS,
            D_B, D_C, D_C, C_HEADS * C_V_DIM, C_HEADS * C_V_DIM, N_BRANCH * D_MODEL)
EPS = 1e-6
NEG = -1e30
LB_FLOOR = 1e-30
INT_MIN = -2 ** 31

LANE = 128
COL_Q, COL_K, COL_V = 0, D_A, 2 * D_A
COL_QI = 3 * D_A
COL_U = COL_QI + 512
COL_CQ = COL_U + 512
COL_CF = COL_CQ + 512
COL_CI = COL_CF + 512
COL_CG = COL_CI + 512
COL_KI = COL_CG + 512
COL_W = COL_KI + LANE
N_PACK = COL_W + LANE

VMEM_LIMIT = 56 * 1024 * 1024

NT_DIMS = (((1,), (1,)), ((), ()))
TN_DIMS = (((0,), (0,)), ((), ()))


def _cparams(sem):
    return pltpu.CompilerParams(dimension_semantics=sem, vmem_limit_bytes=VMEM_LIMIT)


def _rms(x, g):
    ms = jnp.mean(x * x, axis=-1, keepdims=True)
    return x * lax.rsqrt(ms + EPS) * g


def _silu(x):
    return x * jax.nn.sigmoid(x)


def _sort_key(x):
    bits = pltpu.bitcast(x, jnp.int32)
    return bits ^ ((bits >> 31) & jnp.int32(0x7FFFFFFF))


def _kth_largest(count_ge, kk):
    zero = jnp.zeros(kk.shape, jnp.int32)
    t0 = jnp.where(count_ge(zero) >= kk, zero, jnp.int32(INT_MIN))

    def bit_body(i, t):
        c = t | lax.shift_left(jnp.int32(1), 30 - i)
        return jnp.where(count_ge(c) >= kk, c, t)

    return lax.fori_loop(0, 31, bit_body, t0)


def _inproj_kernel(x_ref, g_ref, w_ref, o_ref, xn_ref):
    @pl.when(pl.program_id(1) == 0)
    def _():
        xn_ref[...] = _rms(x_ref[...], g_ref[...]).astype(bf16)

    o_ref[...] = jnp.dot(xn_ref[...], w_ref[...], preferred_element_type=f32)


def _inproj(h, g, wp, tm, tn):
    m, d = h.shape
    n = wp.shape[1]
    return pl.pallas_call(
        _inproj_kernel,
        grid=(m // tm, n // tn),
        in_specs=[pl.BlockSpec((tm, d), lambda i, j: (i, 0)),
                  pl.BlockSpec((1, d), lambda i, j: (0, 0)),
                  pl.BlockSpec((d, tn), lambda i, j: (0, j))],
        out_specs=pl.BlockSpec((tm, tn), lambda i, j: (i, j)),
        out_shape=jax.ShapeDtypeStruct((m, n), f32),
        scratch_shapes=[pltpu.VMEM((tm, d), bf16)],
        compiler_params=_cparams(("parallel", "arbitrary")),
        name="inproj",
    )(h, g.reshape(1, d), wp)


def _idx_scores(qi, w, kib):
    acc = jnp.zeros((qi.shape[0], kib.shape[0]), f32)
    for hh in range(IDX_HEADS):
        d = lax.dot_general(qi[:, hh * IDX_DIM:(hh + 1) * IDX_DIM], kib, NT_DIMS,
                            preferred_element_type=f32)
        acc = acc + w[:, hh:hh + 1] * jnp.maximum(d, 0.0)
    return acc


def _fold_lanes(x):
    part = x[:, :LANE]
    for j in range(1, x.shape[1] // LANE):
        part = part + x[:, j * LANE:(j + 1) * LANE]
    return part


def _tri_incl(n):
    r = lax.broadcasted_iota(jnp.int32, (n, n), 0)
    c = lax.broadcasted_iota(jnp.int32, (n, n), 1)
    return jnp.where(r <= c, 1.0, 0.0).astype(bf16)


def _dsa_prompt_kernel(q_ref, k_ref, v_ref, qi_ref, ki_ref, w_ref, o_ref, keys_ref, bias_ref, *,
                       tq, topk):
    qb = pl.program_id(1)
    nkb = qb + 1

    @pl.when(pl.program_id(2) == 0)
    def _select():
        qi = qi_ref[...].astype(bf16)
        w = w_ref[...][:, :IDX_HEADS] * (IDX_HEADS ** -0.5 * IDX_DIM ** -0.5)
        row = lax.broadcasted_iota(jnp.int32, (tq, tq), 0)
        col = lax.broadcasted_iota(jnp.int32, (tq, tq), 1)
        diag_adm = (col // CHUNK) <= (row // CHUNK)

        def score_body(kb, c):
            off = pl.multiple_of(kb * tq, tq)
            kib = ki_ref[pl.ds(off, tq), :][:, :IDX_DIM].astype(bf16)
            key = _sort_key(_idx_scores(qi, w, kib))
            adm = jnp.logical_or(kb < qb, diag_adm)
            keys_ref[kb] = jnp.where(adm, key, jnp.int32(INT_MIN))
            return c

        lax.fori_loop(0, nkb, score_body, 0)

        def count(pred):
            def body(kb, acc):
                return acc + _fold_lanes(jnp.where(pred(keys_ref[kb]), 1.0, 0.0))
            acc = lax.fori_loop(0, nkb, body, jnp.zeros((tq, LANE), f32))
            return jnp.sum(acc, axis=-1, keepdims=True)

        r1 = lax.broadcasted_iota(jnp.int32, (tq, 1), 0)
        n_adm = ((qb * tq + r1) // CHUNK + 1) * CHUNK
        kk = jnp.minimum(topk, n_adm).astype(f32)
        thr = _kth_largest(lambda c: count(lambda k: k >= c), kk)
        n_ge = count(lambda k: k >= thr)
        has_tie = jnp.max(n_ge - kk) > 0.0

        @pl.when(jnp.logical_not(has_tie))
        def _():
            def body(kb, c):
                bias_ref[kb] = jnp.where(keys_ref[kb] >= thr, 0.0, NEG)
                return c
            lax.fori_loop(0, nkb, body, 0)

        @pl.when(has_tie)
        def _():
            need = kk - count(lambda k: k > thr)
            tri = _tri_incl(tq)

            def body(kb, seen):
                k = keys_ref[kb]
                eq = jnp.where(k == thr, 1.0, 0.0)
                rank = seen + jnp.dot(eq.astype(bf16), tri, preferred_element_type=f32)
                take = jnp.logical_or(k > thr, jnp.logical_and(k == thr, rank <= need))
                bias_ref[kb] = jnp.where(take, 0.0, NEG)
                return seen + jnp.sum(eq, axis=-1, keepdims=True)
            lax.fori_loop(0, nkb, body, jnp.zeros((tq, 1), f32))

    q = q_ref[...].astype(bf16)
    scale = A_HEAD_DIM ** -0.5

    def att_body(kb, carry):
        m, l, acc = carry
        off = pl.multiple_of(kb * tq, tq)
        kblk = k_ref[pl.ds(off, tq), :].astype(bf16)
        vblk = v_ref[pl.ds(off, tq), :].astype(bf16)
        s = lax.dot_general(q, kblk, NT_DIMS, preferred_element_type=f32) * scale + bias_ref[kb]
        m_new = jnp.maximum(m, jnp.max(s, axis=-1, keepdims=True))
        p = jnp.exp(s - m_new)
        corr = jnp.exp(m - m_new)
        l = corr * l + jnp.sum(p, axis=-1, keepdims=True)
        acc = corr * acc + jnp.dot(p.astype(bf16), vblk, preferred_element_type=f32)
        return m_new, l, acc

    init = (jnp.full((tq, 1), NEG, f32), jnp.zeros((tq, 1), f32), jnp.zeros((tq, A_HEAD_DIM), f32))
    _, l, acc = lax.fori_loop(0, nkb, att_body, init)
    o_ref[...] = (acc / l).astype(o_ref.dtype)


def _dsa_prompt(hp, batch, seq, tq):
    nq = seq // tq
    topk = min(TOPK_MAX, seq // 4)
    kern = functools.partial(_dsa_prompt_kernel, tq=tq, topk=topk)
    hb = LANE
    return pl.pallas_call(
        kern,
        grid=(batch, nq, A_HEADS),
        in_specs=[
            pl.BlockSpec((tq, hb), lambda b, i, h: (b * nq + i, COL_Q // hb + h)),
            pl.BlockSpec((seq, hb), lambda b, i, h: (b, COL_K // hb + h)),
            pl.BlockSpec((seq, hb), lambda b, i, h: (b, COL_V // hb + h)),
            pl.BlockSpec((tq, 512), lambda b, i, h: (b * nq + i, COL_QI // 512)),
            pl.BlockSpec((seq, hb), lambda b, i, h: (b, COL_KI // hb)),
            pl.BlockSpec((tq, hb), lambda b, i, h: (b * nq + i, COL_W // hb)),
        ],
        out_specs=pl.BlockSpec((tq, hb), lambda b, i, h: (b * nq + i, h)),
        out_shape=jax.ShapeDtypeStruct((batch * seq, D_A), bf16),
        scratch_shapes=[pltpu.VMEM((nq, tq, tq), jnp.int32), pltpu.VMEM((nq, tq, tq), f32)],
        compiler_params=_cparams(("parallel", "arbitrary", "arbitrary")),
        name="dsa_prompt",
    )(hp, hp, hp, hp, hp, hp)


def _dsa_sample_kernel(q_ref, kn_ref, vn_ref, qi_ref, kin_ref, w_ref, kc_ref, vc_ref, kic_ref, o_ref,
                       biasc_ref, biasn_ref, *, topk):
    nq = q_ref.shape[0]
    past = kc_ref.shape[0]

    @pl.when(pl.program_id(1) == 0)
    def _select():
        qi = qi_ref[...].astype(bf16)
        w = w_ref[...][:, :IDX_HEADS] * (IDX_HEADS ** -0.5 * IDX_DIM ** -0.5)
        key_c = _sort_key(_idx_scores(qi, w, kic_ref[...].astype(bf16)))
        key_n = _sort_key(_idx_scores(qi, w, kin_ref[...][:, :IDX_DIM].astype(bf16)))

        def count(pred):
            cc = jnp.sum(_fold_lanes(jnp.where(pred(key_c), 1.0, 0.0)), axis=-1, keepdims=True)
            return cc + jnp.sum(jnp.where(pred(key_n), 1.0, 0.0), axis=-1, keepdims=True)

        kk = jnp.full((nq, 1), float(topk), f32)
        thr = _kth_largest(lambda c: count(lambda k: k >= c), kk)
        n_ge = count(lambda k: k >= thr)
        has_tie = jnp.max(n_ge - kk) > 0.0

        @pl.when(jnp.logical_not(has_tie))
        def _():
            biasc_ref[...] = jnp.where(key_c >= thr, 0.0, NEG)
            biasn_ref[...] = jnp.where(key_n >= thr, 0.0, NEG)

        @pl.when(has_tie)
        def _():
            need = kk - count(lambda k: k > thr)
            tri = _tri_incl(LANE)
            seen = jnp.zeros((nq, 1), f32)
            for j in range(past // LANE):
                k = key_c[:, j * LANE:(j + 1) * LANE]
                eq = jnp.where(k == thr, 1.0, 0.0)
                rank = seen + jnp.dot(eq.astype(bf16), tri, preferred_element_type=f32)
                take = jnp.logical_or(k > thr, jnp.logical_and(k == thr, rank <= need))
                biasc_ref[:, j * LANE:(j + 1) * LANE] = jnp.where(take, 0.0, NEG)
                seen = seen + jnp.sum(eq, axis=-1, keepdims=True)
            eq = jnp.where(key_n == thr, 1.0, 0.0)
            rank = seen + jnp.dot(eq.astype(bf16), _tri_incl(nq), preferred_element_type=f32)
            take = jnp.logical_or(key_n > thr, jnp.logical_and(key_n == thr, rank <= need))
            biasn_ref[...] = jnp.where(take, 0.0, NEG)

    q = q_ref[...].astype(bf16)
    scale = A_HEAD_DIM ** -0.5
    s_c = lax.dot_general(q, kc_ref[...].astype(bf16), NT_DIMS, preferred_element_type=f32) * scale
    s_c = s_c + biasc_ref[...]
    s_n = lax.dot_general(q, kn_ref[...].astype(bf16), NT_DIMS, preferred_element_type=f32) * scale
    s_n = s_n + biasn_ref[...]
    m = jnp.maximum(jnp.max(s_c, axis=-1, keepdims=True), jnp.max(s_n, axis=-1, keepdims=True))
    p_c = jnp.exp(s_c - m)
    p_n = jnp.exp(s_n - m)
    l = jnp.sum(p_c, axis=-1, keepdims=True) + jnp.sum(p_n, axis=-1, keepdims=True)
    acc = jnp.dot(p_c.astype(bf16), vc_ref[...].astype(bf16), preferred_element_type=f32)
    acc = acc + jnp.dot(p_n.astype(bf16), vn_ref[...].astype(bf16), preferred_element_type=f32)
    o_ref[...] = (acc / l).astype(o_ref.dtype)


def _dsa_sample(hp, row0, nb, nq, cache_k, cache_v, cache_kidx):
    past = cache_k.shape[1]
    topk = min(TOPK_MAX, (past + nq) // 4)
    kc = cache_k.reshape(nb, past, D_A)
    vc = cache_v.reshape(nb, past, D_A)
    rb = row0 // nq
    hb = LANE
    kern = functools.partial(_dsa_sample_kernel, topk=topk)
    return pl.pallas_call(
        kern,
        grid=(nb, A_HEADS),
        in_specs=[
            pl.BlockSpec((nq, hb), lambda b, h: (rb + b, COL_Q // hb + h)),
            pl.BlockSpec((nq, hb), lambda b, h: (rb + b, COL_K // hb + h)),
            pl.BlockSpec((nq, hb), lambda b, h: (rb + b, COL_V // hb + h)),
            pl.BlockSpec((nq, 512), lambda b, h: (rb + b, COL_QI // 512)),
            pl.BlockSpec((nq, hb), lambda b, h: (rb + b, COL_KI // hb)),
            pl.BlockSpec((nq, hb), lambda b, h: (rb + b, COL_W // hb)),
            pl.BlockSpec((None, past, hb), lambda b, h: (b, 0, h)),
            pl.BlockSpec((None, past, hb), lambda b, h: (b, 0, h)),
            pl.BlockSpec((None, past, IDX_DIM), lambda b, h: (b, 0, 0)),
        ],
        out_specs=pl.BlockSpec((nq, hb), lambda b, h: (b, h)),
        out_shape=jax.ShapeDtypeStruct((nb * nq, D_A), bf16),
        scratch_shapes=[pltpu.VMEM((nq, past), f32), pltpu.VMEM((nq, nq), f32)],
        compiler_params=_cparams(("parallel", "arbitrary")),
        name="dsa_sample",
    )(hp, hp, hp, hp, hp, hp, kc, vc, cache_kidx)


def _pool_kernel(prev_ref, u_ref, pw_ref, ps_ref, o_ref, *, tt, first_pos, zero_first):
    tb = pl.program_id(1)
    halo = POOL_STATE + 1
    prev = prev_ref[...]
    if zero_first:
        prev = jnp.where(tb == 0, 0.0, prev)
    cur = u_ref[...]
    ext = jnp.concatenate([prev, cur], axis=0)
    pos = first_pos + tb * tt + lax.broadcasted_iota(jnp.int32, (tt, 1), 0)
    outs = []
    for gi, wdw in enumerate(POOL_WINDOWS):
        cols = slice(gi * POOL_GROUP_DIM, (gi + 1) * POOL_GROUP_DIM)
        a = ext[:, cols]
        n = 1
        while n < wdw:
            a = a[n:] + a[:-n]
            n *= 2
        s = a[halo + 1 - wdw: halo + 1 - wdw + tt]
        cnt = jnp.minimum(pos + 1, wdw).astype(f32)
        d = s / cnt - cur[:, cols]
        outs.append(jnp.dot(d.astype(bf16), pw_ref[gi].astype(bf16), preferred_element_type=f32))
    o = jnp.concatenate(outs, axis=-1) * ps_ref[...]
    o_ref[...] = o.astype(o_ref.dtype)


def _pool(hp, row0, nb, t, tt, prev, pool_w, pool_scale, first_pos):
    ntb = t // tt
    halo = POOL_STATE + 1
    rb = row0 // tt
    if prev is None:
        prev_arr = hp
        per = tt // halo
        prev_spec = pl.BlockSpec(
            (halo, D_B), lambda b, i: (jnp.maximum((row0 // halo) + (b * ntb + i) * per - 1, 0), COL_U // D_B))
    else:
        prev_arr = prev
        prev_spec = pl.BlockSpec((None, halo, D_B), lambda b, i: (b, 0, 0))
    kern = functools.partial(_pool_kernel, tt=tt, first_pos=first_pos, zero_first=prev is None)
    return pl.pallas_call(
        kern,
        grid=(nb, ntb),
        in_specs=[prev_spec,
                  pl.BlockSpec((tt, D_B), lambda b, i: (rb + b * ntb + i, COL_U // D_B)),
                  pl.BlockSpec((len(POOL_WINDOWS), POOL_GROUP_DIM, POOL_GROUP_DIM), lambda b, i: (0, 0, 0)),
                  pl.BlockSpec((1, D_B), lambda b, i: (0, 0))],
        out_specs=pl.BlockSpec((tt, D_B), lambda b, i: (b * ntb + i, 0)),
        out_shape=jax.ShapeDtypeStruct((nb * t, D_B), bf16),
        compiler_params=_cparams(("parallel", "arbitrary")),
        name="pool",
    )(prev_arr, hp, pool_w, pool_scale.reshape(1, D_B))


SAFE_LOG_RANGE = 80.0


def _hgrn_kernel(cq_ref, cf_ref, ci_ref, cg_ref, lb_ref, ng_ref, s0_ref, o_ref, sn_ref,
                 st_ref, b_ref, q_ref, k_ref, osc_ref, *, tt, blk):
    tb = pl.program_id(1)
    nh = C_HEADS
    hd = C_HEAD_DIM

    @pl.when(tb == 0)
    def _init():
        for h in range(nh):
            st_ref[h] = s0_ref[h].T

    lb = lb_ref[...]
    f = jnp.maximum(lb, LB_FLOOR) + (1.0 - lb) * jax.nn.sigmoid(cf_ref[...])
    g = jnp.log(f)
    k_ref[...] = 1.0 - f
    q_ref[...] = _silu(cq_ref[...])
    r = lax.broadcasted_iota(jnp.int32, (tt, tt), 0)
    c = lax.broadcasted_iota(jnp.int32, (tt, tt), 1)
    tri = jnp.where(jnp.logical_and(r // blk == c // blk, c <= r), 1.0, 0.0).astype(bf16)
    g1 = g.astype(bf16)
    e1 = g - g1.astype(f32)
    g2 = e1.astype(bf16)
    g3 = (e1 - g2.astype(f32)).astype(bf16)
    b = (jnp.dot(tri, g1, preferred_element_type=f32) + jnp.dot(tri, g2, preferred_element_type=f32)
         + jnp.dot(tri, g3, preferred_element_type=f32))
    b_ref[...] = b
    safe = jnp.min(b) > -SAFE_LOG_RANGE

    tr = lax.broadcasted_iota(jnp.int32, (blk, blk), 0)
    tc = lax.broadcasted_iota(jnp.int32, (blk, blk), 1)
    causal = tc <= tr
    trow = lax.broadcasted_iota(jnp.int32, (blk, 1), 0)

    def chunk(ci, factored):
        r0 = pl.multiple_of(ci * blk, blk)
        for h in range(nh):
            rows = pl.ds(r0, blk)
            cols = slice(h * hd, (h + 1) * hd)
            bb = b_ref[rows, cols]
            qh = q_ref[rows, cols]
            kh = k_ref[rows, cols]
            ih = ci_ref[rows, cols]
            bl = bb[blk - 1:blk, :]
            ke = (kh * jnp.exp(bl - bb)).astype(bf16)
            ib = ih.astype(bf16)
            st = st_ref[h]
            o = lax.dot_general((qh * jnp.exp(bb)).astype(bf16), st.astype(bf16), NT_DIMS,
                                preferred_element_type=f32)
            if factored:
                qe = (qh * jnp.exp(bb - bl)).astype(bf16)
                a = lax.dot_general(qe, ke, NT_DIMS, preferred_element_type=f32)
                a = jnp.where(causal, a, 0.0)
                o = o + jnp.dot(a.astype(bf16), ib, preferred_element_type=f32)
            else:
                for s in range(blk):
                    live = trow >= s
                    dec = jnp.exp(jnp.where(live, bb - bb[s:s + 1, :], 0.0))
                    a_col = jnp.sum(qh * kh[s:s + 1, :] * dec, axis=-1, keepdims=True)
                    o = o + jnp.where(live, a_col, 0.0) * ih[s:s + 1, :]
            osc_ref[rows, cols] = o
            st_ref[h] = st * jnp.exp(bl) + lax.dot_general(ib, ke, TN_DIMS, preferred_element_type=f32)

    @pl.when(safe)
    def _():
        def body(ci, c):
            chunk(ci, True)
            return c
        lax.fori_loop(0, tt // blk, body, 0)

    @pl.when(jnp.logical_not(safe))
    def _():
        def body(ci, c):
            chunk(ci, False)
            return c
        lax.fori_loop(0, tt // blk, body, 0)

    outs = []
    for h in range(nh):
        cols = slice(h * hd, (h + 1) * hd)
        outs.append(_rms(osc_ref[:, cols], ng_ref[...]))
    o_ref[...] = (jnp.concatenate(outs, axis=-1) * _silu(cg_ref[...])).astype(o_ref.dtype)

    @pl.when(tb == pl.num_programs(1) - 1)
    def _fin():
        for h in range(nh):
            sn_ref[h] = st_ref[h].T


def _hgrn(hp, row0, nb, t, tt, lb, norm_g, s0):
    ntb = t // tt
    rb = row0 // tt

    def col(cidx):
        return pl.BlockSpec((tt, D_C), lambda b, i: (rb + b * ntb + i, cidx // D_C))

    st_spec = pl.BlockSpec((None, C_HEADS, C_HEAD_DIM, C_V_DIM), lambda b, i: (b, 0, 0, 0))
    kern = functools.partial(_hgrn_kernel, tt=tt, blk=HGRN_BLOCK)
    return pl.pallas_call(
        kern,
        grid=(nb, ntb),
        in_specs=[col(COL_CQ), col(COL_CF), col(COL_CI), col(COL_CG),
                  pl.BlockSpec((1, D_C), lambda b, i: (0, 0)),
                  pl.BlockSpec((1, C_V_DIM), lambda b, i: (0, 0)),
                  st_spec],
        out_specs=[pl.BlockSpec((tt, D_C), lambda b, i: (b * ntb + i, 0)), st_spec],
        out_shape=[jax.ShapeDtypeStruct((nb * t, D_C), bf16),
                   jax.ShapeDtypeStruct((nb, C_HEADS, C_HEAD_DIM, C_V_DIM), f32)],
        scratch_shapes=[pltpu.VMEM((C_HEADS, C_V_DIM, C_HEAD_DIM), f32),
                        pltpu.VMEM((tt, D_C), f32), pltpu.VMEM((tt, D_C), f32),
                        pltpu.VMEM((tt, D_C), f32), pltpu.VMEM((tt, D_C), f32)],
        compiler_params=_cparams(("parallel", "arbitrary")),
        name="hgrn",
    )(hp, hp, hp, hp, lb.reshape(1, D_C), norm_g.reshape(1, C_V_DIM), s0)


def _merge_kernel(x_ref, g_ref, oa_ref, ob_ref, oc_ref, wga_ref, wgb_ref, wgc_ref,
                  pa_ref, pb_ref, pc_ref, wo_ref, o_ref, xn_ref, acc_ref):
    j = pl.program_id(1)

    @pl.when(j == 0)
    def _():
        xn_ref[...] = _rms(x_ref[...], g_ref[...]).astype(bf16)
        acc_ref[...] = jnp.zeros_like(acc_ref)

    xn = xn_ref[...]

    def branch(wg_ref, o_ref_, p_ref):
        gate = jax.nn.sigmoid(jnp.dot(xn, wg_ref[...], preferred_element_type=f32))
        return gate * jnp.dot(o_ref_[...], p_ref[...], preferred_element_type=f32)

    merged = branch(wga_ref, oa_ref, pa_ref) + branch(wgb_ref, ob_ref, pb_ref) + branch(wgc_ref, oc_ref, pc_ref)
    acc_ref[...] += jnp.dot(merged.astype(bf16), wo_ref[...], preferred_element_type=f32)

    @pl.when(j == pl.num_programs(1) - 1)
    def _():
        o_ref[...] = x_ref[...] + acc_ref[...]


def _merge(h, g, oa, ob, oc, wg, pa, pb, pc, wo, tm, tj):
    m, d = h.shape
    nj = d // tj
    row = lambda i, j: (i, 0)
    return pl.pallas_call(
        _merge_kernel,
        grid=(m // tm, nj),
        in_specs=[pl.BlockSpec((tm, d), row),
                  pl.BlockSpec((1, d), lambda i, j: (0, 0)),
                  pl.BlockSpec((tm, D_A), row),
                  pl.BlockSpec((tm, D_B), row),
                  pl.BlockSpec((tm, D_C), row),
                  pl.BlockSpec((d, tj), lambda i, j: (0, j)),
                  pl.BlockSpec((d, tj), lambda i, j: (0, nj + j)),
                  pl.BlockSpec((d, tj), lambda i, j: (0, 2 * nj + j)),
                  pl.BlockSpec((D_A, tj), lambda i, j: (0, j)),
                  pl.BlockSpec((D_B, tj), lambda i, j: (0, j)),
                  pl.BlockSpec((D_C, tj), lambda i, j: (0, j)),
                  pl.BlockSpec((tj, d), lambda i, j: (j, 0))],
        out_specs=pl.BlockSpec((tm, d), row),
        out_shape=jax.ShapeDtypeStruct((m, d), f32),
        scratch_shapes=[pltpu.VMEM((tm, d), bf16), pltpu.VMEM((tm, d), f32)],
        compiler_params=_cparams(("parallel", "arbitrary")),
        name="merge",
    )(h, g.reshape(1, d), oa, ob, oc, wg, wg, wg, pa, pb, pc, wo)


def _router_kernel(x_ref, g_ref, r_ref, o_ref):
    xn = _rms(x_ref[...], g_ref[...]).astype(bf16)
    logits = jnp.dot(xn, r_ref[...], preferred_element_type=f32)
    lane = lax.broadcasted_iota(jnp.int32, logits.shape, 1)
    logits = jnp.where(lane < N_EXPERTS, logits, -jnp.inf)
    m1 = jnp.max(logits, axis=-1, keepdims=True)
    i1 = jnp.min(jnp.where(logits == m1, lane, LANE), axis=-1, keepdims=True)
    rest = jnp.where(lane == i1, -jnp.inf, logits)
    m2 = jnp.max(rest, axis=-1, keepdims=True)
    i2 = jnp.min(jnp.where(rest == m2, lane, LANE), axis=-1, keepdims=True)
    e2 = jnp.exp(m2 - m1)
    g1 = 1.0 / (1.0 + e2)
    g2 = e2 / (1.0 + e2)
    comb = jnp.where(lane == i1, g1, 0.0) + jnp.where(lane == i2, g2, 0.0)
    o_ref[...] = comb[:, :N_EXPERTS]


def _router(h, g, router, tm):
    m, d = h.shape
    rp = jnp.pad(router, ((0, 0), (0, LANE - N_EXPERTS))).astype(bf16)
    return pl.pallas_call(
        _router_kernel,
        grid=(m // tm,),
        in_specs=[pl.BlockSpec((tm, d), lambda i: (i, 0)),
                  pl.BlockSpec((1, d), lambda i: (0, 0)),
                  pl.BlockSpec((d, LANE), lambda i: (0, 0))],
        out_specs=pl.BlockSpec((tm, N_EXPERTS), lambda i: (i, 0)),
        out_shape=jax.ShapeDtypeStruct((m, N_EXPERTS), f32),
        compiler_params=_cparams(("parallel",)),
        name="router",
    )(h, g.reshape(1, d), rp)


def _ffn_kernel(x_ref, g_ref, c_ref, wg_ref, wu_ref, wd_ref, o_ref, xn_ref, acc_ref, tot_ref):
    e = pl.program_id(1)
    fi = pl.program_id(2)
    ne = pl.num_programs(1)
    nf = pl.num_programs(2)

    @pl.when(jnp.logical_and(e == 0, fi == 0))
    def _():
        xn_ref[...] = _rms(x_ref[...], g_ref[...]).astype(bf16)
        tot_ref[...] = jnp.zeros_like(tot_ref)

    @pl.when(fi == 0)
    def _():
        acc_ref[...] = jnp.zeros_like(acc_ref)

    xn = xn_ref[...]
    a = jnp.dot(xn, wg_ref[...], preferred_element_type=f32)
    u = jnp.dot(xn, wu_ref[...], preferred_element_type=f32)
    act = (_silu(a) * u).astype(bf16)
    acc_ref[...] += jnp.dot(act, wd_ref[...], preferred_element_type=f32)

    @pl.when(fi == nf - 1)
    def _():
        comb = c_ref[...]
        lane = lax.broadcasted_iota(jnp.int32, comb.shape, 1)
        ce = jnp.sum(jnp.where(lane == e, comb, 0.0), axis=-1, keepdims=True)
        tot_ref[...] += ce * acc_ref[...]

    @pl.when(jnp.logical_and(e == ne - 1, fi == nf - 1))
    def _():
        o_ref[...] = x_ref[...] + tot_ref[...]


def _ffn(h, g, comb, wg, wu, wd, tm, tf):
    m, d = h.shape
    ne, _, ff = wg.shape
    return pl.pallas_call(
        _ffn_kernel,
        grid=(m // tm, ne, ff // tf),
        in_specs=[pl.BlockSpec((tm, d), lambda i, e, f: (i, 0)),
                  pl.BlockSpec((1, d), lambda i, e, f: (0, 0)),
                  pl.BlockSpec((tm, ne), lambda i, e, f: (i, 0)),
                  pl.BlockSpec((None, d, tf), lambda i, e, f: (e, 0, f)),
                  pl.BlockSpec((None, d, tf), lambda i, e, f: (e, 0, f)),
                  pl.BlockSpec((None, tf, d), lambda i, e, f: (e, f, 0))],
        out_specs=pl.BlockSpec((tm, d), lambda i, e, f: (i, 0)),
        out_shape=jax.ShapeDtypeStruct((m, d), f32),
        scratch_shapes=[pltpu.VMEM((tm, d), bf16), pltpu.VMEM((tm, d), f32), pltpu.VMEM((tm, d), f32)],
        compiler_params=_cparams(("parallel", "arbitrary", "arbitrary")),
        name="ffn",
    )(h, g.reshape(1, d), comb, wg, wu, wd)


def _final_norm_kernel(x_ref, g_ref, o_ref):
    o_ref[...] = _rms(x_ref[...], g_ref[...])


def _final_norm(h, g, row0, rows, tm):
    d = h.shape[1]
    rb = row0 // tm
    return pl.pallas_call(
        _final_norm_kernel,
        grid=(rows // tm,),
        in_specs=[pl.BlockSpec((tm, d), lambda i: (rb + i, 0)),
                  pl.BlockSpec((1, d), lambda i: (0, 0))],
        out_specs=pl.BlockSpec((tm, d), lambda i: (i, 0)),
        out_shape=jax.ShapeDtypeStruct((rows, d), f32),
        compiler_params=_cparams(("parallel",)),
        name="final_norm",
    )(h, g.reshape(1, d))


def _pack_in_proj(w):
    o = np.cumsum((0,) + IN_SIZES)
    pad = lambda a: jnp.pad(a, ((0, 0), (0, LANE - a.shape[1])))
    packed = jnp.concatenate([
        w[:, o[0]:o[3]],
        w[:, o[3]:o[4]],
        w[:, o[6]:o[7]],
        w[:, o[7]:o[11]],
        pad(w[:, o[4]:o[5]]),
        pad(w[:, o[5]:o[6]]),
    ], axis=1).astype(bf16)
    return packed, w[:, o[11]:].astype(bf16)


def kernel(x_prompt, x_sample, cache_k, cache_v, cache_kidx, cache_pool, state_hgrn, norm_mix_g, norm_ffn_g, final_norm_g, w_in, w_proj_a, w_proj_b, w_proj_c, w_out, pool_w, pool_scale, hgrn_lb_logits, hgrn_norm_g, ffn_w_gate, ffn_w_up, ffn_w_down, moe_router, moe_w_gate, moe_w_up, moe_w_down):
    batch, seq, d = x_prompt.shape
    nb, nq, _ = x_sample.shape
    depth = w_in.shape[0]
    past = cache_k.shape[2]
    mp = batch * seq
    ms = nb * nq
    m = mp + ms

    lb_soft = jax.nn.softmax(hgrn_lb_logits.astype(f32), axis=0)
    lb_all = jnp.cumsum(lb_soft, axis=0) - lb_soft[0:1]

    h = jnp.concatenate([x_prompt.reshape(mp, d), x_sample.reshape(ms, d)], axis=0)
    s0_prompt = jnp.zeros((batch, C_HEADS, C_HEAD_DIM, C_V_DIM), f32)
    tm = 768 if m % 768 == 0 else 256
    tmh = tm // 2

    outs = [[] for _ in range(10)]
    for l in range(depth):
        wp, wgates = _pack_in_proj(w_in[l])
        hp = _inproj(h, norm_mix_g[l], wp, tm, 640)

        oa_p = _dsa_prompt(hp, batch, seq, 256)
        oa_s = _dsa_sample(hp, mp, nb, nq, cache_k[l], cache_v[l], cache_kidx[l])
        ob_p = _pool(hp, 0, batch, seq, 256, None, pool_w[l], pool_scale[l], 0)
        prev = jnp.pad(cache_pool[l], ((0, 0), (1, 0), (0, 0)))
        ob_s = _pool(hp, mp, nb, nq, nq, prev, pool_w[l], pool_scale[l], past)
        oc_p, st_p = _hgrn(hp, 0, batch, seq, 256, lb_all[l], hgrn_norm_g[l], s0_prompt)
        oc_s, st_s = _hgrn(hp, mp, nb, nq, nq, lb_all[l], hgrn_norm_g[l], state_hgrn[l])

        oa = jnp.concatenate([oa_p, oa_s], axis=0)
        ob = jnp.concatenate([ob_p, ob_s], axis=0)
        oc = jnp.concatenate([oc_p, oc_s], axis=0)
        h = _merge(h, norm_mix_g[l], oa, ob, oc, wgates, w_proj_a[l].astype(bf16), w_proj_b[l].astype(bf16),
                   w_proj_c[l].astype(bf16), w_out[l].astype(bf16), tmh, 512)

        kk = hp[:, COL_K:COL_K + D_A]
        vv = hp[:, COL_V:COL_V + D_A]
        ki = hp[:, COL_KI:COL_KI + IDX_DIM]
        uu = hp[:, COL_U:COL_U + D_B]
        outs[0].append(kk[:mp].reshape(batch, seq, A_HEADS, A_HEAD_DIM))
        outs[1].append(vv[:mp].reshape(batch, seq, A_HEADS, A_HEAD_DIM))
        outs[2].append(ki[:mp].reshape(batch, seq, IDX_DIM))
        outs[3].append(uu[:mp].reshape(batch, seq, D_B)[:, -POOL_STATE:])
        outs[4].append(st_p)
        outs[5].append(kk[mp:].reshape(nb, nq, A_HEADS, A_HEAD_DIM))
        outs[6].append(vv[mp:].reshape(nb, nq, A_HEADS, A_HEAD_DIM))
        outs[7].append(ki[mp:].reshape(nb, nq, IDX_DIM))
        u_ext = jnp.concatenate([cache_pool[l], uu[mp:].reshape(nb, nq, D_B)], axis=1)
        outs[8].append(u_ext[:, -POOL_STATE:])
        outs[9].append(st_s)

        j = l // 2
        if l % 2 == 0:
            comb = jnp.ones((m, 1), f32)
            h = _ffn(h, norm_ffn_g[l], comb, ffn_w_gate[j][None].astype(bf16), ffn_w_up[j][None].astype(bf16),
                     ffn_w_down[j][None].astype(bf16), tmh, 512)
        else:
            comb = _router(h, norm_ffn_g[l], moe_router[j], tm)
            h = _ffn(h, norm_ffn_g[l], comb, moe_w_gate[j].astype(bf16), moe_w_up[j].astype(bf16),
                     moe_w_down[j].astype(bf16), tmh, 512)

    y_prompt = _final_norm(h, final_norm_g, 0, mp, 256).reshape(batch, seq, d)
    y_sample = _final_norm(h, final_norm_g, mp, ms, 256).reshape(nb, nq, d)
    return (y_prompt, y_sample) + tuple(jnp.stack(o) for o in outs)
```

```python
import functools

import numpy as np
import jax
import jax.numpy as jnp
from jax import lax
from jax.experimental import pallas as pl
from jax.experimental.pallas import tpu as pltpu

f32 = jnp.float32
bf16 = jnp.bfloat16

D_MODEL = 2048
CHUNK = 64
A_HEADS = 8
A_HEAD_DIM = 128
D_A = A_HEADS * A_HEAD_DIM
IDX_HEADS = 8
IDX_DIM = 64
TOPK_MAX = 256
POOL_WINDOWS = (2, 4, 8, 16)
POOL_GROUP_DIM = 128
D_B = len(POOL_WINDOWS) * POOL_GROUP_DIM
POOL_STATE = max(POOL_WINDOWS) - 1
C_HEADS = 4
C_HEAD_DIM = 128
C_V_DIM = 128
D_C = C_HEADS * C_HEAD_DIM
HGRN_BLOCK = 16
N_BRANCH = 3
N_EXPERTS = 8
IN_SIZES = (D_A, D_A, D_A, IDX_HEADS * IDX_DIM, IDX_DIM, IDX_HEADS,
            D_B, D_C, D_C, C_HEADS * C_V_DIM, C_HEADS * C_V_DIM, N_BRANCH * D_MODEL)
EPS = 1e-6
NEG = -1e30
LB_FLOOR = 1e-30
INT_MIN = -2 ** 31

LOG2E = 1.4426950408889634
LANE = 128
COUNT_ROWS = 64
COL_Q, COL_K, COL_V = 0, D_A, 2 * D_A
COL_QI = 3 * D_A
COL_U = COL_QI + 512
COL_CQ = COL_U + 512
COL_CF = COL_CQ + 512
COL_CI = COL_CF + 512
COL_CG = COL_CI + 512
COL_KI = COL_CG + 512
COL_W = COL_KI + LANE
N_PACK = COL_W + LANE

VMEM_LIMIT = 56 * 1024 * 1024

NT_DIMS = (((1,), (1,)), ((), ()))
TN_DIMS = (((0,), (0,)), ((), ()))


def _cparams(sem):
    return pltpu.CompilerParams(dimension_semantics=sem, vmem_limit_bytes=VMEM_LIMIT)


def _rms(x, g):
    ms = jnp.mean(x * x, axis=-1, keepdims=True)
    return x * lax.rsqrt(ms + EPS) * g


def _silu(x):
    return x * jax.nn.sigmoid(x)


def _sort_key(x):
    bits = pltpu.bitcast(x, jnp.int32)
    return bits ^ ((bits >> 31) & jnp.int32(0x7FFFFFFF))


def _kth_largest(count_ge, kk):
    zero = jnp.zeros(kk.shape, jnp.int32)
    t0 = jnp.where(count_ge(zero) >= kk, zero, jnp.int32(INT_MIN))

    def bit_body(i, t):
        c = t | lax.shift_left(jnp.int32(1), 30 - i)
        return jnp.where(count_ge(c) >= kk, c, t)

    return lax.fori_loop(0, 31, bit_body, t0)


def _inproj_kernel(x_ref, g_ref, w_ref, o_ref, xn_ref):
    @pl.when(pl.program_id(1) == 0)
    def _():
        xn_ref[...] = _rms(x_ref[...], g_ref[...]).astype(bf16)

    o_ref[...] = jnp.dot(xn_ref[...], w_ref[...], preferred_element_type=f32)


def _inproj(h, g, wp, tm, tn):
    m, d = h.shape
    n = wp.shape[1]
    return pl.pallas_call(
        _inproj_kernel,
        grid=(m // tm, n // tn),
        in_specs=[pl.BlockSpec((tm, d), lambda i, j: (i, 0)),
                  pl.BlockSpec((1, d), lambda i, j: (0, 0)),
                  pl.BlockSpec((d, tn), lambda i, j: (0, j))],
        out_specs=pl.BlockSpec((tm, tn), lambda i, j: (i, j)),
        out_shape=jax.ShapeDtypeStruct((m, n), f32),
        scratch_shapes=[pltpu.VMEM((tm, d), bf16)],
        compiler_params=_cparams(("parallel", "arbitrary")),
        name="inproj",
    )(h, g.reshape(1, d), wp)


def _idx_scores(qi, w, kib):
    acc = jnp.zeros((qi.shape[0], kib.shape[0]), f32)
    for hh in range(IDX_HEADS):
        d = lax.dot_general(qi[:, hh * IDX_DIM:(hh + 1) * IDX_DIM], kib, NT_DIMS,
                            preferred_element_type=f32)
        acc = acc + w[:, hh:hh + 1] * jnp.maximum(d, 0.0)
    return acc


def _fold_lanes(x):
    part = x[:, :LANE]
    for j in range(1, x.shape[1] // LANE):
        part = part + x[:, j * LANE:(j + 1) * LANE]
    return part


def _tri_incl(n):
    r = lax.broadcasted_iota(jnp.int32, (n, n), 0)
    c = lax.broadcasted_iota(jnp.int32, (n, n), 1)
    return jnp.where(r <= c, 1.0, 0.0).astype(bf16)


def _dsa_prompt_kernel(q_ref, k_ref, v_ref, qi_ref, ki_ref, w_ref, o_ref, keys_ref, bias_ref,
                       qs_ref, m_ref, l_ref, acc_ref, *, tq, topk):
    qb = pl.program_id(1)
    nkb = qb + 1

    def _select():
        qi = qi_ref[...].astype(bf16)
        w = w_ref[...][:, :IDX_HEADS] * (IDX_HEADS ** -0.5 * IDX_DIM ** -0.5)
        row = lax.broadcasted_iota(jnp.int32, (tq, tq), 0)
        col = lax.broadcasted_iota(jnp.int32, (tq, tq), 1)
        diag_adm = (col // CHUNK) <= (row // CHUNK)

        def score_body(kb, c):
            off = pl.multiple_of(kb * tq, tq)
            kib = ki_ref[pl.ds(off, tq), :][:, :IDX_DIM].astype(bf16)
            key = _sort_key(_idx_scores(qi, w, kib))
            adm = jnp.logical_or(kb < qb, diag_adm)
            keys_ref[kb] = jnp.where(adm, key, jnp.int32(INT_MIN))
            return c

        lax.fori_loop(0, nkb, score_body, 0)

        def count(cmp, c):
            parts = []
            cb = jnp.broadcast_to(c, (tq, LANE))
            for r0 in range(0, tq, COUNT_ROWS):
                cs = cb[r0:r0 + COUNT_ROWS]

                def body(kb, acc):
                    for j in range(tq // LANE):
                        k = keys_ref[kb, r0:r0 + COUNT_ROWS, j * LANE:(j + 1) * LANE]
                        acc = acc + jnp.where(cmp(k, cs), 1.0, 0.0)
                    return acc
                parts.append(lax.fori_loop(0, nkb, body, jnp.zeros((COUNT_ROWS, LANE), f32)))
            return jnp.sum(jnp.concatenate(parts, axis=0), axis=-1, keepdims=True)

        r1 = lax.broadcasted_iota(jnp.int32, (tq, 1), 0)
        n_adm = ((qb * tq + r1) // CHUNK + 1) * CHUNK
        kk = jnp.minimum(topk, n_adm).astype(f32)
        thr = _kth_largest(lambda c: count(lambda k, t: k >= t, c), kk)
        n_ge = count(lambda k, t: k >= t, thr)
        has_tie = jnp.max(n_ge - kk) > 0.0

        @pl.when(jnp.logical_not(has_tie))
        def _():
            def body(kb, c):
                bias_ref[kb] = jnp.where(keys_ref[kb] >= thr, 0.0, NEG)
                return c
            lax.fori_loop(0, nkb, body, 0)

        @pl.when(has_tie)
        def _():
            need = kk - count(lambda k, t: k > t, thr)
            tri = _tri_incl(tq)

            def body(kb, seen):
                k = keys_ref[kb]
                eq = jnp.where(k == thr, 1.0, 0.0)
                rank = seen + jnp.dot(eq.astype(bf16), tri, preferred_element_type=f32)
                take = jnp.logical_or(k > thr, jnp.logical_and(k == thr, rank <= need))
                bias_ref[kb] = jnp.where(take, 0.0, NEG)
                return seen + jnp.sum(eq, axis=-1, keepdims=True)
            lax.fori_loop(0, nkb, body, jnp.zeros((tq, 1), f32))

    _select()

    hd = A_HEAD_DIM
    qs_ref[...] = (q_ref[...] * (hd ** -0.5 * LOG2E)).astype(bf16)
    m_ref[...] = jnp.full(m_ref.shape, NEG, f32)
    l_ref[...] = jnp.zeros(l_ref.shape, f32)
    acc_ref[...] = jnp.zeros(acc_ref.shape, f32)

    def att_body(kb, c):
        off = pl.multiple_of(kb * tq, tq)
        kblk = k_ref[pl.ds(off, tq), :]
        vblk = v_ref[pl.ds(off, tq), :]
        bias = bias_ref[kb]
        for h in range(A_HEADS):
            cols = slice(h * hd, (h + 1) * hd)
            s = lax.dot_general(qs_ref[:, cols], kblk[:, cols], NT_DIMS, preferred_element_type=f32) + bias
            m_old = m_ref[h]
            m_new = jnp.maximum(m_old, jnp.max(s, axis=-1, keepdims=True))
            p = jnp.exp2(s - m_new)
            corr = jnp.exp2(m_old - m_new)
            l_ref[h] = corr * l_ref[h] + jnp.sum(p, axis=-1, keepdims=True)
            acc_ref[h] = corr * acc_ref[h] + jnp.dot(p.astype(bf16), vblk[:, cols], preferred_element_type=f32)
            m_ref[h] = m_new
        return c

    lax.fori_loop(0, nkb, att_body, 0)
    for h in range(A_HEADS):
        o_ref[:, h * hd:(h + 1) * hd] = (acc_ref[h] / l_ref[h]).astype(o_ref.dtype)


def _dsa_prompt(hp, kv, batch, seq, tq):
    nq = seq // tq
    topk = min(TOPK_MAX, seq // 4)
    kern = functools.partial(_dsa_prompt_kernel, tq=tq, topk=topk)
    once = pl.Buffered(1)
    return pl.pallas_call(
        kern,
        grid=(batch, nq),
        in_specs=[
            pl.BlockSpec((tq, D_A), lambda b, i: (b * nq + i, COL_Q // D_A)),
            pl.BlockSpec((seq, D_A), lambda b, i: (b, 0), pipeline_mode=once),
            pl.BlockSpec((seq, D_A), lambda b, i: (b, 1), pipeline_mode=once),
            pl.BlockSpec((tq, 512), lambda b, i: (b * nq + i, COL_QI // 512)),
            pl.BlockSpec((seq, LANE), lambda b, i: (b, COL_KI // LANE), pipeline_mode=once),
            pl.BlockSpec((tq, LANE), lambda b, i: (b * nq + i, COL_W // LANE)),
        ],
        out_specs=pl.BlockSpec((tq, D_A), lambda b, i: (b * nq + i, 0)),
        out_shape=jax.ShapeDtypeStruct((batch * seq, D_A), bf16),
        scratch_shapes=[pltpu.VMEM((nq, tq, tq), jnp.int32), pltpu.VMEM((nq, tq, tq), f32),
                        pltpu.VMEM((tq, D_A), bf16),
                        pltpu.VMEM((A_HEADS, tq, 1), f32), pltpu.VMEM((A_HEADS, tq, 1), f32),
                        pltpu.VMEM((A_HEADS, tq, A_HEAD_DIM), f32)],
        compiler_params=_cparams(("parallel", "arbitrary")),
        name="dsa_prompt",
    )(hp, kv, kv, hp, hp, hp)


def _dsa_sample_kernel(q_ref, kn_ref, vn_ref, qi_ref, kin_ref, w_ref, kc_ref, vc_ref, kic_ref, o_ref,
                       biasc_ref, biasn_ref, *, topk):
    nq = q_ref.shape[0]
    past = kc_ref.shape[0]

    @pl.when(pl.program_id(1) == 0)
    def _select():
        qi = qi_ref[...].astype(bf16)
        w = w_ref[...][:, :IDX_HEADS] * (IDX_HEADS ** -0.5 * IDX_DIM ** -0.5)
        key_c = _sort_key(_idx_scores(qi, w, kic_ref[...].astype(bf16)))
        key_n = _sort_key(_idx_scores(qi, w, kin_ref[...][:, :IDX_DIM].astype(bf16)))

        def count(pred):
            cc = jnp.sum(_fold_lanes(jnp.where(pred(key_c), 1.0, 0.0)), axis=-1, keepdims=True)
            return cc + jnp.sum(jnp.where(pred(key_n), 1.0, 0.0), axis=-1, keepdims=True)

        kk = jnp.full((nq, 1), float(topk), f32)
        thr = _kth_largest(lambda c: count(lambda k: k >= c), kk)
        n_ge = count(lambda k: k >= thr)
        has_tie = jnp.max(n_ge - kk) > 0.0

        @pl.when(jnp.logical_not(has_tie))
        def _():
            biasc_ref[...] = jnp.where(key_c >= thr, 0.0, NEG)
            biasn_ref[...] = jnp.where(key_n >= thr, 0.0, NEG)

        @pl.when(has_tie)
        def _():
            need = kk - count(lambda k: k > thr)
            tri = _tri_incl(LANE)
            seen = jnp.zeros((nq, 1), f32)
            for j in range(past // LANE):
                k = key_c[:, j * LANE:(j + 1) * LANE]
                eq = jnp.where(k == thr, 1.0, 0.0)
                rank = seen + jnp.dot(eq.astype(bf16), tri, preferred_element_type=f32)
                take = jnp.logical_or(k > thr, jnp.logical_and(k == thr, rank <= need))
                biasc_ref[:, j * LANE:(j + 1) * LANE] = jnp.where(take, 0.0, NEG)
                seen = seen + jnp.sum(eq, axis=-1, keepdims=True)
            eq = jnp.where(key_n == thr, 1.0, 0.0)
            rank = seen + jnp.dot(eq.astype(bf16), _tri_incl(nq), preferred_element_type=f32)
            take = jnp.logical_or(key_n > thr, jnp.logical_and(key_n == thr, rank <= need))
            biasn_ref[...] = jnp.where(take, 0.0, NEG)

    q = q_ref[...].astype(bf16)
    scale = A_HEAD_DIM ** -0.5
    s_c = lax.dot_general(q, kc_ref[...].astype(bf16), NT_DIMS, preferred_element_type=f32) * scale
    s_c = s_c + biasc_ref[...]
    s_n = lax.dot_general(q, kn_ref[...].astype(bf16), NT_DIMS, preferred_element_type=f32) * scale
    s_n = s_n + biasn_ref[...]
    m = jnp.maximum(jnp.max(s_c, axis=-1, keepdims=True), jnp.max(s_n, axis=-1, keepdims=True))
    p_c = jnp.exp(s_c - m)
    p_n = jnp.exp(s_n - m)
    l = jnp.sum(p_c, axis=-1, keepdims=True) + jnp.sum(p_n, axis=-1, keepdims=True)
    acc = jnp.dot(p_c.astype(bf16), vc_ref[...].astype(bf16), preferred_element_type=f32)
    acc = acc + jnp.dot(p_n.astype(bf16), vn_ref[...].astype(bf16), preferred_element_type=f32)
    o_ref[...] = (acc / l).astype(o_ref.dtype)


def _dsa_sample(hp, row0, nb, nq, cache_k, cache_v, cache_kidx):
    past = cache_k.shape[1]
    topk = min(TOPK_MAX, (past + nq) // 4)
    kc = cache_k.reshape(nb, past, D_A)
    vc = cache_v.reshape(nb, past, D_A)
    rb = row0 // nq
    hb = LANE
    kern = functools.partial(_dsa_sample_kernel, topk=topk)
    return pl.pallas_call(
        kern,
        grid=(nb, A_HEADS),
        in_specs=[
            pl.BlockSpec((nq, hb), lambda b, h: (rb + b, COL_Q // hb + h)),
            pl.BlockSpec((nq, hb), lambda b, h: (rb + b, COL_K // hb + h)),
            pl.BlockSpec((nq, hb), lambda b, h: (rb + b, COL_V // hb + h)),
            pl.BlockSpec((nq, 512), lambda b, h: (rb + b, COL_QI // 512)),
            pl.BlockSpec((nq, hb), lambda b, h: (rb + b, COL_KI // hb)),
            pl.BlockSpec((nq, hb), lambda b, h: (rb + b, COL_W // hb)),
            pl.BlockSpec((None, past, hb), lambda b, h: (b, 0, h)),
            pl.BlockSpec((None, past, hb), lambda b, h: (b, 0, h)),
            pl.BlockSpec((None, past, IDX_DIM), lambda b, h: (b, 0, 0)),
        ],
        out_specs=pl.BlockSpec((nq, hb), lambda b, h: (b, h)),
        out_shape=jax.ShapeDtypeStruct((nb * nq, D_A), bf16),
        scratch_shapes=[pltpu.VMEM((nq, past), f32), pltpu.VMEM((nq, nq), f32)],
        compiler_params=_cparams(("parallel", "arbitrary")),
        name="dsa_sample",
    )(hp, hp, hp, hp, hp, hp, kc, vc, cache_kidx)


def _pool_kernel(prev_ref, u_ref, pw_ref, ps_ref, o_ref, *, tt, first_pos, zero_first):
    tb = pl.program_id(1)
    halo = POOL_STATE + 1
    prev = prev_ref[...]
    if zero_first:
        prev = jnp.where(tb == 0, 0.0, prev)
    cur = u_ref[...]
    ext = jnp.concatenate([prev, cur], axis=0)
    pos = first_pos + tb * tt + lax.broadcasted_iota(jnp.int32, (tt, 1), 0)
    outs = []
    for gi, wdw in enumerate(POOL_WINDOWS):
        cols = slice(gi * POOL_GROUP_DIM, (gi + 1) * POOL_GROUP_DIM)
        a = ext[:, cols]
        n = 1
        while n < wdw:
            a = a[n:] + a[:-n]
            n *= 2
        s = a[halo + 1 - wdw: halo + 1 - wdw + tt]
        cnt = jnp.minimum(pos + 1, wdw).astype(f32)
        d = s / cnt - cur[:, cols]
        outs.append(jnp.dot(d.astype(bf16), pw_ref[gi].astype(bf16), preferred_element_type=f32))
    o = jnp.concatenate(outs, axis=-1) * ps_ref[...]
    o_ref[...] = o.astype(o_ref.dtype)


def _pool(hp, row0, nb, t, tt, prev, pool_w, pool_scale, first_pos):
    ntb = t // tt
    halo = POOL_STATE + 1
    rb = row0 // tt
    if prev is None:
        prev_arr = hp
        per = tt // halo
        prev_spec = pl.BlockSpec(
            (halo, D_B), lambda b, i: (jnp.maximum((row0 // halo) + (b * ntb + i) * per - 1, 0), COL_U // D_B))
    else:
        prev_arr = prev
        prev_spec = pl.BlockSpec((None, halo, D_B), lambda b, i: (b, 0, 0))
    kern = functools.partial(_pool_kernel, tt=tt, first_pos=first_pos, zero_first=prev is None)
    return pl.pallas_call(
        kern,
        grid=(nb, ntb),
        in_specs=[prev_spec,
                  pl.BlockSpec((tt, D_B), lambda b, i: (rb + b * ntb + i, COL_U // D_B)),
                  pl.BlockSpec((len(POOL_WINDOWS), POOL_GROUP_DIM, POOL_GROUP_DIM), lambda b, i: (0, 0, 0)),
                  pl.BlockSpec((1, D_B), lambda b, i: (0, 0))],
        out_specs=pl.BlockSpec((tt, D_B), lambda b, i: (b * ntb + i, 0)),
        out_shape=jax.ShapeDtypeStruct((nb * t, D_B), bf16),
        compiler_params=_cparams(("parallel", "arbitrary")),
        name="pool",
    )(prev_arr, hp, pool_w, pool_scale.reshape(1, D_B))


SAFE_LOG_RANGE = 80.0


def _hgrn_kernel(cq_ref, cf_ref, ci_ref, cg_ref, lb_ref, ng_ref, s0_ref, o_ref, sn_ref,
                 st_ref, b_ref, q_ref, k_ref, osc_ref, *, tt, blk):
    tb = pl.program_id(1)
    nh = C_HEADS
    hd = C_HEAD_DIM

    @pl.when(tb == 0)
    def _init():
        for h in range(nh):
            st_ref[h] = s0_ref[h].T

    lb = lb_ref[...]
    f = jnp.maximum(lb, LB_FLOOR) + (1.0 - lb) * jax.nn.sigmoid(cf_ref[...])
    g = jnp.log(f)
    k_ref[...] = 1.0 - f
    q_ref[...] = _silu(cq_ref[...])
    r = lax.broadcasted_iota(jnp.int32, (tt, tt), 0)
    c = lax.broadcasted_iota(jnp.int32, (tt, tt), 1)
    tri = jnp.where(jnp.logical_and(r // blk == c // blk, c <= r), 1.0, 0.0).astype(bf16)
    g1 = g.astype(bf16)
    e1 = g - g1.astype(f32)
    g2 = e1.astype(bf16)
    g3 = (e1 - g2.astype(f32)).astype(bf16)
    b = (jnp.dot(tri, g1, preferred_element_type=f32) + jnp.dot(tri, g2, preferred_element_type=f32)
         + jnp.dot(tri, g3, preferred_element_type=f32))
    b_ref[...] = b
    safe = jnp.min(b) > -SAFE_LOG_RANGE

    tr = lax.broadcasted_iota(jnp.int32, (blk, blk), 0)
    tc = lax.broadcasted_iota(jnp.int32, (blk, blk), 1)
    causal = tc <= tr
    trow = lax.broadcasted_iota(jnp.int32, (blk, 1), 0)

    def chunk(ci, factored):
        r0 = pl.multiple_of(ci * blk, blk)
        for h in range(nh):
            rows = pl.ds(r0, blk)
            cols = slice(h * hd, (h + 1) * hd)
            bb = b_ref[rows, cols]
            qh = q_ref[rows, cols]
            kh = k_ref[rows, cols]
            ih = ci_ref[rows, cols]
            bl = bb[blk - 1:blk, :]
            ke = (kh * jnp.exp(bl - bb)).astype(bf16)
            ib = ih.astype(bf16)
            st = st_ref[h]
            o = lax.dot_general((qh * jnp.exp(bb)).astype(bf16), st.astype(bf16), NT_DIMS,
                                preferred_element_type=f32)
            if factored:
                qe = (qh * jnp.exp(bb - bl)).astype(bf16)
                a = lax.dot_general(qe, ke, NT_DIMS, preferred_element_type=f32)
                a = jnp.where(causal, a, 0.0)
                o = o + jnp.dot(a.astype(bf16), ib, preferred_element_type=f32)
            else:
                for s in range(blk):
                    live = trow >= s
                    dec = jnp.exp(jnp.where(live, bb - bb[s:s + 1, :], 0.0))
                    a_col = jnp.sum(qh * kh[s:s + 1, :] * dec, axis=-1, keepdims=True)
                    o = o + jnp.where(live, a_col, 0.0) * ih[s:s + 1, :]
            osc_ref[rows, cols] = o
            st_ref[h] = st * jnp.exp(bl) + lax.dot_general(ib, ke, TN_DIMS, preferred_element_type=f32)

    @pl.when(safe)
    def _():
        def body(ci, c):
            chunk(ci, True)
            return c
        lax.fori_loop(0, tt // blk, body, 0)

    @pl.when(jnp.logical_not(safe))
    def _():
        def body(ci, c):
            chunk(ci, False)
            return c
        lax.fori_loop(0, tt // blk, body, 0)

    outs = []
    for h in range(nh):
        cols = slice(h * hd, (h + 1) * hd)
        outs.append(_rms(osc_ref[:, cols], ng_ref[...]))
    o_ref[...] = (jnp.concatenate(outs, axis=-1) * _silu(cg_ref[...])).astype(o_ref.dtype)

    @pl.when(tb == pl.num_programs(1) - 1)
    def _fin():
        for h in range(nh):
            sn_ref[h] = st_ref[h].T


def _hgrn(hp, row0, nb, t, tt, lb, norm_g, s0):
    ntb = t // tt
    rb = row0 // tt

    def col(cidx):
        return pl.BlockSpec((tt, D_C), lambda b, i: (rb + b * ntb + i, cidx // D_C))

    st_spec = pl.BlockSpec((None, C_HEADS, C_HEAD_DIM, C_V_DIM), lambda b, i: (b, 0, 0, 0))
    kern = functools.partial(_hgrn_kernel, tt=tt, blk=HGRN_BLOCK)
    return pl.pallas_call(
        kern,
        grid=(nb, ntb),
        in_specs=[col(COL_CQ), col(COL_CF), col(COL_CI), col(COL_CG),
                  pl.BlockSpec((1, D_C), lambda b, i: (0, 0)),
                  pl.BlockSpec((1, C_V_DIM), lambda b, i: (0, 0)),
                  st_spec],
        out_specs=[pl.BlockSpec((tt, D_C), lambda b, i: (b * ntb + i, 0)), st_spec],
        out_shape=[jax.ShapeDtypeStruct((nb * t, D_C), bf16),
                   jax.ShapeDtypeStruct((nb, C_HEADS, C_HEAD_DIM, C_V_DIM), f32)],
        scratch_shapes=[pltpu.VMEM((C_HEADS, C_V_DIM, C_HEAD_DIM), f32),
                        pltpu.VMEM((tt, D_C), f32), pltpu.VMEM((tt, D_C), f32),
                        pltpu.VMEM((tt, D_C), f32), pltpu.VMEM((tt, D_C), f32)],
        compiler_params=_cparams(("parallel", "arbitrary")),
        name="hgrn",
    )(hp, hp, hp, hp, lb.reshape(1, D_C), norm_g.reshape(1, C_V_DIM), s0)


def _merge_kernel(x_ref, g_ref, oa_ref, ob_ref, oc_ref, wga_ref, wgb_ref, wgc_ref,
                  pa_ref, pb_ref, pc_ref, wo_ref, o_ref, xn_ref, acc_ref):
    j = pl.program_id(1)

    @pl.when(j == 0)
    def _():
        xn_ref[...] = _rms(x_ref[...], g_ref[...]).astype(bf16)
        acc_ref[...] = jnp.zeros_like(acc_ref)

    xn = xn_ref[...]

    def branch(wg_ref, o_ref_, p_ref):
        gate = jax.nn.sigmoid(jnp.dot(xn, wg_ref[...], preferred_element_type=f32))
        return gate * jnp.dot(o_ref_[...], p_ref[...], preferred_element_type=f32)

    merged = branch(wga_ref, oa_ref, pa_ref) + branch(wgb_ref, ob_ref, pb_ref) + branch(wgc_ref, oc_ref, pc_ref)
    acc_ref[...] += jnp.dot(merged.astype(bf16), wo_ref[...], preferred_element_type=f32)

    @pl.when(j == pl.num_programs(1) - 1)
    def _():
        o_ref[...] = x_ref[...] + acc_ref[...]


def _merge(h, g, oa, ob, oc, wg, pa, pb, pc, wo, tm, tj):
    m, d = h.shape
    nj = d // tj
    row = lambda i, j: (i, 0)
    return pl.pallas_call(
        _merge_kernel,
        grid=(m // tm, nj),
        in_specs=[pl.BlockSpec((tm, d), row),
                  pl.BlockSpec((1, d), lambda i, j: (0, 0)),
                  pl.BlockSpec((tm, D_A), row),
                  pl.BlockSpec((tm, D_B), row),
                  pl.BlockSpec((tm, D_C), row),
                  pl.BlockSpec((d, tj), lambda i, j: (0, j)),
                  pl.BlockSpec((d, tj), lambda i, j: (0, nj + j)),
                  pl.BlockSpec((d, tj), lambda i, j: (0, 2 * nj + j)),
                  pl.BlockSpec((D_A, tj), lambda i, j: (0, j)),
                  pl.BlockSpec((D_B, tj), lambda i, j: (0, j)),
                  pl.BlockSpec((D_C, tj), lambda i, j: (0, j)),
                  pl.BlockSpec((tj, d), lambda i, j: (j, 0))],
        out_specs=pl.BlockSpec((tm, d), row),
        out_shape=jax.ShapeDtypeStruct((m, d), f32),
        scratch_shapes=[pltpu.VMEM((tm, d), bf16), pltpu.VMEM((tm, d), f32)],
        compiler_params=_cparams(("parallel", "arbitrary")),
        name="merge",
    )(h, g.reshape(1, d), oa, ob, oc, wg, wg, wg, pa, pb, pc, wo)


def _router_kernel(x_ref, g_ref, r_ref, o_ref):
    xn = _rms(x_ref[...], g_ref[...]).astype(bf16)
    logits = jnp.dot(xn, r_ref[...], preferred_element_type=f32)
    lane = lax.broadcasted_iota(jnp.int32, logits.shape, 1)
    logits = jnp.where(lane < N_EXPERTS, logits, -jnp.inf)
    m1 = jnp.max(logits, axis=-1, keepdims=True)
    i1 = jnp.min(jnp.where(logits == m1, lane, LANE), axis=-1, keepdims=True)
    rest = jnp.where(lane == i1, -jnp.inf, logits)
    m2 = jnp.max(rest, axis=-1, keepdims=True)
    i2 = jnp.min(jnp.where(rest == m2, lane, LANE), axis=-1, keepdims=True)
    e2 = jnp.exp(m2 - m1)
    g1 = 1.0 / (1.0 + e2)
    g2 = e2 / (1.0 + e2)
    route = jnp.where(lane == 0, i1.astype(f32), jnp.where(lane == 1, i2.astype(f32),
                      jnp.where(lane == 2, g1, jnp.where(lane == 3, g2, 0.0))))
    o_ref[...] = route[:, :N_EXPERTS]


def _router(h, g, router, tm):
    m, d = h.shape
    rp = jnp.pad(router, ((0, 0), (0, LANE - N_EXPERTS))).astype(bf16)
    return pl.pallas_call(
        _router_kernel,
        grid=(m // tm,),
        in_specs=[pl.BlockSpec((tm, d), lambda i: (i, 0)),
                  pl.BlockSpec((1, d), lambda i: (0, 0)),
                  pl.BlockSpec((d, LANE), lambda i: (0, 0))],
        out_specs=pl.BlockSpec((tm, N_EXPERTS), lambda i: (i, 0)),
        out_shape=jax.ShapeDtypeStruct((m, N_EXPERTS), f32),
        compiler_params=_cparams(("parallel",)),
        name="router",
    )(h, g.reshape(1, d), rp)


def _ffn_kernel(x_ref, g_ref, c_ref, wg_ref, wu_ref, wd_ref, o_ref, xn_ref, acc_ref, tot_ref):
    e = pl.program_id(1)
    fi = pl.program_id(2)
    ne = pl.num_programs(1)
    nf = pl.num_programs(2)

    @pl.when(jnp.logical_and(e == 0, fi == 0))
    def _():
        xn_ref[...] = _rms(x_ref[...], g_ref[...]).astype(bf16)
        tot_ref[...] = jnp.zeros_like(tot_ref)

    @pl.when(fi == 0)
    def _():
        acc_ref[...] = jnp.zeros_like(acc_ref)

    xn = xn_ref[...]
    a = jnp.dot(xn, wg_ref[...], preferred_element_type=f32)
    u = jnp.dot(xn, wu_ref[...], preferred_element_type=f32)
    act = (_silu(a) * u).astype(bf16)
    acc_ref[...] += jnp.dot(act, wd_ref[...], preferred_element_type=f32)

    @pl.when(fi == nf - 1)
    def _():
        comb = c_ref[...]
        lane = lax.broadcasted_iota(jnp.int32, comb.shape, 1)
        ce = jnp.sum(jnp.where(lane == e, comb, 0.0), axis=-1, keepdims=True)
        tot_ref[...] += ce * acc_ref[...]

    @pl.when(jnp.logical_and(e == ne - 1, fi == nf - 1))
    def _():
        o_ref[...] = x_ref[...] + tot_ref[...]


def _ffn(h, g, comb, wg, wu, wd, tm, tf):
    m, d = h.shape
    ne, _, ff = wg.shape
    return pl.pallas_call(
        _ffn_kernel,
        grid=(m // tm, ne, ff // tf),
        in_specs=[pl.BlockSpec((tm, d), lambda i, e, f: (i, 0)),
                  pl.BlockSpec((1, d), lambda i, e, f: (0, 0)),
                  pl.BlockSpec((tm, ne), lambda i, e, f: (i, 0)),
                  pl.BlockSpec((None, d, tf), lambda i, e, f: (e, 0, f)),
                  pl.BlockSpec((None, d, tf), lambda i, e, f: (e, 0, f)),
                  pl.BlockSpec((None, tf, d), lambda i, e, f: (e, f, 0))],
        out_specs=pl.BlockSpec((tm, d), lambda i, e, f: (i, 0)),
        out_shape=jax.ShapeDtypeStruct((m, d), f32),
        scratch_shapes=[pltpu.VMEM((tm, d), bf16), pltpu.VMEM((tm, d), f32), pltpu.VMEM((tm, d), f32)],
        compiler_params=_cparams(("parallel", "arbitrary", "arbitrary")),
        name="ffn",
    )(h, g.reshape(1, d), comb, wg, wu, wd)


MOE_ROWS = 1024
MOE_SUB = 256


def _moe_plan(route, m):
    n = 2 * m
    n_groups = -(-n // MOE_ROWS) + N_EXPERTS
    ef = route[:, 0:2].astype(jnp.int32).T.reshape(n)
    tok = jnp.tile(jnp.arange(m, dtype=jnp.int32), 2)
    onehot = (ef[:, None] == jnp.arange(N_EXPERTS, dtype=jnp.int32)[None, :]).astype(jnp.int32)
    counts = jnp.sum(onehot, axis=0)
    first = jnp.cumsum(counts) - counts
    rank = jnp.sum((jnp.cumsum(onehot, axis=0) - 1) * onehot, axis=1)
    slot = first[ef] + rank
    row_tok = jnp.zeros((n,), jnp.int32).at[slot].set(tok)
    row_dst = jnp.zeros((n,), jnp.int32).at[slot].set(jnp.arange(n, dtype=jnp.int32))
    per_e = (counts + MOE_ROWS - 1) // MOE_ROWS
    ends = jnp.cumsum(per_e)
    gi = jnp.arange(n_groups, dtype=jnp.int32)
    ge = jnp.minimum(jnp.sum((gi[:, None] >= ends[None, :]).astype(jnp.int32), axis=1), N_EXPERTS - 1)
    j = gi - (ends - per_e)[ge]
    g_start = first[ge] + j * MOE_ROWS
    g_cnt = jnp.clip(counts[ge] - j * MOE_ROWS, 0, MOE_ROWS)
    g_cnt = jnp.where(gi < ends[-1], g_cnt, 0)
    last_e = ge[jnp.maximum(ends[-1] - 1, 0)]
    ge = jnp.where(gi < ends[-1], ge, last_e)
    return ge, g_start.astype(jnp.int32), g_cnt.astype(jnp.int32), row_tok, row_dst


def _moe_kernel(ge_ref, gs_ref, gc_ref, tok_ref, dst_ref, h_hbm, g_ref, wg_ref, wu_ref, wd_ref, y_hbm,
                xs_ref, acc_ref, spare_ref, gsem, ssem):
    s = pl.program_id(0)
    fi = pl.program_id(1)
    nf = pl.num_programs(1)
    cnt = gc_ref[s]
    start = gs_ref[s]
    n_rows = tok_ref.shape[0]
    nsub = (cnt + MOE_SUB - 1) // MOE_SUB

    @pl.when(cnt > 0)
    def _active():
        @pl.when(fi == 0)
        def _gather():
            nrow = nsub * MOE_SUB

            def issue(r, c):
                tok = tok_ref[jnp.minimum(start + r, n_rows - 1)]
                pltpu.make_async_copy(h_hbm.at[pl.ds(tok, 1)], acc_ref.at[pl.ds(r, 1)], gsem).start()
                return c
            lax.fori_loop(0, nrow, issue, 0)

            def wait_tile(j, c):
                pltpu.make_async_copy(h_hbm.at[pl.ds(0, MOE_SUB)], acc_ref.at[pl.ds(0, MOE_SUB)], gsem).wait()
                return c
            lax.fori_loop(0, nsub, wait_tile, 0)

            def norm(j, c):
                rows = pl.ds(pl.multiple_of(j * MOE_SUB, MOE_SUB), MOE_SUB)
                xs_ref[rows, :] = _rms(acc_ref[rows, :], g_ref[...]).astype(bf16)
                acc_ref[rows, :] = jnp.zeros((MOE_SUB, acc_ref.shape[1]), f32)
                return c
            lax.fori_loop(0, nsub, norm, 0)

        wg = wg_ref[...].astype(bf16)
        wu = wu_ref[...].astype(bf16)
        wd = wd_ref[...].astype(bf16)

        def body(j, c):
            rows = pl.ds(pl.multiple_of(j * MOE_SUB, MOE_SUB), MOE_SUB)
            x = xs_ref[rows, :]
            a = jnp.dot(x, wg, preferred_element_type=f32)
            u = jnp.dot(x, wu, preferred_element_type=f32)
            act = (_silu(a) * u).astype(bf16)
            acc_ref[rows, :] += jnp.dot(act, wd, preferred_element_type=f32)
            return c
        lax.fori_loop(0, nsub, body, 0)

        @pl.when(fi == nf - 1)
        def _scatter():
            def issue(r, c):
                pltpu.make_async_copy(acc_ref.at[pl.ds(r, 1)], y_hbm.at[pl.ds(dst_ref[start + r], 1)], ssem).start()
                return c
            lax.fori_loop(0, cnt, issue, 0)

            def issue_spare(r, c):
                pltpu.make_async_copy(acc_ref.at[pl.ds(r, 1)], spare_ref.at[pl.ds(r - cnt, 1)], ssem).start()
                return c
            lax.fori_loop(cnt, nsub * MOE_SUB, issue_spare, 0)

            def wait_tile(j, c):
                pltpu.make_async_copy(acc_ref.at[pl.ds(0, MOE_SUB)], y_hbm.at[pl.ds(0, MOE_SUB)], ssem).wait()
                return c
            lax.fori_loop(0, nsub, wait_tile, 0)


def _moe(h, g, route, wg, wu, wd, tf):
    m, d = h.shape
    ne, _, ff = wg.shape
    nf = ff // tf
    ge, gs, gc, row_tok, row_dst = _moe_plan(route, m)
    n_groups = ge.shape[0]

    def wcol(s, f, ge_r, gs_r, gc_r, tok_r, dst_r):
        return (ge_r[s], 0, jnp.where(gc_r[s] > 0, f, nf - 1))

    def wrow(s, f, ge_r, gs_r, gc_r, tok_r, dst_r):
        return (ge_r[s], jnp.where(gc_r[s] > 0, f, nf - 1), 0)

    grid_spec = pltpu.PrefetchScalarGridSpec(
        num_scalar_prefetch=5,
        grid=(n_groups, nf),
        in_specs=[pl.BlockSpec(memory_space=pl.ANY),
                  pl.BlockSpec((1, d), lambda s, f, *_: (0, 0)),
                  pl.BlockSpec((None, d, tf), wcol),
                  pl.BlockSpec((None, d, tf), wcol),
                  pl.BlockSpec((None, tf, d), wrow)],
        out_specs=pl.BlockSpec(memory_space=pl.ANY),
        scratch_shapes=[pltpu.VMEM((MOE_ROWS, d), bf16), pltpu.VMEM((MOE_ROWS, d), f32),
                        pltpu.VMEM((MOE_SUB, d), f32),
                        pltpu.SemaphoreType.DMA(()), pltpu.SemaphoreType.DMA(())],
    )
    return pl.pallas_call(
        _moe_kernel,
        grid_spec=grid_spec,
        out_shape=jax.ShapeDtypeStruct((2 * m, d), f32),
        compiler_params=_cparams(("arbitrary", "arbitrary")),
        name="moe",
    )(ge, gs, gc, row_tok, row_dst, h, g.reshape(1, d), wg, wu, wd)


def _combine_kernel(x_ref, r_ref, y1_ref, y2_ref, o_ref):
    r = r_ref[...]
    o_ref[...] = x_ref[...] + (r[:, 2:3] * y1_ref[...] + r[:, 3:4] * y2_ref[...])


def _combine(h, route, y, tm):
    m, d = h.shape
    nb = m // tm
    return pl.pallas_call(
        _combine_kernel,
        grid=(nb,),
        in_specs=[pl.BlockSpec((tm, d), lambda i: (i, 0)),
                  pl.BlockSpec((tm, N_EXPERTS), lambda i: (i, 0)),
                  pl.BlockSpec((tm, d), lambda i: (i, 0)),
                  pl.BlockSpec((tm, d), lambda i: (nb + i, 0))],
        out_specs=pl.BlockSpec((tm, d), lambda i: (i, 0)),
        out_shape=jax.ShapeDtypeStruct((m, d), f32),
        compiler_params=_cparams(("parallel",)),
        name="moe_combine",
    )(h, route, y, y)


def _final_norm_kernel(x_ref, g_ref, o_ref):
    o_ref[...] = _rms(x_ref[...], g_ref[...])


def _final_norm(h, g, row0, rows, tm):
    d = h.shape[1]
    rb = row0 // tm
    return pl.pallas_call(
        _final_norm_kernel,
        grid=(rows // tm,),
        in_specs=[pl.BlockSpec((tm, d), lambda i: (rb + i, 0)),
                  pl.BlockSpec((1, d), lambda i: (0, 0))],
        out_specs=pl.BlockSpec((tm, d), lambda i: (i, 0)),
        out_shape=jax.ShapeDtypeStruct((rows, d), f32),
        compiler_params=_cparams(("parallel",)),
        name="final_norm",
    )(h, g.reshape(1, d))


def _pack_in_proj(w):
    o = np.cumsum((0,) + IN_SIZES)
    pad = lambda a: jnp.pad(a, ((0, 0), (0, LANE - a.shape[1])))
    packed = jnp.concatenate([
        w[:, o[0]:o[3]],
        w[:, o[3]:o[4]],
        w[:, o[6]:o[7]],
        w[:, o[7]:o[11]],
        pad(w[:, o[4]:o[5]]),
        pad(w[:, o[5]:o[6]]),
    ], axis=1).astype(bf16)
    return packed, w[:, o[11]:].astype(bf16)


def kernel(x_prompt, x_sample, cache_k, cache_v, cache_kidx, cache_pool, state_hgrn, norm_mix_g, norm_ffn_g, final_norm_g, w_in, w_proj_a, w_proj_b, w_proj_c, w_out, pool_w, pool_scale, hgrn_lb_logits, hgrn_norm_g, ffn_w_gate, ffn_w_up, ffn_w_down, moe_router, moe_w_gate, moe_w_up, moe_w_down):
    batch, seq, d = x_prompt.shape
    nb, nq, _ = x_sample.shape
    depth = w_in.shape[0]
    past = cache_k.shape[2]
    mp = batch * seq
    ms = nb * nq
    m = mp + ms

    lb_soft = jax.nn.softmax(hgrn_lb_logits.astype(f32), axis=0)
    lb_all = jnp.cumsum(lb_soft, axis=0) - lb_soft[0:1]

    h = jnp.concatenate([x_prompt.reshape(mp, d), x_sample.reshape(ms, d)], axis=0)
    s0_prompt = jnp.zeros((batch, C_HEADS, C_HEAD_DIM, C_V_DIM), f32)
    tm = 768 if m % 768 == 0 else 256
    tmh = tm // 2

    outs = [[] for _ in range(10)]
    for l in range(depth):
        wp, wgates = _pack_in_proj(w_in[l])
        hp = _inproj(h, norm_mix_g[l], wp, tm, 640)

        kv = hp[:mp, COL_K:COL_K + 2 * D_A].astype(bf16)
        oa_p = _dsa_prompt(hp, kv, batch, seq, 256)
        oa_s = _dsa_sample(hp, mp, nb, nq, cache_k[l], cache_v[l], cache_kidx[l])
        ob_p = _pool(hp, 0, batch, seq, 256, None, pool_w[l], pool_scale[l], 0)
        prev = jnp.pad(cache_pool[l], ((0, 0), (1, 0), (0, 0)))
        ob_s = _pool(hp, mp, nb, nq, nq, prev, pool_w[l], pool_scale[l], past)
        oc_p, st_p = _hgrn(hp, 0, batch, seq, 256, lb_all[l], hgrn_norm_g[l], s0_prompt)
        oc_s, st_s = _hgrn(hp, mp, nb, nq, nq, lb_all[l], hgrn_norm_g[l], state_hgrn[l])

        oa = jnp.concatenate([oa_p, oa_s], axis=0)
        ob = jnp.concatenate([ob_p, ob_s], axis=0)
        oc = jnp.concatenate([oc_p, oc_s], axis=0)
        h = _merge(h, norm_mix_g[l], oa, ob, oc, wgates, w_proj_a[l].astype(bf16), w_proj_b[l].astype(bf16),
                   w_proj_c[l].astype(bf16), w_out[l].astype(bf16), tmh, 512)

        kk = hp[:, COL_K:COL_K + D_A]
        vv = hp[:, COL_V:COL_V + D_A]
        ki = hp[:, COL_KI:COL_KI + IDX_DIM]
        uu = hp[:, COL_U:COL_U + D_B]
        outs[0].append(kk[:mp].reshape(batch, seq, A_HEADS, A_HEAD_DIM))
        outs[1].append(vv[:mp].reshape(batch, seq, A_HEADS, A_HEAD_DIM))
        outs[2].append(ki[:mp].reshape(batch, seq, IDX_DIM))
        outs[3].append(uu[:mp].reshape(batch, seq, D_B)[:, -POOL_STATE:])
        outs[4].append(st_p)
        outs[5].append(kk[mp:].reshape(nb, nq, A_HEADS, A_HEAD_DIM))
        outs[6].append(vv[mp:].reshape(nb, nq, A_HEADS, A_HEAD_DIM))
        outs[7].append(ki[mp:].reshape(nb, nq, IDX_DIM))
        u_ext = jnp.concatenate([cache_pool[l], uu[mp:].reshape(nb, nq, D_B)], axis=1)
        outs[8].append(u_ext[:, -POOL_STATE:])
        outs[9].append(st_s)

        j = l // 2
        if l % 2 == 0:
            comb = jnp.ones((m, 1), f32)
            h = _ffn(h, norm_ffn_g[l], comb, ffn_w_gate[j][None].astype(bf16), ffn_w_up[j][None].astype(bf16),
                     ffn_w_down[j][None].astype(bf16), tmh, 512)
        else:
            route = _router(h, norm_ffn_g[l], moe_router[j], tm)
            y = _moe(h, norm_ffn_g[l], route, moe_w_gate[j], moe_w_up[j], moe_w_down[j], 512)
            h = _combine(h, route, y, tmh)

    y_prompt = _final_norm(h, final_norm_g, 0, mp, 256).reshape(batch, seq, d)
    y_sample = _final_norm(h, final_norm_g, mp, ms, 256).reshape(nb, nq, d)
    return (y_prompt, y_sample) + tuple(jnp.stack(o) for o in outs)
```

```python
import functools

import numpy as np
import jax
import jax.numpy as jnp
from jax import lax
from jax.experimental import pallas as pl
from jax.experimental.pallas import tpu as pltpu

f32 = jnp.float32
bf16 = jnp.bfloat16

D_MODEL = 2048
CHUNK = 64
A_HEADS = 8
A_HEAD_DIM = 128
D_A = A_HEADS * A_HEAD_DIM
IDX_HEADS = 8
IDX_DIM = 64
TOPK_MAX = 256
POOL_WINDOWS = (2, 4, 8, 16)
POOL_GROUP_DIM = 128
D_B = len(POOL_WINDOWS) * POOL_GROUP_DIM
POOL_STATE = max(POOL_WINDOWS) - 1
C_HEADS = 4
C_HEAD_DIM = 128
C_V_DIM = 128
D_C = C_HEADS * C_HEAD_DIM
HGRN_BLOCK = 16
N_BRANCH = 3
N_EXPERTS = 8
IN_SIZES = (D_A, D_A, D_A, IDX_HEADS * IDX_DIM, IDX_DIM, IDX_HEADS,
            D_B, D_C, D_C, C_HEADS * C_V_DIM, C_HEADS * C_V_DIM, N_BRANCH * D_MODEL)
EPS = 1e-6
NEG = -1e30
LB_FLOOR = 1e-30
INT_MIN = -2 ** 31

LOG2E = 1.4426950408889634
LANE = 128
COUNT_ROWS = 64
COL_Q, COL_K, COL_V = 0, D_A, 2 * D_A
COL_QI = 3 * D_A
COL_U = COL_QI + 512
COL_CQ = COL_U + 512
COL_CF = COL_CQ + 512
COL_CI = COL_CF + 512
COL_CG = COL_CI + 512
COL_KI = COL_CG + 512
COL_W = COL_KI + LANE
N_PACK = COL_W + LANE

VMEM_LIMIT = 56 * 1024 * 1024

NT_DIMS = (((1,), (1,)), ((), ()))
TN_DIMS = (((0,), (0,)), ((), ()))


def _cparams(sem):
    return pltpu.CompilerParams(dimension_semantics=sem, vmem_limit_bytes=VMEM_LIMIT)


def _rms(x, g):
    ms = jnp.mean(x * x, axis=-1, keepdims=True)
    return x * lax.rsqrt(ms + EPS) * g


def _silu(x):
    return x * jax.nn.sigmoid(x)


def _sort_key(x):
    bits = pltpu.bitcast(x, jnp.int32)
    return bits ^ ((bits >> 31) & jnp.int32(0x7FFFFFFF))


def _kth_largest(count_ge, kk):
    zero = jnp.zeros(kk.shape, jnp.int32)
    t0 = jnp.where(count_ge(zero) >= kk, zero, jnp.int32(INT_MIN))

    def bit_body(i, t):
        c = t | lax.shift_left(jnp.int32(1), 30 - i)
        return jnp.where(count_ge(c) >= kk, c, t)

    return lax.fori_loop(0, 31, bit_body, t0)


def _inproj_kernel(x_ref, g_ref, w_ref, o_ref, xn_ref):
    @pl.when(pl.program_id(1) == 0)
    def _():
        xn_ref[...] = _rms(x_ref[...], g_ref[...]).astype(bf16)

    o_ref[...] = jnp.dot(xn_ref[...], w_ref[...], preferred_element_type=f32)


def _inproj(h, g, wp, tm, tn):
    m, d = h.shape
    n = wp.shape[1]
    return pl.pallas_call(
        _inproj_kernel,
        grid=(m // tm, n // tn),
        in_specs=[pl.BlockSpec((tm, d), lambda i, j: (i, 0)),
                  pl.BlockSpec((1, d), lambda i, j: (0, 0)),
                  pl.BlockSpec((d, tn), lambda i, j: (0, j))],
        out_specs=pl.BlockSpec((tm, tn), lambda i, j: (i, j)),
        out_shape=jax.ShapeDtypeStruct((m, n), f32),
        scratch_shapes=[pltpu.VMEM((tm, d), bf16)],
        compiler_params=_cparams(("parallel", "arbitrary")),
        name="inproj",
    )(h, g.reshape(1, d), wp)


def _idx_scores(qi, w, kib):
    acc = jnp.zeros((qi.shape[0], kib.shape[0]), f32)
    for hh in range(IDX_HEADS):
        d = lax.dot_general(qi[:, hh * IDX_DIM:(hh + 1) * IDX_DIM], kib, NT_DIMS,
                            preferred_element_type=f32)
        acc = acc + w[:, hh:hh + 1] * jnp.maximum(d, 0.0)
    return acc


def _fold_lanes(x):
    part = x[:, :LANE]
    for j in range(1, x.shape[1] // LANE):
        part = part + x[:, j * LANE:(j + 1) * LANE]
    return part


def _tri_incl(n):
    r = lax.broadcasted_iota(jnp.int32, (n, n), 0)
    c = lax.broadcasted_iota(jnp.int32, (n, n), 1)
    return jnp.where(r <= c, 1.0, 0.0).astype(bf16)


def _dsa_prompt_kernel(q_ref, k_ref, vt_ref, qi_ref, ki_ref, w_ref, o_ref, keys_ref, bias_ref, qs_ref,
                       *acc_refs, tq, topk):
    qb = pl.program_id(1)
    nkb = qb + 1
    hd = A_HEAD_DIM

    qi = qi_ref[...].astype(bf16)
    wt = (w_ref[...] * (IDX_HEADS ** -0.5 * IDX_DIM ** -0.5)).T
    srow = lax.broadcasted_iota(jnp.int32, (tq, tq), 0)
    qcol = lax.broadcasted_iota(jnp.int32, (tq, tq), 1)
    diag_adm = (srow // CHUNK) <= (qcol // CHUNK)

    def score_body(kb, c):
        off = pl.multiple_of(kb * tq, tq)
        kib = ki_ref[pl.ds(off, tq), :][:, :IDX_DIM].astype(bf16)
        acc = jnp.zeros((tq, tq), f32)
        for hh in range(IDX_HEADS):
            d = lax.dot_general(kib, qi[:, hh * IDX_DIM:(hh + 1) * IDX_DIM], NT_DIMS,
                                preferred_element_type=f32)
            acc = acc + wt[hh:hh + 1, :] * jnp.maximum(d, 0.0)
        adm = jnp.logical_or(kb < qb, diag_adm)
        keys_ref[kb] = jnp.where(adm, _sort_key(acc), jnp.int32(INT_MIN))
        return c

    lax.fori_loop(0, nkb, score_body, 0)

    def count(cmp, c):
        cb = jnp.broadcast_to(c, (COUNT_ROWS, tq))

        def body(kb, acc):
            for r0 in range(0, tq, COUNT_ROWS):
                acc = acc + jnp.where(cmp(keys_ref[kb, r0:r0 + COUNT_ROWS, :], cb), 1.0, 0.0)
            return acc
        acc = lax.fori_loop(0, nkb, body, jnp.zeros((COUNT_ROWS, tq), f32))
        return jnp.sum(acc, axis=0, keepdims=True)

    lane = lax.broadcasted_iota(jnp.int32, (1, tq), 1)
    n_adm = ((qb * tq + lane) // CHUNK + 1) * CHUNK
    kk = jnp.minimum(topk, n_adm).astype(f32)
    thr = _kth_largest(lambda c: count(lambda k, t: k >= t, c), kk)
    n_ge = count(lambda k, t: k >= t, thr)
    has_tie = jnp.max(n_ge - kk) > 0.0

    @pl.when(jnp.logical_not(has_tie))
    def _():
        def body(kb, c):
            bias_ref[kb] = jnp.where(keys_ref[kb] >= thr, 0.0, NEG)
            return c
        lax.fori_loop(0, nkb, body, 0)

    @pl.when(has_tie)
    def _():
        need = kk - count(lambda k, t: k > t, thr)
        tri = jnp.where(qcol <= srow, 1.0, 0.0).astype(bf16)

        def body(kb, seen):
            k = keys_ref[kb]
            eq = jnp.where(k == thr, 1.0, 0.0)
            rank = seen + jnp.dot(tri, eq.astype(bf16), preferred_element_type=f32)
            take = jnp.logical_or(k > thr, jnp.logical_and(k == thr, rank <= need))
            bias_ref[kb] = jnp.where(take, 0.0, NEG)
            return seen + jnp.sum(eq, axis=0, keepdims=True)
        lax.fori_loop(0, nkb, body, jnp.zeros((1, tq), f32))

    qs_ref[...] = (q_ref[...] * (hd ** -0.5 * LOG2E)).astype(bf16)
    for h in range(A_HEADS):
        acc_refs[h][...] = jnp.zeros((hd, tq), f32)

    def att_body(kb, carry):
        ms, ls = carry
        off = pl.multiple_of(kb * tq, tq)
        kblk = k_ref[pl.ds(off, tq), :]
        bias = bias_ref[kb]
        sts = []
        for h in range(A_HEADS):
            cols = slice(h * hd, (h + 1) * hd)
            sts.append(lax.dot_general(kblk[:, cols], qs_ref[:, cols], NT_DIMS, preferred_element_type=f32))
        new_ms, new_ls, ps, corrs = [], [], [], []
        for h in range(A_HEADS):
            st = sts[h] + bias
            m_new = jnp.maximum(ms[h], jnp.max(st, axis=0, keepdims=True))
            p = jnp.exp2(st - m_new)
            corr = jnp.exp2(ms[h] - m_new)
            new_ls.append(corr * ls[h] + jnp.sum(p, axis=0, keepdims=True))
            new_ms.append(m_new)
            ps.append(p.astype(bf16))
            corrs.append(corr)
        for h in range(A_HEADS):
            acc_refs[h][...] = corrs[h] * acc_refs[h][...] + jnp.dot(vt_ref[kb, h], ps[h],
                                                                     preferred_element_type=f32)
        return tuple(new_ms), tuple(new_ls)

    init = (tuple(jnp.full((1, tq), NEG, f32) for _ in range(A_HEADS)),
            tuple(jnp.zeros((1, tq), f32) for _ in range(A_HEADS)))
    _, ls = lax.fori_loop(0, nkb, att_body, init)
    for h in range(A_HEADS):
        o_ref[:, h * hd:(h + 1) * hd] = (acc_refs[h][...] / ls[h]).T.astype(o_ref.dtype)


def _dsa_prompt(hp, kb16, vt16, batch, seq, tq):
    nq = seq // tq
    topk = min(TOPK_MAX, seq // 4)
    kern = functools.partial(_dsa_prompt_kernel, tq=tq, topk=topk)
    once = pl.Buffered(1)
    return pl.pallas_call(
        kern,
        grid=(batch, nq),
        in_specs=[
            pl.BlockSpec((tq, D_A), lambda b, i: (b * nq + i, COL_Q // D_A)),
            pl.BlockSpec((seq, D_A), lambda b, i: (b, 0), pipeline_mode=once),
            pl.BlockSpec((None, nq, A_HEADS, A_HEAD_DIM, tq), lambda b, i: (b, 0, 0, 0, 0), pipeline_mode=once),
            pl.BlockSpec((tq, 512), lambda b, i: (b * nq + i, COL_QI // 512)),
            pl.BlockSpec((seq, LANE), lambda b, i: (b, COL_KI // LANE), pipeline_mode=once),
            pl.BlockSpec((tq, LANE), lambda b, i: (b * nq + i, COL_W // LANE)),
        ],
        out_specs=pl.BlockSpec((tq, D_A), lambda b, i: (b * nq + i, 0)),
        out_shape=jax.ShapeDtypeStruct((batch * seq, D_A), bf16),
        scratch_shapes=[pltpu.VMEM((nq, tq, tq), jnp.int32), pltpu.VMEM((nq, tq, tq), f32),
                        pltpu.VMEM((tq, D_A), bf16)]
                       + [pltpu.VMEM((A_HEAD_DIM, tq), f32) for _ in range(A_HEADS)],
        compiler_params=_cparams(("parallel", "arbitrary")),
        name="dsa_prompt",
    )(hp, kb16, vt16, hp, hp, hp)


def _dsa_sample_kernel(q_ref, kn_ref, vn_ref, qi_ref, kin_ref, w_ref, kc_ref, vc_ref, kic_ref, o_ref,
                       biasc_ref, biasn_ref, *, topk):
    nq = q_ref.shape[0]
    past = kc_ref.shape[0]

    def _select():
        qi = qi_ref[...].astype(bf16)
        w = w_ref[...][:, :IDX_HEADS] * (IDX_HEADS ** -0.5 * IDX_DIM ** -0.5)
        key_c = _sort_key(_idx_scores(qi, w, kic_ref[...].astype(bf16)))
        key_n = _sort_key(_idx_scores(qi, w, kin_ref[...][:, :IDX_DIM].astype(bf16)))

        def count(pred):
            cc = jnp.sum(_fold_lanes(jnp.where(pred(key_c), 1.0, 0.0)), axis=-1, keepdims=True)
            return cc + jnp.sum(jnp.where(pred(key_n), 1.0, 0.0), axis=-1, keepdims=True)

        kk = jnp.full((nq, 1), float(topk), f32)
        thr = _kth_largest(lambda c: count(lambda k: k >= c), kk)
        n_ge = count(lambda k: k >= thr)
        has_tie = jnp.max(n_ge - kk) > 0.0

        @pl.when(jnp.logical_not(has_tie))
        def _():
            biasc_ref[...] = jnp.where(key_c >= thr, 0.0, NEG)
            biasn_ref[...] = jnp.where(key_n >= thr, 0.0, NEG)

        @pl.when(has_tie)
        def _():
            need = kk - count(lambda k: k > thr)
            tri = _tri_incl(LANE)
            seen = jnp.zeros((nq, 1), f32)
            for j in range(past // LANE):
                k = key_c[:, j * LANE:(j + 1) * LANE]
                eq = jnp.where(k == thr, 1.0, 0.0)
                rank = seen + jnp.dot(eq.astype(bf16), tri, preferred_element_type=f32)
                take = jnp.logical_or(k > thr, jnp.logical_and(k == thr, rank <= need))
                biasc_ref[:, j * LANE:(j + 1) * LANE] = jnp.where(take, 0.0, NEG)
                seen = seen + jnp.sum(eq, axis=-1, keepdims=True)
            eq = jnp.where(key_n == thr, 1.0, 0.0)
            rank = seen + jnp.dot(eq.astype(bf16), _tri_incl(nq), preferred_element_type=f32)
            take = jnp.logical_or(key_n > thr, jnp.logical_and(key_n == thr, rank <= need))
            biasn_ref[...] = jnp.where(take, 0.0, NEG)

    _select()

    hd = A_HEAD_DIM
    q = (q_ref[...] * (hd ** -0.5 * LOG2E)).astype(bf16)
    bias_c = biasc_ref[...]
    bias_n = biasn_ref[...]
    scs, sns = [], []
    for h in range(A_HEADS):
        cols = slice(h * hd, (h + 1) * hd)
        scs.append(lax.dot_general(q[:, cols], kc_ref[:, cols].astype(bf16), NT_DIMS, preferred_element_type=f32))
        sns.append(lax.dot_general(q[:, cols], kn_ref[:, cols].astype(bf16), NT_DIMS, preferred_element_type=f32))
    pcs, pns, ls = [], [], []
    for h in range(A_HEADS):
        s_c = scs[h] + bias_c
        s_n = sns[h] + bias_n
        m = jnp.maximum(jnp.max(s_c, axis=-1, keepdims=True), jnp.max(s_n, axis=-1, keepdims=True))
        p_c = jnp.exp2(s_c - m)
        p_n = jnp.exp2(s_n - m)
        ls.append(jnp.sum(p_c, axis=-1, keepdims=True) + jnp.sum(p_n, axis=-1, keepdims=True))
        pcs.append(p_c.astype(bf16))
        pns.append(p_n.astype(bf16))
    for h in range(A_HEADS):
        cols = slice(h * hd, (h + 1) * hd)
        acc = jnp.dot(pcs[h], vc_ref[:, cols].astype(bf16), preferred_element_type=f32)
        acc = acc + jnp.dot(pns[h], vn_ref[:, cols].astype(bf16), preferred_element_type=f32)
        o_ref[:, cols] = (acc / ls[h]).astype(o_ref.dtype)


def _dsa_sample(hp, row0, nb, nq, cache_k, cache_v, cache_kidx):
    past = cache_k.shape[1]
    topk = min(TOPK_MAX, (past + nq) // 4)
    kc = cache_k.reshape(nb, past, D_A)
    vc = cache_v.reshape(nb, past, D_A)
    rb = row0 // nq
    kern = functools.partial(_dsa_sample_kernel, topk=topk)
    return pl.pallas_call(
        kern,
        grid=(nb,),
        in_specs=[
            pl.BlockSpec((nq, D_A), lambda b: (rb + b, COL_Q // D_A)),
            pl.BlockSpec((nq, D_A), lambda b: (rb + b, COL_K // D_A)),
            pl.BlockSpec((nq, D_A), lambda b: (rb + b, COL_V // D_A)),
            pl.BlockSpec((nq, 512), lambda b: (rb + b, COL_QI // 512)),
            pl.BlockSpec((nq, LANE), lambda b: (rb + b, COL_KI // LANE)),
            pl.BlockSpec((nq, LANE), lambda b: (rb + b, COL_W // LANE)),
            pl.BlockSpec((None, past, D_A), lambda b: (b, 0, 0)),
            pl.BlockSpec((None, past, D_A), lambda b: (b, 0, 0)),
            pl.BlockSpec((None, past, IDX_DIM), lambda b: (b, 0, 0)),
        ],
        out_specs=pl.BlockSpec((nq, D_A), lambda b: (b, 0)),
        out_shape=jax.ShapeDtypeStruct((nb * nq, D_A), bf16),
        scratch_shapes=[pltpu.VMEM((nq, past), f32), pltpu.VMEM((nq, nq), f32)],
        compiler_params=_cparams(("parallel",)),
        name="dsa_sample",
    )(hp, hp, hp, hp, hp, hp, kc, vc, cache_kidx)


def _pool_kernel(prev_ref, u_ref, pw_ref, ps_ref, o_ref, *, tt, first_pos, zero_first):
    tb = pl.program_id(1)
    halo = POOL_STATE + 1
    prev = prev_ref[...]
    if zero_first:
        prev = jnp.where(tb == 0, 0.0, prev)
    cur = u_ref[...]
    ext = jnp.concatenate([prev, cur], axis=0)
    pos = first_pos + tb * tt + lax.broadcasted_iota(jnp.int32, (tt, 1), 0)
    outs = []
    for gi, wdw in enumerate(POOL_WINDOWS):
        cols = slice(gi * POOL_GROUP_DIM, (gi + 1) * POOL_GROUP_DIM)
        a = ext[:, cols]
        n = 1
        while n < wdw:
            a = a[n:] + a[:-n]
            n *= 2
        s = a[halo + 1 - wdw: halo + 1 - wdw + tt]
        cnt = jnp.minimum(pos + 1, wdw).astype(f32)
        d = s / cnt - cur[:, cols]
        outs.append(jnp.dot(d.astype(bf16), pw_ref[gi].astype(bf16), preferred_element_type=f32))
    o = jnp.concatenate(outs, axis=-1) * ps_ref[...]
    o_ref[...] = o.astype(o_ref.dtype)


def _pool(hp, row0, nb, t, tt, prev, pool_w, pool_scale, first_pos):
    ntb = t // tt
    halo = POOL_STATE + 1
    rb = row0 // tt
    if prev is None:
        prev_arr = hp
        per = tt // halo
        prev_spec = pl.BlockSpec(
            (halo, D_B), lambda b, i: (jnp.maximum((row0 // halo) + (b * ntb + i) * per - 1, 0), COL_U // D_B))
    else:
        prev_arr = prev
        prev_spec = pl.BlockSpec((None, halo, D_B), lambda b, i: (b, 0, 0))
    kern = functools.partial(_pool_kernel, tt=tt, first_pos=first_pos, zero_first=prev is None)
    return pl.pallas_call(
        kern,
        grid=(nb, ntb),
        in_specs=[prev_spec,
                  pl.BlockSpec((tt, D_B), lambda b, i: (rb + b * ntb + i, COL_U // D_B)),
                  pl.BlockSpec((len(POOL_WINDOWS), POOL_GROUP_DIM, POOL_GROUP_DIM), lambda b, i: (0, 0, 0)),
                  pl.BlockSpec((1, D_B), lambda b, i: (0, 0))],
        out_specs=pl.BlockSpec((tt, D_B), lambda b, i: (b * ntb + i, 0)),
        out_shape=jax.ShapeDtypeStruct((nb * t, D_B), bf16),
        compiler_params=_cparams(("parallel", "arbitrary")),
        name="pool",
    )(prev_arr, hp, pool_w, pool_scale.reshape(1, D_B))


SAFE_LOG_RANGE = 80.0


def _hgrn_kernel(cq_ref, cf_ref, ci_ref, cg_ref, lb_ref, ng_ref, s0_ref, o_ref, sn_ref,
                 st_ref, b_ref, q_ref, k_ref, osc_ref, *, tt, blk):
    tb = pl.program_id(1)
    nh = C_HEADS
    hd = C_HEAD_DIM

    @pl.when(tb == 0)
    def _init():
        for h in range(nh):
            st_ref[h] = s0_ref[h].T

    lb = lb_ref[...]
    f = jnp.maximum(lb, LB_FLOOR) + (1.0 - lb) * jax.nn.sigmoid(cf_ref[...])
    g = jnp.log(f)
    k_ref[...] = 1.0 - f
    q_ref[...] = _silu(cq_ref[...])
    r = lax.broadcasted_iota(jnp.int32, (tt, tt), 0)
    c = lax.broadcasted_iota(jnp.int32, (tt, tt), 1)
    tri = jnp.where(jnp.logical_and(r // blk == c // blk, c <= r), 1.0, 0.0).astype(bf16)
    g1 = g.astype(bf16)
    e1 = g - g1.astype(f32)
    g2 = e1.astype(bf16)
    g3 = (e1 - g2.astype(f32)).astype(bf16)
    b = (jnp.dot(tri, g1, preferred_element_type=f32) + jnp.dot(tri, g2, preferred_element_type=f32)
         + jnp.dot(tri, g3, preferred_element_type=f32))
    b_ref[...] = b
    safe = jnp.min(b) > -SAFE_LOG_RANGE

    tr = lax.broadcasted_iota(jnp.int32, (blk, blk), 0)
    tc = lax.broadcasted_iota(jnp.int32, (blk, blk), 1)
    causal = tc <= tr
    trow = lax.broadcasted_iota(jnp.int32, (blk, 1), 0)

    def chunk(ci, factored):
        r0 = pl.multiple_of(ci * blk, blk)
        for h in range(nh):
            rows = pl.ds(r0, blk)
            cols = slice(h * hd, (h + 1) * hd)
            bb = b_ref[rows, cols]
            qh = q_ref[rows, cols]
            kh = k_ref[rows, cols]
            ih = ci_ref[rows, cols]
            bl = bb[blk - 1:blk, :]
            ke = (kh * jnp.exp(bl - bb)).astype(bf16)
            ib = ih.astype(bf16)
            st = st_ref[h]
            o = lax.dot_general((qh * jnp.exp(bb)).astype(bf16), st.astype(bf16), NT_DIMS,
                                preferred_element_type=f32)
            if factored:
                qe = (qh * jnp.exp(bb - bl)).astype(bf16)
                a = lax.dot_general(qe, ke, NT_DIMS, preferred_element_type=f32)
                a = jnp.where(causal, a, 0.0)
                o = o + jnp.dot(a.astype(bf16), ib, preferred_element_type=f32)
            else:
                for s in range(blk):
                    live = trow >= s
                    dec = jnp.exp(jnp.where(live, bb - bb[s:s + 1, :], 0.0))
                    a_col = jnp.sum(qh * kh[s:s + 1, :] * dec, axis=-1, keepdims=True)
                    o = o + jnp.where(live, a_col, 0.0) * ih[s:s + 1, :]
            osc_ref[rows, cols] = o
            st_ref[h] = st * jnp.exp(bl) + lax.dot_general(ib, ke, TN_DIMS, preferred_element_type=f32)

    @pl.when(safe)
    def _():
        def body(ci, c):
            chunk(ci, True)
            return c
        lax.fori_loop(0, tt // blk, body, 0)

    @pl.when(jnp.logical_not(safe))
    def _():
        def body(ci, c):
            chunk(ci, False)
            return c
        lax.fori_loop(0, tt // blk, body, 0)

    outs = []
    for h in range(nh):
        cols = slice(h * hd, (h + 1) * hd)
        outs.append(_rms(osc_ref[:, cols], ng_ref[...]))
    o_ref[...] = (jnp.concatenate(outs, axis=-1) * _silu(cg_ref[...])).astype(o_ref.dtype)

    @pl.when(tb == pl.num_programs(1) - 1)
    def _fin():
        for h in range(nh):
            sn_ref[h] = st_ref[h].T


def _hgrn(hp, row0, nb, t, tt, lb, norm_g, s0):
    ntb = t // tt
    rb = row0 // tt

    def col(cidx):
        return pl.BlockSpec((tt, D_C), lambda b, i: (rb + b * ntb + i, cidx // D_C))

    st_spec = pl.BlockSpec((None, C_HEADS, C_HEAD_DIM, C_V_DIM), lambda b, i: (b, 0, 0, 0))
    kern = functools.partial(_hgrn_kernel, tt=tt, blk=HGRN_BLOCK)
    return pl.pallas_call(
        kern,
        grid=(nb, ntb),
        in_specs=[col(COL_CQ), col(COL_CF), col(COL_CI), col(COL_CG),
                  pl.BlockSpec((1, D_C), lambda b, i: (0, 0)),
                  pl.BlockSpec((1, C_V_DIM), lambda b, i: (0, 0)),
                  st_spec],
        out_specs=[pl.BlockSpec((tt, D_C), lambda b, i: (b * ntb + i, 0)), st_spec],
        out_shape=[jax.ShapeDtypeStruct((nb * t, D_C), bf16),
                   jax.ShapeDtypeStruct((nb, C_HEADS, C_HEAD_DIM, C_V_DIM), f32)],
        scratch_shapes=[pltpu.VMEM((C_HEADS, C_V_DIM, C_HEAD_DIM), f32),
                        pltpu.VMEM((tt, D_C), f32), pltpu.VMEM((tt, D_C), f32),
                        pltpu.VMEM((tt, D_C), f32), pltpu.VMEM((tt, D_C), f32)],
        compiler_params=_cparams(("parallel", "arbitrary")),
        name="hgrn",
    )(hp, hp, hp, hp, lb.reshape(1, D_C), norm_g.reshape(1, C_V_DIM), s0)


def _merge_kernel(x_ref, g_ref, oa_ref, ob_ref, oc_ref, wga_ref, wgb_ref, wgc_ref,
                  pa_ref, pb_ref, pc_ref, wo_ref, o_ref, xn_ref, acc_ref):
    j = pl.program_id(1)

    @pl.when(j == 0)
    def _():
        xn_ref[...] = _rms(x_ref[...], g_ref[...]).astype(bf16)
        acc_ref[...] = jnp.zeros_like(acc_ref)

    xn = xn_ref[...]

    def branch(wg_ref, o_ref_, p_ref):
        gate = jax.nn.sigmoid(jnp.dot(xn, wg_ref[...], preferred_element_type=f32))
        return gate * jnp.dot(o_ref_[...], p_ref[...], preferred_element_type=f32)

    merged = branch(wga_ref, oa_ref, pa_ref) + branch(wgb_ref, ob_ref, pb_ref) + branch(wgc_ref, oc_ref, pc_ref)
    acc_ref[...] += jnp.dot(merged.astype(bf16), wo_ref[...], preferred_element_type=f32)

    @pl.when(j == pl.num_programs(1) - 1)
    def _():
        o_ref[...] = x_ref[...] + acc_ref[...]


def _merge(h, g, oa, ob, oc, wg, pa, pb, pc, wo, tm, tj):
    m, d = h.shape
    nj = d // tj
    row = lambda i, j: (i, 0)
    return pl.pallas_call(
        _merge_kernel,
        grid=(m // tm, nj),
        in_specs=[pl.BlockSpec((tm, d), row),
                  pl.BlockSpec((1, d), lambda i, j: (0, 0)),
                  pl.BlockSpec((tm, D_A), row),
                  pl.BlockSpec((tm, D_B), row),
                  pl.BlockSpec((tm, D_C), row),
                  pl.BlockSpec((d, tj), lambda i, j: (0, j)),
                  pl.BlockSpec((d, tj), lambda i, j: (0, nj + j)),
                  pl.BlockSpec((d, tj), lambda i, j: (0, 2 * nj + j)),
                  pl.BlockSpec((D_A, tj), lambda i, j: (0, j)),
                  pl.BlockSpec((D_B, tj), lambda i, j: (0, j)),
                  pl.BlockSpec((D_C, tj), lambda i, j: (0, j)),
                  pl.BlockSpec((tj, d), lambda i, j: (j, 0))],
        out_specs=pl.BlockSpec((tm, d), row),
        out_shape=jax.ShapeDtypeStruct((m, d), f32),
        scratch_shapes=[pltpu.VMEM((tm, d), bf16), pltpu.VMEM((tm, d), f32)],
        compiler_params=_cparams(("parallel", "arbitrary")),
        name="merge",
    )(h, g.reshape(1, d), oa, ob, oc, wg, wg, wg, pa, pb, pc, wo)


def _router_kernel(x_ref, g_ref, r_ref, o_ref):
    xn = _rms(x_ref[...], g_ref[...]).astype(bf16)
    logits = jnp.dot(xn, r_ref[...], preferred_element_type=f32)
    lane = lax.broadcasted_iota(jnp.int32, logits.shape, 1)
    logits = jnp.where(lane < N_EXPERTS, logits, -jnp.inf)
    m1 = jnp.max(logits, axis=-1, keepdims=True)
    i1 = jnp.min(jnp.where(logits == m1, lane, LANE), axis=-1, keepdims=True)
    rest = jnp.where(lane == i1, -jnp.inf, logits)
    m2 = jnp.max(rest, axis=-1, keepdims=True)
    i2 = jnp.min(jnp.where(rest == m2, lane, LANE), axis=-1, keepdims=True)
    e2 = jnp.exp(m2 - m1)
    g1 = 1.0 / (1.0 + e2)
    g2 = e2 / (1.0 + e2)
    route = jnp.where(lane == 0, i1.astype(f32), jnp.where(lane == 1, i2.astype(f32),
                      jnp.where(lane == 2, g1, jnp.where(lane == 3, g2, 0.0))))
    o_ref[...] = route[:, :N_EXPERTS]


def _router(h, g, router, tm):
    m, d = h.shape
    rp = jnp.pad(router, ((0, 0), (0, LANE - N_EXPERTS))).astype(bf16)
    return pl.pallas_call(
        _router_kernel,
        grid=(m // tm,),
        in_specs=[pl.BlockSpec((tm, d), lambda i: (i, 0)),
                  pl.BlockSpec((1, d), lambda i: (0, 0)),
                  pl.BlockSpec((d, LANE), lambda i: (0, 0))],
        out_specs=pl.BlockSpec((tm, N_EXPERTS), lambda i: (i, 0)),
        out_shape=jax.ShapeDtypeStruct((m, N_EXPERTS), f32),
        compiler_params=_cparams(("parallel",)),
        name="router",
    )(h, g.reshape(1, d), rp)


def _ffn_kernel(x_ref, g_ref, c_ref, wg_ref, wu_ref, wd_ref, o_ref, xn_ref, acc_ref, tot_ref):
    e = pl.program_id(1)
    fi = pl.program_id(2)
    ne = pl.num_programs(1)
    nf = pl.num_programs(2)

    @pl.when(jnp.logical_and(e == 0, fi == 0))
    def _():
        xn_ref[...] = _rms(x_ref[...], g_ref[...]).astype(bf16)
        tot_ref[...] = jnp.zeros_like(tot_ref)

    @pl.when(fi == 0)
    def _():
        acc_ref[...] = jnp.zeros_like(acc_ref)

    xn = xn_ref[...]
    a = jnp.dot(xn, wg_ref[...], preferred_element_type=f32)
    u = jnp.dot(xn, wu_ref[...], preferred_element_type=f32)
    act = (_silu(a) * u).astype(bf16)
    acc_ref[...] += jnp.dot(act, wd_ref[...], preferred_element_type=f32)

    @pl.when(fi == nf - 1)
    def _():
        comb = c_ref[...]
        lane = lax.broadcasted_iota(jnp.int32, comb.shape, 1)
        ce = jnp.sum(jnp.where(lane == e, comb, 0.0), axis=-1, keepdims=True)
        tot_ref[...] += ce * acc_ref[...]

    @pl.when(jnp.logical_and(e == ne - 1, fi == nf - 1))
    def _():
        o_ref[...] = x_ref[...] + tot_ref[...]


def _ffn(h, g, comb, wg, wu, wd, tm, tf):
    m, d = h.shape
    ne, _, ff = wg.shape
    return pl.pallas_call(
        _ffn_kernel,
        grid=(m // tm, ne, ff // tf),
        in_specs=[pl.BlockSpec((tm, d), lambda i, e, f: (i, 0)),
                  pl.BlockSpec((1, d), lambda i, e, f: (0, 0)),
                  pl.BlockSpec((tm, ne), lambda i, e, f: (i, 0)),
                  pl.BlockSpec((None, d, tf), lambda i, e, f: (e, 0, f)),
                  pl.BlockSpec((None, d, tf), lambda i, e, f: (e, 0, f)),
                  pl.BlockSpec((None, tf, d), lambda i, e, f: (e, f, 0))],
        out_specs=pl.BlockSpec((tm, d), lambda i, e, f: (i, 0)),
        out_shape=jax.ShapeDtypeStruct((m, d), f32),
        scratch_shapes=[pltpu.VMEM((tm, d), bf16), pltpu.VMEM((tm, d), f32), pltpu.VMEM((tm, d), f32)],
        compiler_params=_cparams(("parallel", "arbitrary", "arbitrary")),
        name="ffn",
    )(h, g.reshape(1, d), comb, wg, wu, wd)


MOE_ROWS = 1024
MOE_SUB = 256


def _moe_plan(route, m):
    n = 2 * m
    n_groups = -(-n // MOE_ROWS) + N_EXPERTS
    ef = route[:, 0:2].astype(jnp.int32).T.reshape(n)
    tok = jnp.tile(jnp.arange(m, dtype=jnp.int32), 2)
    onehot = (ef[:, None] == jnp.arange(N_EXPERTS, dtype=jnp.int32)[None, :]).astype(jnp.int32)
    counts = jnp.sum(onehot, axis=0)
    first = jnp.cumsum(counts) - counts
    rank = jnp.sum((jnp.cumsum(onehot, axis=0) - 1) * onehot, axis=1)
    slot = first[ef] + rank
    row_tok = jnp.zeros((n,), jnp.int32).at[slot].set(tok)
    row_dst = jnp.zeros((n,), jnp.int32).at[slot].set(jnp.arange(n, dtype=jnp.int32))
    per_e = (counts + MOE_ROWS - 1) // MOE_ROWS
    ends = jnp.cumsum(per_e)
    gi = jnp.arange(n_groups, dtype=jnp.int32)
    ge = jnp.minimum(jnp.sum((gi[:, None] >= ends[None, :]).astype(jnp.int32), axis=1), N_EXPERTS - 1)
    j = gi - (ends - per_e)[ge]
    g_start = first[ge] + j * MOE_ROWS
    g_cnt = jnp.clip(counts[ge] - j * MOE_ROWS, 0, MOE_ROWS)
    g_cnt = jnp.where(gi < ends[-1], g_cnt, 0)
    last_e = ge[jnp.maximum(ends[-1] - 1, 0)]
    ge = jnp.where(gi < ends[-1], ge, last_e)
    return ge, g_start.astype(jnp.int32), g_cnt.astype(jnp.int32), row_tok, row_dst


def _moe_kernel(ge_ref, gs_ref, gc_ref, tok_ref, dst_ref, h_hbm, g_ref, wg_ref, wu_ref, wd_ref, y_hbm,
                xs_ref, acc_ref, spare_ref, gsem, ssem):
    s = pl.program_id(0)
    fi = pl.program_id(1)
    nf = pl.num_programs(1)
    cnt = gc_ref[s]
    start = gs_ref[s]
    n_rows = tok_ref.shape[0]
    nsub = (cnt + MOE_SUB - 1) // MOE_SUB

    @pl.when(cnt > 0)
    def _active():
        @pl.when(fi == 0)
        def _gather():
            nrow = nsub * MOE_SUB

            def issue(r, c):
                tok = tok_ref[jnp.minimum(start + r, n_rows - 1)]
                pltpu.make_async_copy(h_hbm.at[pl.ds(tok, 1)], acc_ref.at[pl.ds(r, 1)], gsem).start()
                return c
            lax.fori_loop(0, nrow, issue, 0)

            def wait_tile(j, c):
                pltpu.make_async_copy(h_hbm.at[pl.ds(0, MOE_SUB)], acc_ref.at[pl.ds(0, MOE_SUB)], gsem).wait()
                return c
            lax.fori_loop(0, nsub, wait_tile, 0)

            def norm(j, c):
                rows = pl.ds(pl.multiple_of(j * MOE_SUB, MOE_SUB), MOE_SUB)
                xs_ref[rows, :] = _rms(acc_ref[rows, :], g_ref[...]).astype(bf16)
                acc_ref[rows, :] = jnp.zeros((MOE_SUB, acc_ref.shape[1]), f32)
                return c
            lax.fori_loop(0, nsub, norm, 0)

        wg = wg_ref[...].astype(bf16)
        wu = wu_ref[...].astype(bf16)
        wd = wd_ref[...].astype(bf16)

        def body(j, c):
            rows = pl.ds(pl.multiple_of(j * MOE_SUB, MOE_SUB), MOE_SUB)
            x = xs_ref[rows, :]
            a = jnp.dot(x, wg, preferred_element_type=f32)
            u = jnp.dot(x, wu, preferred_element_type=f32)
            act = (_silu(a) * u).astype(bf16)
            acc_ref[rows, :] += jnp.dot(act, wd, preferred_element_type=f32)
            return c
        lax.fori_loop(0, nsub, body, 0)

        @pl.when(fi == nf - 1)
        def _scatter():
            def issue(r, c):
                pltpu.make_async_copy(acc_ref.at[pl.ds(r, 1)], y_hbm.at[pl.ds(dst_ref[start + r], 1)], ssem).start()
                return c
            lax.fori_loop(0, cnt, issue, 0)

            def issue_spare(r, c):
                pltpu.make_async_copy(acc_ref.at[pl.ds(r, 1)], spare_ref.at[pl.ds(r - cnt, 1)], ssem).start()
                return c
            lax.fori_loop(cnt, nsub * MOE_SUB, issue_spare, 0)

            def wait_tile(j, c):
                pltpu.make_async_copy(acc_ref.at[pl.ds(0, MOE_SUB)], y_hbm.at[pl.ds(0, MOE_SUB)], ssem).wait()
                return c
            lax.fori_loop(0, nsub, wait_tile, 0)


def _moe(h, g, route, wg, wu, wd, tf):
    m, d = h.shape
    ne, _, ff = wg.shape
    nf = ff // tf
    ge, gs, gc, row_tok, row_dst = _moe_plan(route, m)
    n_groups = ge.shape[0]

    def wcol(s, f, ge_r, gs_r, gc_r, tok_r, dst_r):
        return (ge_r[s], 0, jnp.where(gc_r[s] > 0, f, nf - 1))

    def wrow(s, f, ge_r, gs_r, gc_r, tok_r, dst_r):
        return (ge_r[s], jnp.where(gc_r[s] > 0, f, nf - 1), 0)

    grid_spec = pltpu.PrefetchScalarGridSpec(
        num_scalar_prefetch=5,
        grid=(n_groups, nf),
        in_specs=[pl.BlockSpec(memory_space=pl.ANY),
                  pl.BlockSpec((1, d), lambda s, f, *_: (0, 0)),
                  pl.BlockSpec((None, d, tf), wcol),
                  pl.BlockSpec((None, d, tf), wcol),
                  pl.BlockSpec((None, tf, d), wrow)],
        out_specs=pl.BlockSpec(memory_space=pl.ANY),
        scratch_shapes=[pltpu.VMEM((MOE_ROWS, d), bf16), pltpu.VMEM((MOE_ROWS, d), f32),
                        pltpu.VMEM((MOE_SUB, d), f32),
                        pltpu.SemaphoreType.DMA(()), pltpu.SemaphoreType.DMA(())],
    )
    return pl.pallas_call(
        _moe_kernel,
        grid_spec=grid_spec,
        out_shape=jax.ShapeDtypeStruct((2 * m, d), f32),
        compiler_params=_cparams(("arbitrary", "arbitrary")),
        name="moe",
    )(ge, gs, gc, row_tok, row_dst, h, g.reshape(1, d), wg, wu, wd)


def _combine_kernel(x_ref, r_ref, y1_ref, y2_ref, o_ref):
    r = r_ref[...]
    o_ref[...] = x_ref[...] + (r[:, 2:3] * y1_ref[...] + r[:, 3:4] * y2_ref[...])


def _combine(h, route, y, tm):
    m, d = h.shape
    nb = m // tm
    return pl.pallas_call(
        _combine_kernel,
        grid=(nb,),
        in_specs=[pl.BlockSpec((tm, d), lambda i: (i, 0)),
                  pl.BlockSpec((tm, N_EXPERTS), lambda i: (i, 0)),
                  pl.BlockSpec((tm, d), lambda i: (i, 0)),
                  pl.BlockSpec((tm, d), lambda i: (nb + i, 0))],
        out_specs=pl.BlockSpec((tm, d), lambda i: (i, 0)),
        out_shape=jax.ShapeDtypeStruct((m, d), f32),
        compiler_params=_cparams(("parallel",)),
        name="moe_combine",
    )(h, route, y, y)


def _final_norm_kernel(x_ref, g_ref, o_ref):
    o_ref[...] = _rms(x_ref[...], g_ref[...])


def _final_norm(h, g, row0, rows, tm):
    d = h.shape[1]
    rb = row0 // tm
    return pl.pallas_call(
        _final_norm_kernel,
        grid=(rows // tm,),
        in_specs=[pl.BlockSpec((tm, d), lambda i: (rb + i, 0)),
                  pl.BlockSpec((1, d), lambda i: (0, 0))],
        out_specs=pl.BlockSpec((tm, d), lambda i: (i, 0)),
        out_shape=jax.ShapeDtypeStruct((rows, d), f32),
        compiler_params=_cparams(("parallel",)),
        name="final_norm",
    )(h, g.reshape(1, d))


def _pack_in_proj(w):
    o = np.cumsum((0,) + IN_SIZES)
    pad = lambda a: jnp.pad(a, ((0, 0), (0, LANE - a.shape[1])))
    packed = jnp.concatenate([
        w[:, o[0]:o[3]],
        w[:, o[3]:o[4]],
        w[:, o[6]:o[7]],
        w[:, o[7]:o[11]],
        pad(w[:, o[4]:o[5]]),
        pad(w[:, o[5]:o[6]]),
    ], axis=1).astype(bf16)
    return packed, w[:, o[11]:].astype(bf16)


def kernel(x_prompt, x_sample, cache_k, cache_v, cache_kidx, cache_pool, state_hgrn, norm_mix_g, norm_ffn_g, final_norm_g, w_in, w_proj_a, w_proj_b, w_proj_c, w_out, pool_w, pool_scale, hgrn_lb_logits, hgrn_norm_g, ffn_w_gate, ffn_w_up, ffn_w_down, moe_router, moe_w_gate, moe_w_up, moe_w_down):
    batch, seq, d = x_prompt.shape
    nb, nq, _ = x_sample.shape
    depth = w_in.shape[0]
    past = cache_k.shape[2]
    mp = batch * seq
    ms = nb * nq
    m = mp + ms

    lb_soft = jax.nn.softmax(hgrn_lb_logits.astype(f32), axis=0)
    lb_all = jnp.cumsum(lb_soft, axis=0) - lb_soft[0:1]

    h = jnp.concatenate([x_prompt.reshape(mp, d), x_sample.reshape(ms, d)], axis=0)
    s0_prompt = jnp.zeros((batch, C_HEADS, C_HEAD_DIM, C_V_DIM), f32)
    tm = 768 if m % 768 == 0 else 256
    tmh = tm // 2

    outs = [[] for _ in range(10)]
    for l in range(depth):
        wp, wgates = _pack_in_proj(w_in[l])
        hp = _inproj(h, norm_mix_g[l], wp, tm, 640)

        tq = 256
        kb16 = hp[:mp, COL_K:COL_K + D_A].astype(bf16)
        vt16 = hp[:mp, COL_V:COL_V + D_A].astype(bf16).reshape(batch, seq // tq, tq, A_HEADS, A_HEAD_DIM)
        vt16 = vt16.transpose(0, 1, 3, 4, 2)
        oa_p = _dsa_prompt(hp, kb16, vt16, batch, seq, tq)
        oa_s = _dsa_sample(hp, mp, nb, nq, cache_k[l], cache_v[l], cache_kidx[l])
        ob_p = _pool(hp, 0, batch, seq, 256, None, pool_w[l], pool_scale[l], 0)
        prev = jnp.pad(cache_pool[l], ((0, 0), (1, 0), (0, 0)))
        ob_s = _pool(hp, mp, nb, nq, nq, prev, pool_w[l], pool_scale[l], past)
        oc_p, st_p = _hgrn(hp, 0, batch, seq, 256, lb_all[l], hgrn_norm_g[l], s0_prompt)
        oc_s, st_s = _hgrn(hp, mp, nb, nq, nq, lb_all[l], hgrn_norm_g[l], state_hgrn[l])

        oa = jnp.concatenate([oa_p, oa_s], axis=0)
        ob = jnp.concatenate([ob_p, ob_s], axis=0)
        oc = jnp.concatenate([oc_p, oc_s], axis=0)
        h = _merge(h, norm_mix_g[l], oa, ob, oc, wgates, w_proj_a[l].astype(bf16), w_proj_b[l].astype(bf16),
                   w_proj_c[l].astype(bf16), w_out[l].astype(bf16), tmh, 512)

        kk = hp[:, COL_K:COL_K + D_A]
        vv = hp[:, COL_V:COL_V + D_A]
        ki = hp[:, COL_KI:COL_KI + IDX_DIM]
        uu = hp[:, COL_U:COL_U + D_B]
        outs[0].append(kk[:mp].reshape(batch, seq, A_HEADS, A_HEAD_DIM))
        outs[1].append(vv[:mp].reshape(batch, seq, A_HEADS, A_HEAD_DIM))
        outs[2].append(ki[:mp].reshape(batch, seq, IDX_DIM))
        outs[3].append(uu[:mp].reshape(batch, seq, D_B)[:, -POOL_STATE:])
        outs[4].append(st_p)
        outs[5].append(kk[mp:].reshape(nb, nq, A_HEADS, A_HEAD_DIM))
        outs[6].append(vv[mp:].reshape(nb, nq, A_HEADS, A_HEAD_DIM))
        outs[7].append(ki[mp:].reshape(nb, nq, IDX_DIM))
        u_ext = jnp.concatenate([cache_pool[l], uu[mp:].reshape(nb, nq, D_B)], axis=1)
        outs[8].append(u_ext[:, -POOL_STATE:])
        outs[9].append(st_s)

        j = l // 2
        if l % 2 == 0:
            comb = jnp.ones((m, 1), f32)
            h = _ffn(h, norm_ffn_g[l], comb, ffn_w_gate[j][None].astype(bf16), ffn_w_up[j][None].astype(bf16),
                     ffn_w_down[j][None].astype(bf16), tmh, 512)
        else:
            route = _router(h, norm_ffn_g[l], moe_router[j], tm)
            y = _moe(h, norm_ffn_g[l], route, moe_w_gate[j], moe_w_up[j], moe_w_down[j], 512)
            h = _combine(h, route, y, tmh)

    y_prompt = _final_norm(h, final_norm_g, 0, mp, 256).reshape(batch, seq, d)
    y_sample = _final_norm(h, final_norm_g, mp, ms, 256).reshape(nb, nq, d)
    return (y_prompt, y_sample) + tuple(jnp.stack(o) for o in outs)
```

```python
import functools

import numpy as np
import jax
import jax.numpy as jnp
from jax import lax
from jax.experimental import pallas as pl
from jax.experimental.pallas import tpu as pltpu

f32 = jnp.float32
bf16 = jnp.bfloat16

D_MODEL = 2048
CHUNK = 64
A_HEADS = 8
A_HEAD_DIM = 128
D_A = A_HEADS * A_HEAD_DIM
IDX_HEADS = 8
IDX_DIM = 64
TOPK_MAX = 256
POOL_WINDOWS = (2, 4, 8, 16)
POOL_GROUP_DIM = 128
D_B = len(POOL_WINDOWS) * POOL_GROUP_DIM
POOL_STATE = max(POOL_WINDOWS) - 1
C_HEADS = 4
C_HEAD_DIM = 128
C_V_DIM = 128
D_C = C_HEADS * C_HEAD_DIM
HGRN_BLOCK = 16
N_BRANCH = 3
N_EXPERTS = 8
IN_SIZES = (D_A, D_A, D_A, IDX_HEADS * IDX_DIM, IDX_DIM, IDX_HEADS,
            D_B, D_C, D_C, C_HEADS * C_V_DIM, C_HEADS * C_V_DIM, N_BRANCH * D_MODEL)
EPS = 1e-6
NEG = -1e30
LB_FLOOR = 1e-30
INT_MIN = -2 ** 31

LOG2E = 1.4426950408889634
LANE = 128
COUNT_ROWS = 64
ATT_BLOCK = 256
COL_Q = 0
COL_QI = D_A
COL_U = COL_QI + 512
COL_CQ = COL_U + 512
COL_CF = COL_CQ + 512
COL_CI = COL_CF + 512
COL_CG = COL_CI + 512
N_PACK = COL_CG + 512
KIW_KI, KIW_W = 0, LANE

VMEM_LIMIT = 56 * 1024 * 1024

NT_DIMS = (((1,), (1,)), ((), ()))
TN_DIMS = (((0,), (0,)), ((), ()))


def _cparams(sem):
    return pltpu.CompilerParams(dimension_semantics=sem, vmem_limit_bytes=VMEM_LIMIT)


def _rms(x, g):
    ms = jnp.mean(x * x, axis=-1, keepdims=True)
    return x * lax.rsqrt(ms + EPS) * g


def _silu(x):
    return x * jax.nn.sigmoid(x)


def _sort_key(x):
    bits = pltpu.bitcast(x, jnp.int32)
    return bits ^ ((bits >> 31) & jnp.int32(0x7FFFFFFF))


def _kth_largest(count_ge, kk):
    zero = jnp.zeros(kk.shape, jnp.int32)
    t0 = jnp.where(count_ge(zero) >= kk, zero, jnp.int32(INT_MIN))

    def bit_body(i, t):
        c = t | lax.shift_left(jnp.int32(1), 30 - i)
        return jnp.where(count_ge(c) >= kk, c, t)

    return lax.fori_loop(0, 31, bit_body, t0)


def _inproj_kernel(x_ref, g_ref, w_ref, o_ref, xn_ref):
    @pl.when(pl.program_id(1) == 0)
    def _():
        xn_ref[...] = _rms(x_ref[...], g_ref[...]).astype(bf16)

    o_ref[...] = jnp.dot(xn_ref[...], w_ref[...], preferred_element_type=f32)


def _inproj(h, g, wp, tm, tn):
    m, d = h.shape
    n = wp.shape[1]
    return pl.pallas_call(
        _inproj_kernel,
        grid=(m // tm, n // tn),
        in_specs=[pl.BlockSpec((tm, d), lambda i, j: (i, 0)),
                  pl.BlockSpec((1, d), lambda i, j: (0, 0)),
                  pl.BlockSpec((d, tn), lambda i, j: (0, j))],
        out_specs=pl.BlockSpec((tm, tn), lambda i, j: (i, j)),
        out_shape=jax.ShapeDtypeStruct((m, n), f32),
        scratch_shapes=[pltpu.VMEM((tm, d), bf16)],
        compiler_params=_cparams(("parallel", "arbitrary")),
        name="inproj",
    )(h, g.reshape(1, d), wp)


def _inproj_k_kernel(x_ref, g_ref, w_ref, k_ref, kb_ref, kiw_ref):
    xn = _rms(x_ref[...], g_ref[...]).astype(bf16)
    r = jnp.dot(xn, w_ref[...], preferred_element_type=f32)
    k_ref[...] = r[:, :D_A]
    kb_ref[...] = r[:, :D_A].astype(bf16)
    kiw_ref[...] = r[:, D_A:]


def _inproj_k(h, g, wk, tm):
    m, d = h.shape
    n = wk.shape[1]
    row = lambda i: (i, 0)
    return pl.pallas_call(
        _inproj_k_kernel,
        grid=(m // tm,),
        in_specs=[pl.BlockSpec((tm, d), row),
                  pl.BlockSpec((1, d), lambda i: (0, 0)),
                  pl.BlockSpec((d, n), lambda i: (0, 0))],
        out_specs=[pl.BlockSpec((tm, D_A), row), pl.BlockSpec((tm, D_A), row), pl.BlockSpec((tm, n - D_A), row)],
        out_shape=[jax.ShapeDtypeStruct((m, D_A), f32), jax.ShapeDtypeStruct((m, D_A), bf16),
                   jax.ShapeDtypeStruct((m, n - D_A), f32)],
        compiler_params=_cparams(("parallel",)),
        name="inproj_k",
    )(h, g.reshape(1, d), wk)


def _inproj_v_kernel(x_ref, g_ref, w_ref, v_ref, vt_ref):
    xn = _rms(x_ref[...], g_ref[...]).astype(bf16)
    r = jnp.dot(xn, w_ref[...], preferred_element_type=f32)
    v_ref[...] = r
    for c in range(vt_ref.shape[0]):
        for hh in range(A_HEADS):
            blk = r[c * ATT_BLOCK:(c + 1) * ATT_BLOCK, hh * A_HEAD_DIM:(hh + 1) * A_HEAD_DIM]
            vt_ref[c, hh] = blk.T.astype(bf16)


def _inproj_v(h, g, wv, tm):
    m, d = h.shape
    nblk = tm // ATT_BLOCK
    return pl.pallas_call(
        _inproj_v_kernel,
        grid=(m // tm,),
        in_specs=[pl.BlockSpec((tm, d), lambda i: (i, 0)),
                  pl.BlockSpec((1, d), lambda i: (0, 0)),
                  pl.BlockSpec((d, D_A), lambda i: (0, 0))],
        out_specs=[pl.BlockSpec((tm, D_A), lambda i: (i, 0)),
                   pl.BlockSpec((nblk, A_HEADS, A_HEAD_DIM, ATT_BLOCK), lambda i: (i, 0, 0, 0))],
        out_shape=[jax.ShapeDtypeStruct((m, D_A), f32),
                   jax.ShapeDtypeStruct((m // ATT_BLOCK, A_HEADS, A_HEAD_DIM, ATT_BLOCK), bf16)],
        compiler_params=_cparams(("parallel",)),
        name="inproj_v",
    )(h, g.reshape(1, d), wv)


def _idx_scores(qi, w, kib):
    acc = jnp.zeros((qi.shape[0], kib.shape[0]), f32)
    for hh in range(IDX_HEADS):
        d = lax.dot_general(qi[:, hh * IDX_DIM:(hh + 1) * IDX_DIM], kib, NT_DIMS,
                            preferred_element_type=f32)
        acc = acc + w[:, hh:hh + 1] * jnp.maximum(d, 0.0)
    return acc


def _fold_lanes(x):
    part = x[:, :LANE]
    for j in range(1, x.shape[1] // LANE):
        part = part + x[:, j * LANE:(j + 1) * LANE]
    return part


def _tri_incl(n):
    r = lax.broadcasted_iota(jnp.int32, (n, n), 0)
    c = lax.broadcasted_iota(jnp.int32, (n, n), 1)
    return jnp.where(r <= c, 1.0, 0.0).astype(bf16)


def _dsa_prompt_kernel(q_ref, k_ref, vt_ref, qi_ref, ki_ref, w_ref, o_ref, keys_ref, bias_ref, qs_ref,
                       *acc_refs, tq, topk):
    qb = pl.program_id(1)
    nkb = qb + 1
    hd = A_HEAD_DIM

    qi = qi_ref[...].astype(bf16)
    wt = (w_ref[...] * (IDX_HEADS ** -0.5 * IDX_DIM ** -0.5)).T
    srow = lax.broadcasted_iota(jnp.int32, (tq, tq), 0)
    qcol = lax.broadcasted_iota(jnp.int32, (tq, tq), 1)
    diag_adm = (srow // CHUNK) <= (qcol // CHUNK)

    def score_body(kb, c):
        off = pl.multiple_of(kb * tq, tq)
        kib = ki_ref[pl.ds(off, tq), :][:, :IDX_DIM].astype(bf16)
        acc = jnp.zeros((tq, tq), f32)
        for hh in range(IDX_HEADS):
            d = lax.dot_general(kib, qi[:, hh * IDX_DIM:(hh + 1) * IDX_DIM], NT_DIMS,
                                preferred_element_type=f32)
            acc = acc + wt[hh:hh + 1, :] * jnp.maximum(d, 0.0)
        adm = jnp.logical_or(kb < qb, diag_adm)
        keys_ref[kb] = jnp.where(adm, _sort_key(acc), jnp.int32(INT_MIN))
        return c

    lax.fori_loop(0, nkb, score_body, 0)

    def count(cmp, c):
        cb = jnp.broadcast_to(c, (COUNT_ROWS, tq))

        def body(kb, acc):
            for r0 in range(0, tq, COUNT_ROWS):
                acc = acc + jnp.where(cmp(keys_ref[kb, r0:r0 + COUNT_ROWS, :], cb), 1.0, 0.0)
            return acc
        acc = lax.fori_loop(0, nkb, body, jnp.zeros((COUNT_ROWS, tq), f32))
        return jnp.sum(acc, axis=0, keepdims=True)

    lane = lax.broadcasted_iota(jnp.int32, (1, tq), 1)
    n_adm = ((qb * tq + lane) // CHUNK + 1) * CHUNK
    kk = jnp.minimum(topk, n_adm).astype(f32)
    thr = _kth_largest(lambda c: count(lambda k, t: k >= t, c), kk)
    n_ge = count(lambda k, t: k >= t, thr)
    has_tie = jnp.max(n_ge - kk) > 0.0

    @pl.when(jnp.logical_not(has_tie))
    def _():
        def body(kb, c):
            bias_ref[kb] = jnp.where(keys_ref[kb] >= thr, 0.0, NEG)
            return c
        lax.fori_loop(0, nkb, body, 0)

    @pl.when(has_tie)
    def _():
        need = kk - count(lambda k, t: k > t, thr)
        tri = jnp.where(qcol <= srow, 1.0, 0.0).astype(bf16)

        def body(kb, seen):
            k = keys_ref[kb]
            eq = jnp.where(k == thr, 1.0, 0.0)
            rank = seen + jnp.dot(tri, eq.astype(bf16), preferred_element_type=f32)
            take = jnp.logical_or(k > thr, jnp.logical_and(k == thr, rank <= need))
            bias_ref[kb] = jnp.where(take, 0.0, NEG)
            return seen + jnp.sum(eq, axis=0, keepdims=True)
        lax.fori_loop(0, nkb, body, jnp.zeros((1, tq), f32))

    qs_ref[...] = (q_ref[...] * (hd ** -0.5 * LOG2E)).astype(bf16)
    for h in range(A_HEADS):
        acc_refs[h][...] = jnp.zeros((hd, tq), f32)

    def att_body(kb, carry):
        ms, ls = carry
        off = pl.multiple_of(kb * tq, tq)
        kblk = k_ref[pl.ds(off, tq), :]
        bias = bias_ref[kb]
        sts = []
        for h in range(A_HEADS):
            cols = slice(h * hd, (h + 1) * hd)
            sts.append(lax.dot_general(kblk[:, cols], qs_ref[:, cols], NT_DIMS, preferred_element_type=f32))
        new_ms, new_ls, ps, corrs = [], [], [], []
        for h in range(A_HEADS):
            st = sts[h] + bias
            m_new = jnp.maximum(ms[h], jnp.max(st, axis=0, keepdims=True))
            p = jnp.exp2(st - m_new)
            corr = jnp.exp2(ms[h] - m_new)
            new_ls.append(corr * ls[h] + jnp.sum(p, axis=0, keepdims=True))
            new_ms.append(m_new)
            ps.append(p.astype(bf16))
            corrs.append(corr)
        for h in range(A_HEADS):
            acc_refs[h][...] = corrs[h] * acc_refs[h][...] + jnp.dot(vt_ref[kb, h], ps[h],
                                                                     preferred_element_type=f32)
        return tuple(new_ms), tuple(new_ls)

    init = (tuple(jnp.full((1, tq), NEG, f32) for _ in range(A_HEADS)),
            tuple(jnp.zeros((1, tq), f32) for _ in range(A_HEADS)))
    _, ls = lax.fori_loop(0, nkb, att_body, init)
    for h in range(A_HEADS):
        o_ref[:, h * hd:(h + 1) * hd] = (acc_refs[h][...] / ls[h]).T.astype(o_ref.dtype)


def _dsa_prompt(hp, kb16, vt16, kiw, batch, seq):
    tq = ATT_BLOCK
    nq = seq // tq
    topk = min(TOPK_MAX, seq // 4)
    kern = functools.partial(_dsa_prompt_kernel, tq=tq, topk=topk)
    once = pl.Buffered(1)
    return pl.pallas_call(
        kern,
        grid=(batch, nq),
        in_specs=[
            pl.BlockSpec((tq, D_A), lambda b, i: (b * nq + i, COL_Q // D_A)),
            pl.BlockSpec((seq, D_A), lambda b, i: (b, 0), pipeline_mode=once),
            pl.BlockSpec((nq, A_HEADS, A_HEAD_DIM, tq), lambda b, i: (b, 0, 0, 0), pipeline_mode=once),
            pl.BlockSpec((tq, 512), lambda b, i: (b * nq + i, COL_QI // 512)),
            pl.BlockSpec((seq, LANE), lambda b, i: (b, KIW_KI // LANE), pipeline_mode=once),
            pl.BlockSpec((tq, LANE), lambda b, i: (b * nq + i, KIW_W // LANE)),
        ],
        out_specs=pl.BlockSpec((tq, D_A), lambda b, i: (b * nq + i, 0)),
        out_shape=jax.ShapeDtypeStruct((batch * seq, D_A), bf16),
        scratch_shapes=[pltpu.VMEM((nq, tq, tq), jnp.int32), pltpu.VMEM((nq, tq, tq), f32),
                        pltpu.VMEM((tq, D_A), bf16)]
                       + [pltpu.VMEM((A_HEAD_DIM, tq), f32) for _ in range(A_HEADS)],
        compiler_params=_cparams(("parallel", "arbitrary")),
        name="dsa_prompt",
    )(hp, kb16, vt16, hp, kiw, kiw)


def _dsa_sample_kernel(q_ref, kn_ref, vn_ref, qi_ref, kin_ref, w_ref, kc_ref, vc_ref, kic_ref, o_ref,
                       biasc_ref, biasn_ref, *, topk):
    nq = q_ref.shape[0]
    past = kc_ref.shape[0]

    def _select():
        qi = qi_ref[...].astype(bf16)
        w = w_ref[...][:, :IDX_HEADS] * (IDX_HEADS ** -0.5 * IDX_DIM ** -0.5)
        key_c = _sort_key(_idx_scores(qi, w, kic_ref[...].astype(bf16)))
        key_n = _sort_key(_idx_scores(qi, w, kin_ref[...][:, :IDX_DIM].astype(bf16)))

        def count(pred):
            cc = jnp.sum(_fold_lanes(jnp.where(pred(key_c), 1.0, 0.0)), axis=-1, keepdims=True)
            return cc + jnp.sum(jnp.where(pred(key_n), 1.0, 0.0), axis=-1, keepdims=True)

        kk = jnp.full((nq, 1), float(topk), f32)
        thr = _kth_largest(lambda c: count(lambda k: k >= c), kk)
        n_ge = count(lambda k: k >= thr)
        has_tie = jnp.max(n_ge - kk) > 0.0

        @pl.when(jnp.logical_not(has_tie))
        def _():
            biasc_ref[...] = jnp.where(key_c >= thr, 0.0, NEG)
            biasn_ref[...] = jnp.where(key_n >= thr, 0.0, NEG)

        @pl.when(has_tie)
        def _():
            need = kk - count(lambda k: k > thr)
            tri = _tri_incl(LANE)
            seen = jnp.zeros((nq, 1), f32)
            for j in range(past // LANE):
                k = key_c[:, j * LANE:(j + 1) * LANE]
                eq = jnp.where(k == thr, 1.0, 0.0)
                rank = seen + jnp.dot(eq.astype(bf16), tri, preferred_element_type=f32)
                take = jnp.logical_or(k > thr, jnp.logical_and(k == thr, rank <= need))
                biasc_ref[:, j * LANE:(j + 1) * LANE] = jnp.where(take, 0.0, NEG)
                seen = seen + jnp.sum(eq, axis=-1, keepdims=True)
            eq = jnp.where(key_n == thr, 1.0, 0.0)
            rank = seen + jnp.dot(eq.astype(bf16), _tri_incl(nq), preferred_element_type=f32)
            take = jnp.logical_or(key_n > thr, jnp.logical_and(key_n == thr, rank <= need))
            biasn_ref[...] = jnp.where(take, 0.0, NEG)

    _select()

    hd = A_HEAD_DIM
    q = (q_ref[...] * (hd ** -0.5 * LOG2E)).astype(bf16)
    bias_c = biasc_ref[...]
    bias_n = biasn_ref[...]
    scs, sns = [], []
    for h in range(A_HEADS):
        cols = slice(h * hd, (h + 1) * hd)
        scs.append(lax.dot_general(q[:, cols], kc_ref[:, cols].astype(bf16), NT_DIMS, preferred_element_type=f32))
        sns.append(lax.dot_general(q[:, cols], kn_ref[:, cols].astype(bf16), NT_DIMS, preferred_element_type=f32))
    pcs, pns, ls = [], [], []
    for h in range(A_HEADS):
        s_c = scs[h] + bias_c
        s_n = sns[h] + bias_n
        m = jnp.maximum(jnp.max(s_c, axis=-1, keepdims=True), jnp.max(s_n, axis=-1, keepdims=True))
        p_c = jnp.exp2(s_c - m)
        p_n = jnp.exp2(s_n - m)
        ls.append(jnp.sum(p_c, axis=-1, keepdims=True) + jnp.sum(p_n, axis=-1, keepdims=True))
        pcs.append(p_c.astype(bf16))
        pns.append(p_n.astype(bf16))
    for h in range(A_HEADS):
        cols = slice(h * hd, (h + 1) * hd)
        acc = jnp.dot(pcs[h], vc_ref[:, cols].astype(bf16), preferred_element_type=f32)
        acc = acc + jnp.dot(pns[h], vn_ref[:, cols].astype(bf16), preferred_element_type=f32)
        o_ref[:, cols] = (acc / ls[h]).astype(o_ref.dtype)


def _dsa_sample(hp, k32, v32, kiw, row0, nb, nq, cache_k, cache_v, cache_kidx):
    past = cache_k.shape[1]
    topk = min(TOPK_MAX, (past + nq) // 4)
    kc = cache_k.reshape(nb, past, D_A)
    vc = cache_v.reshape(nb, past, D_A)
    rb = row0 // nq
    kern = functools.partial(_dsa_sample_kernel, topk=topk)
    return pl.pallas_call(
        kern,
        grid=(nb,),
        in_specs=[
            pl.BlockSpec((nq, D_A), lambda b: (rb + b, COL_Q // D_A)),
            pl.BlockSpec((nq, D_A), lambda b: (rb + b, 0)),
            pl.BlockSpec((nq, D_A), lambda b: (rb + b, 0)),
            pl.BlockSpec((nq, 512), lambda b: (rb + b, COL_QI // 512)),
            pl.BlockSpec((nq, LANE), lambda b: (rb + b, KIW_KI // LANE)),
            pl.BlockSpec((nq, LANE), lambda b: (rb + b, KIW_W // LANE)),
            pl.BlockSpec((None, past, D_A), lambda b: (b, 0, 0)),
            pl.BlockSpec((None, past, D_A), lambda b: (b, 0, 0)),
            pl.BlockSpec((None, past, IDX_DIM), lambda b: (b, 0, 0)),
        ],
        out_specs=pl.BlockSpec((nq, D_A), lambda b: (b, 0)),
        out_shape=jax.ShapeDtypeStruct((nb * nq, D_A), bf16),
        scratch_shapes=[pltpu.VMEM((nq, past), f32), pltpu.VMEM((nq, nq), f32)],
        compiler_params=_cparams(("parallel",)),
        name="dsa_sample",
    )(hp, k32, v32, hp, kiw, kiw, kc, vc, cache_kidx)


def _pool_kernel(prev_ref, u_ref, pw_ref, ps_ref, o_ref, *, tt, first_pos, zero_first):
    tb = pl.program_id(1)
    halo = POOL_STATE + 1
    prev = prev_ref[...]
    if zero_first:
        prev = jnp.where(tb == 0, 0.0, prev)
    cur = u_ref[...]
    ext = jnp.concatenate([prev, cur], axis=0)
    pos = first_pos + tb * tt + lax.broadcasted_iota(jnp.int32, (tt, 1), 0)
    outs = []
    for gi, wdw in enumerate(POOL_WINDOWS):
        cols = slice(gi * POOL_GROUP_DIM, (gi + 1) * POOL_GROUP_DIM)
        a = ext[:, cols]
        n = 1
        while n < wdw:
            a = a[n:] + a[:-n]
            n *= 2
        s = a[halo + 1 - wdw: halo + 1 - wdw + tt]
        cnt = jnp.minimum(pos + 1, wdw).astype(f32)
        d = s / cnt - cur[:, cols]
        outs.append(jnp.dot(d.astype(bf16), pw_ref[gi].astype(bf16), preferred_element_type=f32))
    o = jnp.concatenate(outs, axis=-1) * ps_ref[...]
    o_ref[...] = o.astype(o_ref.dtype)


def _pool(hp, row0, nb, t, tt, prev, pool_w, pool_scale, first_pos):
    ntb = t // tt
    halo = POOL_STATE + 1
    rb = row0 // tt
    if prev is None:
        prev_arr = hp
        per = tt // halo
        prev_spec = pl.BlockSpec(
            (halo, D_B), lambda b, i: (jnp.maximum((row0 // halo) + (b * ntb + i) * per - 1, 0), COL_U // D_B))
    else:
        prev_arr = prev
        prev_spec = pl.BlockSpec((None, halo, D_B), lambda b, i: (b, 0, 0))
    kern = functools.partial(_pool_kernel, tt=tt, first_pos=first_pos, zero_first=prev is None)
    return pl.pallas_call(
        kern,
        grid=(nb, ntb),
        in_specs=[prev_spec,
                  pl.BlockSpec((tt, D_B), lambda b, i: (rb + b * ntb + i, COL_U // D_B)),
                  pl.BlockSpec((len(POOL_WINDOWS), POOL_GROUP_DIM, POOL_GROUP_DIM), lambda b, i: (0, 0, 0)),
                  pl.BlockSpec((1, D_B), lambda b, i: (0, 0))],
        out_specs=pl.BlockSpec((tt, D_B), lambda b, i: (b * ntb + i, 0)),
        out_shape=jax.ShapeDtypeStruct((nb * t, D_B), bf16),
        compiler_params=_cparams(("parallel", "arbitrary")),
        name="pool",
    )(prev_arr, hp, pool_w, pool_scale.reshape(1, D_B))


SAFE_LOG_RANGE = 80.0


def _hgrn_kernel(cq_ref, cf_ref, ci_ref, cg_ref, lb_ref, ng_ref, s0_ref, o_ref, sn_ref,
                 st_ref, b_ref, q_ref, k_ref, osc_ref, *, tt, blk):
    tb = pl.program_id(1)
    nh = C_HEADS
    hd = C_HEAD_DIM

    @pl.when(tb == 0)
    def _init():
        for h in range(nh):
            st_ref[h] = s0_ref[h].T

    lb = lb_ref[...]
    f = jnp.maximum(lb, LB_FLOOR) + (1.0 - lb) * jax.nn.sigmoid(cf_ref[...])
    g = jnp.log(f)
    k_ref[...] = 1.0 - f
    q_ref[...] = _silu(cq_ref[...])
    r = lax.broadcasted_iota(jnp.int32, (tt, tt), 0)
    c = lax.broadcasted_iota(jnp.int32, (tt, tt), 1)
    tri = jnp.where(jnp.logical_and(r // blk == c // blk, c <= r), 1.0, 0.0).astype(bf16)
    g1 = g.astype(bf16)
    e1 = g - g1.astype(f32)
    g2 = e1.astype(bf16)
    g3 = (e1 - g2.astype(f32)).astype(bf16)
    b = (jnp.dot(tri, g1, preferred_element_type=f32) + jnp.dot(tri, g2, preferred_element_type=f32)
         + jnp.dot(tri, g3, preferred_element_type=f32))
    b_ref[...] = b
    safe = jnp.min(b) > -SAFE_LOG_RANGE

    tr = lax.broadcasted_iota(jnp.int32, (blk, blk), 0)
    tc = lax.broadcasted_iota(jnp.int32, (blk, blk), 1)
    causal = tc <= tr
    trow = lax.broadcasted_iota(jnp.int32, (blk, 1), 0)

    def chunk(ci, factored):
        r0 = pl.multiple_of(ci * blk, blk)
        for h in range(nh):
            rows = pl.ds(r0, blk)
            cols = slice(h * hd, (h + 1) * hd)
            bb = b_ref[rows, cols]
            qh = q_ref[rows, cols]
            kh = k_ref[rows, cols]
            ih = ci_ref[rows, cols]
            bl = bb[blk - 1:blk, :]
            ke = (kh * jnp.exp(bl - bb)).astype(bf16)
            ib = ih.astype(bf16)
            st = st_ref[h]
            o = lax.dot_general((qh * jnp.exp(bb)).astype(bf16), st.astype(bf16), NT_DIMS,
                                preferred_element_type=f32)
            if factored:
                qe = (qh * jnp.exp(bb - bl)).astype(bf16)
                a = lax.dot_general(qe, ke, NT_DIMS, preferred_element_type=f32)
                a = jnp.where(causal, a, 0.0)
                o = o + jnp.dot(a.astype(bf16), ib, preferred_element_type=f32)
            else:
                for s in range(blk):
                    live = trow >= s
                    dec = jnp.exp(jnp.where(live, bb - bb[s:s + 1, :], 0.0))
                    a_col = jnp.sum(qh * kh[s:s + 1, :] * dec, axis=-1, keepdims=True)
                    o = o + jnp.where(live, a_col, 0.0) * ih[s:s + 1, :]
            osc_ref[rows, cols] = o
            st_ref[h] = st * jnp.exp(bl) + lax.dot_general(ib, ke, TN_DIMS, preferred_element_type=f32)

    @pl.when(safe)
    def _():
        def body(ci, c):
            chunk(ci, True)
            return c
        lax.fori_loop(0, tt // blk, body, 0)

    @pl.when(jnp.logical_not(safe))
    def _():
        def body(ci, c):
            chunk(ci, False)
            return c
        lax.fori_loop(0, tt // blk, body, 0)

    outs = []
    for h in range(nh):
        cols = slice(h * hd, (h + 1) * hd)
        outs.append(_rms(osc_ref[:, cols], ng_ref[...]))
    o_ref[...] = (jnp.concatenate(outs, axis=-1) * _silu(cg_ref[...])).astype(o_ref.dtype)

    @pl.when(tb == pl.num_programs(1) - 1)
    def _fin():
        for h in range(nh):
            sn_ref[h] = st_ref[h].T


def _hgrn(hp, row0, nb, t, tt, lb, norm_g, s0):
    ntb = t // tt
    rb = row0 // tt

    def col(cidx):
        return pl.BlockSpec((tt, D_C), lambda b, i: (rb + b * ntb + i, cidx // D_C))

    st_spec = pl.BlockSpec((None, C_HEADS, C_HEAD_DIM, C_V_DIM), lambda b, i: (b, 0, 0, 0))
    kern = functools.partial(_hgrn_kernel, tt=tt, blk=HGRN_BLOCK)
    return pl.pallas_call(
        kern,
        grid=(nb, ntb),
        in_specs=[col(COL_CQ), col(COL_CF), col(COL_CI), col(COL_CG),
                  pl.BlockSpec((1, D_C), lambda b, i: (0, 0)),
                  pl.BlockSpec((1, C_V_DIM), lambda b, i: (0, 0)),
                  st_spec],
        out_specs=[pl.BlockSpec((tt, D_C), lambda b, i: (b * ntb + i, 0)), st_spec],
        out_shape=[jax.ShapeDtypeStruct((nb * t, D_C), bf16),
                   jax.ShapeDtypeStruct((nb, C_HEADS, C_HEAD_DIM, C_V_DIM), f32)],
        scratch_shapes=[pltpu.VMEM((C_HEADS, C_V_DIM, C_HEAD_DIM), f32),
                        pltpu.VMEM((tt, D_C), f32), pltpu.VMEM((tt, D_C), f32),
                        pltpu.VMEM((tt, D_C), f32), pltpu.VMEM((tt, D_C), f32)],
        compiler_params=_cparams(("parallel", "arbitrary")),
        name="hgrn",
    )(hp, hp, hp, hp, lb.reshape(1, D_C), norm_g.reshape(1, C_V_DIM), s0)


def _merge_kernel(x_ref, g_ref, oa_ref, ob_ref, oc_ref, wga_ref, wgb_ref, wgc_ref,
                  pa_ref, pb_ref, pc_ref, wo_ref, o_ref, xn_ref, acc_ref):
    j = pl.program_id(1)

    @pl.when(j == 0)
    def _():
        xn_ref[...] = _rms(x_ref[...], g_ref[...]).astype(bf16)
        acc_ref[...] = jnp.zeros_like(acc_ref)

    xn = xn_ref[...]

    def branch(wg_ref, o_ref_, p_ref):
        gate = jax.nn.sigmoid(jnp.dot(xn, wg_ref[...], preferred_element_type=f32))
        return gate * jnp.dot(o_ref_[...], p_ref[...], preferred_element_type=f32)

    merged = branch(wga_ref, oa_ref, pa_ref) + branch(wgb_ref, ob_ref, pb_ref) + branch(wgc_ref, oc_ref, pc_ref)
    acc_ref[...] += jnp.dot(merged.astype(bf16), wo_ref[...], preferred_element_type=f32)

    @pl.when(j == pl.num_programs(1) - 1)
    def _():
        o_ref[...] = x_ref[...] + acc_ref[...]


def _merge(h, g, oa, ob, oc, wg, pa, pb, pc, wo, tm, tj):
    m, d = h.shape
    nj = d // tj
    row = lambda i, j: (i, 0)
    return pl.pallas_call(
        _merge_kernel,
        grid=(m // tm, nj),
        in_specs=[pl.BlockSpec((tm, d), row),
                  pl.BlockSpec((1, d), lambda i, j: (0, 0)),
                  pl.BlockSpec((tm, D_A), row),
                  pl.BlockSpec((tm, D_B), row),
                  pl.BlockSpec((tm, D_C), row),
                  pl.BlockSpec((d, tj), lambda i, j: (0, j)),
                  pl.BlockSpec((d, tj), lambda i, j: (0, nj + j)),
                  pl.BlockSpec((d, tj), lambda i, j: (0, 2 * nj + j)),
                  pl.BlockSpec((D_A, tj), lambda i, j: (0, j)),
                  pl.BlockSpec((D_B, tj), lambda i, j: (0, j)),
                  pl.BlockSpec((D_C, tj), lambda i, j: (0, j)),
                  pl.BlockSpec((tj, d), lambda i, j: (j, 0))],
        out_specs=pl.BlockSpec((tm, d), row),
        out_shape=jax.ShapeDtypeStruct((m, d), f32),
        scratch_shapes=[pltpu.VMEM((tm, d), bf16), pltpu.VMEM((tm, d), f32)],
        compiler_params=_cparams(("parallel", "arbitrary")),
        name="merge",
    )(h, g.reshape(1, d), oa, ob, oc, wg, wg, wg, pa, pb, pc, wo)


def _router_kernel(x_ref, g_ref, r_ref, o_ref):
    xn = _rms(x_ref[...], g_ref[...]).astype(bf16)
    logits = jnp.dot(xn, r_ref[...], preferred_element_type=f32)
    lane = lax.broadcasted_iota(jnp.int32, logits.shape, 1)
    logits = jnp.where(lane < N_EXPERTS, logits, -jnp.inf)
    m1 = jnp.max(logits, axis=-1, keepdims=True)
    i1 = jnp.min(jnp.where(logits == m1, lane, LANE), axis=-1, keepdims=True)
    rest = jnp.where(lane == i1, -jnp.inf, logits)
    m2 = jnp.max(rest, axis=-1, keepdims=True)
    i2 = jnp.min(jnp.where(rest == m2, lane, LANE), axis=-1, keepdims=True)
    e2 = jnp.exp(m2 - m1)
    g1 = 1.0 / (1.0 + e2)
    g2 = e2 / (1.0 + e2)
    route = jnp.where(lane == 0, i1.astype(f32), jnp.where(lane == 1, i2.astype(f32),
                      jnp.where(lane == 2, g1, jnp.where(lane == 3, g2, 0.0))))
    o_ref[...] = route[:, :N_EXPERTS]


def _router(h, g, router, tm):
    m, d = h.shape
    rp = jnp.pad(router, ((0, 0), (0, LANE - N_EXPERTS))).astype(bf16)
    return pl.pallas_call(
        _router_kernel,
        grid=(m // tm,),
        in_specs=[pl.BlockSpec((tm, d), lambda i: (i, 0)),
                  pl.BlockSpec((1, d), lambda i: (0, 0)),
                  pl.BlockSpec((d, LANE), lambda i: (0, 0))],
        out_specs=pl.BlockSpec((tm, N_EXPERTS), lambda i: (i, 0)),
        out_shape=jax.ShapeDtypeStruct((m, N_EXPERTS), f32),
        compiler_params=_cparams(("parallel",)),
        name="router",
    )(h, g.reshape(1, d), rp)


def _ffn_kernel(x_ref, g_ref, c_ref, wg_ref, wu_ref, wd_ref, o_ref, xn_ref, acc_ref, tot_ref):
    e = pl.program_id(1)
    fi = pl.program_id(2)
    ne = pl.num_programs(1)
    nf = pl.num_programs(2)

    @pl.when(jnp.logical_and(e == 0, fi == 0))
    def _():
        xn_ref[...] = _rms(x_ref[...], g_ref[...]).astype(bf16)
        tot_ref[...] = jnp.zeros_like(tot_ref)

    @pl.when(fi == 0)
    def _():
        acc_ref[...] = jnp.zeros_like(acc_ref)

    xn = xn_ref[...]
    a = jnp.dot(xn, wg_ref[...], preferred_element_type=f32)
    u = jnp.dot(xn, wu_ref[...], preferred_element_type=f32)
    act = (_silu(a) * u).astype(bf16)
    acc_ref[...] += jnp.dot(act, wd_ref[...], preferred_element_type=f32)

    @pl.when(fi == nf - 1)
    def _():
        comb = c_ref[...]
        lane = lax.broadcasted_iota(jnp.int32, comb.shape, 1)
        ce = jnp.sum(jnp.where(lane == e, comb, 0.0), axis=-1, keepdims=True)
        tot_ref[...] += ce * acc_ref[...]

    @pl.when(jnp.logical_and(e == ne - 1, fi == nf - 1))
    def _():
        o_ref[...] = x_ref[...] + tot_ref[...]


def _ffn(h, g, comb, wg, wu, wd, tm, tf):
    m, d = h.shape
    ne, _, ff = wg.shape
    return pl.pallas_call(
        _ffn_kernel,
        grid=(m // tm, ne, ff // tf),
        in_specs=[pl.BlockSpec((tm, d), lambda i, e, f: (i, 0)),
                  pl.BlockSpec((1, d), lambda i, e, f: (0, 0)),
                  pl.BlockSpec((tm, ne), lambda i, e, f: (i, 0)),
                  pl.BlockSpec((None, d, tf), lambda i, e, f: (e, 0, f)),
                  pl.BlockSpec((None, d, tf), lambda i, e, f: (e, 0, f)),
                  pl.BlockSpec((None, tf, d), lambda i, e, f: (e, f, 0))],
        out_specs=pl.BlockSpec((tm, d), lambda i, e, f: (i, 0)),
        out_shape=jax.ShapeDtypeStruct((m, d), f32),
        scratch_shapes=[pltpu.VMEM((tm, d), bf16), pltpu.VMEM((tm, d), f32), pltpu.VMEM((tm, d), f32)],
        compiler_params=_cparams(("parallel", "arbitrary", "arbitrary")),
        name="ffn",
    )(h, g.reshape(1, d), comb, wg, wu, wd)


MOE_ROWS = 1024
MOE_SUB = 256
DMA_UNROLL = 8


def _moe_plan(route, m):
    n = 2 * m
    n_groups = -(-n // MOE_ROWS) + N_EXPERTS
    ef = route[:, 0:2].astype(jnp.int32).T.reshape(n)
    row_dst = jnp.argsort(ef, stable=True).astype(jnp.int32)
    row_tok = jnp.where(row_dst >= m, row_dst - m, row_dst)
    counts = jnp.sum((ef[:, None] == jnp.arange(N_EXPERTS, dtype=jnp.int32)[None, :]).astype(jnp.int32), axis=0)
    first = jnp.cumsum(counts) - counts
    per_e = (counts + MOE_ROWS - 1) // MOE_ROWS
    size = (counts + jnp.maximum(per_e, 1) - 1) // jnp.maximum(per_e, 1)
    size = (size + DMA_UNROLL - 1) // DMA_UNROLL * DMA_UNROLL
    ends = jnp.cumsum(per_e)
    gi = jnp.arange(n_groups, dtype=jnp.int32)
    ge = jnp.minimum(jnp.sum((gi[:, None] >= ends[None, :]).astype(jnp.int32), axis=1), N_EXPERTS - 1)
    j = gi - (ends - per_e)[ge]
    g_start = first[ge] + j * size[ge]
    g_cnt = jnp.clip(counts[ge] - j * size[ge], 0, size[ge])
    g_cnt = jnp.where(gi < ends[-1], g_cnt, 0)
    last_e = ge[jnp.maximum(ends[-1] - 1, 0)]
    ge = jnp.where(gi < ends[-1], ge, last_e)
    return ge, g_start.astype(jnp.int32), g_cnt.astype(jnp.int32), row_tok, row_dst


def _moe_kernel(ge_ref, gs_ref, gc_ref, tok_ref, dst_ref, h_hbm, g_ref, wg_ref, wu_ref, wd_ref, y_hbm,
                xs_ref, acc_ref, spare_ref, wgb_ref, wub_ref, wdb_ref, gsem, ssem):
    s = pl.program_id(0)
    fi = pl.program_id(1)
    nf = pl.num_programs(1)
    cnt = gc_ref[s]
    start = gs_ref[s]
    n_rows = tok_ref.shape[0]
    nsub = (cnt + MOE_SUB - 1) // MOE_SUB

    @pl.when(cnt > 0)
    def _active():
        @pl.when(fi == 0)
        def _gather():
            nrow = nsub * MOE_SUB

            def issue(r8, c):
                for u in range(DMA_UNROLL):
                    r = r8 * DMA_UNROLL + u
                    tok = tok_ref[jnp.minimum(start + r, n_rows - 1)]
                    pltpu.make_async_copy(h_hbm.at[pl.ds(tok, 1)], acc_ref.at[pl.ds(r, 1)], gsem).start()
                return c
            lax.fori_loop(0, nrow // DMA_UNROLL, issue, 0)

            def wait_tile(j, c):
                pltpu.make_async_copy(h_hbm.at[pl.ds(0, MOE_SUB)], acc_ref.at[pl.ds(0, MOE_SUB)], gsem).wait()
                return c
            lax.fori_loop(0, nsub, wait_tile, 0)

            def norm(j, c):
                rows = pl.ds(pl.multiple_of(j * MOE_SUB, MOE_SUB), MOE_SUB)
                xs_ref[rows, :] = _rms(acc_ref[rows, :], g_ref[...]).astype(bf16)
                acc_ref[rows, :] = jnp.zeros((MOE_SUB, acc_ref.shape[1]), f32)
                return c
            lax.fori_loop(0, nsub, norm, 0)

        wgb_ref[...] = wg_ref[...].astype(bf16)
        wub_ref[...] = wu_ref[...].astype(bf16)
        wdb_ref[...] = wd_ref[...].astype(bf16)

        def body(j, c):
            rows = pl.ds(pl.multiple_of(j * MOE_SUB, MOE_SUB), MOE_SUB)
            x = xs_ref[rows, :]
            a = jnp.dot(x, wgb_ref[...], preferred_element_type=f32)
            u = jnp.dot(x, wub_ref[...], preferred_element_type=f32)
            act = (_silu(a) * u).astype(bf16)
            acc_ref[rows, :] += jnp.dot(act, wdb_ref[...], preferred_element_type=f32)
            return c
        lax.fori_loop(0, nsub, body, 0)

        @pl.when(fi == nf - 1)
        def _scatter():
            def issue_one(r):
                pltpu.make_async_copy(acc_ref.at[pl.ds(r, 1)], y_hbm.at[pl.ds(dst_ref[start + r], 1)], ssem).start()

            def issue(r8, c):
                for u in range(DMA_UNROLL):
                    issue_one(r8 * DMA_UNROLL + u)
                return c
            n8 = cnt // DMA_UNROLL
            lax.fori_loop(0, n8, issue, 0)

            def issue_tail(r, c):
                issue_one(r)
                return c
            lax.fori_loop(n8 * DMA_UNROLL, cnt, issue_tail, 0)

            def issue_spare(r, c):
                pltpu.make_async_copy(acc_ref.at[pl.ds(r, 1)], spare_ref.at[pl.ds(r - cnt, 1)], ssem).start()
                return c
            lax.fori_loop(cnt, nsub * MOE_SUB, issue_spare, 0)

            def wait_tile(j, c):
                pltpu.make_async_copy(acc_ref.at[pl.ds(0, MOE_SUB)], y_hbm.at[pl.ds(0, MOE_SUB)], ssem).wait()
                return c
            lax.fori_loop(0, nsub, wait_tile, 0)


def _moe(h, g, route, wg, wu, wd, tf):
    m, d = h.shape
    ne, _, ff = wg.shape
    nf = ff // tf
    ge, gs, gc, row_tok, row_dst = _moe_plan(route, m)
    n_groups = ge.shape[0]

    def wcol(s, f, ge_r, gs_r, gc_r, tok_r, dst_r):
        return (ge_r[s], 0, jnp.where(gc_r[s] > 0, f, nf - 1))

    def wrow(s, f, ge_r, gs_r, gc_r, tok_r, dst_r):
        return (ge_r[s], jnp.where(gc_r[s] > 0, f, nf - 1), 0)

    grid_spec = pltpu.PrefetchScalarGridSpec(
        num_scalar_prefetch=5,
        grid=(n_groups, nf),
        in_specs=[pl.BlockSpec(memory_space=pl.ANY),
                  pl.BlockSpec((1, d), lambda s, f, *_: (0, 0)),
                  pl.BlockSpec((None, d, tf), wcol),
                  pl.BlockSpec((None, d, tf), wcol),
                  pl.BlockSpec((None, tf, d), wrow)],
        out_specs=pl.BlockSpec(memory_space=pl.ANY),
        scratch_shapes=[pltpu.VMEM((MOE_ROWS, d), bf16), pltpu.VMEM((MOE_ROWS, d), f32),
                        pltpu.VMEM((MOE_SUB, d), f32),
                        pltpu.VMEM((d, tf), bf16), pltpu.VMEM((d, tf), bf16), pltpu.VMEM((tf, d), bf16),
                        pltpu.SemaphoreType.DMA(()), pltpu.SemaphoreType.DMA(())],
    )
    return pl.pallas_call(
        _moe_kernel,
        grid_spec=grid_spec,
        out_shape=jax.ShapeDtypeStruct((2 * m, d), f32),
        compiler_params=_cparams(("arbitrary", "arbitrary")),
        name="moe",
    )(ge, gs, gc, row_tok, row_dst, h, g.reshape(1, d), wg, wu, wd)


def _combine_kernel(x_ref, r_ref, y1_ref, y2_ref, o_ref):
    r = r_ref[...]
    o_ref[...] = x_ref[...] + (r[:, 2:3] * y1_ref[...] + r[:, 3:4] * y2_ref[...])


def _combine(h, route, y, tm):
    m, d = h.shape
    nb = m // tm
    return pl.pallas_call(
        _combine_kernel,
        grid=(nb,),
        in_specs=[pl.BlockSpec((tm, d), lambda i: (i, 0)),
                  pl.BlockSpec((tm, N_EXPERTS), lambda i: (i, 0)),
                  pl.BlockSpec((tm, d), lambda i: (i, 0)),
                  pl.BlockSpec((tm, d), lambda i: (nb + i, 0))],
        out_specs=pl.BlockSpec((tm, d), lambda i: (i, 0)),
        out_shape=jax.ShapeDtypeStruct((m, d), f32),
        compiler_params=_cparams(("parallel",)),
        name="moe_combine",
    )(h, route, y, y)


def _final_norm_kernel(x_ref, g_ref, o_ref):
    o_ref[...] = _rms(x_ref[...], g_ref[...])


def _final_norm(h, g, row0, rows, tm):
    d = h.shape[1]
    rb = row0 // tm
    return pl.pallas_call(
        _final_norm_kernel,
        grid=(rows // tm,),
        in_specs=[pl.BlockSpec((tm, d), lambda i: (rb + i, 0)),
                  pl.BlockSpec((1, d), lambda i: (0, 0))],
        out_specs=pl.BlockSpec((tm, d), lambda i: (i, 0)),
        out_shape=jax.ShapeDtypeStruct((rows, d), f32),
        compiler_params=_cparams(("parallel",)),
        name="final_norm",
    )(h, g.reshape(1, d))


def _pack_in_proj(w):
    o = np.cumsum((0,) + IN_SIZES)
    pad = lambda a: jnp.pad(a, ((0, 0), (0, LANE - a.shape[1])))
    packed = jnp.concatenate([
        w[:, o[0]:o[1]],
        w[:, o[3]:o[4]],
        w[:, o[6]:o[7]],
        w[:, o[7]:o[11]],
    ], axis=1).astype(bf16)
    wk = jnp.concatenate([w[:, o[1]:o[2]], pad(w[:, o[4]:o[5]]), pad(w[:, o[5]:o[6]])], axis=1).astype(bf16)
    return packed, wk, w[:, o[2]:o[3]].astype(bf16), w[:, o[11]:].astype(bf16)


def kernel(x_prompt, x_sample, cache_k, cache_v, cache_kidx, cache_pool, state_hgrn, norm_mix_g, norm_ffn_g, final_norm_g, w_in, w_proj_a, w_proj_b, w_proj_c, w_out, pool_w, pool_scale, hgrn_lb_logits, hgrn_norm_g, ffn_w_gate, ffn_w_up, ffn_w_down, moe_router, moe_w_gate, moe_w_up, moe_w_down):
    batch, seq, d = x_prompt.shape
    nb, nq, _ = x_sample.shape
    depth = w_in.shape[0]
    past = cache_k.shape[2]
    mp = batch * seq
    ms = nb * nq
    m = mp + ms

    lb_soft = jax.nn.softmax(hgrn_lb_logits.astype(f32), axis=0)
    lb_all = jnp.cumsum(lb_soft, axis=0) - lb_soft[0:1]

    h = jnp.concatenate([x_prompt.reshape(mp, d), x_sample.reshape(ms, d)], axis=0)
    s0_prompt = jnp.zeros((batch, C_HEADS, C_HEAD_DIM, C_V_DIM), f32)
    tm = 768 if m % 768 == 0 else 256
    tmh = tm // 2

    outs = [[] for _ in range(10)]
    for l in range(depth):
        wp, wk, wv, wgates = _pack_in_proj(w_in[l])
        hp = _inproj(h, norm_mix_g[l], wp, tm, 1024)
        k32, kb16, kiw = _inproj_k(h, norm_mix_g[l], wk, tm)
        v32, vt16 = _inproj_v(h, norm_mix_g[l], wv, tm)

        oa_p = _dsa_prompt(hp, kb16, vt16, kiw, batch, seq)
        oa_s = _dsa_sample(hp, k32, v32, kiw, mp, nb, nq, cache_k[l], cache_v[l], cache_kidx[l])
        ob_p = _pool(hp, 0, batch, seq, 256, None, pool_w[l], pool_scale[l], 0)
        prev = jnp.pad(cache_pool[l], ((0, 0), (1, 0), (0, 0)))
        ob_s = _pool(hp, mp, nb, nq, nq, prev, pool_w[l], pool_scale[l], past)
        oc_p, st_p = _hgrn(hp, 0, batch, seq, 256, lb_all[l], hgrn_norm_g[l], s0_prompt)
        oc_s, st_s = _hgrn(hp, mp, nb, nq, nq, lb_all[l], hgrn_norm_g[l], state_hgrn[l])

        oa = jnp.concatenate([oa_p, oa_s], axis=0)
        ob = jnp.concatenate([ob_p, ob_s], axis=0)
        oc = jnp.concatenate([oc_p, oc_s], axis=0)
        h = _merge(h, norm_mix_g[l], oa, ob, oc, wgates, w_proj_a[l].astype(bf16), w_proj_b[l].astype(bf16),
                   w_proj_c[l].astype(bf16), w_out[l].astype(bf16), tmh, 512)

        kk = k32
        vv = v32
        ki = kiw[:, KIW_KI:KIW_KI + IDX_DIM]
        uu = hp[:, COL_U:COL_U + D_B]
        outs[0].append(kk[:mp].reshape(batch, seq, A_HEADS, A_HEAD_DIM))
        outs[1].append(vv[:mp].reshape(batch, seq, A_HEADS, A_HEAD_DIM))
        outs[2].append(ki[:mp].reshape(batch, seq, IDX_DIM))
        outs[3].append(uu[:mp].reshape(batch, seq, D_B)[:, -POOL_STATE:])
        outs[4].append(st_p)
        outs[5].append(kk[mp:].reshape(nb, nq, A_HEADS, A_HEAD_DIM))
        outs[6].append(vv[mp:].reshape(nb, nq, A_HEADS, A_HEAD_DIM))
        outs[7].append(ki[mp:].reshape(nb, nq, IDX_DIM))
        u_ext = jnp.concatenate([cache_pool[l], uu[mp:].reshape(nb, nq, D_B)], axis=1)
        outs[8].append(u_ext[:, -POOL_STATE:])
        outs[9].append(st_s)

        j = l // 2
        if l % 2 == 0:
            comb = jnp.ones((m, 1), f32)
            h = _ffn(h, norm_ffn_g[l], comb, ffn_w_gate[j][None].astype(bf16), ffn_w_up[j][None].astype(bf16),
                     ffn_w_down[j][None].astype(bf16), tmh, 512)
        else:
            route = _router(h, norm_ffn_g[l], moe_router[j], tm)
            y = _moe(h, norm_ffn_g[l], route, moe_w_gate[j], moe_w_up[j], moe_w_down[j], 512)
            h = _combine(h, route, y, tmh)

    y_prompt = _final_norm(h, final_norm_g, 0, mp, 256).reshape(batch, seq, d)
    y_sample = _final_norm(h, final_norm_g, mp, ms, 256).reshape(nb, nq, d)
    return (y_prompt, y_sample) + tuple(jnp.stack(o) for o in outs)
```

```python
import functools

import numpy as np
import jax
import jax.numpy as jnp
from jax import lax
from jax.experimental import pallas as pl
from jax.experimental.pallas import tpu as pltpu

f32 = jnp.float32
bf16 = jnp.bfloat16

D_MODEL = 2048
CHUNK = 64
A_HEADS = 8
A_HEAD_DIM = 128
D_A = A_HEADS * A_HEAD_DIM
IDX_HEADS = 8
IDX_DIM = 64
TOPK_MAX = 256
POOL_WINDOWS = (2, 4, 8, 16)
POOL_GROUP_DIM = 128
D_B = len(POOL_WINDOWS) * POOL_GROUP_DIM
POOL_STATE = max(POOL_WINDOWS) - 1
C_HEADS = 4
C_HEAD_DIM = 128
C_V_DIM = 128
D_C = C_HEADS * C_HEAD_DIM
HGRN_BLOCK = 16
N_BRANCH = 3
N_EXPERTS = 8
IN_SIZES = (D_A, D_A, D_A, IDX_HEADS * IDX_DIM, IDX_DIM, IDX_HEADS,
            D_B, D_C, D_C, C_HEADS * C_V_DIM, C_HEADS * C_V_DIM, N_BRANCH * D_MODEL)
EPS = 1e-6
NEG = -1e30
LB_FLOOR = 1e-30
INT_MIN = -2 ** 31

LOG2E = 1.4426950408889634
LANE = 128
COUNT_ROWS = 64
ATT_BLOCK = 256
COL_Q = 0
COL_QI = D_A
COL_U = COL_QI + 512
COL_CQ = COL_U + 512
COL_CF = COL_CQ + 512
COL_CI = COL_CF + 512
COL_CG = COL_CI + 512
N_PACK = COL_CG + 512
KIW_KI, KIW_W = 0, LANE

VMEM_LIMIT = 56 * 1024 * 1024

NT_DIMS = (((1,), (1,)), ((), ()))
TN_DIMS = (((0,), (0,)), ((), ()))


def _cparams(sem):
    return pltpu.CompilerParams(dimension_semantics=sem, vmem_limit_bytes=VMEM_LIMIT)


def _rms(x, g):
    ms = jnp.mean(x * x, axis=-1, keepdims=True)
    return x * lax.rsqrt(ms + EPS) * g


def _silu(x):
    return x * jax.nn.sigmoid(x)


def _sort_key(x):
    bits = pltpu.bitcast(x, jnp.int32)
    return bits ^ ((bits >> 31) & jnp.int32(0x7FFFFFFF))


def _kth_largest(count_ge, kk):
    zero = jnp.zeros(kk.shape, jnp.int32)
    t0 = jnp.where(count_ge(zero) >= kk, zero, jnp.int32(INT_MIN))

    def bit_body(i, t):
        c = t | lax.shift_left(jnp.int32(1), 30 - i)
        return jnp.where(count_ge(c) >= kk, c, t)

    return lax.fori_loop(0, 31, bit_body, t0)


def _inproj_kernel(x_ref, g_ref, w_ref, o_ref, xn_ref):
    @pl.when(pl.program_id(1) == 0)
    def _():
        xn_ref[...] = _rms(x_ref[...], g_ref[...]).astype(bf16)

    o_ref[...] = jnp.dot(xn_ref[...], w_ref[...], preferred_element_type=f32)


def _inproj(h, g, wp, tm, tn):
    m, d = h.shape
    n = wp.shape[1]
    return pl.pallas_call(
        _inproj_kernel,
        grid=(m // tm, n // tn),
        in_specs=[pl.BlockSpec((tm, d), lambda i, j: (i, 0)),
                  pl.BlockSpec((1, d), lambda i, j: (0, 0)),
                  pl.BlockSpec((d, tn), lambda i, j: (0, j))],
        out_specs=pl.BlockSpec((tm, tn), lambda i, j: (i, j)),
        out_shape=jax.ShapeDtypeStruct((m, n), f32),
        scratch_shapes=[pltpu.VMEM((tm, d), bf16)],
        compiler_params=_cparams(("parallel", "arbitrary")),
        name="inproj",
    )(h, g.reshape(1, d), wp)


def _inproj_k_kernel(x_ref, g_ref, w_ref, k_ref, kb_ref, kiw_ref):
    xn = _rms(x_ref[...], g_ref[...]).astype(bf16)
    r = jnp.dot(xn, w_ref[...], preferred_element_type=f32)
    k_ref[...] = r[:, :D_A]
    kb_ref[...] = r[:, :D_A].astype(bf16)
    kiw_ref[...] = r[:, D_A:]


def _inproj_k(h, g, wk, tm):
    m, d = h.shape
    n = wk.shape[1]
    row = lambda i: (i, 0)
    return pl.pallas_call(
        _inproj_k_kernel,
        grid=(m // tm,),
        in_specs=[pl.BlockSpec((tm, d), row),
                  pl.BlockSpec((1, d), lambda i: (0, 0)),
                  pl.BlockSpec((d, n), lambda i: (0, 0))],
        out_specs=[pl.BlockSpec((tm, D_A), row), pl.BlockSpec((tm, D_A), row), pl.BlockSpec((tm, n - D_A), row)],
        out_shape=[jax.ShapeDtypeStruct((m, D_A), f32), jax.ShapeDtypeStruct((m, D_A), bf16),
                   jax.ShapeDtypeStruct((m, n - D_A), f32)],
        compiler_params=_cparams(("parallel",)),
        name="inproj_k",
    )(h, g.reshape(1, d), wk)


def _inproj_v_kernel(x_ref, g_ref, w_ref, v_ref, vt_ref):
    xn = _rms(x_ref[...], g_ref[...]).astype(bf16)
    r = jnp.dot(xn, w_ref[...], preferred_element_type=f32)
    v_ref[...] = r
    for c in range(vt_ref.shape[0]):
        for hh in range(A_HEADS):
            blk = r[c * ATT_BLOCK:(c + 1) * ATT_BLOCK, hh * A_HEAD_DIM:(hh + 1) * A_HEAD_DIM]
            vt_ref[c, hh] = blk.T.astype(bf16)


def _inproj_v(h, g, wv, tm):
    m, d = h.shape
    nblk = tm // ATT_BLOCK
    return pl.pallas_call(
        _inproj_v_kernel,
        grid=(m // tm,),
        in_specs=[pl.BlockSpec((tm, d), lambda i: (i, 0)),
                  pl.BlockSpec((1, d), lambda i: (0, 0)),
                  pl.BlockSpec((d, D_A), lambda i: (0, 0))],
        out_specs=[pl.BlockSpec((tm, D_A), lambda i: (i, 0)),
                   pl.BlockSpec((nblk, A_HEADS, A_HEAD_DIM, ATT_BLOCK), lambda i: (i, 0, 0, 0))],
        out_shape=[jax.ShapeDtypeStruct((m, D_A), f32),
                   jax.ShapeDtypeStruct((m // ATT_BLOCK, A_HEADS, A_HEAD_DIM, ATT_BLOCK), bf16)],
        compiler_params=_cparams(("parallel",)),
        name="inproj_v",
    )(h, g.reshape(1, d), wv)


def _idx_scores(qi, w, kib):
    acc = jnp.zeros((qi.shape[0], kib.shape[0]), f32)
    for hh in range(IDX_HEADS):
        d = lax.dot_general(qi[:, hh * IDX_DIM:(hh + 1) * IDX_DIM], kib, NT_DIMS,
                            preferred_element_type=f32)
        acc = acc + w[:, hh:hh + 1] * jnp.maximum(d, 0.0)
    return acc


def _fold_lanes(x):
    part = x[:, :LANE]
    for j in range(1, x.shape[1] // LANE):
        part = part + x[:, j * LANE:(j + 1) * LANE]
    return part


def _tri_incl(n):
    r = lax.broadcasted_iota(jnp.int32, (n, n), 0)
    c = lax.broadcasted_iota(jnp.int32, (n, n), 1)
    return jnp.where(r <= c, 1.0, 0.0).astype(bf16)


def _dsa_prompt_kernel(q_ref, k_ref, vt_ref, qi_ref, ki_ref, w_ref, o_ref, keys_ref, bias_ref, qs_ref,
                       *acc_refs, tq, topk):
    qb = pl.program_id(1)
    nkb = qb + 1
    hd = A_HEAD_DIM

    qi = qi_ref[...].astype(bf16)
    wt = (w_ref[...] * (IDX_HEADS ** -0.5 * IDX_DIM ** -0.5)).T
    srow = lax.broadcasted_iota(jnp.int32, (tq, tq), 0)
    qcol = lax.broadcasted_iota(jnp.int32, (tq, tq), 1)
    diag_adm = (srow // CHUNK) <= (qcol // CHUNK)

    def score_body(kb, c):
        off = pl.multiple_of(kb * tq, tq)
        kib = ki_ref[pl.ds(off, tq), :][:, :IDX_DIM].astype(bf16)
        ds = [lax.dot_general(kib, qi[:, hh * IDX_DIM:(hh + 1) * IDX_DIM], NT_DIMS, preferred_element_type=f32)
              for hh in range(IDX_HEADS)]
        acc = jnp.zeros((tq, tq), f32)
        for hh in range(IDX_HEADS):
            acc = acc + wt[hh:hh + 1, :] * jnp.maximum(ds[hh], 0.0)
        adm = jnp.logical_or(kb < qb, diag_adm)
        keys_ref[kb] = jnp.where(adm, _sort_key(acc), jnp.int32(INT_MIN))
        return c

    lax.fori_loop(0, nkb, score_body, 0)

    def count(cmp, c):
        cb = jnp.broadcast_to(c, (COUNT_ROWS, tq))

        def body(kb, acc):
            for r0 in range(0, tq, COUNT_ROWS):
                acc = acc + jnp.where(cmp(keys_ref[kb, r0:r0 + COUNT_ROWS, :], cb), 1.0, 0.0)
            return acc
        acc = lax.fori_loop(0, nkb, body, jnp.zeros((COUNT_ROWS, tq), f32))
        return jnp.sum(acc, axis=0, keepdims=True)

    lane = lax.broadcasted_iota(jnp.int32, (1, tq), 1)
    n_adm = ((qb * tq + lane) // CHUNK + 1) * CHUNK
    kk = jnp.minimum(topk, n_adm).astype(f32)
    thr = _kth_largest(lambda c: count(lambda k, t: k >= t, c), kk)
    n_ge = count(lambda k, t: k >= t, thr)
    has_tie = jnp.max(n_ge - kk) > 0.0

    @pl.when(jnp.logical_not(has_tie))
    def _():
        def body(kb, c):
            bias_ref[kb] = jnp.where(keys_ref[kb] >= thr, 0.0, NEG)
            return c
        lax.fori_loop(0, nkb, body, 0)

    @pl.when(has_tie)
    def _():
        need = kk - count(lambda k, t: k > t, thr)
        tri = jnp.where(qcol <= srow, 1.0, 0.0).astype(bf16)

        def body(kb, seen):
            k = keys_ref[kb]
            eq = jnp.where(k == thr, 1.0, 0.0)
            rank = seen + jnp.dot(tri, eq.astype(bf16), preferred_element_type=f32)
            take = jnp.logical_or(k > thr, jnp.logical_and(k == thr, rank <= need))
            bias_ref[kb] = jnp.where(take, 0.0, NEG)
            return seen + jnp.sum(eq, axis=0, keepdims=True)
        lax.fori_loop(0, nkb, body, jnp.zeros((1, tq), f32))

    qs_ref[...] = (q_ref[...] * (hd ** -0.5 * LOG2E)).astype(bf16)
    for h in range(A_HEADS):
        acc_refs[h][...] = jnp.zeros((hd, tq), f32)

    def att_body(kb, carry):
        ms, ls = carry
        off = pl.multiple_of(kb * tq, tq)
        kblk = k_ref[pl.ds(off, tq), :]
        bias = bias_ref[kb]
        sts = []
        for h in range(A_HEADS):
            cols = slice(h * hd, (h + 1) * hd)
            sts.append(lax.dot_general(kblk[:, cols], qs_ref[:, cols], NT_DIMS, preferred_element_type=f32))
        new_ms, new_ls, ps, corrs = [], [], [], []
        for h in range(A_HEADS):
            st = sts[h] + bias
            m_new = jnp.maximum(ms[h], jnp.max(st, axis=0, keepdims=True))
            p = jnp.exp2(st - m_new)
            corr = jnp.exp2(ms[h] - m_new)
            new_ls.append(corr * ls[h] + jnp.sum(p, axis=0, keepdims=True))
            new_ms.append(m_new)
            ps.append(p.astype(bf16))
            corrs.append(corr)
        for h in range(A_HEADS):
            acc_refs[h][...] = corrs[h] * acc_refs[h][...] + jnp.dot(vt_ref[kb, h], ps[h],
                                                                     preferred_element_type=f32)
        return tuple(new_ms), tuple(new_ls)

    init = (tuple(jnp.full((1, tq), NEG, f32) for _ in range(A_HEADS)),
            tuple(jnp.zeros((1, tq), f32) for _ in range(A_HEADS)))
    _, ls = lax.fori_loop(0, nkb, att_body, init)
    for h in range(A_HEADS):
        o_ref[:, h * hd:(h + 1) * hd] = (acc_refs[h][...] / ls[h]).T.astype(o_ref.dtype)


def _dsa_prompt(hp, kb16, vt16, kiw, batch, seq):
    tq = ATT_BLOCK
    nq = seq // tq
    topk = min(TOPK_MAX, seq // 4)
    kern = functools.partial(_dsa_prompt_kernel, tq=tq, topk=topk)
    once = pl.Buffered(1)
    return pl.pallas_call(
        kern,
        grid=(batch, nq),
        in_specs=[
            pl.BlockSpec((tq, D_A), lambda b, i: (b * nq + i, COL_Q // D_A)),
            pl.BlockSpec((seq, D_A), lambda b, i: (b, 0), pipeline_mode=once),
            pl.BlockSpec((nq, A_HEADS, A_HEAD_DIM, tq), lambda b, i: (b, 0, 0, 0), pipeline_mode=once),
            pl.BlockSpec((tq, 512), lambda b, i: (b * nq + i, COL_QI // 512)),
            pl.BlockSpec((seq, LANE), lambda b, i: (b, KIW_KI // LANE), pipeline_mode=once),
            pl.BlockSpec((tq, LANE), lambda b, i: (b * nq + i, KIW_W // LANE)),
        ],
        out_specs=pl.BlockSpec((tq, D_A), lambda b, i: (b * nq + i, 0)),
        out_shape=jax.ShapeDtypeStruct((batch * seq, D_A), bf16),
        scratch_shapes=[pltpu.VMEM((nq, tq, tq), jnp.int32), pltpu.VMEM((nq, tq, tq), f32),
                        pltpu.VMEM((tq, D_A), bf16)]
                       + [pltpu.VMEM((A_HEAD_DIM, tq), f32) for _ in range(A_HEADS)],
        compiler_params=_cparams(("parallel", "arbitrary")),
        name="dsa_prompt",
    )(hp, kb16, vt16, hp, kiw, kiw)


def _dsa_sample_kernel(q_ref, kn_ref, vn_ref, qi_ref, kin_ref, w_ref, kc_ref, vc_ref, kic_ref, o_ref,
                       biasc_ref, biasn_ref, *, topk):
    nq = q_ref.shape[0]
    past = kc_ref.shape[0]

    def _select():
        qi = qi_ref[...].astype(bf16)
        w = w_ref[...][:, :IDX_HEADS] * (IDX_HEADS ** -0.5 * IDX_DIM ** -0.5)
        key_c = _sort_key(_idx_scores(qi, w, kic_ref[...].astype(bf16)))
        key_n = _sort_key(_idx_scores(qi, w, kin_ref[...][:, :IDX_DIM].astype(bf16)))

        def count(pred):
            cc = jnp.sum(_fold_lanes(jnp.where(pred(key_c), 1.0, 0.0)), axis=-1, keepdims=True)
            return cc + jnp.sum(jnp.where(pred(key_n), 1.0, 0.0), axis=-1, keepdims=True)

        kk = jnp.full((nq, 1), float(topk), f32)
        thr = _kth_largest(lambda c: count(lambda k: k >= c), kk)
        n_ge = count(lambda k: k >= thr)
        has_tie = jnp.max(n_ge - kk) > 0.0

        @pl.when(jnp.logical_not(has_tie))
        def _():
            biasc_ref[...] = jnp.where(key_c >= thr, 0.0, NEG)
            biasn_ref[...] = jnp.where(key_n >= thr, 0.0, NEG)

        @pl.when(has_tie)
        def _():
            need = kk - count(lambda k: k > thr)
            tri = _tri_incl(LANE)
            seen = jnp.zeros((nq, 1), f32)
            for j in range(past // LANE):
                k = key_c[:, j * LANE:(j + 1) * LANE]
                eq = jnp.where(k == thr, 1.0, 0.0)
                rank = seen + jnp.dot(eq.astype(bf16), tri, preferred_element_type=f32)
                take = jnp.logical_or(k > thr, jnp.logical_and(k == thr, rank <= need))
                biasc_ref[:, j * LANE:(j + 1) * LANE] = jnp.where(take, 0.0, NEG)
                seen = seen + jnp.sum(eq, axis=-1, keepdims=True)
            eq = jnp.where(key_n == thr, 1.0, 0.0)
            rank = seen + jnp.dot(eq.astype(bf16), _tri_incl(nq), preferred_element_type=f32)
            take = jnp.logical_or(key_n > thr, jnp.logical_and(key_n == thr, rank <= need))
            biasn_ref[...] = jnp.where(take, 0.0, NEG)

    _select()

    hd = A_HEAD_DIM
    q = (q_ref[...] * (hd ** -0.5 * LOG2E)).astype(bf16)
    bias_c = biasc_ref[...]
    bias_n = biasn_ref[...]
    scs, sns = [], []
    for h in range(A_HEADS):
        cols = slice(h * hd, (h + 1) * hd)
        scs.append(lax.dot_general(q[:, cols], kc_ref[:, cols].astype(bf16), NT_DIMS, preferred_element_type=f32))
        sns.append(lax.dot_general(q[:, cols], kn_ref[:, cols].astype(bf16), NT_DIMS, preferred_element_type=f32))
    pcs, pns, ls = [], [], []
    for h in range(A_HEADS):
        s_c = scs[h] + bias_c
        s_n = sns[h] + bias_n
        m = jnp.maximum(jnp.max(s_c, axis=-1, keepdims=True), jnp.max(s_n, axis=-1, keepdims=True))
        p_c = jnp.exp2(s_c - m)
        p_n = jnp.exp2(s_n - m)
        ls.append(jnp.sum(p_c, axis=-1, keepdims=True) + jnp.sum(p_n, axis=-1, keepdims=True))
        pcs.append(p_c.astype(bf16))
        pns.append(p_n.astype(bf16))
    for h in range(A_HEADS):
        cols = slice(h * hd, (h + 1) * hd)
        acc = jnp.dot(pcs[h], vc_ref[:, cols].astype(bf16), preferred_element_type=f32)
        acc = acc + jnp.dot(pns[h], vn_ref[:, cols].astype(bf16), preferred_element_type=f32)
        o_ref[:, cols] = (acc / ls[h]).astype(o_ref.dtype)


def _dsa_sample(hp, k32, v32, kiw, row0, nb, nq, cache_k, cache_v, cache_kidx):
    past = cache_k.shape[1]
    topk = min(TOPK_MAX, (past + nq) // 4)
    kc = cache_k.reshape(nb, past, D_A)
    vc = cache_v.reshape(nb, past, D_A)
    rb = row0 // nq
    kern = functools.partial(_dsa_sample_kernel, topk=topk)
    return pl.pallas_call(
        kern,
        grid=(nb,),
        in_specs=[
            pl.BlockSpec((nq, D_A), lambda b: (rb + b, COL_Q // D_A)),
            pl.BlockSpec((nq, D_A), lambda b: (rb + b, 0)),
            pl.BlockSpec((nq, D_A), lambda b: (rb + b, 0)),
            pl.BlockSpec((nq, 512), lambda b: (rb + b, COL_QI // 512)),
            pl.BlockSpec((nq, LANE), lambda b: (rb + b, KIW_KI // LANE)),
            pl.BlockSpec((nq, LANE), lambda b: (rb + b, KIW_W // LANE)),
            pl.BlockSpec((None, past, D_A), lambda b: (b, 0, 0)),
            pl.BlockSpec((None, past, D_A), lambda b: (b, 0, 0)),
            pl.BlockSpec((None, past, IDX_DIM), lambda b: (b, 0, 0)),
        ],
        out_specs=pl.BlockSpec((nq, D_A), lambda b: (b, 0)),
        out_shape=jax.ShapeDtypeStruct((nb * nq, D_A), bf16),
        scratch_shapes=[pltpu.VMEM((nq, past), f32), pltpu.VMEM((nq, nq), f32)],
        compiler_params=_cparams(("parallel",)),
        name="dsa_sample",
    )(hp, k32, v32, hp, kiw, kiw, kc, vc, cache_kidx)


def _pool_kernel(prev_ref, u_ref, pw_ref, ps_ref, o_ref, *, tt, first_pos, zero_first):
    tb = pl.program_id(1)
    halo = POOL_STATE + 1
    prev = prev_ref[...]
    if zero_first:
        prev = jnp.where(tb == 0, 0.0, prev)
    cur = u_ref[...]
    ext = jnp.concatenate([prev, cur], axis=0)
    pos = first_pos + tb * tt + lax.broadcasted_iota(jnp.int32, (tt, 1), 0)
    outs = []
    for gi, wdw in enumerate(POOL_WINDOWS):
        cols = slice(gi * POOL_GROUP_DIM, (gi + 1) * POOL_GROUP_DIM)
        a = ext[:, cols]
        n = 1
        while n < wdw:
            a = a[n:] + a[:-n]
            n *= 2
        s = a[halo + 1 - wdw: halo + 1 - wdw + tt]
        cnt = jnp.minimum(pos + 1, wdw).astype(f32)
        d = s / cnt - cur[:, cols]
        outs.append(jnp.dot(d.astype(bf16), pw_ref[gi].astype(bf16), preferred_element_type=f32))
    o = jnp.concatenate(outs, axis=-1) * ps_ref[...]
    o_ref[...] = o.astype(o_ref.dtype)


def _pool(hp, row0, nb, t, tt, prev, pool_w, pool_scale, first_pos):
    ntb = t // tt
    halo = POOL_STATE + 1
    rb = row0 // tt
    if prev is None:
        prev_arr = hp
        per = tt // halo
        prev_spec = pl.BlockSpec(
            (halo, D_B), lambda b, i: (jnp.maximum((row0 // halo) + (b * ntb + i) * per - 1, 0), COL_U // D_B))
    else:
        prev_arr = prev
        prev_spec = pl.BlockSpec((None, halo, D_B), lambda b, i: (b, 0, 0))
    kern = functools.partial(_pool_kernel, tt=tt, first_pos=first_pos, zero_first=prev is None)
    return pl.pallas_call(
        kern,
        grid=(nb, ntb),
        in_specs=[prev_spec,
                  pl.BlockSpec((tt, D_B), lambda b, i: (rb + b * ntb + i, COL_U // D_B)),
                  pl.BlockSpec((len(POOL_WINDOWS), POOL_GROUP_DIM, POOL_GROUP_DIM), lambda b, i: (0, 0, 0)),
                  pl.BlockSpec((1, D_B), lambda b, i: (0, 0))],
        out_specs=pl.BlockSpec((tt, D_B), lambda b, i: (b * ntb + i, 0)),
        out_shape=jax.ShapeDtypeStruct((nb * t, D_B), bf16),
        compiler_params=_cparams(("parallel", "arbitrary")),
        name="pool",
    )(prev_arr, hp, pool_w, pool_scale.reshape(1, D_B))


SAFE_LOG_RANGE = 80.0


def _hgrn_kernel(cq_ref, cf_ref, ci_ref, cg_ref, lb_ref, ng_ref, s0_ref, o_ref, sn_ref,
                 st_ref, b_ref, q_ref, k_ref, osc_ref, *, tt, blk):
    tb = pl.program_id(1)
    nh = C_HEADS
    hd = C_HEAD_DIM

    @pl.when(tb == 0)
    def _init():
        for h in range(nh):
            st_ref[h] = s0_ref[h].T

    lb = lb_ref[...]
    f = jnp.maximum(lb, LB_FLOOR) + (1.0 - lb) * jax.nn.sigmoid(cf_ref[...])
    g = jnp.log(f)
    k_ref[...] = 1.0 - f
    q_ref[...] = _silu(cq_ref[...])
    r = lax.broadcasted_iota(jnp.int32, (tt, tt), 0)
    c = lax.broadcasted_iota(jnp.int32, (tt, tt), 1)
    tri = jnp.where(jnp.logical_and(r // blk == c // blk, c <= r), 1.0, 0.0).astype(bf16)
    g1 = g.astype(bf16)
    e1 = g - g1.astype(f32)
    g2 = e1.astype(bf16)
    g3 = (e1 - g2.astype(f32)).astype(bf16)
    b = (jnp.dot(tri, g1, preferred_element_type=f32) + jnp.dot(tri, g2, preferred_element_type=f32)
         + jnp.dot(tri, g3, preferred_element_type=f32))
    b_ref[...] = b
    safe = jnp.min(b) > -SAFE_LOG_RANGE

    trow = lax.broadcasted_iota(jnp.int32, (blk, 1), 0)

    def exact_chunk(ci):
        r0 = pl.multiple_of(ci * blk, blk)
        for h in range(nh):
            rows = pl.ds(r0, blk)
            cols = slice(h * hd, (h + 1) * hd)
            bb = b_ref[rows, cols]
            qh = q_ref[rows, cols]
            kh = k_ref[rows, cols]
            ih = ci_ref[rows, cols]
            bl = bb[blk - 1:blk, :]
            ke = (kh * jnp.exp(bl - bb)).astype(bf16)
            ib = ih.astype(bf16)
            st = st_ref[h]
            o = lax.dot_general((qh * jnp.exp(bb)).astype(bf16), st.astype(bf16), NT_DIMS,
                                preferred_element_type=f32)
            for s in range(blk):
                live = trow >= s
                dec = jnp.exp(jnp.where(live, bb - bb[s:s + 1, :], 0.0))
                a_col = jnp.sum(qh * kh[s:s + 1, :] * dec, axis=-1, keepdims=True)
                o = o + jnp.where(live, a_col, 0.0) * ih[s:s + 1, :]
            osc_ref[rows, cols] = o
            st_ref[h] = st * jnp.exp(bl) + lax.dot_general(ib, ke, TN_DIMS, preferred_element_type=f32)

    @pl.when(safe)
    def _():
        same_blk = r // blk == c // blk
        blk_causal = jnp.logical_and(same_blk, c <= r)
        ones_blk = jnp.where(same_blk, 1.0, 0.0).astype(bf16)
        bl_all = (jnp.dot(ones_blk, g1, preferred_element_type=f32) + jnp.dot(ones_blk, g2, preferred_element_type=f32)
                  + jnp.dot(ones_blk, g3, preferred_element_type=f32))
        for h in range(nh):
            cols = slice(h * hd, (h + 1) * hd)
            bb = b[:, cols]
            blh = bl_all[:, cols]
            qh = q_ref[:, cols]
            ib = ci_ref[:, cols].astype(bf16)
            ke = (k_ref[:, cols] * jnp.exp(blh - bb)).astype(bf16)
            qe = (qh * jnp.exp(bb - blh)).astype(bf16)
            qb = (qh * jnp.exp(bb)).astype(bf16)
            a = jnp.where(blk_causal, lax.dot_general(qe, ke, NT_DIMS, preferred_element_type=f32), 0.0)
            o_intra = jnp.dot(a.astype(bf16), ib, preferred_element_type=f32)
            incs = [lax.dot_general(ib[c0:c0 + blk], ke[c0:c0 + blk], TN_DIMS, preferred_element_type=f32)
                    for c0 in range(0, tt, blk)]
            dec = jnp.exp(blh)
            st = st_ref[h]
            o_inter = []
            for ci, c0 in enumerate(range(0, tt, blk)):
                o_inter.append(lax.dot_general(qb[c0:c0 + blk], st.astype(bf16), NT_DIMS,
                                               preferred_element_type=f32))
                st = st * dec[c0:c0 + 1, :] + incs[ci]
            st_ref[h] = st
            osc_ref[:, cols] = o_intra + jnp.concatenate(o_inter, axis=0)

    @pl.when(jnp.logical_not(safe))
    def _():
        def body(ci, c):
            exact_chunk(ci)
            return c
        lax.fori_loop(0, tt // blk, body, 0)

    outs = []
    for h in range(nh):
        cols = slice(h * hd, (h + 1) * hd)
        outs.append(_rms(osc_ref[:, cols], ng_ref[...]))
    o_ref[...] = (jnp.concatenate(outs, axis=-1) * _silu(cg_ref[...])).astype(o_ref.dtype)

    @pl.when(tb == pl.num_programs(1) - 1)
    def _fin():
        for h in range(nh):
            sn_ref[h] = st_ref[h].T


def _hgrn(hp, row0, nb, t, tt, lb, norm_g, s0):
    ntb = t // tt
    rb = row0 // tt

    def col(cidx):
        return pl.BlockSpec((tt, D_C), lambda b, i: (rb + b * ntb + i, cidx // D_C))

    st_spec = pl.BlockSpec((None, C_HEADS, C_HEAD_DIM, C_V_DIM), lambda b, i: (b, 0, 0, 0))
    kern = functools.partial(_hgrn_kernel, tt=tt, blk=HGRN_BLOCK)
    return pl.pallas_call(
        kern,
        grid=(nb, ntb),
        in_specs=[col(COL_CQ), col(COL_CF), col(COL_CI), col(COL_CG),
                  pl.BlockSpec((1, D_C), lambda b, i: (0, 0)),
                  pl.BlockSpec((1, C_V_DIM), lambda b, i: (0, 0)),
                  st_spec],
        out_specs=[pl.BlockSpec((tt, D_C), lambda b, i: (b * ntb + i, 0)), st_spec],
        out_shape=[jax.ShapeDtypeStruct((nb * t, D_C), bf16),
                   jax.ShapeDtypeStruct((nb, C_HEADS, C_HEAD_DIM, C_V_DIM), f32)],
        scratch_shapes=[pltpu.VMEM((C_HEADS, C_V_DIM, C_HEAD_DIM), f32),
                        pltpu.VMEM((tt, D_C), f32), pltpu.VMEM((tt, D_C), f32),
                        pltpu.VMEM((tt, D_C), f32), pltpu.VMEM((tt, D_C), f32)],
        compiler_params=_cparams(("parallel", "arbitrary")),
        name="hgrn",
    )(hp, hp, hp, hp, lb.reshape(1, D_C), norm_g.reshape(1, C_V_DIM), s0)


def _merge_kernel(x_ref, g_ref, oa_ref, ob_ref, oc_ref, wga_ref, wgb_ref, wgc_ref,
                  pa_ref, pb_ref, pc_ref, wo_ref, o_ref, xn_ref, acc_ref):
    j = pl.program_id(1)

    @pl.when(j == 0)
    def _():
        xn_ref[...] = _rms(x_ref[...], g_ref[...]).astype(bf16)
        acc_ref[...] = jnp.zeros_like(acc_ref)

    xn = xn_ref[...]

    def branch(wg_ref, o_ref_, p_ref):
        gate = jax.nn.sigmoid(jnp.dot(xn, wg_ref[...], preferred_element_type=f32))
        return gate * jnp.dot(o_ref_[...], p_ref[...], preferred_element_type=f32)

    merged = branch(wga_ref, oa_ref, pa_ref) + branch(wgb_ref, ob_ref, pb_ref) + branch(wgc_ref, oc_ref, pc_ref)
    acc_ref[...] += jnp.dot(merged.astype(bf16), wo_ref[...], preferred_element_type=f32)

    @pl.when(j == pl.num_programs(1) - 1)
    def _():
        o_ref[...] = x_ref[...] + acc_ref[...]


def _merge(h, g, oa, ob, oc, wg, pa, pb, pc, wo, tm, tj):
    m, d = h.shape
    nj = d // tj
    row = lambda i, j: (i, 0)
    return pl.pallas_call(
        _merge_kernel,
        grid=(m // tm, nj),
        in_specs=[pl.BlockSpec((tm, d), row),
                  pl.BlockSpec((1, d), lambda i, j: (0, 0)),
                  pl.BlockSpec((tm, D_A), row),
                  pl.BlockSpec((tm, D_B), row),
                  pl.BlockSpec((tm, D_C), row),
                  pl.BlockSpec((d, tj), lambda i, j: (0, j)),
                  pl.BlockSpec((d, tj), lambda i, j: (0, nj + j)),
                  pl.BlockSpec((d, tj), lambda i, j: (0, 2 * nj + j)),
                  pl.BlockSpec((D_A, tj), lambda i, j: (0, j)),
                  pl.BlockSpec((D_B, tj), lambda i, j: (0, j)),
                  pl.BlockSpec((D_C, tj), lambda i, j: (0, j)),
                  pl.BlockSpec((tj, d), lambda i, j: (j, 0))],
        out_specs=pl.BlockSpec((tm, d), row),
        out_shape=jax.ShapeDtypeStruct((m, d), f32),
        scratch_shapes=[pltpu.VMEM((tm, d), bf16), pltpu.VMEM((tm, d), f32)],
        compiler_params=_cparams(("parallel", "arbitrary")),
        name="merge",
    )(h, g.reshape(1, d), oa, ob, oc, wg, wg, wg, pa, pb, pc, wo)


def _router_kernel(x_ref, g_ref, r_ref, o_ref):
    xn = _rms(x_ref[...], g_ref[...]).astype(bf16)
    logits = jnp.dot(xn, r_ref[...], preferred_element_type=f32)
    lane = lax.broadcasted_iota(jnp.int32, logits.shape, 1)
    logits = jnp.where(lane < N_EXPERTS, logits, -jnp.inf)
    m1 = jnp.max(logits, axis=-1, keepdims=True)
    i1 = jnp.min(jnp.where(logits == m1, lane, LANE), axis=-1, keepdims=True)
    rest = jnp.where(lane == i1, -jnp.inf, logits)
    m2 = jnp.max(rest, axis=-1, keepdims=True)
    i2 = jnp.min(jnp.where(rest == m2, lane, LANE), axis=-1, keepdims=True)
    e2 = jnp.exp(m2 - m1)
    g1 = 1.0 / (1.0 + e2)
    g2 = e2 / (1.0 + e2)
    route = jnp.where(lane == 0, i1.astype(f32), jnp.where(lane == 1, i2.astype(f32),
                      jnp.where(lane == 2, g1, jnp.where(lane == 3, g2, 0.0))))
    o_ref[...] = route[:, :N_EXPERTS]


def _router(h, g, router, tm):
    m, d = h.shape
    rp = jnp.pad(router, ((0, 0), (0, LANE - N_EXPERTS))).astype(bf16)
    return pl.pallas_call(
        _router_kernel,
        grid=(m // tm,),
        in_specs=[pl.BlockSpec((tm, d), lambda i: (i, 0)),
                  pl.BlockSpec((1, d), lambda i: (0, 0)),
                  pl.BlockSpec((d, LANE), lambda i: (0, 0))],
        out_specs=pl.BlockSpec((tm, N_EXPERTS), lambda i: (i, 0)),
        out_shape=jax.ShapeDtypeStruct((m, N_EXPERTS), f32),
        compiler_params=_cparams(("parallel",)),
        name="router",
    )(h, g.reshape(1, d), rp)


def _ffn_kernel(x_ref, g_ref, wg_ref, wu_ref, wd_ref, o_ref, xn_ref):
    @pl.when(pl.program_id(1) == 0)
    def _():
        x = x_ref[...]
        xn_ref[...] = _rms(x, g_ref[...]).astype(bf16)
        o_ref[...] = x

    xn = xn_ref[...]
    a = jnp.dot(xn, wg_ref[...], preferred_element_type=f32)
    u = jnp.dot(xn, wu_ref[...], preferred_element_type=f32)
    act = (_silu(a) * u).astype(bf16)
    o_ref[...] += jnp.dot(act, wd_ref[...], preferred_element_type=f32)


def _ffn(h, g, wg, wu, wd, tm, tf):
    m, d = h.shape
    ff = wg.shape[1]
    return pl.pallas_call(
        _ffn_kernel,
        grid=(m // tm, ff // tf),
        in_specs=[pl.BlockSpec((tm, d), lambda i, f: (i, 0)),
                  pl.BlockSpec((1, d), lambda i, f: (0, 0)),
                  pl.BlockSpec((d, tf), lambda i, f: (0, f)),
                  pl.BlockSpec((d, tf), lambda i, f: (0, f)),
                  pl.BlockSpec((tf, d), lambda i, f: (f, 0))],
        out_specs=pl.BlockSpec((tm, d), lambda i, f: (i, 0)),
        out_shape=jax.ShapeDtypeStruct((m, d), f32),
        scratch_shapes=[pltpu.VMEM((tm, d), bf16)],
        compiler_params=_cparams(("parallel", "arbitrary")),
        name="ffn",
    )(h, g.reshape(1, d), wg, wu, wd)


MOE_ROWS = 1024
MOE_SUB = 256
DMA_UNROLL = 8


def _moe_plan(route, m):
    n = 2 * m
    n_groups = -(-n // MOE_ROWS) + N_EXPERTS
    ef = route[:, 0:2].astype(jnp.int32).T.reshape(n)
    row_dst = jnp.argsort(ef, stable=True).astype(jnp.int32)
    row_tok = jnp.where(row_dst >= m, row_dst - m, row_dst)
    counts = jnp.sum((ef[:, None] == jnp.arange(N_EXPERTS, dtype=jnp.int32)[None, :]).astype(jnp.int32), axis=0)
    first = jnp.cumsum(counts) - counts
    per_e = (counts + MOE_ROWS - 1) // MOE_ROWS
    size = (counts + jnp.maximum(per_e, 1) - 1) // jnp.maximum(per_e, 1)
    size = (size + DMA_UNROLL - 1) // DMA_UNROLL * DMA_UNROLL
    ends = jnp.cumsum(per_e)
    gi = jnp.arange(n_groups, dtype=jnp.int32)
    ge = jnp.minimum(jnp.sum((gi[:, None] >= ends[None, :]).astype(jnp.int32), axis=1), N_EXPERTS - 1)
    j = gi - (ends - per_e)[ge]
    g_start = first[ge] + j * size[ge]
    g_cnt = jnp.clip(counts[ge] - j * size[ge], 0, size[ge])
    g_cnt = jnp.where(gi < ends[-1], g_cnt, 0)
    last_e = ge[jnp.maximum(ends[-1] - 1, 0)]
    ge = jnp.where(gi < ends[-1], ge, last_e)
    return ge, g_start.astype(jnp.int32), g_cnt.astype(jnp.int32), row_tok, row_dst


def _moe_kernel(ge_ref, gs_ref, gc_ref, tok_ref, dst_ref, h_hbm, g_ref, wg_ref, wu_ref, wd_ref, y_hbm,
                xs_ref, acc_ref, spare_ref, gsem, ssem):
    s = pl.program_id(0)
    fi = pl.program_id(1)
    nf = pl.num_programs(1)
    cnt = gc_ref[s]
    start = gs_ref[s]
    n_rows = tok_ref.shape[0]
    nsub = (cnt + MOE_SUB - 1) // MOE_SUB

    @pl.when(cnt > 0)
    def _active():
        @pl.when(fi == 0)
        def _gather():
            nrow = nsub * MOE_SUB

            def issue(r8, c):
                for u in range(DMA_UNROLL):
                    r = r8 * DMA_UNROLL + u
                    tok = tok_ref[jnp.minimum(start + r, n_rows - 1)]
                    pltpu.make_async_copy(h_hbm.at[pl.ds(tok, 1)], acc_ref.at[pl.ds(r, 1)], gsem).start()
                return c
            lax.fori_loop(0, nrow // DMA_UNROLL, issue, 0)

            def wait_tile(j, c):
                pltpu.make_async_copy(h_hbm.at[pl.ds(0, MOE_SUB)], acc_ref.at[pl.ds(0, MOE_SUB)], gsem).wait()
                return c
            lax.fori_loop(0, nsub, wait_tile, 0)

            def norm(j, c):
                rows = pl.ds(pl.multiple_of(j * MOE_SUB, MOE_SUB), MOE_SUB)
                xs_ref[rows, :] = _rms(acc_ref[rows, :], g_ref[...]).astype(bf16)
                acc_ref[rows, :] = jnp.zeros((MOE_SUB, acc_ref.shape[1]), f32)
                return c
            lax.fori_loop(0, nsub, norm, 0)

        def body(j, c):
            rows = pl.ds(pl.multiple_of(j * MOE_SUB, MOE_SUB), MOE_SUB)
            x = xs_ref[rows, :]
            a = jnp.dot(x, wg_ref[...].astype(bf16), preferred_element_type=f32)
            u = jnp.dot(x, wu_ref[...].astype(bf16), preferred_element_type=f32)
            act = (_silu(a) * u).astype(bf16)
            acc_ref[rows, :] += jnp.dot(act, wd_ref[...].astype(bf16), preferred_element_type=f32)
            return c
        lax.fori_loop(0, nsub, body, 0)

        @pl.when(fi == nf - 1)
        def _scatter():
            def issue_one(r):
                pltpu.make_async_copy(acc_ref.at[pl.ds(r, 1)], y_hbm.at[pl.ds(dst_ref[start + r], 1)], ssem).start()

            def issue(r8, c):
                for u in range(DMA_UNROLL):
                    issue_one(r8 * DMA_UNROLL + u)
                return c
            n8 = cnt // DMA_UNROLL
            lax.fori_loop(0, n8, issue, 0)

            def issue_tail(r, c):
                issue_one(r)
                return c
            lax.fori_loop(n8 * DMA_UNROLL, cnt, issue_tail, 0)

            def issue_spare(r, c):
                pltpu.make_async_copy(acc_ref.at[pl.ds(r, 1)], spare_ref.at[pl.ds(r - cnt, 1)], ssem).start()
                return c
            lax.fori_loop(cnt, nsub * MOE_SUB, issue_spare, 0)

            def wait_tile(j, c):
                pltpu.make_async_copy(acc_ref.at[pl.ds(0, MOE_SUB)], y_hbm.at[pl.ds(0, MOE_SUB)], ssem).wait()
                return c
            lax.fori_loop(0, nsub, wait_tile, 0)


def _moe(h, g, route, wg, wu, wd, tf):
    m, d = h.shape
    ne, _, ff = wg.shape
    nf = ff // tf
    ge, gs, gc, row_tok, row_dst = _moe_plan(route, m)
    n_groups = ge.shape[0]

    def wcol(s, f, ge_r, gs_r, gc_r, tok_r, dst_r):
        return (ge_r[s], 0, jnp.where(gc_r[s] > 0, f, nf - 1))

    def wrow(s, f, ge_r, gs_r, gc_r, tok_r, dst_r):
        return (ge_r[s], jnp.where(gc_r[s] > 0, f, nf - 1), 0)

    grid_spec = pltpu.PrefetchScalarGridSpec(
        num_scalar_prefetch=5,
        grid=(n_groups, nf),
        in_specs=[pl.BlockSpec(memory_space=pl.ANY),
                  pl.BlockSpec((1, d), lambda s, f, *_: (0, 0)),
                  pl.BlockSpec((None, d, tf), wcol),
                  pl.BlockSpec((None, d, tf), wcol),
                  pl.BlockSpec((None, tf, d), wrow)],
        out_specs=pl.BlockSpec(memory_space=pl.ANY),
        scratch_shapes=[pltpu.VMEM((MOE_ROWS, d), bf16), pltpu.VMEM((MOE_ROWS, d), f32),
                        pltpu.VMEM((MOE_SUB, d), f32),
                        pltpu.SemaphoreType.DMA(()), pltpu.SemaphoreType.DMA(())],
    )
    return pl.pallas_call(
        _moe_kernel,
        grid_spec=grid_spec,
        out_shape=jax.ShapeDtypeStruct((2 * m, d), f32),
        compiler_params=_cparams(("arbitrary", "arbitrary")),
        name="moe",
    )(ge, gs, gc, row_tok, row_dst, h, g.reshape(1, d), wg, wu, wd)


def _combine_kernel(x_ref, r_ref, y1_ref, y2_ref, o_ref):
    r = r_ref[...]
    o_ref[...] = x_ref[...] + (r[:, 2:3] * y1_ref[...] + r[:, 3:4] * y2_ref[...])


def _combine(h, route, y, tm):
    m, d = h.shape
    nb = m // tm
    return pl.pallas_call(
        _combine_kernel,
        grid=(nb,),
        in_specs=[pl.BlockSpec((tm, d), lambda i: (i, 0)),
                  pl.BlockSpec((tm, N_EXPERTS), lambda i: (i, 0)),
                  pl.BlockSpec((tm, d), lambda i: (i, 0)),
                  pl.BlockSpec((tm, d), lambda i: (nb + i, 0))],
        out_specs=pl.BlockSpec((tm, d), lambda i: (i, 0)),
        out_shape=jax.ShapeDtypeStruct((m, d), f32),
        compiler_params=_cparams(("parallel",)),
        name="moe_combine",
    )(h, route, y, y)


def _final_norm_kernel(x_ref, g_ref, o_ref):
    o_ref[...] = _rms(x_ref[...], g_ref[...])


def _final_norm(h, g, row0, rows, tm):
    d = h.shape[1]
    rb = row0 // tm
    return pl.pallas_call(
        _final_norm_kernel,
        grid=(rows // tm,),
        in_specs=[pl.BlockSpec((tm, d), lambda i: (rb + i, 0)),
                  pl.BlockSpec((1, d), lambda i: (0, 0))],
        out_specs=pl.BlockSpec((tm, d), lambda i: (i, 0)),
        out_shape=jax.ShapeDtypeStruct((rows, d), f32),
        compiler_params=_cparams(("parallel",)),
        name="final_norm",
    )(h, g.reshape(1, d))


def _pack_in_proj(w):
    o = np.cumsum((0,) + IN_SIZES)
    pad = lambda a: jnp.pad(a, ((0, 0), (0, LANE - a.shape[1])))
    packed = jnp.concatenate([
        w[:, o[0]:o[1]],
        w[:, o[3]:o[4]],
        w[:, o[6]:o[7]],
        w[:, o[7]:o[11]],
    ], axis=1).astype(bf16)
    wk = jnp.concatenate([w[:, o[1]:o[2]], pad(w[:, o[4]:o[5]]), pad(w[:, o[5]:o[6]])], axis=1).astype(bf16)
    return packed, wk, w[:, o[2]:o[3]].astype(bf16), w[:, o[11]:].astype(bf16)


def kernel(x_prompt, x_sample, cache_k, cache_v, cache_kidx, cache_pool, state_hgrn, norm_mix_g, norm_ffn_g, final_norm_g, w_in, w_proj_a, w_proj_b, w_proj_c, w_out, pool_w, pool_scale, hgrn_lb_logits, hgrn_norm_g, ffn_w_gate, ffn_w_up, ffn_w_down, moe_router, moe_w_gate, moe_w_up, moe_w_down):
    batch, seq, d = x_prompt.shape
    nb, nq, _ = x_sample.shape
    depth = w_in.shape[0]
    past = cache_k.shape[2]
    mp = batch * seq
    ms = nb * nq
    m = mp + ms

    lb_soft = jax.nn.softmax(hgrn_lb_logits.astype(f32), axis=0)
    lb_all = jnp.cumsum(lb_soft, axis=0) - lb_soft[0:1]

    h = jnp.concatenate([x_prompt.reshape(mp, d), x_sample.reshape(ms, d)], axis=0)
    s0_prompt = jnp.zeros((batch, C_HEADS, C_HEAD_DIM, C_V_DIM), f32)
    tm = 768 if m % 768 == 0 else 256
    tmh = tm // 2

    outs = [[] for _ in range(10)]
    for l in range(depth):
        wp, wk, wv, wgates = _pack_in_proj(w_in[l])
        hp = _inproj(h, norm_mix_g[l], wp, tm, 1024)
        k32, kb16, kiw = _inproj_k(h, norm_mix_g[l], wk, tm)
        v32, vt16 = _inproj_v(h, norm_mix_g[l], wv, tm)

        oa_p = _dsa_prompt(hp, kb16, vt16, kiw, batch, seq)
        oa_s = _dsa_sample(hp, k32, v32, kiw, mp, nb, nq, cache_k[l], cache_v[l], cache_kidx[l])
        ob_p = _pool(hp, 0, batch, seq, 256, None, pool_w[l], pool_scale[l], 0)
        prev = jnp.pad(cache_pool[l], ((0, 0), (1, 0), (0, 0)))
        ob_s = _pool(hp, mp, nb, nq, nq, prev, pool_w[l], pool_scale[l], past)
        oc_p, st_p = _hgrn(hp, 0, batch, seq, 256, lb_all[l], hgrn_norm_g[l], s0_prompt)
        oc_s, st_s = _hgrn(hp, mp, nb, nq, nq, lb_all[l], hgrn_norm_g[l], state_hgrn[l])

        oa = jnp.concatenate([oa_p, oa_s], axis=0)
        ob = jnp.concatenate([ob_p, ob_s], axis=0)
        oc = jnp.concatenate([oc_p, oc_s], axis=0)
        h = _merge(h, norm_mix_g[l], oa, ob, oc, wgates, w_proj_a[l].astype(bf16), w_proj_b[l].astype(bf16),
                   w_proj_c[l].astype(bf16), w_out[l].astype(bf16), tmh, 512)

        kk = k32
        vv = v32
        ki = kiw[:, KIW_KI:KIW_KI + IDX_DIM]
        uu = hp[:, COL_U:COL_U + D_B]
        outs[0].append(kk[:mp].reshape(batch, seq, A_HEADS, A_HEAD_DIM))
        outs[1].append(vv[:mp].reshape(batch, seq, A_HEADS, A_HEAD_DIM))
        outs[2].append(ki[:mp].reshape(batch, seq, IDX_DIM))
        outs[3].append(uu[:mp].reshape(batch, seq, D_B)[:, -POOL_STATE:])
        outs[4].append(st_p)
        outs[5].append(kk[mp:].reshape(nb, nq, A_HEADS, A_HEAD_DIM))
        outs[6].append(vv[mp:].reshape(nb, nq, A_HEADS, A_HEAD_DIM))
        outs[7].append(ki[mp:].reshape(nb, nq, IDX_DIM))
        u_ext = jnp.concatenate([cache_pool[l], uu[mp:].reshape(nb, nq, D_B)], axis=1)
        outs[8].append(u_ext[:, -POOL_STATE:])
        outs[9].append(st_s)

        j = l // 2
        if l % 2 == 0:
            h = _ffn(h, norm_ffn_g[l], ffn_w_gate[j].astype(bf16), ffn_w_up[j].astype(bf16),
                     ffn_w_down[j].astype(bf16), tm, 512)
        else:
            route = _router(h, norm_ffn_g[l], moe_router[j], tm)
            y = _moe(h, norm_ffn_g[l], route, moe_w_gate[j], moe_w_up[j], moe_w_down[j], 512)
            h = _combine(h, route, y, tmh)

    y_prompt = _final_norm(h, final_norm_g, 0, mp, 256).reshape(batch, seq, d)
    y_sample = _final_norm(h, final_norm_g, mp, ms, 256).reshape(nb, nq, d)
    return (y_prompt, y_sample) + tuple(jnp.stack(o) for o in outs)
```

```python
import functools

import numpy as np
import jax
import jax.numpy as jnp
from jax import lax
from jax.experimental import pallas as pl
from jax.experimental.pallas import tpu as pltpu

f32 = jnp.float32
bf16 = jnp.bfloat16

D_MODEL = 2048
CHUNK = 64
A_HEADS = 8
A_HEAD_DIM = 128
D_A = A_HEADS * A_HEAD_DIM
IDX_HEADS = 8
IDX_DIM = 64
TOPK_MAX = 256
POOL_WINDOWS = (2, 4, 8, 16)
POOL_GROUP_DIM = 128
D_B = len(POOL_WINDOWS) * POOL_GROUP_DIM
POOL_STATE = max(POOL_WINDOWS) - 1
C_HEADS = 4
C_HEAD_DIM = 128
C_V_DIM = 128
D_C = C_HEADS * C_HEAD_DIM
HGRN_BLOCK = 16
N_BRANCH = 3
N_EXPERTS = 8
IN_SIZES = (D_A, D_A, D_A, IDX_HEADS * IDX_DIM, IDX_DIM, IDX_HEADS,
            D_B, D_C, D_C, C_HEADS * C_V_DIM, C_HEADS * C_V_DIM, N_BRANCH * D_MODEL)
EPS = 1e-6
NEG = -1e30
LB_FLOOR = 1e-30
INT_MIN = -2 ** 31

LOG2E = 1.4426950408889634
LANE = 128
COUNT_ROWS = 64
ATT_BLOCK = 256
COL_Q = 0
COL_QI = D_A
COL_U = COL_QI + 512
COL_CQ = COL_U + 512
COL_CF = COL_CQ + 512
COL_CI = COL_CF + 512
COL_CG = COL_CI + 512
N_PACK = COL_CG + 512
KIW_KI, KIW_W = 0, LANE

VMEM_LIMIT = 56 * 1024 * 1024

NT_DIMS = (((1,), (1,)), ((), ()))
TN_DIMS = (((0,), (0,)), ((), ()))


def _cparams(sem):
    return pltpu.CompilerParams(dimension_semantics=sem, vmem_limit_bytes=VMEM_LIMIT)


def _rms(x, g):
    ms = jnp.mean(x * x, axis=-1, keepdims=True)
    return x * lax.rsqrt(ms + EPS) * g


def _silu(x):
    return x * jax.nn.sigmoid(x)


def _sort_key(x):
    bits = pltpu.bitcast(x, jnp.int32)
    return bits ^ ((bits >> 31) & jnp.int32(0x7FFFFFFF))


def _kth_largest(count_ge, kk):
    zero = jnp.zeros(kk.shape, jnp.int32)
    t0 = jnp.where(count_ge(zero) >= kk, zero, jnp.int32(INT_MIN))

    def bit_body(i, t):
        c = t | lax.shift_left(jnp.int32(1), 30 - i)
        return jnp.where(count_ge(c) >= kk, c, t)

    return lax.fori_loop(0, 31, bit_body, t0)


def _inproj_kernel(x_ref, g_ref, w_ref, o_ref, xn_ref):
    @pl.when(pl.program_id(1) == 0)
    def _():
        xn_ref[...] = _rms(x_ref[...], g_ref[...]).astype(bf16)

    o_ref[...] = jnp.dot(xn_ref[...], w_ref[...], preferred_element_type=f32)


def _inproj(h, g, wp, tm, tn):
    m, d = h.shape
    n = wp.shape[1]
    return pl.pallas_call(
        _inproj_kernel,
        grid=(m // tm, n // tn),
        in_specs=[pl.BlockSpec((tm, d), lambda i, j: (i, 0)),
                  pl.BlockSpec((1, d), lambda i, j: (0, 0)),
                  pl.BlockSpec((d, tn), lambda i, j: (0, j))],
        out_specs=pl.BlockSpec((tm, tn), lambda i, j: (i, j)),
        out_shape=jax.ShapeDtypeStruct((m, n), f32),
        scratch_shapes=[pltpu.VMEM((tm, d), bf16)],
        compiler_params=_cparams(("parallel", "arbitrary")),
        name="inproj",
    )(h, g.reshape(1, d), wp)


def _stacked_specs(layer, n_p, n_s, tm):
    prompt = pl.BlockSpec((tm, D_A), lambda i: (layer * n_p + jnp.minimum(i, n_p - 1), 0))
    sample = pl.BlockSpec((tm, D_A), lambda i: (layer * n_s + jnp.maximum(i - n_p, 0), 0))
    return prompt, sample


def _inproj_k_kernel(x_ref, g_ref, w_ref, kp_in, ks_in, kp_ref, ks_ref, kb_ref, kiw_ref, *, n_p):
    del kp_in, ks_in
    xn = _rms(x_ref[...], g_ref[...]).astype(bf16)
    r = jnp.dot(xn, w_ref[...], preferred_element_type=f32)
    k = r[:, :D_A]

    @pl.when(pl.program_id(0) < n_p)
    def _():
        kp_ref[...] = k

    @pl.when(pl.program_id(0) >= n_p)
    def _():
        ks_ref[...] = k

    kb_ref[...] = k.astype(bf16)
    kiw_ref[...] = r[:, D_A:]


def _inproj_k(h, g, wk, kp, ks, layer, mp, tm):
    m, d = h.shape
    n = wk.shape[1]
    n_p, n_s = mp // tm, (m - mp) // tm
    row = lambda i: (i, 0)
    anyspec = pl.BlockSpec(memory_space=pl.ANY)
    return pl.pallas_call(
        functools.partial(_inproj_k_kernel, n_p=n_p),
        grid=(m // tm,),
        in_specs=[pl.BlockSpec((tm, d), row),
                  pl.BlockSpec((1, d), lambda i: (0, 0)),
                  pl.BlockSpec((d, n), lambda i: (0, 0)),
                  anyspec, anyspec],
        out_specs=[*_stacked_specs(layer, n_p, n_s, tm),
                   pl.BlockSpec((tm, D_A), row), pl.BlockSpec((tm, n - D_A), row)],
        out_shape=[jax.ShapeDtypeStruct(kp.shape, f32), jax.ShapeDtypeStruct(ks.shape, f32),
                   jax.ShapeDtypeStruct((m, D_A), bf16), jax.ShapeDtypeStruct((m, n - D_A), f32)],
        input_output_aliases={3: 0, 4: 1},
        compiler_params=_cparams(("arbitrary",)),
        name="inproj_k",
    )(h, g.reshape(1, d), wk, kp, ks)


def _inproj_v_kernel(x_ref, g_ref, w_ref, vp_in, vs_in, vp_ref, vs_ref, vt_ref, *, n_p):
    del vp_in, vs_in
    xn = _rms(x_ref[...], g_ref[...]).astype(bf16)
    r = jnp.dot(xn, w_ref[...], preferred_element_type=f32)

    @pl.when(pl.program_id(0) < n_p)
    def _():
        vp_ref[...] = r

    @pl.when(pl.program_id(0) >= n_p)
    def _():
        vs_ref[...] = r

    for c in range(vt_ref.shape[0]):
        for hh in range(A_HEADS):
            blk = r[c * ATT_BLOCK:(c + 1) * ATT_BLOCK, hh * A_HEAD_DIM:(hh + 1) * A_HEAD_DIM]
            vt_ref[c, hh] = blk.T.astype(bf16)


def _inproj_v(h, g, wv, vp, vs, layer, mp, tm):
    m, d = h.shape
    n_p, n_s = mp // tm, (m - mp) // tm
    nblk = tm // ATT_BLOCK
    anyspec = pl.BlockSpec(memory_space=pl.ANY)
    return pl.pallas_call(
        functools.partial(_inproj_v_kernel, n_p=n_p),
        grid=(m // tm,),
        in_specs=[pl.BlockSpec((tm, d), lambda i: (i, 0)),
                  pl.BlockSpec((1, d), lambda i: (0, 0)),
                  pl.BlockSpec((d, D_A), lambda i: (0, 0)),
                  anyspec, anyspec],
        out_specs=[*_stacked_specs(layer, n_p, n_s, tm),
                   pl.BlockSpec((nblk, A_HEADS, A_HEAD_DIM, ATT_BLOCK), lambda i: (i, 0, 0, 0))],
        out_shape=[jax.ShapeDtypeStruct(vp.shape, f32), jax.ShapeDtypeStruct(vs.shape, f32),
                   jax.ShapeDtypeStruct((m // ATT_BLOCK, A_HEADS, A_HEAD_DIM, ATT_BLOCK), bf16)],
        input_output_aliases={3: 0, 4: 1},
        compiler_params=_cparams(("arbitrary",)),
        name="inproj_v",
    )(h, g.reshape(1, d), wv, vp, vs)


def _idx_scores(qi, w, kib):
    acc = jnp.zeros((qi.shape[0], kib.shape[0]), f32)
    for hh in range(IDX_HEADS):
        d = lax.dot_general(qi[:, hh * IDX_DIM:(hh + 1) * IDX_DIM], kib, NT_DIMS,
                            preferred_element_type=f32)
        acc = acc + w[:, hh:hh + 1] * jnp.maximum(d, 0.0)
    return acc


def _fold_lanes(x):
    part = x[:, :LANE]
    for j in range(1, x.shape[1] // LANE):
        part = part + x[:, j * LANE:(j + 1) * LANE]
    return part


def _tri_incl(n):
    r = lax.broadcasted_iota(jnp.int32, (n, n), 0)
    c = lax.broadcasted_iota(jnp.int32, (n, n), 1)
    return jnp.where(r <= c, 1.0, 0.0).astype(bf16)


def _dsa_prompt_kernel(q_ref, k_ref, vt_ref, qi_ref, ki_ref, w_ref, _o_in, o_ref, keys_ref, bias_ref, qs_ref,
                       *acc_refs, tq, topk):
    qb = pl.program_id(1)
    nkb = qb + 1
    hd = A_HEAD_DIM

    qi = qi_ref[...].astype(bf16)
    wt = (w_ref[...] * (IDX_HEADS ** -0.5 * IDX_DIM ** -0.5)).T
    srow = lax.broadcasted_iota(jnp.int32, (tq, tq), 0)
    qcol = lax.broadcasted_iota(jnp.int32, (tq, tq), 1)
    diag_adm = (srow // CHUNK) <= (qcol // CHUNK)

    def score_body(kb, c):
        off = pl.multiple_of(kb * tq, tq)
        kib = ki_ref[pl.ds(off, tq), :][:, :IDX_DIM].astype(bf16)
        ds = [lax.dot_general(kib, qi[:, hh * IDX_DIM:(hh + 1) * IDX_DIM], NT_DIMS, preferred_element_type=f32)
              for hh in range(IDX_HEADS)]
        acc = jnp.zeros((tq, tq), f32)
        for hh in range(IDX_HEADS):
            acc = acc + wt[hh:hh + 1, :] * jnp.maximum(ds[hh], 0.0)
        adm = jnp.logical_or(kb < qb, diag_adm)
        keys_ref[kb] = jnp.where(adm, _sort_key(acc), jnp.int32(INT_MIN))
        return c

    lax.fori_loop(0, nkb, score_body, 0)

    def count(cmp, c):
        cb = jnp.broadcast_to(c, (COUNT_ROWS, tq))

        def body(kb, acc):
            for r0 in range(0, tq, COUNT_ROWS):
                acc = acc + jnp.where(cmp(keys_ref[kb, r0:r0 + COUNT_ROWS, :], cb), 1.0, 0.0)
            return acc
        acc = lax.fori_loop(0, nkb, body, jnp.zeros((COUNT_ROWS, tq), f32))
        return jnp.sum(acc, axis=0, keepdims=True)

    lane = lax.broadcasted_iota(jnp.int32, (1, tq), 1)
    n_adm = ((qb * tq + lane) // CHUNK + 1) * CHUNK
    kk = jnp.minimum(topk, n_adm).astype(f32)
    thr = _kth_largest(lambda c: count(lambda k, t: k >= t, c), kk)
    n_ge = count(lambda k, t: k >= t, thr)
    has_tie = jnp.max(n_ge - kk) > 0.0

    @pl.when(jnp.logical_not(has_tie))
    def _():
        def body(kb, c):
            bias_ref[kb] = jnp.where(keys_ref[kb] >= thr, 0.0, NEG)
            return c
        lax.fori_loop(0, nkb, body, 0)

    @pl.when(has_tie)
    def _():
        need = kk - count(lambda k, t: k > t, thr)
        tri = jnp.where(qcol <= srow, 1.0, 0.0).astype(bf16)

        def body(kb, seen):
            k = keys_ref[kb]
            eq = jnp.where(k == thr, 1.0, 0.0)
            rank = seen + jnp.dot(tri, eq.astype(bf16), preferred_element_type=f32)
            take = jnp.logical_or(k > thr, jnp.logical_and(k == thr, rank <= need))
            bias_ref[kb] = jnp.where(take, 0.0, NEG)
            return seen + jnp.sum(eq, axis=0, keepdims=True)
        lax.fori_loop(0, nkb, body, jnp.zeros((1, tq), f32))

    qs_ref[...] = (q_ref[...] * (hd ** -0.5 * LOG2E)).astype(bf16)
    for h in range(A_HEADS):
        acc_refs[h][...] = jnp.zeros((hd, tq), f32)

    def att_body(kb, carry):
        ms, ls = carry
        off = pl.multiple_of(kb * tq, tq)
        kblk = k_ref[pl.ds(off, tq), :]
        bias = bias_ref[kb]
        sts = []
        for h in range(A_HEADS):
            cols = slice(h * hd, (h + 1) * hd)
            sts.append(lax.dot_general(kblk[:, cols], qs_ref[:, cols], NT_DIMS, preferred_element_type=f32))
        new_ms, new_ls, ps, corrs = [], [], [], []
        for h in range(A_HEADS):
            st = sts[h] + bias
            m_new = jnp.maximum(ms[h], jnp.max(st, axis=0, keepdims=True))
            p = jnp.exp2(st - m_new)
            corr = jnp.exp2(ms[h] - m_new)
            new_ls.append(corr * ls[h] + jnp.sum(p, axis=0, keepdims=True))
            new_ms.append(m_new)
            ps.append(p.astype(bf16))
            corrs.append(corr)
        for h in range(A_HEADS):
            acc_refs[h][...] = corrs[h] * acc_refs[h][...] + jnp.dot(vt_ref[kb, h], ps[h],
                                                                     preferred_element_type=f32)
        return tuple(new_ms), tuple(new_ls)

    init = (tuple(jnp.full((1, tq), NEG, f32) for _ in range(A_HEADS)),
            tuple(jnp.zeros((1, tq), f32) for _ in range(A_HEADS)))
    _, ls = lax.fori_loop(0, nkb, att_body, init)
    for h in range(A_HEADS):
        o_ref[:, h * hd:(h + 1) * hd] = (acc_refs[h][...] / ls[h]).T.astype(o_ref.dtype)


def _dsa_prompt(hp, kb16, vt16, kiw, obuf, batch, seq):
    tq = ATT_BLOCK
    nq = seq // tq
    topk = min(TOPK_MAX, seq // 4)
    kern = functools.partial(_dsa_prompt_kernel, tq=tq, topk=topk)
    once = pl.Buffered(1)
    return pl.pallas_call(
        kern,
        grid=(batch, nq),
        in_specs=[
            pl.BlockSpec((tq, D_A), lambda b, i: (b * nq + i, COL_Q // D_A)),
            pl.BlockSpec((seq, D_A), lambda b, i: (b, 0), pipeline_mode=once),
            pl.BlockSpec((nq, A_HEADS, A_HEAD_DIM, tq), lambda b, i: (b, 0, 0, 0), pipeline_mode=once),
            pl.BlockSpec((tq, 512), lambda b, i: (b * nq + i, COL_QI // 512)),
            pl.BlockSpec((seq, LANE), lambda b, i: (b, KIW_KI // LANE), pipeline_mode=once),
            pl.BlockSpec((tq, LANE), lambda b, i: (b * nq + i, KIW_W // LANE)),
            pl.BlockSpec(memory_space=pl.ANY),
        ],
        out_specs=pl.BlockSpec((tq, D_A), lambda b, i: (b * nq + i, 0)),
        out_shape=jax.ShapeDtypeStruct(obuf.shape, obuf.dtype),
        input_output_aliases={6: 0},
        scratch_shapes=[pltpu.VMEM((nq, tq, tq), jnp.int32), pltpu.VMEM((nq, tq, tq), f32),
                        pltpu.VMEM((tq, D_A), bf16)]
                       + [pltpu.VMEM((A_HEAD_DIM, tq), f32) for _ in range(A_HEADS)],
        compiler_params=_cparams(("parallel", "arbitrary")),
        name="dsa_prompt",
    )(hp, kb16, vt16, hp, kiw, kiw, obuf)


def _dsa_sample_kernel(q_ref, kn_ref, vn_ref, qi_ref, kin_ref, w_ref, kc_ref, vc_ref, kic_ref, _o_in, o_ref,
                       biasc_ref, biasn_ref, *, topk):
    nq = q_ref.shape[0]
    past = kc_ref.shape[0]

    def _select():
        qi = qi_ref[...].astype(bf16)
        w = w_ref[...][:, :IDX_HEADS] * (IDX_HEADS ** -0.5 * IDX_DIM ** -0.5)
        key_c = _sort_key(_idx_scores(qi, w, kic_ref[...].astype(bf16)))
        key_n = _sort_key(_idx_scores(qi, w, kin_ref[...][:, :IDX_DIM].astype(bf16)))

        def count(pred):
            cc = jnp.sum(_fold_lanes(jnp.where(pred(key_c), 1.0, 0.0)), axis=-1, keepdims=True)
            return cc + jnp.sum(jnp.where(pred(key_n), 1.0, 0.0), axis=-1, keepdims=True)

        kk = jnp.full((nq, 1), float(topk), f32)
        thr = _kth_largest(lambda c: count(lambda k: k >= c), kk)
        n_ge = count(lambda k: k >= thr)
        has_tie = jnp.max(n_ge - kk) > 0.0

        @pl.when(jnp.logical_not(has_tie))
        def _():
            biasc_ref[...] = jnp.where(key_c >= thr, 0.0, NEG)
            biasn_ref[...] = jnp.where(key_n >= thr, 0.0, NEG)

        @pl.when(has_tie)
        def _():
            need = kk - count(lambda k: k > thr)
            tri = _tri_incl(LANE)
            seen = jnp.zeros((nq, 1), f32)
            for j in range(past // LANE):
                k = key_c[:, j * LANE:(j + 1) * LANE]
                eq = jnp.where(k == thr, 1.0, 0.0)
                rank = seen + jnp.dot(eq.astype(bf16), tri, preferred_element_type=f32)
                take = jnp.logical_or(k > thr, jnp.logical_and(k == thr, rank <= need))
                biasc_ref[:, j * LANE:(j + 1) * LANE] = jnp.where(take, 0.0, NEG)
                seen = seen + jnp.sum(eq, axis=-1, keepdims=True)
            eq = jnp.where(key_n == thr, 1.0, 0.0)
            rank = seen + jnp.dot(eq.astype(bf16), _tri_incl(nq), preferred_element_type=f32)
            take = jnp.logical_or(key_n > thr, jnp.logical_and(key_n == thr, rank <= need))
            biasn_ref[...] = jnp.where(take, 0.0, NEG)

    _select()

    hd = A_HEAD_DIM
    q = (q_ref[...] * (hd ** -0.5 * LOG2E)).astype(bf16)
    bias_c = biasc_ref[...]
    bias_n = biasn_ref[...]
    scs, sns = [], []
    for h in range(A_HEADS):
        cols = slice(h * hd, (h + 1) * hd)
        scs.append(lax.dot_general(q[:, cols], kc_ref[:, cols].astype(bf16), NT_DIMS, preferred_element_type=f32))
        sns.append(lax.dot_general(q[:, cols], kn_ref[:, cols].astype(bf16), NT_DIMS, preferred_element_type=f32))
    pcs, pns, ls = [], [], []
    for h in range(A_HEADS):
        s_c = scs[h] + bias_c
        s_n = sns[h] + bias_n
        m = jnp.maximum(jnp.max(s_c, axis=-1, keepdims=True), jnp.max(s_n, axis=-1, keepdims=True))
        p_c = jnp.exp2(s_c - m)
        p_n = jnp.exp2(s_n - m)
        ls.append(jnp.sum(p_c, axis=-1, keepdims=True) + jnp.sum(p_n, axis=-1, keepdims=True))
        pcs.append(p_c.astype(bf16))
        pns.append(p_n.astype(bf16))
    for h in range(A_HEADS):
        cols = slice(h * hd, (h + 1) * hd)
        acc = jnp.dot(pcs[h], vc_ref[:, cols].astype(bf16), preferred_element_type=f32)
        acc = acc + jnp.dot(pns[h], vn_ref[:, cols].astype(bf16), preferred_element_type=f32)
        o_ref[:, cols] = (acc / ls[h]).astype(o_ref.dtype)


def _dsa_sample(hp, ks, vs, kiw, obuf, layer, row0, nb, nq, cache_k, cache_v, cache_kidx):
    past = cache_k.shape[1]
    topk = min(TOPK_MAX, (past + nq) // 4)
    kc = cache_k.reshape(nb, past, D_A)
    vc = cache_v.reshape(nb, past, D_A)
    rb = row0 // nq
    kern = functools.partial(_dsa_sample_kernel, topk=topk)
    return pl.pallas_call(
        kern,
        grid=(nb,),
        in_specs=[
            pl.BlockSpec((nq, D_A), lambda b: (rb + b, COL_Q // D_A)),
            pl.BlockSpec((nq, D_A), lambda b: (layer * nb + b, 0)),
            pl.BlockSpec((nq, D_A), lambda b: (layer * nb + b, 0)),
            pl.BlockSpec((nq, 512), lambda b: (rb + b, COL_QI // 512)),
            pl.BlockSpec((nq, LANE), lambda b: (rb + b, KIW_KI // LANE)),
            pl.BlockSpec((nq, LANE), lambda b: (rb + b, KIW_W // LANE)),
            pl.BlockSpec((None, past, D_A), lambda b: (b, 0, 0)),
            pl.BlockSpec((None, past, D_A), lambda b: (b, 0, 0)),
            pl.BlockSpec((None, past, IDX_DIM), lambda b: (b, 0, 0)),
            pl.BlockSpec(memory_space=pl.ANY),
        ],
        out_specs=pl.BlockSpec((nq, D_A), lambda b: (rb + b, 0)),
        out_shape=jax.ShapeDtypeStruct(obuf.shape, obuf.dtype),
        input_output_aliases={9: 0},
        scratch_shapes=[pltpu.VMEM((nq, past), f32), pltpu.VMEM((nq, nq), f32)],
        compiler_params=_cparams(("parallel",)),
        name="dsa_sample",
    )(hp, ks, vs, hp, kiw, kiw, kc, vc, cache_kidx, obuf)


def _pool_kernel(prev_ref, u_ref, pw_ref, ps_ref, _o_in, o_ref, *, tt, first_pos, zero_first):
    tb = pl.program_id(1)
    halo = POOL_STATE + 1
    prev = prev_ref[...]
    if zero_first:
        prev = jnp.where(tb == 0, 0.0, prev)
    cur = u_ref[...]
    ext = jnp.concatenate([prev, cur], axis=0)
    pos = first_pos + tb * tt + lax.broadcasted_iota(jnp.int32, (tt, 1), 0)
    outs = []
    for gi, wdw in enumerate(POOL_WINDOWS):
        cols = slice(gi * POOL_GROUP_DIM, (gi + 1) * POOL_GROUP_DIM)
        a = ext[:, cols]
        n = 1
        while n < wdw:
            a = a[n:] + a[:-n]
            n *= 2
        s = a[halo + 1 - wdw: halo + 1 - wdw + tt]
        cnt = jnp.minimum(pos + 1, wdw).astype(f32)
        d = s / cnt - cur[:, cols]
        outs.append(jnp.dot(d.astype(bf16), pw_ref[gi].astype(bf16), preferred_element_type=f32))
    o = jnp.concatenate(outs, axis=-1) * ps_ref[...]
    o_ref[...] = o.astype(o_ref.dtype)


def _pool(hp, obuf, row0, nb, t, tt, prev, pool_w, pool_scale, first_pos):
    ntb = t // tt
    halo = POOL_STATE + 1
    rb = row0 // tt
    if prev is None:
        prev_arr = hp
        per = tt // halo
        prev_spec = pl.BlockSpec(
            (halo, D_B), lambda b, i: (jnp.maximum((row0 // halo) + (b * ntb + i) * per - 1, 0), COL_U // D_B))
    else:
        prev_arr = prev
        prev_spec = pl.BlockSpec((None, halo, D_B), lambda b, i: (b, 0, 0))
    kern = functools.partial(_pool_kernel, tt=tt, first_pos=first_pos, zero_first=prev is None)
    return pl.pallas_call(
        kern,
        grid=(nb, ntb),
        in_specs=[prev_spec,
                  pl.BlockSpec((tt, D_B), lambda b, i: (rb + b * ntb + i, COL_U // D_B)),
                  pl.BlockSpec((len(POOL_WINDOWS), POOL_GROUP_DIM, POOL_GROUP_DIM), lambda b, i: (0, 0, 0)),
                  pl.BlockSpec((1, D_B), lambda b, i: (0, 0)),
                  pl.BlockSpec(memory_space=pl.ANY)],
        out_specs=pl.BlockSpec((tt, D_B), lambda b, i: (rb + b * ntb + i, 0)),
        out_shape=jax.ShapeDtypeStruct(obuf.shape, obuf.dtype),
        input_output_aliases={4: 0},
        compiler_params=_cparams(("parallel", "arbitrary")),
        name="pool",
    )(prev_arr, hp, pool_w, pool_scale.reshape(1, D_B), obuf)


SAFE_LOG_RANGE = 80.0


def _hgrn_kernel(cq_ref, cf_ref, ci_ref, cg_ref, lb_ref, ng_ref, s0_ref, _o_in, o_ref, sn_ref,
                 st_ref, b_ref, q_ref, k_ref, osc_ref, *, tt, blk):
    tb = pl.program_id(1)
    nh = C_HEADS
    hd = C_HEAD_DIM

    @pl.when(tb == 0)
    def _init():
        for h in range(nh):
            st_ref[h] = s0_ref[h].T

    lb = lb_ref[...]
    f = jnp.maximum(lb, LB_FLOOR) + (1.0 - lb) * jax.nn.sigmoid(cf_ref[...])
    g = jnp.log(f)
    k_ref[...] = 1.0 - f
    q_ref[...] = _silu(cq_ref[...])
    r = lax.broadcasted_iota(jnp.int32, (tt, tt), 0)
    c = lax.broadcasted_iota(jnp.int32, (tt, tt), 1)
    tri = jnp.where(jnp.logical_and(r // blk == c // blk, c <= r), 1.0, 0.0).astype(bf16)
    g1 = g.astype(bf16)
    e1 = g - g1.astype(f32)
    g2 = e1.astype(bf16)
    g3 = (e1 - g2.astype(f32)).astype(bf16)
    b = (jnp.dot(tri, g1, preferred_element_type=f32) + jnp.dot(tri, g2, preferred_element_type=f32)
         + jnp.dot(tri, g3, preferred_element_type=f32))
    b_ref[...] = b
    safe = jnp.min(b) > -SAFE_LOG_RANGE

    trow = lax.broadcasted_iota(jnp.int32, (blk, 1), 0)

    def exact_chunk(ci):
        r0 = pl.multiple_of(ci * blk, blk)
        for h in range(nh):
            rows = pl.ds(r0, blk)
            cols = slice(h * hd, (h + 1) * hd)
            bb = b_ref[rows, cols]
            qh = q_ref[rows, cols]
            kh = k_ref[rows, cols]
            ih = ci_ref[rows, cols]
            bl = bb[blk - 1:blk, :]
            ke = (kh * jnp.exp(bl - bb)).astype(bf16)
            ib = ih.astype(bf16)
            st = st_ref[h]
            o = lax.dot_general((qh * jnp.exp(bb)).astype(bf16), st.astype(bf16), NT_DIMS,
                                preferred_element_type=f32)
            for s in range(blk):
                live = trow >= s
                dec = jnp.exp(jnp.where(live, bb - bb[s:s + 1, :], 0.0))
                a_col = jnp.sum(qh * kh[s:s + 1, :] * dec, axis=-1, keepdims=True)
                o = o + jnp.where(live, a_col, 0.0) * ih[s:s + 1, :]
            osc_ref[rows, cols] = o
            st_ref[h] = st * jnp.exp(bl) + lax.dot_general(ib, ke, TN_DIMS, preferred_element_type=f32)

    @pl.when(safe)
    def _():
        same_blk = r // blk == c // blk
        blk_causal = jnp.logical_and(same_blk, c <= r)
        ones_blk = jnp.where(same_blk, 1.0, 0.0).astype(bf16)
        bl_all = (jnp.dot(ones_blk, g1, preferred_element_type=f32) + jnp.dot(ones_blk, g2, preferred_element_type=f32)
                  + jnp.dot(ones_blk, g3, preferred_element_type=f32))
        for h in range(nh):
            cols = slice(h * hd, (h + 1) * hd)
            bb = b[:, cols]
            blh = bl_all[:, cols]
            qh = q_ref[:, cols]
            ib = ci_ref[:, cols].astype(bf16)
            ke = (k_ref[:, cols] * jnp.exp(blh - bb)).astype(bf16)
            qe = (qh * jnp.exp(bb - blh)).astype(bf16)
            qb = (qh * jnp.exp(bb)).astype(bf16)
            a = jnp.where(blk_causal, lax.dot_general(qe, ke, NT_DIMS, preferred_element_type=f32), 0.0)
            o_intra = jnp.dot(a.astype(bf16), ib, preferred_element_type=f32)
            incs = [lax.dot_general(ib[c0:c0 + blk], ke[c0:c0 + blk], TN_DIMS, preferred_element_type=f32)
                    for c0 in range(0, tt, blk)]
            dec = jnp.exp(blh)
            st = st_ref[h]
            o_inter = []
            for ci, c0 in enumerate(range(0, tt, blk)):
                o_inter.append(lax.dot_general(qb[c0:c0 + blk], st.astype(bf16), NT_DIMS,
                                               preferred_element_type=f32))
                st = st * dec[c0:c0 + 1, :] + incs[ci]
            st_ref[h] = st
            osc_ref[:, cols] = o_intra + jnp.concatenate(o_inter, axis=0)

    @pl.when(jnp.logical_not(safe))
    def _():
        def body(ci, c):
            exact_chunk(ci)
            return c
        lax.fori_loop(0, tt // blk, body, 0)

    outs = []
    for h in range(nh):
        cols = slice(h * hd, (h + 1) * hd)
        outs.append(_rms(osc_ref[:, cols], ng_ref[...]))
    o_ref[...] = (jnp.concatenate(outs, axis=-1) * _silu(cg_ref[...])).astype(o_ref.dtype)

    @pl.when(tb == pl.num_programs(1) - 1)
    def _fin():
        for h in range(nh):
            sn_ref[h] = st_ref[h].T


def _hgrn(hp, obuf, row0, nb, t, tt, lb, norm_g, s0):
    ntb = t // tt
    rb = row0 // tt

    def col(cidx):
        return pl.BlockSpec((tt, D_C), lambda b, i: (rb + b * ntb + i, cidx // D_C))

    st_spec = pl.BlockSpec((None, C_HEADS, C_HEAD_DIM, C_V_DIM), lambda b, i: (b, 0, 0, 0))
    kern = functools.partial(_hgrn_kernel, tt=tt, blk=HGRN_BLOCK)
    return pl.pallas_call(
        kern,
        grid=(nb, ntb),
        in_specs=[col(COL_CQ), col(COL_CF), col(COL_CI), col(COL_CG),
                  pl.BlockSpec((1, D_C), lambda b, i: (0, 0)),
                  pl.BlockSpec((1, C_V_DIM), lambda b, i: (0, 0)),
                  st_spec,
                  pl.BlockSpec(memory_space=pl.ANY)],
        out_specs=[pl.BlockSpec((tt, D_C), lambda b, i: (rb + b * ntb + i, 0)), st_spec],
        out_shape=[jax.ShapeDtypeStruct(obuf.shape, obuf.dtype),
                   jax.ShapeDtypeStruct((nb, C_HEADS, C_HEAD_DIM, C_V_DIM), f32)],
        input_output_aliases={7: 0},
        scratch_shapes=[pltpu.VMEM((C_HEADS, C_V_DIM, C_HEAD_DIM), f32),
                        pltpu.VMEM((tt, D_C), f32), pltpu.VMEM((tt, D_C), f32),
                        pltpu.VMEM((tt, D_C), f32), pltpu.VMEM((tt, D_C), f32)],
        compiler_params=_cparams(("parallel", "arbitrary")),
        name="hgrn",
    )(hp, hp, hp, hp, lb.reshape(1, D_C), norm_g.reshape(1, C_V_DIM), s0, obuf)


def _merge_kernel(x_ref, g_ref, oa_ref, ob_ref, oc_ref, wga_ref, wgb_ref, wgc_ref,
                  pa_ref, pb_ref, pc_ref, wo_ref, o_ref, xn_ref, acc_ref):
    j = pl.program_id(1)

    @pl.when(j == 0)
    def _():
        xn_ref[...] = _rms(x_ref[...], g_ref[...]).astype(bf16)
        acc_ref[...] = jnp.zeros_like(acc_ref)

    xn = xn_ref[...]

    def branch(wg_ref, o_ref_, p_ref):
        gate = jax.nn.sigmoid(jnp.dot(xn, wg_ref[...], preferred_element_type=f32))
        return gate * jnp.dot(o_ref_[...], p_ref[...], preferred_element_type=f32)

    merged = branch(wga_ref, oa_ref, pa_ref) + branch(wgb_ref, ob_ref, pb_ref) + branch(wgc_ref, oc_ref, pc_ref)
    acc_ref[...] += jnp.dot(merged.astype(bf16), wo_ref[...], preferred_element_type=f32)

    @pl.when(j == pl.num_programs(1) - 1)
    def _():
        o_ref[...] = x_ref[...] + acc_ref[...]


def _merge(h, g, oa, ob, oc, wg, pa, pb, pc, wo, tm, tj):
    m, d = h.shape
    nj = d // tj
    row = lambda i, j: (i, 0)
    return pl.pallas_call(
        _merge_kernel,
        grid=(m // tm, nj),
        in_specs=[pl.BlockSpec((tm, d), row),
                  pl.BlockSpec((1, d), lambda i, j: (0, 0)),
                  pl.BlockSpec((tm, D_A), row),
                  pl.BlockSpec((tm, D_B), row),
                  pl.BlockSpec((tm, D_C), row),
                  pl.BlockSpec((d, tj), lambda i, j: (0, j)),
                  pl.BlockSpec((d, tj), lambda i, j: (0, nj + j)),
                  pl.BlockSpec((d, tj), lambda i, j: (0, 2 * nj + j)),
                  pl.BlockSpec((D_A, tj), lambda i, j: (0, j)),
                  pl.BlockSpec((D_B, tj), lambda i, j: (0, j)),
                  pl.BlockSpec((D_C, tj), lambda i, j: (0, j)),
                  pl.BlockSpec((tj, d), lambda i, j: (j, 0))],
        out_specs=pl.BlockSpec((tm, d), row),
        out_shape=jax.ShapeDtypeStruct((m, d), f32),
        scratch_shapes=[pltpu.VMEM((tm, d), bf16), pltpu.VMEM((tm, d), f32)],
        compiler_params=_cparams(("parallel", "arbitrary")),
        name="merge",
    )(h, g.reshape(1, d), oa, ob, oc, wg, wg, wg, pa, pb, pc, wo)


def _router_kernel(x_ref, g_ref, r_ref, o_ref):
    xn = _rms(x_ref[...], g_ref[...]).astype(bf16)
    logits = jnp.dot(xn, r_ref[...], preferred_element_type=f32)
    lane = lax.broadcasted_iota(jnp.int32, logits.shape, 1)
    logits = jnp.where(lane < N_EXPERTS, logits, -jnp.inf)
    m1 = jnp.max(logits, axis=-1, keepdims=True)
    i1 = jnp.min(jnp.where(logits == m1, lane, LANE), axis=-1, keepdims=True)
    rest = jnp.where(lane == i1, -jnp.inf, logits)
    m2 = jnp.max(rest, axis=-1, keepdims=True)
    i2 = jnp.min(jnp.where(rest == m2, lane, LANE), axis=-1, keepdims=True)
    e2 = jnp.exp(m2 - m1)
    g1 = 1.0 / (1.0 + e2)
    g2 = e2 / (1.0 + e2)
    route = jnp.where(lane == 0, i1.astype(f32), jnp.where(lane == 1, i2.astype(f32),
                      jnp.where(lane == 2, g1, jnp.where(lane == 3, g2, 0.0))))
    o_ref[...] = route[:, :N_EXPERTS]


def _router(h, g, router, tm):
    m, d = h.shape
    rp = jnp.pad(router, ((0, 0), (0, LANE - N_EXPERTS))).astype(bf16)
    return pl.pallas_call(
        _router_kernel,
        grid=(m // tm,),
        in_specs=[pl.BlockSpec((tm, d), lambda i: (i, 0)),
                  pl.BlockSpec((1, d), lambda i: (0, 0)),
                  pl.BlockSpec((d, LANE), lambda i: (0, 0))],
        out_specs=pl.BlockSpec((tm, N_EXPERTS), lambda i: (i, 0)),
        out_shape=jax.ShapeDtypeStruct((m, N_EXPERTS), f32),
        compiler_params=_cparams(("parallel",)),
        name="router",
    )(h, g.reshape(1, d), rp)


def _ffn_kernel(x_ref, g_ref, wg_ref, wu_ref, wd_ref, o_ref, xn_ref):
    @pl.when(pl.program_id(1) == 0)
    def _():
        x = x_ref[...]
        xn_ref[...] = _rms(x, g_ref[...]).astype(bf16)
        o_ref[...] = x

    xn = xn_ref[...]
    a = jnp.dot(xn, wg_ref[...], preferred_element_type=f32)
    u = jnp.dot(xn, wu_ref[...], preferred_element_type=f32)
    act = (_silu(a) * u).astype(bf16)
    o_ref[...] += jnp.dot(act, wd_ref[...], preferred_element_type=f32)


def _ffn(h, g, wg, wu, wd, tm, tf):
    m, d = h.shape
    ff = wg.shape[1]
    return pl.pallas_call(
        _ffn_kernel,
        grid=(m // tm, ff // tf),
        in_specs=[pl.BlockSpec((tm, d), lambda i, f: (i, 0)),
                  pl.BlockSpec((1, d), lambda i, f: (0, 0)),
                  pl.BlockSpec((d, tf), lambda i, f: (0, f)),
                  pl.BlockSpec((d, tf), lambda i, f: (0, f)),
                  pl.BlockSpec((tf, d), lambda i, f: (f, 0))],
        out_specs=pl.BlockSpec((tm, d), lambda i, f: (i, 0)),
        out_shape=jax.ShapeDtypeStruct((m, d), f32),
        scratch_shapes=[pltpu.VMEM((tm, d), bf16)],
        compiler_params=_cparams(("parallel", "arbitrary")),
        name="ffn",
    )(h, g.reshape(1, d), wg, wu, wd)


MOE_ROWS = 1024
MOE_SUB = 256
DMA_UNROLL = 8


def _moe_plan(route, m):
    n = 2 * m
    n_groups = -(-n // MOE_ROWS) + N_EXPERTS
    ef = route[:, 0:2].astype(jnp.int32).T.reshape(n)
    row_dst = jnp.argsort(ef, stable=True).astype(jnp.int32)
    row_tok = jnp.where(row_dst >= m, row_dst - m, row_dst)
    counts = jnp.sum((ef[:, None] == jnp.arange(N_EXPERTS, dtype=jnp.int32)[None, :]).astype(jnp.int32), axis=0)
    first = jnp.cumsum(counts) - counts
    per_e = (counts + MOE_ROWS - 1) // MOE_ROWS
    size = (counts + jnp.maximum(per_e, 1) - 1) // jnp.maximum(per_e, 1)
    size = (size + DMA_UNROLL - 1) // DMA_UNROLL * DMA_UNROLL
    ends = jnp.cumsum(per_e)
    gi = jnp.arange(n_groups, dtype=jnp.int32)
    ge = jnp.minimum(jnp.sum((gi[:, None] >= ends[None, :]).astype(jnp.int32), axis=1), N_EXPERTS - 1)
    j = gi - (ends - per_e)[ge]
    g_start = first[ge] + j * size[ge]
    g_cnt = jnp.clip(counts[ge] - j * size[ge], 0, size[ge])
    g_cnt = jnp.where(gi < ends[-1], g_cnt, 0)
    last_e = ge[jnp.maximum(ends[-1] - 1, 0)]
    ge = jnp.where(gi < ends[-1], ge, last_e)
    return ge, g_start.astype(jnp.int32), g_cnt.astype(jnp.int32), row_tok, row_dst


def _moe_kernel(ge_ref, gs_ref, gc_ref, tok_ref, dst_ref, h_hbm, g_ref, wg_ref, wu_ref, wd_ref, y_hbm,
                xs_ref, acc_ref, spare_ref, gsem, ssem):
    s = pl.program_id(0)
    fi = pl.program_id(1)
    nf = pl.num_programs(1)
    cnt = gc_ref[s]
    start = gs_ref[s]
    n_rows = tok_ref.shape[0]
    nsub = (cnt + MOE_SUB - 1) // MOE_SUB

    @pl.when(cnt > 0)
    def _active():
        @pl.when(fi == 0)
        def _gather():
            nrow = nsub * MOE_SUB

            def issue(r8, c):
                for u in range(DMA_UNROLL):
                    r = r8 * DMA_UNROLL + u
                    tok = tok_ref[jnp.minimum(start + r, n_rows - 1)]
                    pltpu.make_async_copy(h_hbm.at[pl.ds(tok, 1)], acc_ref.at[pl.ds(r, 1)], gsem).start()
                return c
            lax.fori_loop(0, nrow // DMA_UNROLL, issue, 0)

            def wait_tile(j, c):
                pltpu.make_async_copy(h_hbm.at[pl.ds(0, MOE_SUB)], acc_ref.at[pl.ds(0, MOE_SUB)], gsem).wait()
                return c
            lax.fori_loop(0, nsub, wait_tile, 0)

            def norm(j, c):
                rows = pl.ds(pl.multiple_of(j * MOE_SUB, MOE_SUB), MOE_SUB)
                xs_ref[rows, :] = _rms(acc_ref[rows, :], g_ref[...]).astype(bf16)
                acc_ref[rows, :] = jnp.zeros((MOE_SUB, acc_ref.shape[1]), f32)
                return c
            lax.fori_loop(0, nsub, norm, 0)

        def body(j, c):
            rows = pl.ds(pl.multiple_of(j * MOE_SUB, MOE_SUB), MOE_SUB)
            x = xs_ref[rows, :]
            a = jnp.dot(x, wg_ref[...].astype(bf16), preferred_element_type=f32)
            u = jnp.dot(x, wu_ref[...].astype(bf16), preferred_element_type=f32)
            act = (_silu(a) * u).astype(bf16)
            acc_ref[rows, :] += jnp.dot(act, wd_ref[...].astype(bf16), preferred_element_type=f32)
            return c
        lax.fori_loop(0, nsub, body, 0)

        @pl.when(fi == nf - 1)
        def _scatter():
            def issue_one(r):
                pltpu.make_async_copy(acc_ref.at[pl.ds(r, 1)], y_hbm.at[pl.ds(dst_ref[start + r], 1)], ssem).start()

            def issue(r8, c):
                for u in range(DMA_UNROLL):
                    issue_one(r8 * DMA_UNROLL + u)
                return c
            n8 = cnt // DMA_UNROLL
            lax.fori_loop(0, n8, issue, 0)

            def issue_tail(r, c):
                issue_one(r)
                return c
            lax.fori_loop(n8 * DMA_UNROLL, cnt, issue_tail, 0)

            def issue_spare(r, c):
                pltpu.make_async_copy(acc_ref.at[pl.ds(r, 1)], spare_ref.at[pl.ds(r - cnt, 1)], ssem).start()
                return c
            lax.fori_loop(cnt, nsub * MOE_SUB, issue_spare, 0)

            def wait_tile(j, c):
                pltpu.make_async_copy(acc_ref.at[pl.ds(0, MOE_SUB)], y_hbm.at[pl.ds(0, MOE_SUB)], ssem).wait()
                return c
            lax.fori_loop(0, nsub, wait_tile, 0)


def _moe(h, g, route, wg, wu, wd, tf):
    m, d = h.shape
    ne, _, ff = wg.shape
    nf = ff // tf
    ge, gs, gc, row_tok, row_dst = _moe_plan(route, m)
    n_groups = ge.shape[0]

    def wcol(s, f, ge_r, gs_r, gc_r, tok_r, dst_r):
        return (ge_r[s], 0, jnp.where(gc_r[s] > 0, f, nf - 1))

    def wrow(s, f, ge_r, gs_r, gc_r, tok_r, dst_r):
        return (ge_r[s], jnp.where(gc_r[s] > 0, f, nf - 1), 0)

    grid_spec = pltpu.PrefetchScalarGridSpec(
        num_scalar_prefetch=5,
        grid=(n_groups, nf),
        in_specs=[pl.BlockSpec(memory_space=pl.ANY),
                  pl.BlockSpec((1, d), lambda s, f, *_: (0, 0)),
                  pl.BlockSpec((None, d, tf), wcol),
                  pl.BlockSpec((None, d, tf), wcol),
                  pl.BlockSpec((None, tf, d), wrow)],
        out_specs=pl.BlockSpec(memory_space=pl.ANY),
        scratch_shapes=[pltpu.VMEM((MOE_ROWS, d), bf16), pltpu.VMEM((MOE_ROWS, d), f32),
                        pltpu.VMEM((MOE_SUB, d), f32),
                        pltpu.SemaphoreType.DMA(()), pltpu.SemaphoreType.DMA(())],
    )
    return pl.pallas_call(
        _moe_kernel,
        grid_spec=grid_spec,
        out_shape=jax.ShapeDtypeStruct((2 * m, d), f32),
        compiler_params=_cparams(("arbitrary", "arbitrary")),
        name="moe",
    )(ge, gs, gc, row_tok, row_dst, h, g.reshape(1, d), wg, wu, wd)


def _combine_kernel(x_ref, r_ref, y1_ref, y2_ref, o_ref):
    r = r_ref[...]
    o_ref[...] = x_ref[...] + (r[:, 2:3] * y1_ref[...] + r[:, 3:4] * y2_ref[...])


def _combine(h, route, y, tm):
    m, d = h.shape
    nb = m // tm
    return pl.pallas_call(
        _combine_kernel,
        grid=(nb,),
        in_specs=[pl.BlockSpec((tm, d), lambda i: (i, 0)),
                  pl.BlockSpec((tm, N_EXPERTS), lambda i: (i, 0)),
                  pl.BlockSpec((tm, d), lambda i: (i, 0)),
                  pl.BlockSpec((tm, d), lambda i: (nb + i, 0))],
        out_specs=pl.BlockSpec((tm, d), lambda i: (i, 0)),
        out_shape=jax.ShapeDtypeStruct((m, d), f32),
        compiler_params=_cparams(("parallel",)),
        name="moe_combine",
    )(h, route, y, y)


def _final_norm_kernel(x_ref, g_ref, o_ref):
    o_ref[...] = _rms(x_ref[...], g_ref[...])


def _final_norm(h, g, row0, rows, tm):
    d = h.shape[1]
    rb = row0 // tm
    return pl.pallas_call(
        _final_norm_kernel,
        grid=(rows // tm,),
        in_specs=[pl.BlockSpec((tm, d), lambda i: (rb + i, 0)),
                  pl.BlockSpec((1, d), lambda i: (0, 0))],
        out_specs=pl.BlockSpec((tm, d), lambda i: (i, 0)),
        out_shape=jax.ShapeDtypeStruct((rows, d), f32),
        compiler_params=_cparams(("parallel",)),
        name="final_norm",
    )(h, g.reshape(1, d))


def _pack_kernel(w_ref, wp_ref, wk_ref, wv_ref, wg_ref):
    o = np.cumsum((0,) + IN_SIZES)
    w = w_ref[...]
    rows = w.shape[0]
    cut = lambda a, b: w[:, int(o[a]):int(o[b])].astype(bf16)
    pad = lambda a: jnp.concatenate([a, jnp.zeros((rows, LANE - a.shape[1]), bf16)], axis=1)
    wp_ref[...] = jnp.concatenate([cut(0, 1), cut(3, 4), cut(6, 7), cut(7, 11)], axis=1)
    wk_ref[...] = jnp.concatenate([cut(1, 2), pad(cut(4, 5)), pad(cut(5, 6))], axis=1)
    wv_ref[...] = cut(2, 3)
    wg_ref[...] = cut(11, 12)


def _pack_in_proj(w, layer, tr):
    _, d, n = w.shape
    widths = (N_PACK, D_A + 2 * LANE, D_A, N_BRANCH * D_MODEL)
    return pl.pallas_call(
        _pack_kernel,
        grid=(d // tr,),
        in_specs=[pl.BlockSpec((None, tr, n), lambda i: (layer, i, 0))],
        out_specs=[pl.BlockSpec((tr, c), lambda i: (i, 0)) for c in widths],
        out_shape=[jax.ShapeDtypeStruct((d, c), bf16) for c in widths],
        compiler_params=_cparams(("parallel",)),
        name="pack_w_in",
    )(w)


def kernel(x_prompt, x_sample, cache_k, cache_v, cache_kidx, cache_pool, state_hgrn, norm_mix_g, norm_ffn_g, final_norm_g, w_in, w_proj_a, w_proj_b, w_proj_c, w_out, pool_w, pool_scale, hgrn_lb_logits, hgrn_norm_g, ffn_w_gate, ffn_w_up, ffn_w_down, moe_router, moe_w_gate, moe_w_up, moe_w_down):
    batch, seq, d = x_prompt.shape
    nb, nq, _ = x_sample.shape
    depth = w_in.shape[0]
    past = cache_k.shape[2]
    mp = batch * seq
    ms = nb * nq
    m = mp + ms

    lb_soft = jax.nn.softmax(hgrn_lb_logits.astype(f32), axis=0)
    lb_all = jnp.cumsum(lb_soft, axis=0) - lb_soft[0:1]

    h = jnp.concatenate([x_prompt.reshape(mp, d), x_sample.reshape(ms, d)], axis=0)
    s0_prompt = jnp.zeros((batch, C_HEADS, C_HEAD_DIM, C_V_DIM), f32)
    tm = 768 if m % 768 == 0 else 256
    tmh = tm // 2

    kp = jnp.zeros((depth * mp, D_A), f32)
    vp = jnp.zeros((depth * mp, D_A), f32)
    ks = jnp.zeros((depth * ms, D_A), f32)
    vs = jnp.zeros((depth * ms, D_A), f32)
    tkv = ATT_BLOCK
    outs = [[] for _ in range(6)]
    for l in range(depth):
        wp, wk, wv, wgates = _pack_in_proj(w_in, l, 128)
        hp = _inproj(h, norm_mix_g[l], wp, tm, 1024)
        kp, ks, kb16, kiw = _inproj_k(h, norm_mix_g[l], wk, kp, ks, l, mp, tkv)
        vp, vs, vt16 = _inproj_v(h, norm_mix_g[l], wv, vp, vs, l, mp, tkv)

        oa = _dsa_prompt(hp, kb16, vt16, kiw, jnp.zeros((m, D_A), bf16), batch, seq)
        oa = _dsa_sample(hp, ks, vs, kiw, oa, l, mp, nb, nq, cache_k[l], cache_v[l], cache_kidx[l])
        ob = _pool(hp, jnp.zeros((m, D_B), bf16), 0, batch, seq, 256, None, pool_w[l], pool_scale[l], 0)
        prev = jnp.pad(cache_pool[l], ((0, 0), (1, 0), (0, 0)))
        ob = _pool(hp, ob, mp, nb, nq, nq, prev, pool_w[l], pool_scale[l], past)
        oc, st_p = _hgrn(hp, jnp.zeros((m, D_C), bf16), 0, batch, seq, 256, lb_all[l], hgrn_norm_g[l], s0_prompt)
        oc, st_s = _hgrn(hp, oc, mp, nb, nq, nq, lb_all[l], hgrn_norm_g[l], state_hgrn[l])

        h = _merge(h, norm_mix_g[l], oa, ob, oc, wgates, w_proj_a[l].astype(bf16), w_proj_b[l].astype(bf16),
                   w_proj_c[l].astype(bf16), w_out[l].astype(bf16), tmh, 512)

        ki = kiw[:, KIW_KI:KIW_KI + IDX_DIM]
        uu = hp[:, COL_U:COL_U + D_B]
        outs[0].append(ki[:mp].reshape(batch, seq, IDX_DIM))
        outs[1].append(uu[:mp].reshape(batch, seq, D_B)[:, -POOL_STATE:])
        outs[2].append(st_p)
        outs[3].append(ki[mp:].reshape(nb, nq, IDX_DIM))
        u_ext = jnp.concatenate([cache_pool[l], uu[mp:].reshape(nb, nq, D_B)], axis=1)
        outs[4].append(u_ext[:, -POOL_STATE:])
        outs[5].append(st_s)

        j = l // 2
        if l % 2 == 0:
            h = _ffn(h, norm_ffn_g[l], ffn_w_gate[j].astype(bf16), ffn_w_up[j].astype(bf16),
                     ffn_w_down[j].astype(bf16), tm, 512)
        else:
            route = _router(h, norm_ffn_g[l], moe_router[j], tm)
            y = _moe(h, norm_ffn_g[l], route, moe_w_gate[j], moe_w_up[j], moe_w_down[j], 512)
            h = _combine(h, route, y, tmh)

    y_prompt = _final_norm(h, final_norm_g, 0, mp, 256).reshape(batch, seq, d)
    y_sample = _final_norm(h, final_norm_g, mp, ms, 256).reshape(nb, nq, d)
    kidx_p, pool_p, st_p, kidx_s, pool_s, st_s = (jnp.stack(o) for o in outs)
    return (y_prompt, y_sample,
            kp.reshape(depth, batch, seq, A_HEADS, A_HEAD_DIM), vp.reshape(depth, batch, seq, A_HEADS, A_HEAD_DIM),
            kidx_p, pool_p, st_p,
            ks.reshape(depth, nb, nq, A_HEADS, A_HEAD_DIM), vs.reshape(depth, nb, nq, A_HEADS, A_HEAD_DIM),
            kidx_s, pool_s, st_s)
```

```python
import functools

import numpy as np
import jax
import jax.numpy as jnp
from jax import lax
from jax.experimental import pallas as pl
from jax.experimental.pallas import tpu as pltpu

f32 = jnp.float32
bf16 = jnp.bfloat16

D_MODEL = 2048
CHUNK = 64
A_HEADS = 8
A_HEAD_DIM = 128
D_A = A_HEADS * A_HEAD_DIM
IDX_HEADS = 8
IDX_DIM = 64
TOPK_MAX = 256
POOL_WINDOWS = (2, 4, 8, 16)
POOL_GROUP_DIM = 128
D_B = len(POOL_WINDOWS) * POOL_GROUP_DIM
POOL_STATE = max(POOL_WINDOWS) - 1
C_HEADS = 4
C_HEAD_DIM = 128
C_V_DIM = 128
D_C = C_HEADS * C_HEAD_DIM
HGRN_BLOCK = 16
N_BRANCH = 3
N_EXPERTS = 8
IN_SIZES = (D_A, D_A, D_A, IDX_HEADS * IDX_DIM, IDX_DIM, IDX_HEADS,
            D_B, D_C, D_C, C_HEADS * C_V_DIM, C_HEADS * C_V_DIM, N_BRANCH * D_MODEL)
EPS = 1e-6
NEG = -1e30
LB_FLOOR = 1e-30
INT_MIN = -2 ** 31

LOG2E = 1.4426950408889634
LANE = 128
COUNT_ROWS = 64
ATT_BLOCK = 256
COL_Q = 0
COL_QI = D_A
COL_U = COL_QI + 512
COL_CQ = COL_U + 512
COL_CF = COL_CQ + 512
COL_CI = COL_CF + 512
COL_CG = COL_CI + 512
N_PACK = COL_CG + 512
KIW_KI, KIW_W = 0, LANE

VMEM_LIMIT = 56 * 1024 * 1024

NT_DIMS = (((1,), (1,)), ((), ()))
TN_DIMS = (((0,), (0,)), ((), ()))


def _cparams(sem):
    return pltpu.CompilerParams(dimension_semantics=sem, vmem_limit_bytes=VMEM_LIMIT)


def _rms(x, g):
    ms = jnp.mean(x * x, axis=-1, keepdims=True)
    return x * lax.rsqrt(ms + EPS) * g


def _silu(x):
    return x * jax.nn.sigmoid(x)


def _sort_key(x):
    bits = pltpu.bitcast(x, jnp.int32)
    return bits ^ ((bits >> 31) & jnp.int32(0x7FFFFFFF))


def _kth_largest(count_ge, kk):
    zero = jnp.zeros(kk.shape, jnp.int32)
    t0 = jnp.where(count_ge(zero) >= kk, zero, jnp.int32(INT_MIN))

    def bit_body(i, t):
        c = t | lax.shift_left(jnp.int32(1), 30 - i)
        return jnp.where(count_ge(c) >= kk, c, t)

    return lax.fori_loop(0, 31, bit_body, t0)


def _inproj_kernel(x_ref, g_ref, w_ref, o_ref, xn_ref):
    @pl.when(pl.program_id(1) == 0)
    def _():
        xn_ref[...] = _rms(x_ref[...], g_ref[...]).astype(bf16)

    o_ref[...] = jnp.dot(xn_ref[...], w_ref[...], preferred_element_type=f32)


def _inproj(h, g, wp, tm, tn):
    m, d = h.shape
    n = wp.shape[1]
    return pl.pallas_call(
        _inproj_kernel,
        grid=(m // tm, n // tn),
        in_specs=[pl.BlockSpec((tm, d), lambda i, j: (i, 0)),
                  pl.BlockSpec((1, d), lambda i, j: (0, 0)),
                  pl.BlockSpec((d, tn), lambda i, j: (0, j))],
        out_specs=pl.BlockSpec((tm, tn), lambda i, j: (i, j)),
        out_shape=jax.ShapeDtypeStruct((m, n), f32),
        scratch_shapes=[pltpu.VMEM((tm, d), bf16)],
        compiler_params=_cparams(("parallel", "arbitrary")),
        name="inproj",
    )(h, g.reshape(1, d), wp)


def _stacked_specs(layer, n_p, n_s, tm):
    prompt = pl.BlockSpec((tm, D_A), lambda i: (layer * n_p + jnp.minimum(i, n_p - 1), 0))
    sample = pl.BlockSpec((tm, D_A), lambda i: (layer * n_s + jnp.maximum(i - n_p, 0), 0))
    return prompt, sample


def _inproj_k_kernel(x_ref, g_ref, w_ref, kp_in, ks_in, kp_ref, ks_ref, kb_ref, kiw_ref, *, n_p):
    del kp_in, ks_in
    xn = _rms(x_ref[...], g_ref[...]).astype(bf16)
    r = jnp.dot(xn, w_ref[...], preferred_element_type=f32)
    k = r[:, :D_A]

    @pl.when(pl.program_id(0) < n_p)
    def _():
        kp_ref[...] = k

    @pl.when(pl.program_id(0) >= n_p)
    def _():
        ks_ref[...] = k

    kb_ref[...] = k.astype(bf16)
    kiw_ref[...] = r[:, D_A:]


def _inproj_k(h, g, wk, kp, ks, layer, mp, tm):
    m, d = h.shape
    n = wk.shape[1]
    n_p, n_s = mp // tm, (m - mp) // tm
    row = lambda i: (i, 0)
    anyspec = pl.BlockSpec(memory_space=pl.ANY)
    return pl.pallas_call(
        functools.partial(_inproj_k_kernel, n_p=n_p),
        grid=(m // tm,),
        in_specs=[pl.BlockSpec((tm, d), row),
                  pl.BlockSpec((1, d), lambda i: (0, 0)),
                  pl.BlockSpec((d, n), lambda i: (0, 0)),
                  anyspec, anyspec],
        out_specs=[*_stacked_specs(layer, n_p, n_s, tm),
                   pl.BlockSpec((tm, D_A), row), pl.BlockSpec((tm, n - D_A), row)],
        out_shape=[jax.ShapeDtypeStruct(kp.shape, f32), jax.ShapeDtypeStruct(ks.shape, f32),
                   jax.ShapeDtypeStruct((m, D_A), bf16), jax.ShapeDtypeStruct((m, n - D_A), f32)],
        input_output_aliases={3: 0, 4: 1},
        compiler_params=_cparams(("arbitrary",)),
        name="inproj_k",
    )(h, g.reshape(1, d), wk, kp, ks)


def _inproj_v_kernel(x_ref, g_ref, w_ref, vp_in, vs_in, vp_ref, vs_ref, vt_ref, *, n_p):
    del vp_in, vs_in
    xn = _rms(x_ref[...], g_ref[...]).astype(bf16)
    r = jnp.dot(xn, w_ref[...], preferred_element_type=f32)

    @pl.when(pl.program_id(0) < n_p)
    def _():
        vp_ref[...] = r

    @pl.when(pl.program_id(0) >= n_p)
    def _():
        vs_ref[...] = r

    for c in range(vt_ref.shape[0]):
        for hh in range(A_HEADS):
            blk = r[c * ATT_BLOCK:(c + 1) * ATT_BLOCK, hh * A_HEAD_DIM:(hh + 1) * A_HEAD_DIM]
            vt_ref[c, hh] = blk.T.astype(bf16)


def _inproj_v(h, g, wv, vp, vs, layer, mp, tm):
    m, d = h.shape
    n_p, n_s = mp // tm, (m - mp) // tm
    nblk = tm // ATT_BLOCK
    anyspec = pl.BlockSpec(memory_space=pl.ANY)
    return pl.pallas_call(
        functools.partial(_inproj_v_kernel, n_p=n_p),
        grid=(m // tm,),
        in_specs=[pl.BlockSpec((tm, d), lambda i: (i, 0)),
                  pl.BlockSpec((1, d), lambda i: (0, 0)),
                  pl.BlockSpec((d, D_A), lambda i: (0, 0)),
                  anyspec, anyspec],
        out_specs=[*_stacked_specs(layer, n_p, n_s, tm),
                   pl.BlockSpec((nblk, A_HEADS, A_HEAD_DIM, ATT_BLOCK), lambda i: (i, 0, 0, 0))],
        out_shape=[jax.ShapeDtypeStruct(vp.shape, f32), jax.ShapeDtypeStruct(vs.shape, f32),
                   jax.ShapeDtypeStruct((m // ATT_BLOCK, A_HEADS, A_HEAD_DIM, ATT_BLOCK), bf16)],
        input_output_aliases={3: 0, 4: 1},
        compiler_params=_cparams(("arbitrary",)),
        name="inproj_v",
    )(h, g.reshape(1, d), wv, vp, vs)


def _idx_scores(qi, w, kib):
    acc = jnp.zeros((qi.shape[0], kib.shape[0]), f32)
    for hh in range(IDX_HEADS):
        d = lax.dot_general(qi[:, hh * IDX_DIM:(hh + 1) * IDX_DIM], kib, NT_DIMS,
                            preferred_element_type=f32)
        acc = acc + w[:, hh:hh + 1] * jnp.maximum(d, 0.0)
    return acc


def _fold_lanes(x):
    part = x[:, :LANE]
    for j in range(1, x.shape[1] // LANE):
        part = part + x[:, j * LANE:(j + 1) * LANE]
    return part


def _tri_incl(n):
    r = lax.broadcasted_iota(jnp.int32, (n, n), 0)
    c = lax.broadcasted_iota(jnp.int32, (n, n), 1)
    return jnp.where(r <= c, 1.0, 0.0).astype(bf16)


def _dsa_prompt_kernel(q_ref, k_ref, vt_ref, qi_ref, ki_ref, w_ref, _o_in, o_ref, keys_ref, bias_ref, qs_ref,
                       *acc_refs, tq, topk):
    qb = pl.program_id(1)
    nkb = qb + 1
    hd = A_HEAD_DIM

    qi = qi_ref[...].astype(bf16)
    wt = (w_ref[...] * (IDX_HEADS ** -0.5 * IDX_DIM ** -0.5)).T
    srow = lax.broadcasted_iota(jnp.int32, (tq, tq), 0)
    qcol = lax.broadcasted_iota(jnp.int32, (tq, tq), 1)
    diag_adm = (srow // CHUNK) <= (qcol // CHUNK)

    def score_body(kb, c):
        off = pl.multiple_of(kb * tq, tq)
        kib = ki_ref[pl.ds(off, tq), :][:, :IDX_DIM].astype(bf16)
        ds = [lax.dot_general(kib, qi[:, hh * IDX_DIM:(hh + 1) * IDX_DIM], NT_DIMS, preferred_element_type=f32)
              for hh in range(IDX_HEADS)]
        acc = jnp.zeros((tq, tq), f32)
        for hh in range(IDX_HEADS):
            acc = acc + wt[hh:hh + 1, :] * jnp.maximum(ds[hh], 0.0)
        adm = jnp.logical_or(kb < qb, diag_adm)
        keys_ref[kb] = jnp.where(adm, _sort_key(acc), jnp.int32(INT_MIN))
        return c

    lax.fori_loop(0, nkb, score_body, 0)

    def count(cmp, c):
        cb = jnp.broadcast_to(c, (COUNT_ROWS, tq))

        def body(kb, acc):
            for r0 in range(0, tq, COUNT_ROWS):
                acc = acc + jnp.where(cmp(keys_ref[kb, r0:r0 + COUNT_ROWS, :], cb), 1.0, 0.0)
            return acc
        acc = lax.fori_loop(0, nkb, body, jnp.zeros((COUNT_ROWS, tq), f32))
        return jnp.sum(acc, axis=0, keepdims=True)

    lane = lax.broadcasted_iota(jnp.int32, (1, tq), 1)
    n_adm = ((qb * tq + lane) // CHUNK + 1) * CHUNK
    kk = jnp.minimum(topk, n_adm).astype(f32)
    thr = _kth_largest(lambda c: count(lambda k, t: k >= t, c), kk)
    n_ge = count(lambda k, t: k >= t, thr)
    has_tie = jnp.max(n_ge - kk) > 0.0

    @pl.when(jnp.logical_not(has_tie))
    def _():
        def body(kb, c):
            bias_ref[kb] = jnp.where(keys_ref[kb] >= thr, 0.0, NEG)
            return c
        lax.fori_loop(0, nkb, body, 0)

    @pl.when(has_tie)
    def _():
        need = kk - count(lambda k, t: k > t, thr)
        tri = jnp.where(qcol <= srow, 1.0, 0.0).astype(bf16)

        def body(kb, seen):
            k = keys_ref[kb]
            eq = jnp.where(k == thr, 1.0, 0.0)
            rank = seen + jnp.dot(tri, eq.astype(bf16), preferred_element_type=f32)
            take = jnp.logical_or(k > thr, jnp.logical_and(k == thr, rank <= need))
            bias_ref[kb] = jnp.where(take, 0.0, NEG)
            return seen + jnp.sum(eq, axis=0, keepdims=True)
        lax.fori_loop(0, nkb, body, jnp.zeros((1, tq), f32))

    qs_ref[...] = (q_ref[...] * (hd ** -0.5 * LOG2E)).astype(bf16)
    for h in range(A_HEADS):
        acc_refs[h][...] = jnp.zeros((hd, tq), f32)

    def att_body(kb, carry):
        ms, ls = carry
        off = pl.multiple_of(kb * tq, tq)
        kblk = k_ref[pl.ds(off, tq), :]
        bias = bias_ref[kb]
        sts = []
        for h in range(A_HEADS):
            cols = slice(h * hd, (h + 1) * hd)
            sts.append(lax.dot_general(kblk[:, cols], qs_ref[:, cols], NT_DIMS, preferred_element_type=f32))
        new_ms, new_ls, ps, corrs = [], [], [], []
        for h in range(A_HEADS):
            st = sts[h] + bias
            m_new = jnp.maximum(ms[h], jnp.max(st, axis=0, keepdims=True))
            p = jnp.exp2(st - m_new)
            corr = jnp.exp2(ms[h] - m_new)
            new_ls.append(corr * ls[h] + jnp.sum(p, axis=0, keepdims=True))
            new_ms.append(m_new)
            ps.append(p.astype(bf16))
            corrs.append(corr)
        for h in range(A_HEADS):
            acc_refs[h][...] = corrs[h] * acc_refs[h][...] + jnp.dot(vt_ref[kb, h], ps[h],
                                                                     preferred_element_type=f32)
        return tuple(new_ms), tuple(new_ls)

    init = (tuple(jnp.full((1, tq), NEG, f32) for _ in range(A_HEADS)),
            tuple(jnp.zeros((1, tq), f32) for _ in range(A_HEADS)))
    _, ls = lax.fori_loop(0, nkb, att_body, init)
    for h in range(A_HEADS):
        o_ref[:, h * hd:(h + 1) * hd] = (acc_refs[h][...] / ls[h]).T.astype(o_ref.dtype)


def _dsa_prompt(hp, kb16, vt16, kiw, obuf, batch, seq):
    tq = ATT_BLOCK
    nq = seq // tq
    topk = min(TOPK_MAX, seq // 4)
    kern = functools.partial(_dsa_prompt_kernel, tq=tq, topk=topk)
    once = pl.Buffered(1)
    return pl.pallas_call(
        kern,
        grid=(batch, nq),
        in_specs=[
            pl.BlockSpec((tq, D_A), lambda b, i: (b * nq + i, COL_Q // D_A)),
            pl.BlockSpec((seq, D_A), lambda b, i: (b, 0), pipeline_mode=once),
            pl.BlockSpec((nq, A_HEADS, A_HEAD_DIM, tq), lambda b, i: (b, 0, 0, 0), pipeline_mode=once),
            pl.BlockSpec((tq, 512), lambda b, i: (b * nq + i, COL_QI // 512)),
            pl.BlockSpec((seq, LANE), lambda b, i: (b, KIW_KI // LANE), pipeline_mode=once),
            pl.BlockSpec((tq, LANE), lambda b, i: (b * nq + i, KIW_W // LANE)),
            pl.BlockSpec(memory_space=pl.ANY),
        ],
        out_specs=pl.BlockSpec((tq, D_A), lambda b, i: (b * nq + i, 0)),
        out_shape=jax.ShapeDtypeStruct(obuf.shape, obuf.dtype),
        input_output_aliases={6: 0},
        scratch_shapes=[pltpu.VMEM((nq, tq, tq), jnp.int32), pltpu.VMEM((nq, tq, tq), f32),
                        pltpu.VMEM((tq, D_A), bf16)]
                       + [pltpu.VMEM((A_HEAD_DIM, tq), f32) for _ in range(A_HEADS)],
        compiler_params=_cparams(("parallel", "arbitrary")),
        name="dsa_prompt",
    )(hp, kb16, vt16, hp, kiw, kiw, obuf)


def _dsa_sample_kernel(q_ref, kn_ref, vn_ref, qi_ref, kin_ref, w_ref, kc_ref, vc_ref, kic_ref, _o_in, o_ref,
                       biasc_ref, biasn_ref, *, topk):
    nq = q_ref.shape[0]
    past = kc_ref.shape[0]

    def _select():
        qi = qi_ref[...].astype(bf16)
        w = w_ref[...][:, :IDX_HEADS] * (IDX_HEADS ** -0.5 * IDX_DIM ** -0.5)
        key_c = _sort_key(_idx_scores(qi, w, kic_ref[...].astype(bf16)))
        key_n = _sort_key(_idx_scores(qi, w, kin_ref[...][:, :IDX_DIM].astype(bf16)))

        def count(pred):
            cc = jnp.sum(_fold_lanes(jnp.where(pred(key_c), 1.0, 0.0)), axis=-1, keepdims=True)
            return cc + jnp.sum(jnp.where(pred(key_n), 1.0, 0.0), axis=-1, keepdims=True)

        kk = jnp.full((nq, 1), float(topk), f32)
        thr = _kth_largest(lambda c: count(lambda k: k >= c), kk)
        n_ge = count(lambda k: k >= thr)
        has_tie = jnp.max(n_ge - kk) > 0.0

        @pl.when(jnp.logical_not(has_tie))
        def _():
            biasc_ref[...] = jnp.where(key_c >= thr, 0.0, NEG)
            biasn_ref[...] = jnp.where(key_n >= thr, 0.0, NEG)

        @pl.when(has_tie)
        def _():
            need = kk - count(lambda k: k > thr)
            tri = _tri_incl(LANE)
            seen = jnp.zeros((nq, 1), f32)
            for j in range(past // LANE):
                k = key_c[:, j * LANE:(j + 1) * LANE]
                eq = jnp.where(k == thr, 1.0, 0.0)
                rank = seen + jnp.dot(eq.astype(bf16), tri, preferred_element_type=f32)
                take = jnp.logical_or(k > thr, jnp.logical_and(k == thr, rank <= need))
                biasc_ref[:, j * LANE:(j + 1) * LANE] = jnp.where(take, 0.0, NEG)
                seen = seen + jnp.sum(eq, axis=-1, keepdims=True)
            eq = jnp.where(key_n == thr, 1.0, 0.0)
            rank = seen + jnp.dot(eq.astype(bf16), _tri_incl(nq), preferred_element_type=f32)
            take = jnp.logical_or(key_n > thr, jnp.logical_and(key_n == thr, rank <= need))
            biasn_ref[...] = jnp.where(take, 0.0, NEG)

    _select()

    hd = A_HEAD_DIM
    q = (q_ref[...] * (hd ** -0.5 * LOG2E)).astype(bf16)
    bias_c = biasc_ref[...]
    bias_n = biasn_ref[...]
    scs, sns = [], []
    for h in range(A_HEADS):
        cols = slice(h * hd, (h + 1) * hd)
        scs.append(lax.dot_general(q[:, cols], kc_ref[:, cols].astype(bf16), NT_DIMS, preferred_element_type=f32))
        sns.append(lax.dot_general(q[:, cols], kn_ref[:, cols].astype(bf16), NT_DIMS, preferred_element_type=f32))
    pcs, pns, ls = [], [], []
    for h in range(A_HEADS):
        s_c = scs[h] + bias_c
        s_n = sns[h] + bias_n
        m = jnp.maximum(jnp.max(s_c, axis=-1, keepdims=True), jnp.max(s_n, axis=-1, keepdims=True))
        p_c = jnp.exp2(s_c - m)
        p_n = jnp.exp2(s_n - m)
        ls.append(jnp.sum(p_c, axis=-1, keepdims=True) + jnp.sum(p_n, axis=-1, keepdims=True))
        pcs.append(p_c.astype(bf16))
        pns.append(p_n.astype(bf16))
    for h in range(A_HEADS):
        cols = slice(h * hd, (h + 1) * hd)
        acc = jnp.dot(pcs[h], vc_ref[:, cols].astype(bf16), preferred_element_type=f32)
        acc = acc + jnp.dot(pns[h], vn_ref[:, cols].astype(bf16), preferred_element_type=f32)
        o_ref[:, cols] = (acc / ls[h]).astype(o_ref.dtype)


def _dsa_sample(hp, ks, vs, kiw, obuf, layer, row0, nb, nq, cache_k, cache_v, cache_kidx):
    past = cache_k.shape[1]
    topk = min(TOPK_MAX, (past + nq) // 4)
    kc = cache_k.reshape(nb, past, D_A)
    vc = cache_v.reshape(nb, past, D_A)
    rb = row0 // nq
    kern = functools.partial(_dsa_sample_kernel, topk=topk)
    return pl.pallas_call(
        kern,
        grid=(nb,),
        in_specs=[
            pl.BlockSpec((nq, D_A), lambda b: (rb + b, COL_Q // D_A)),
            pl.BlockSpec((nq, D_A), lambda b: (layer * nb + b, 0)),
            pl.BlockSpec((nq, D_A), lambda b: (layer * nb + b, 0)),
            pl.BlockSpec((nq, 512), lambda b: (rb + b, COL_QI // 512)),
            pl.BlockSpec((nq, LANE), lambda b: (rb + b, KIW_KI // LANE)),
            pl.BlockSpec((nq, LANE), lambda b: (rb + b, KIW_W // LANE)),
            pl.BlockSpec((None, past, D_A), lambda b: (b, 0, 0)),
            pl.BlockSpec((None, past, D_A), lambda b: (b, 0, 0)),
            pl.BlockSpec((None, past, IDX_DIM), lambda b: (b, 0, 0)),
            pl.BlockSpec(memory_space=pl.ANY),
        ],
        out_specs=pl.BlockSpec((nq, D_A), lambda b: (rb + b, 0)),
        out_shape=jax.ShapeDtypeStruct(obuf.shape, obuf.dtype),
        input_output_aliases={9: 0},
        scratch_shapes=[pltpu.VMEM((nq, past), f32), pltpu.VMEM((nq, nq), f32)],
        compiler_params=_cparams(("parallel",)),
        name="dsa_sample",
    )(hp, ks, vs, hp, kiw, kiw, kc, vc, cache_kidx, obuf)


def _pool_kernel(prev_ref, u_ref, pw_ref, ps_ref, _o_in, o_ref, *, tt, first_pos, zero_first):
    tb = pl.program_id(1)
    halo = POOL_STATE + 1
    prev = prev_ref[...]
    if zero_first:
        prev = jnp.where(tb == 0, 0.0, prev)
    cur = u_ref[...]
    ext = jnp.concatenate([prev, cur], axis=0)
    pos = first_pos + tb * tt + lax.broadcasted_iota(jnp.int32, (tt, 1), 0)
    outs = []
    for gi, wdw in enumerate(POOL_WINDOWS):
        cols = slice(gi * POOL_GROUP_DIM, (gi + 1) * POOL_GROUP_DIM)
        a = ext[:, cols]
        n = 1
        while n < wdw:
            a = a[n:] + a[:-n]
            n *= 2
        s = a[halo + 1 - wdw: halo + 1 - wdw + tt]
        cnt = jnp.minimum(pos + 1, wdw).astype(f32)
        d = s / cnt - cur[:, cols]
        outs.append(jnp.dot(d.astype(bf16), pw_ref[gi].astype(bf16), preferred_element_type=f32))
    o = jnp.concatenate(outs, axis=-1) * ps_ref[...]
    o_ref[...] = o.astype(o_ref.dtype)


def _pool(hp, obuf, row0, nb, t, tt, prev, pool_w, pool_scale, first_pos):
    ntb = t // tt
    halo = POOL_STATE + 1
    rb = row0 // tt
    if prev is None:
        prev_arr = hp
        per = tt // halo
        prev_spec = pl.BlockSpec(
            (halo, D_B), lambda b, i: (jnp.maximum((row0 // halo) + (b * ntb + i) * per - 1, 0), COL_U // D_B))
    else:
        prev_arr = prev
        prev_spec = pl.BlockSpec((None, halo, D_B), lambda b, i: (b, 0, 0))
    kern = functools.partial(_pool_kernel, tt=tt, first_pos=first_pos, zero_first=prev is None)
    return pl.pallas_call(
        kern,
        grid=(nb, ntb),
        in_specs=[prev_spec,
                  pl.BlockSpec((tt, D_B), lambda b, i: (rb + b * ntb + i, COL_U // D_B)),
                  pl.BlockSpec((len(POOL_WINDOWS), POOL_GROUP_DIM, POOL_GROUP_DIM), lambda b, i: (0, 0, 0)),
                  pl.BlockSpec((1, D_B), lambda b, i: (0, 0)),
                  pl.BlockSpec(memory_space=pl.ANY)],
        out_specs=pl.BlockSpec((tt, D_B), lambda b, i: (rb + b * ntb + i, 0)),
        out_shape=jax.ShapeDtypeStruct(obuf.shape, obuf.dtype),
        input_output_aliases={4: 0},
        compiler_params=_cparams(("parallel", "arbitrary")),
        name="pool",
    )(prev_arr, hp, pool_w, pool_scale.reshape(1, D_B), obuf)


SAFE_LOG_RANGE = 80.0


def _hgrn_kernel(cq_ref, cf_ref, ci_ref, cg_ref, lb_ref, ng_ref, s0_ref, _o_in, o_ref, sn_ref,
                 st_ref, b_ref, q_ref, k_ref, osc_ref, *, tt, blk):
    tb = pl.program_id(1)
    nh = C_HEADS
    hd = C_HEAD_DIM

    @pl.when(tb == 0)
    def _init():
        for h in range(nh):
            st_ref[h] = s0_ref[h].T

    lb = lb_ref[...]
    f = jnp.maximum(lb, LB_FLOOR) + (1.0 - lb) * jax.nn.sigmoid(cf_ref[...])
    g = jnp.log(f)
    k_ref[...] = 1.0 - f
    q_ref[...] = _silu(cq_ref[...])
    r = lax.broadcasted_iota(jnp.int32, (tt, tt), 0)
    c = lax.broadcasted_iota(jnp.int32, (tt, tt), 1)
    tri = jnp.where(jnp.logical_and(r // blk == c // blk, c <= r), 1.0, 0.0).astype(bf16)
    g1 = g.astype(bf16)
    e1 = g - g1.astype(f32)
    g2 = e1.astype(bf16)
    g3 = (e1 - g2.astype(f32)).astype(bf16)
    b = (jnp.dot(tri, g1, preferred_element_type=f32) + jnp.dot(tri, g2, preferred_element_type=f32)
         + jnp.dot(tri, g3, preferred_element_type=f32))
    b_ref[...] = b
    safe = jnp.min(b) > -SAFE_LOG_RANGE

    trow = lax.broadcasted_iota(jnp.int32, (blk, 1), 0)

    def exact_chunk(ci):
        r0 = pl.multiple_of(ci * blk, blk)
        for h in range(nh):
            rows = pl.ds(r0, blk)
            cols = slice(h * hd, (h + 1) * hd)
            bb = b_ref[rows, cols]
            qh = q_ref[rows, cols]
            kh = k_ref[rows, cols]
            ih = ci_ref[rows, cols]
            bl = bb[blk - 1:blk, :]
            ke = (kh * jnp.exp(bl - bb)).astype(bf16)
            ib = ih.astype(bf16)
            st = st_ref[h]
            o = lax.dot_general((qh * jnp.exp(bb)).astype(bf16), st.astype(bf16), NT_DIMS,
                                preferred_element_type=f32)
            for s in range(blk):
                live = trow >= s
                dec = jnp.exp(jnp.where(live, bb - bb[s:s + 1, :], 0.0))
                a_col = jnp.sum(qh * kh[s:s + 1, :] * dec, axis=-1, keepdims=True)
                o = o + jnp.where(live, a_col, 0.0) * ih[s:s + 1, :]
            osc_ref[rows, cols] = o
            st_ref[h] = st * jnp.exp(bl) + lax.dot_general(ib, ke, TN_DIMS, preferred_element_type=f32)

    @pl.when(safe)
    def _():
        same_blk = r // blk == c // blk
        blk_causal = jnp.logical_and(same_blk, c <= r)
        ones_blk = jnp.where(same_blk, 1.0, 0.0).astype(bf16)
        bl_all = (jnp.dot(ones_blk, g1, preferred_element_type=f32) + jnp.dot(ones_blk, g2, preferred_element_type=f32)
                  + jnp.dot(ones_blk, g3, preferred_element_type=f32))
        for h in range(nh):
            cols = slice(h * hd, (h + 1) * hd)
            bb = b[:, cols]
            blh = bl_all[:, cols]
            qh = q_ref[:, cols]
            ib = ci_ref[:, cols].astype(bf16)
            ke = (k_ref[:, cols] * jnp.exp(blh - bb)).astype(bf16)
            qe = (qh * jnp.exp(bb - blh)).astype(bf16)
            qb = (qh * jnp.exp(bb)).astype(bf16)
            a = jnp.where(blk_causal, lax.dot_general(qe, ke, NT_DIMS, preferred_element_type=f32), 0.0)
            o_intra = jnp.dot(a.astype(bf16), ib, preferred_element_type=f32)
            incs = [lax.dot_general(ib[c0:c0 + blk], ke[c0:c0 + blk], TN_DIMS, preferred_element_type=f32)
                    for c0 in range(0, tt, blk)]
            dec = jnp.exp(blh)
            st = st_ref[h]
            o_inter = []
            for ci, c0 in enumerate(range(0, tt, blk)):
                o_inter.append(lax.dot_general(qb[c0:c0 + blk], st.astype(bf16), NT_DIMS,
                                               preferred_element_type=f32))
                st = st * dec[c0:c0 + 1, :] + incs[ci]
            st_ref[h] = st
            osc_ref[:, cols] = o_intra + jnp.concatenate(o_inter, axis=0)

    @pl.when(jnp.logical_not(safe))
    def _():
        def body(ci, c):
            exact_chunk(ci)
            return c
        lax.fori_loop(0, tt // blk, body, 0)

    outs = []
    for h in range(nh):
        cols = slice(h * hd, (h + 1) * hd)
        outs.append(_rms(osc_ref[:, cols], ng_ref[...]))
    o_ref[...] = (jnp.concatenate(outs, axis=-1) * _silu(cg_ref[...])).astype(o_ref.dtype)

    @pl.when(tb == pl.num_programs(1) - 1)
    def _fin():
        for h in range(nh):
            sn_ref[h] = st_ref[h].T


def _hgrn(hp, obuf, row0, nb, t, tt, lb, norm_g, s0):
    ntb = t // tt
    rb = row0 // tt

    def col(cidx):
        return pl.BlockSpec((tt, D_C), lambda b, i: (rb + b * ntb + i, cidx // D_C))

    st_spec = pl.BlockSpec((None, C_HEADS, C_HEAD_DIM, C_V_DIM), lambda b, i: (b, 0, 0, 0))
    kern = functools.partial(_hgrn_kernel, tt=tt, blk=HGRN_BLOCK)
    return pl.pallas_call(
        kern,
        grid=(nb, ntb),
        in_specs=[col(COL_CQ), col(COL_CF), col(COL_CI), col(COL_CG),
                  pl.BlockSpec((1, D_C), lambda b, i: (0, 0)),
                  pl.BlockSpec((1, C_V_DIM), lambda b, i: (0, 0)),
                  st_spec,
                  pl.BlockSpec(memory_space=pl.ANY)],
        out_specs=[pl.BlockSpec((tt, D_C), lambda b, i: (rb + b * ntb + i, 0)), st_spec],
        out_shape=[jax.ShapeDtypeStruct(obuf.shape, obuf.dtype),
                   jax.ShapeDtypeStruct((nb, C_HEADS, C_HEAD_DIM, C_V_DIM), f32)],
        input_output_aliases={7: 0},
        scratch_shapes=[pltpu.VMEM((C_HEADS, C_V_DIM, C_HEAD_DIM), f32),
                        pltpu.VMEM((tt, D_C), f32), pltpu.VMEM((tt, D_C), f32),
                        pltpu.VMEM((tt, D_C), f32), pltpu.VMEM((tt, D_C), f32)],
        compiler_params=_cparams(("parallel", "arbitrary")),
        name="hgrn",
    )(hp, hp, hp, hp, lb.reshape(1, D_C), norm_g.reshape(1, C_V_DIM), s0, obuf)


def _merge_kernel(x_ref, g_ref, oa_ref, ob_ref, oc_ref, wga_ref, wgb_ref, wgc_ref,
                  pa_ref, pb_ref, pc_ref, wo_ref, o_ref, xn_ref, acc_ref):
    j = pl.program_id(1)

    @pl.when(j == 0)
    def _():
        xn_ref[...] = _rms(x_ref[...], g_ref[...]).astype(bf16)
        acc_ref[...] = jnp.zeros_like(acc_ref)

    xn = xn_ref[...]

    def branch(wg_ref, o_ref_, p_ref):
        gate = jax.nn.sigmoid(jnp.dot(xn, wg_ref[...], preferred_element_type=f32))
        return gate * jnp.dot(o_ref_[...], p_ref[...], preferred_element_type=f32)

    merged = branch(wga_ref, oa_ref, pa_ref) + branch(wgb_ref, ob_ref, pb_ref) + branch(wgc_ref, oc_ref, pc_ref)
    acc_ref[...] += jnp.dot(merged.astype(bf16), wo_ref[...], preferred_element_type=f32)

    @pl.when(j == pl.num_programs(1) - 1)
    def _():
        o_ref[...] = x_ref[...] + acc_ref[...]


def _merge(h, g, oa, ob, oc, wg, pa, pb, pc, wo, tm, tj):
    m, d = h.shape
    nj = d // tj
    row = lambda i, j: (i, 0)
    return pl.pallas_call(
        _merge_kernel,
        grid=(m // tm, nj),
        in_specs=[pl.BlockSpec((tm, d), row),
                  pl.BlockSpec((1, d), lambda i, j: (0, 0)),
                  pl.BlockSpec((tm, D_A), row),
                  pl.BlockSpec((tm, D_B), row),
                  pl.BlockSpec((tm, D_C), row),
                  pl.BlockSpec((d, tj), lambda i, j: (0, j)),
                  pl.BlockSpec((d, tj), lambda i, j: (0, nj + j)),
                  pl.BlockSpec((d, tj), lambda i, j: (0, 2 * nj + j)),
                  pl.BlockSpec((D_A, tj), lambda i, j: (0, j)),
                  pl.BlockSpec((D_B, tj), lambda i, j: (0, j)),
                  pl.BlockSpec((D_C, tj), lambda i, j: (0, j)),
                  pl.BlockSpec((tj, d), lambda i, j: (j, 0))],
        out_specs=pl.BlockSpec((tm, d), row),
        out_shape=jax.ShapeDtypeStruct((m, d), f32),
        scratch_shapes=[pltpu.VMEM((tm, d), bf16), pltpu.VMEM((tm, d), f32)],
        compiler_params=_cparams(("parallel", "arbitrary")),
        name="merge",
    )(h, g.reshape(1, d), oa, ob, oc, wg, wg, wg, pa, pb, pc, wo)


def _router_kernel(x_ref, g_ref, r_ref, o_ref):
    xn = _rms(x_ref[...], g_ref[...]).astype(bf16)
    logits = jnp.dot(xn, r_ref[...], preferred_element_type=f32)
    lane = lax.broadcasted_iota(jnp.int32, logits.shape, 1)
    logits = jnp.where(lane < N_EXPERTS, logits, -jnp.inf)
    m1 = jnp.max(logits, axis=-1, keepdims=True)
    i1 = jnp.min(jnp.where(logits == m1, lane, LANE), axis=-1, keepdims=True)
    rest = jnp.where(lane == i1, -jnp.inf, logits)
    m2 = jnp.max(rest, axis=-1, keepdims=True)
    i2 = jnp.min(jnp.where(rest == m2, lane, LANE), axis=-1, keepdims=True)
    e2 = jnp.exp(m2 - m1)
    g1 = 1.0 / (1.0 + e2)
    g2 = e2 / (1.0 + e2)
    route = jnp.where(lane == 0, i1.astype(f32), jnp.where(lane == 1, i2.astype(f32),
                      jnp.where(lane == 2, g1, jnp.where(lane == 3, g2, 0.0))))
    o_ref[...] = route[:, :N_EXPERTS]


def _router(h, g, router, tm):
    m, d = h.shape
    rp = jnp.pad(router, ((0, 0), (0, LANE - N_EXPERTS))).astype(bf16)
    return pl.pallas_call(
        _router_kernel,
        grid=(m // tm,),
        in_specs=[pl.BlockSpec((tm, d), lambda i: (i, 0)),
                  pl.BlockSpec((1, d), lambda i: (0, 0)),
                  pl.BlockSpec((d, LANE), lambda i: (0, 0))],
        out_specs=pl.BlockSpec((tm, N_EXPERTS), lambda i: (i, 0)),
        out_shape=jax.ShapeDtypeStruct((m, N_EXPERTS), f32),
        compiler_params=_cparams(("parallel",)),
        name="router",
    )(h, g.reshape(1, d), rp)


def _ffn_kernel(x_ref, g_ref, wg_ref, wu_ref, wd_ref, o_ref, xn_ref):
    @pl.when(pl.program_id(1) == 0)
    def _():
        x = x_ref[...]
        xn_ref[...] = _rms(x, g_ref[...]).astype(bf16)
        o_ref[...] = x

    xn = xn_ref[...]
    a = jnp.dot(xn, wg_ref[...], preferred_element_type=f32)
    u = jnp.dot(xn, wu_ref[...], preferred_element_type=f32)
    act = (_silu(a) * u).astype(bf16)
    o_ref[...] += jnp.dot(act, wd_ref[...], preferred_element_type=f32)


def _ffn(h, g, wg, wu, wd, tm, tf):
    m, d = h.shape
    ff = wg.shape[1]
    return pl.pallas_call(
        _ffn_kernel,
        grid=(m // tm, ff // tf),
        in_specs=[pl.BlockSpec((tm, d), lambda i, f: (i, 0)),
                  pl.BlockSpec((1, d), lambda i, f: (0, 0)),
                  pl.BlockSpec((d, tf), lambda i, f: (0, f)),
                  pl.BlockSpec((d, tf), lambda i, f: (0, f)),
                  pl.BlockSpec((tf, d), lambda i, f: (f, 0))],
        out_specs=pl.BlockSpec((tm, d), lambda i, f: (i, 0)),
        out_shape=jax.ShapeDtypeStruct((m, d), f32),
        scratch_shapes=[pltpu.VMEM((tm, d), bf16)],
        compiler_params=_cparams(("parallel", "arbitrary")),
        name="ffn",
    )(h, g.reshape(1, d), wg, wu, wd)


MOE_ROWS = 1024
MOE_SUB = 256
DMA_UNROLL = 8


def _moe_plan(route, m):
    n = 2 * m
    n_groups = -(-n // MOE_ROWS) + N_EXPERTS
    ef = route[:, 0:2].astype(jnp.int32).T.reshape(n)
    row_dst = jnp.argsort(ef, stable=True).astype(jnp.int32)
    row_tok = jnp.where(row_dst >= m, row_dst - m, row_dst)
    counts = jnp.sum((ef[:, None] == jnp.arange(N_EXPERTS, dtype=jnp.int32)[None, :]).astype(jnp.int32), axis=0)
    first = jnp.cumsum(counts) - counts
    per_e = (counts + MOE_ROWS - 1) // MOE_ROWS
    size = (counts + jnp.maximum(per_e, 1) - 1) // jnp.maximum(per_e, 1)
    size = (size + DMA_UNROLL - 1) // DMA_UNROLL * DMA_UNROLL
    ends = jnp.cumsum(per_e)
    gi = jnp.arange(n_groups, dtype=jnp.int32)
    ge = jnp.minimum(jnp.sum((gi[:, None] >= ends[None, :]).astype(jnp.int32), axis=1), N_EXPERTS - 1)
    j = gi - (ends - per_e)[ge]
    g_start = first[ge] + j * size[ge]
    g_cnt = jnp.clip(counts[ge] - j * size[ge], 0, size[ge])
    g_cnt = jnp.where(gi < ends[-1], g_cnt, 0)
    last_e = ge[jnp.maximum(ends[-1] - 1, 0)]
    ge = jnp.where(gi < ends[-1], ge, last_e)
    return ge, g_start.astype(jnp.int32), g_cnt.astype(jnp.int32), row_tok, row_dst


def _moe_kernel(ge_ref, gs_ref, gc_ref, tok_ref, dst_ref, h_hbm, g_ref, wg_ref, wu_ref, wd_ref, y_hbm,
                xs_ref, acc_ref, stage_ref, spare_ref, gsem, ssem, *, nf):
    s = pl.program_id(0)
    fi = pl.program_id(1)
    cnt = gc_ref[s]
    start = gs_ref[s]
    n_rows = tok_ref.shape[0]
    nsub = (cnt + MOE_SUB - 1) // MOE_SUB
    chunk = stage_ref.shape[0] // nf

    def gather_chunk(first_row, base):
        for u in range(chunk):
            tok = tok_ref[jnp.minimum(first_row + base + u, n_rows - 1)]
            pltpu.make_async_copy(h_hbm.at[pl.ds(tok, 1)], stage_ref.at[pl.ds(base + u, 1)], gsem).start()

    def row_tile(j):
        rows = pl.ds(pl.multiple_of(j * MOE_SUB, MOE_SUB), MOE_SUB)
        x = xs_ref[rows, :]
        a = jnp.dot(x, wg_ref[...].astype(bf16), preferred_element_type=f32)
        u = jnp.dot(x, wu_ref[...].astype(bf16), preferred_element_type=f32)
        act = (_silu(a) * u).astype(bf16)
        acc_ref[rows, :] += jnp.dot(act, wd_ref[...].astype(bf16), preferred_element_type=f32)

    @pl.when(cnt > 0)
    def _active():
        @pl.when(fi == 0)
        def _stage_in():
            @pl.when(s == 0)
            def _():
                def issue(c, carry):
                    gather_chunk(start, c * chunk)
                    return carry
                lax.fori_loop(0, nf, issue, 0)

            def wait_chunk(c, carry):
                pltpu.make_async_copy(h_hbm.at[pl.ds(0, chunk)], stage_ref.at[pl.ds(0, chunk)], gsem).wait()
                return carry
            lax.fori_loop(0, nf, wait_chunk, 0)

            def norm(j, c):
                rows = pl.ds(pl.multiple_of(j * MOE_SUB, MOE_SUB), MOE_SUB)
                xs_ref[rows, :] = _rms(stage_ref[rows, :], g_ref[...]).astype(bf16)
                acc_ref[rows, :] = jnp.zeros((MOE_SUB, acc_ref.shape[1]), f32)
                return c
            lax.fori_loop(0, nsub, norm, 0)

        nxt = jnp.minimum(s + 1, pl.num_programs(0) - 1)
        prefetch = jnp.logical_and(s + 1 < pl.num_programs(0), gc_ref[nxt] > 0)

        @pl.when(prefetch)
        def _():
            gather_chunk(gs_ref[nxt], fi * chunk)
            row_tile(0)

        @pl.when(jnp.logical_not(prefetch))
        def _():
            row_tile(0)

        def body(j, c):
            row_tile(j)
            return c
        lax.fori_loop(1, nsub, body, 0)

        @pl.when(fi == nf - 1)
        def _scatter():
            def issue_one(r):
                pltpu.make_async_copy(acc_ref.at[pl.ds(r, 1)], y_hbm.at[pl.ds(dst_ref[start + r], 1)], ssem).start()

            def issue(r8, c):
                for u in range(DMA_UNROLL):
                    issue_one(r8 * DMA_UNROLL + u)
                return c
            n8 = cnt // DMA_UNROLL
            lax.fori_loop(0, n8, issue, 0)

            def issue_tail(r, c):
                issue_one(r)
                return c
            lax.fori_loop(n8 * DMA_UNROLL, cnt, issue_tail, 0)

            def issue_spare(r, c):
                pltpu.make_async_copy(acc_ref.at[pl.ds(r, 1)], spare_ref.at[pl.ds(r - cnt, 1)], ssem).start()
                return c
            lax.fori_loop(cnt, nsub * MOE_SUB, issue_spare, 0)

            def wait_tile(j, c):
                pltpu.make_async_copy(acc_ref.at[pl.ds(0, MOE_SUB)], y_hbm.at[pl.ds(0, MOE_SUB)], ssem).wait()
                return c
            lax.fori_loop(0, nsub, wait_tile, 0)


def _moe(h, g, route, wg, wu, wd, tf):
    m, d = h.shape
    ne, _, ff = wg.shape
    nf = ff // tf
    chunk = -(-MOE_ROWS // nf)
    chunk = -(-chunk // DMA_UNROLL) * DMA_UNROLL
    ge, gs, gc, row_tok, row_dst = _moe_plan(route, m)
    n_groups = ge.shape[0]

    def wcol(s, f, ge_r, gs_r, gc_r, tok_r, dst_r):
        return (ge_r[s], 0, jnp.where(gc_r[s] > 0, f, nf - 1))

    def wrow(s, f, ge_r, gs_r, gc_r, tok_r, dst_r):
        return (ge_r[s], jnp.where(gc_r[s] > 0, f, nf - 1), 0)

    grid_spec = pltpu.PrefetchScalarGridSpec(
        num_scalar_prefetch=5,
        grid=(n_groups, nf),
        in_specs=[pl.BlockSpec(memory_space=pl.ANY),
                  pl.BlockSpec((1, d), lambda s, f, *_: (0, 0)),
                  pl.BlockSpec((None, d, tf), wcol),
                  pl.BlockSpec((None, d, tf), wcol),
                  pl.BlockSpec((None, tf, d), wrow)],
        out_specs=pl.BlockSpec(memory_space=pl.ANY),
        scratch_shapes=[pltpu.VMEM((MOE_ROWS, d), bf16), pltpu.VMEM((MOE_ROWS, d), f32),
                        pltpu.VMEM((chunk * nf, d), f32), pltpu.VMEM((MOE_SUB, d), f32),
                        pltpu.SemaphoreType.DMA(()), pltpu.SemaphoreType.DMA(())],
    )
    return pl.pallas_call(
        functools.partial(_moe_kernel, nf=nf),
        grid_spec=grid_spec,
        out_shape=jax.ShapeDtypeStruct((2 * m, d), f32),
        compiler_params=_cparams(("arbitrary", "arbitrary")),
        name="moe",
    )(ge, gs, gc, row_tok, row_dst, h, g.reshape(1, d), wg, wu, wd)


def _combine_kernel(x_ref, r_ref, y1_ref, y2_ref, o_ref):
    r = r_ref[...]
    o_ref[...] = x_ref[...] + (r[:, 2:3] * y1_ref[...] + r[:, 3:4] * y2_ref[...])


def _combine(h, route, y, tm):
    m, d = h.shape
    nb = m // tm
    return pl.pallas_call(
        _combine_kernel,
        grid=(nb,),
        in_specs=[pl.BlockSpec((tm, d), lambda i: (i, 0)),
                  pl.BlockSpec((tm, N_EXPERTS), lambda i: (i, 0)),
                  pl.BlockSpec((tm, d), lambda i: (i, 0)),
                  pl.BlockSpec((tm, d), lambda i: (nb + i, 0))],
        out_specs=pl.BlockSpec((tm, d), lambda i: (i, 0)),
        out_shape=jax.ShapeDtypeStruct((m, d), f32),
        compiler_params=_cparams(("parallel",)),
        name="moe_combine",
    )(h, route, y, y)


def _combine_norm_kernel(x_ref, r_ref, y1_ref, y2_ref, g_ref, op_ref, os_ref, *, n_p):
    r = r_ref[...]
    v = _rms(x_ref[...] + (r[:, 2:3] * y1_ref[...] + r[:, 3:4] * y2_ref[...]), g_ref[...])

    @pl.when(pl.program_id(0) < n_p)
    def _():
        op_ref[...] = v

    @pl.when(pl.program_id(0) >= n_p)
    def _():
        os_ref[...] = v


def _combine_norm(h, route, y, g, mp, tm):
    m, d = h.shape
    nb = m // tm
    n_p = mp // tm
    return pl.pallas_call(
        functools.partial(_combine_norm_kernel, n_p=n_p),
        grid=(nb,),
        in_specs=[pl.BlockSpec((tm, d), lambda i: (i, 0)),
                  pl.BlockSpec((tm, N_EXPERTS), lambda i: (i, 0)),
                  pl.BlockSpec((tm, d), lambda i: (i, 0)),
                  pl.BlockSpec((tm, d), lambda i: (nb + i, 0)),
                  pl.BlockSpec((1, d), lambda i: (0, 0))],
        out_specs=[pl.BlockSpec((tm, d), lambda i: (jnp.minimum(i, n_p - 1), 0)),
                   pl.BlockSpec((tm, d), lambda i: (jnp.maximum(i - n_p, 0), 0))],
        out_shape=[jax.ShapeDtypeStruct((mp, d), f32), jax.ShapeDtypeStruct((m - mp, d), f32)],
        compiler_params=_cparams(("arbitrary",)),
        name="moe_combine_norm",
    )(h, route, y, y, g.reshape(1, d))


def _final_norm_kernel(x_ref, g_ref, o_ref):
    o_ref[...] = _rms(x_ref[...], g_ref[...])


def _final_norm(h, g, row0, rows, tm):
    d = h.shape[1]
    rb = row0 // tm
    return pl.pallas_call(
        _final_norm_kernel,
        grid=(rows // tm,),
        in_specs=[pl.BlockSpec((tm, d), lambda i: (rb + i, 0)),
                  pl.BlockSpec((1, d), lambda i: (0, 0))],
        out_specs=pl.BlockSpec((tm, d), lambda i: (i, 0)),
        out_shape=jax.ShapeDtypeStruct((rows, d), f32),
        compiler_params=_cparams(("parallel",)),
        name="final_norm",
    )(h, g.reshape(1, d))


def _pack_kernel(w_ref, wp_ref, wk_ref, wv_ref, wg_ref):
    o = np.cumsum((0,) + IN_SIZES)
    w = w_ref[...]
    rows = w.shape[0]
    cut = lambda a, b: w[:, int(o[a]):int(o[b])].astype(bf16)
    pad = lambda a: jnp.concatenate([a, jnp.zeros((rows, LANE - a.shape[1]), bf16)], axis=1)
    wp_ref[...] = jnp.concatenate([cut(0, 1), cut(3, 4), cut(6, 7), cut(7, 11)], axis=1)
    wk_ref[...] = jnp.concatenate([cut(1, 2), pad(cut(4, 5)), pad(cut(5, 6))], axis=1)
    wv_ref[...] = cut(2, 3)
    wg_ref[...] = cut(11, 12)


def _pack_in_proj(w, layer, tr):
    _, d, n = w.shape
    widths = (N_PACK, D_A + 2 * LANE, D_A, N_BRANCH * D_MODEL)
    return pl.pallas_call(
        _pack_kernel,
        grid=(d // tr,),
        in_specs=[pl.BlockSpec((None, tr, n), lambda i: (layer, i, 0))],
        out_specs=[pl.BlockSpec((tr, c), lambda i: (i, 0)) for c in widths],
        out_shape=[jax.ShapeDtypeStruct((d, c), bf16) for c in widths],
        compiler_params=_cparams(("parallel",)),
        name="pack_w_in",
    )(w)


def kernel(x_prompt, x_sample, cache_k, cache_v, cache_kidx, cache_pool, state_hgrn, norm_mix_g, norm_ffn_g, final_norm_g, w_in, w_proj_a, w_proj_b, w_proj_c, w_out, pool_w, pool_scale, hgrn_lb_logits, hgrn_norm_g, ffn_w_gate, ffn_w_up, ffn_w_down, moe_router, moe_w_gate, moe_w_up, moe_w_down):
    batch, seq, d = x_prompt.shape
    nb, nq, _ = x_sample.shape
    depth = w_in.shape[0]
    past = cache_k.shape[2]
    mp = batch * seq
    ms = nb * nq
    m = mp + ms

    lb_soft = jax.nn.softmax(hgrn_lb_logits.astype(f32), axis=0)
    lb_all = jnp.cumsum(lb_soft, axis=0) - lb_soft[0:1]

    h = jnp.concatenate([x_prompt.reshape(mp, d), x_sample.reshape(ms, d)], axis=0)
    s0_prompt = jnp.zeros((batch, C_HEADS, C_HEAD_DIM, C_V_DIM), f32)
    tm = 768 if m % 768 == 0 else 256
    tmh = tm // 2

    kp = jnp.zeros((depth * mp, D_A), f32)
    vp = jnp.zeros((depth * mp, D_A), f32)
    ks = jnp.zeros((depth * ms, D_A), f32)
    vs = jnp.zeros((depth * ms, D_A), f32)
    tkv = ATT_BLOCK
    outs = [[] for _ in range(6)]
    for l in range(depth):
        wp, wk, wv, wgates = _pack_in_proj(w_in, l, 128)
        hp = _inproj(h, norm_mix_g[l], wp, tm, 1024)
        kp, ks, kb16, kiw = _inproj_k(h, norm_mix_g[l], wk, kp, ks, l, mp, tkv)
        vp, vs, vt16 = _inproj_v(h, norm_mix_g[l], wv, vp, vs, l, mp, tkv)

        oa = _dsa_prompt(hp, kb16, vt16, kiw, jnp.zeros((m, D_A), bf16), batch, seq)
        oa = _dsa_sample(hp, ks, vs, kiw, oa, l, mp, nb, nq, cache_k[l], cache_v[l], cache_kidx[l])
        ob = _pool(hp, jnp.zeros((m, D_B), bf16), 0, batch, seq, 256, None, pool_w[l], pool_scale[l], 0)
        prev = jnp.pad(cache_pool[l], ((0, 0), (1, 0), (0, 0)))
        ob = _pool(hp, ob, mp, nb, nq, nq, prev, pool_w[l], pool_scale[l], past)
        oc, st_p = _hgrn(hp, jnp.zeros((m, D_C), bf16), 0, batch, seq, 256, lb_all[l], hgrn_norm_g[l], s0_prompt)
        oc, st_s = _hgrn(hp, oc, mp, nb, nq, nq, lb_all[l], hgrn_norm_g[l], state_hgrn[l])

        h = _merge(h, norm_mix_g[l], oa, ob, oc, wgates, w_proj_a[l].astype(bf16), w_proj_b[l].astype(bf16),
                   w_proj_c[l].astype(bf16), w_out[l].astype(bf16), tmh, 512)

        ki = kiw[:, KIW_KI:KIW_KI + IDX_DIM]
        uu = hp[:, COL_U:COL_U + D_B]
        outs[0].append(ki[:mp].reshape(batch, seq, IDX_DIM))
        outs[1].append(uu[:mp].reshape(batch, seq, D_B)[:, -POOL_STATE:])
        outs[2].append(st_p)
        outs[3].append(ki[mp:].reshape(nb, nq, IDX_DIM))
        u_ext = jnp.concatenate([cache_pool[l], uu[mp:].reshape(nb, nq, D_B)], axis=1)
        outs[4].append(u_ext[:, -POOL_STATE:])
        outs[5].append(st_s)

        j = l // 2
        if l % 2 == 0:
            h = _ffn(h, norm_ffn_g[l], ffn_w_gate[j].astype(bf16), ffn_w_up[j].astype(bf16),
                     ffn_w_down[j].astype(bf16), tm, 512)
        else:
            route = _router(h, norm_ffn_g[l], moe_router[j], tm)
            y = _moe(h, norm_ffn_g[l], route, moe_w_gate[j], moe_w_up[j], moe_w_down[j], 512)
            if l == depth - 1:
                y_prompt, y_sample = _combine_norm(h, route, y, final_norm_g, mp, tkv)
                h = None
            else:
                h = _combine(h, route, y, tmh)

    if h is not None:
        y_prompt = _final_norm(h, final_norm_g, 0, mp, 256)
        y_sample = _final_norm(h, final_norm_g, mp, ms, 256)
    y_prompt = y_prompt.reshape(batch, seq, d)
    y_sample = y_sample.reshape(nb, nq, d)
    kidx_p, pool_p, st_p, kidx_s, pool_s, st_s = (jnp.stack(o) for o in outs)
    return (y_prompt, y_sample,
            kp.reshape(depth, batch, seq, A_HEADS, A_HEAD_DIM), vp.reshape(depth, batch, seq, A_HEADS, A_HEAD_DIM),
            kidx_p, pool_p, st_p,
            ks.reshape(depth, nb, nq, A_HEADS, A_HEAD_DIM), vs.reshape(depth, nb, nq, A_HEADS, A_HEAD_DIM),
            kidx_s, pool_s, st_s)
```

```python
import functools

import numpy as np
import jax
import jax.numpy as jnp
from jax import lax
from jax.experimental import pallas as pl
from jax.experimental.pallas import tpu as pltpu

f32 = jnp.float32
bf16 = jnp.bfloat16

D_MODEL = 2048
CHUNK = 64
A_HEADS = 8
A_HEAD_DIM = 128
D_A = A_HEADS * A_HEAD_DIM
IDX_HEADS = 8
IDX_DIM = 64
TOPK_MAX = 256
POOL_WINDOWS = (2, 4, 8, 16)
POOL_GROUP_DIM = 128
D_B = len(POOL_WINDOWS) * POOL_GROUP_DIM
POOL_STATE = max(POOL_WINDOWS) - 1
C_HEADS = 4
C_HEAD_DIM = 128
C_V_DIM = 128
D_C = C_HEADS * C_HEAD_DIM
HGRN_BLOCK = 16
N_BRANCH = 3
N_EXPERTS = 8
IN_SIZES = (D_A, D_A, D_A, IDX_HEADS * IDX_DIM, IDX_DIM, IDX_HEADS,
            D_B, D_C, D_C, C_HEADS * C_V_DIM, C_HEADS * C_V_DIM, N_BRANCH * D_MODEL)
EPS = 1e-6
NEG = -1e30
LB_FLOOR = 1e-30
INT_MIN = -2 ** 31

LOG2E = 1.4426950408889634
LANE = 128
COUNT_ROWS = 64
ATT_BLOCK = 256
COL_Q = 0
COL_QI = D_A
COL_U = COL_QI + 512
COL_CQ = COL_U + 512
COL_CF = COL_CQ + 512
COL_CI = COL_CF + 512
COL_CG = COL_CI + 512
N_PACK = COL_CG + 512
KIW_KI, KIW_W = 0, LANE

VMEM_LIMIT = 56 * 1024 * 1024

NT_DIMS = (((1,), (1,)), ((), ()))
TN_DIMS = (((0,), (0,)), ((), ()))


def _cparams(sem):
    return pltpu.CompilerParams(dimension_semantics=sem, vmem_limit_bytes=VMEM_LIMIT)


def _rms(x, g):
    ms = jnp.mean(x * x, axis=-1, keepdims=True)
    return x * lax.rsqrt(ms + EPS) * g


def _silu(x):
    return x * jax.nn.sigmoid(x)


def _sort_key(x):
    bits = pltpu.bitcast(x, jnp.int32)
    return bits ^ ((bits >> 31) & jnp.int32(0x7FFFFFFF))


def _kth_largest(count_ge, kk):
    zero = jnp.zeros(kk.shape, jnp.int32)
    t0 = jnp.where(count_ge(zero) >= kk, zero, jnp.int32(INT_MIN))

    def bit_body(i, t):
        c = t | lax.shift_left(jnp.int32(1), 30 - i)
        return jnp.where(count_ge(c) >= kk, c, t)

    return lax.fori_loop(0, 31, bit_body, t0)


def _inproj_kernel(x_ref, g_ref, w_ref, o_ref, xn_ref):
    @pl.when(pl.program_id(1) == 0)
    def _():
        xn_ref[...] = _rms(x_ref[...], g_ref[...]).astype(bf16)

    o_ref[...] = jnp.dot(xn_ref[...], w_ref[...], preferred_element_type=f32)


def _inproj(h, g, wp, tm, tn):
    m, d = h.shape
    n = wp.shape[1]
    return pl.pallas_call(
        _inproj_kernel,
        grid=(m // tm, n // tn),
        in_specs=[pl.BlockSpec((tm, d), lambda i, j: (i, 0)),
                  pl.BlockSpec((1, d), lambda i, j: (0, 0)),
                  pl.BlockSpec((d, tn), lambda i, j: (0, j))],
        out_specs=pl.BlockSpec((tm, tn), lambda i, j: (i, j)),
        out_shape=jax.ShapeDtypeStruct((m, n), f32),
        scratch_shapes=[pltpu.VMEM((tm, d), bf16)],
        compiler_params=_cparams(("parallel", "arbitrary")),
        name="inproj",
    )(h, g.reshape(1, d), wp)


def _inproj_k_kernel(x_ref, g_ref, w_ref, dst_in, k_ref, kb_ref, kiw_ref):
    del dst_in
    xn = _rms(x_ref[...], g_ref[...]).astype(bf16)
    r = jnp.dot(xn, w_ref[...], preferred_element_type=f32)
    k_ref[...] = r[:, :D_A]
    kb_ref[...] = r[:, :D_A].astype(bf16)
    kiw_ref[...] = r[:, D_A:]


def _inproj_k(h, g, wk, dst, layer, row0, rows, tm):
    d = h.shape[1]
    n = wk.shape[1]
    nt, rb = rows // tm, row0 // tm
    row = lambda i: (i, 0)
    return pl.pallas_call(
        _inproj_k_kernel,
        grid=(nt,),
        in_specs=[pl.BlockSpec((tm, d), lambda i: (rb + i, 0)),
                  pl.BlockSpec((1, d), lambda i: (0, 0)),
                  pl.BlockSpec((d, n), lambda i: (0, 0)),
                  pl.BlockSpec(memory_space=pl.ANY)],
        out_specs=[pl.BlockSpec((tm, D_A), lambda i: (layer * nt + i, 0)),
                   pl.BlockSpec((tm, D_A), row), pl.BlockSpec((tm, n - D_A), row)],
        out_shape=[jax.ShapeDtypeStruct(dst.shape, f32),
                   jax.ShapeDtypeStruct((rows, D_A), bf16), jax.ShapeDtypeStruct((rows, n - D_A), f32)],
        input_output_aliases={3: 0},
        compiler_params=_cparams(("parallel",)),
        name="inproj_k",
    )(h, g.reshape(1, d), wk, dst)


def _inproj_v_kernel(x_ref, g_ref, w_ref, dst_in, v_ref, *vt_ref):
    del dst_in
    xn = _rms(x_ref[...], g_ref[...]).astype(bf16)
    r = jnp.dot(xn, w_ref[...], preferred_element_type=f32)
    v_ref[...] = r
    for vt in vt_ref:
        for c in range(vt.shape[0]):
            for hh in range(A_HEADS):
                blk = r[c * ATT_BLOCK:(c + 1) * ATT_BLOCK, hh * A_HEAD_DIM:(hh + 1) * A_HEAD_DIM]
                vt[c, hh] = blk.T.astype(bf16)


def _inproj_v(h, g, wv, dst, layer, row0, rows, tm, transposed):
    d = h.shape[1]
    nt, rb = rows // tm, row0 // tm
    out_specs = [pl.BlockSpec((tm, D_A), lambda i: (layer * nt + i, 0))]
    out_shape = [jax.ShapeDtypeStruct(dst.shape, f32)]
    if transposed:
        out_specs.append(pl.BlockSpec((tm // ATT_BLOCK, A_HEADS, A_HEAD_DIM, ATT_BLOCK), lambda i: (i, 0, 0, 0)))
        out_shape.append(jax.ShapeDtypeStruct((rows // ATT_BLOCK, A_HEADS, A_HEAD_DIM, ATT_BLOCK), bf16))
    return pl.pallas_call(
        _inproj_v_kernel,
        grid=(nt,),
        in_specs=[pl.BlockSpec((tm, d), lambda i: (rb + i, 0)),
                  pl.BlockSpec((1, d), lambda i: (0, 0)),
                  pl.BlockSpec((d, D_A), lambda i: (0, 0)),
                  pl.BlockSpec(memory_space=pl.ANY)],
        out_specs=out_specs,
        out_shape=out_shape,
        input_output_aliases={3: 0},
        compiler_params=_cparams(("parallel",)),
        name="inproj_v",
    )(h, g.reshape(1, d), wv, dst)


def _idx_scores(qi, w, kib):
    acc = jnp.zeros((qi.shape[0], kib.shape[0]), f32)
    for hh in range(IDX_HEADS):
        d = lax.dot_general(qi[:, hh * IDX_DIM:(hh + 1) * IDX_DIM], kib, NT_DIMS,
                            preferred_element_type=f32)
        acc = acc + w[:, hh:hh + 1] * jnp.maximum(d, 0.0)
    return acc


def _fold_lanes(x):
    part = x[:, :LANE]
    for j in range(1, x.shape[1] // LANE):
        part = part + x[:, j * LANE:(j + 1) * LANE]
    return part


def _tri_incl(n):
    r = lax.broadcasted_iota(jnp.int32, (n, n), 0)
    c = lax.broadcasted_iota(jnp.int32, (n, n), 1)
    return jnp.where(r <= c, 1.0, 0.0).astype(bf16)


def _dsa_prompt_kernel(q_ref, k_ref, vt_ref, qi_ref, ki_ref, w_ref, _o_in, o_ref, keys_ref, bias_ref, qs_ref,
                       *acc_refs, tq, topk):
    qb = pl.program_id(1)
    nkb = qb + 1
    hd = A_HEAD_DIM

    qi = qi_ref[...].astype(bf16)
    wt = (w_ref[...] * (IDX_HEADS ** -0.5 * IDX_DIM ** -0.5)).T
    srow = lax.broadcasted_iota(jnp.int32, (tq, tq), 0)
    qcol = lax.broadcasted_iota(jnp.int32, (tq, tq), 1)
    diag_adm = (srow // CHUNK) <= (qcol // CHUNK)

    def score_body(kb, c):
        off = pl.multiple_of(kb * tq, tq)
        kib = ki_ref[pl.ds(off, tq), :][:, :IDX_DIM].astype(bf16)
        ds = [lax.dot_general(kib, qi[:, hh * IDX_DIM:(hh + 1) * IDX_DIM], NT_DIMS, preferred_element_type=f32)
              for hh in range(IDX_HEADS)]
        acc = jnp.zeros((tq, tq), f32)
        for hh in range(IDX_HEADS):
            acc = acc + wt[hh:hh + 1, :] * jnp.maximum(ds[hh], 0.0)
        adm = jnp.logical_or(kb < qb, diag_adm)
        keys_ref[kb] = jnp.where(adm, _sort_key(acc), jnp.int32(INT_MIN))
        return c

    lax.fori_loop(0, nkb, score_body, 0)

    def count(cmp, c):
        cb = jnp.broadcast_to(c, (COUNT_ROWS, tq))

        def body(kb, acc):
            for r0 in range(0, tq, COUNT_ROWS):
                acc = acc + jnp.where(cmp(keys_ref[kb, r0:r0 + COUNT_ROWS, :], cb), 1.0, 0.0)
            return acc
        acc = lax.fori_loop(0, nkb, body, jnp.zeros((COUNT_ROWS, tq), f32))
        return jnp.sum(acc, axis=0, keepdims=True)

    lane = lax.broadcasted_iota(jnp.int32, (1, tq), 1)
    n_adm = ((qb * tq + lane) // CHUNK + 1) * CHUNK
    kk = jnp.minimum(topk, n_adm).astype(f32)
    thr = _kth_largest(lambda c: count(lambda k, t: k >= t, c), kk)
    n_ge = count(lambda k, t: k >= t, thr)
    has_tie = jnp.max(n_ge - kk) > 0.0

    @pl.when(jnp.logical_not(has_tie))
    def _():
        def body(kb, c):
            bias_ref[kb] = jnp.where(keys_ref[kb] >= thr, 0.0, NEG)
            return c
        lax.fori_loop(0, nkb, body, 0)

    @pl.when(has_tie)
    def _():
        need = kk - count(lambda k, t: k > t, thr)
        tri = jnp.where(qcol <= srow, 1.0, 0.0).astype(bf16)

        def body(kb, seen):
            k = keys_ref[kb]
            eq = jnp.where(k == thr, 1.0, 0.0)
            rank = seen + jnp.dot(tri, eq.astype(bf16), preferred_element_type=f32)
            take = jnp.logical_or(k > thr, jnp.logical_and(k == thr, rank <= need))
            bias_ref[kb] = jnp.where(take, 0.0, NEG)
            return seen + jnp.sum(eq, axis=0, keepdims=True)
        lax.fori_loop(0, nkb, body, jnp.zeros((1, tq), f32))

    qs_ref[...] = (q_ref[...] * (hd ** -0.5 * LOG2E)).astype(bf16)
    for h in range(A_HEADS):
        acc_refs[h][...] = jnp.zeros((hd, tq), f32)

    def att_body(kb, carry):
        ms, ls = carry
        off = pl.multiple_of(kb * tq, tq)
        kblk = k_ref[pl.ds(off, tq), :]
        bias = bias_ref[kb]
        sts = []
        for h in range(A_HEADS):
            cols = slice(h * hd, (h + 1) * hd)
            sts.append(lax.dot_general(kblk[:, cols], qs_ref[:, cols], NT_DIMS, preferred_element_type=f32))
        new_ms, new_ls, ps, corrs = [], [], [], []
        for h in range(A_HEADS):
            st = sts[h] + bias
            m_new = jnp.maximum(ms[h], jnp.max(st, axis=0, keepdims=True))
            p = jnp.exp2(st - m_new)
            corr = jnp.exp2(ms[h] - m_new)
            new_ls.append(corr * ls[h] + jnp.sum(p, axis=0, keepdims=True))
            new_ms.append(m_new)
            ps.append(p.astype(bf16))
            corrs.append(corr)
        for h in range(A_HEADS):
            acc_refs[h][...] = corrs[h] * acc_refs[h][...] + jnp.dot(vt_ref[kb, h], ps[h],
                                                                     preferred_element_type=f32)
        return tuple(new_ms), tuple(new_ls)

    init = (tuple(jnp.full((1, tq), NEG, f32) for _ in range(A_HEADS)),
            tuple(jnp.zeros((1, tq), f32) for _ in range(A_HEADS)))
    _, ls = lax.fori_loop(0, nkb, att_body, init)
    for h in range(A_HEADS):
        o_ref[:, h * hd:(h + 1) * hd] = (acc_refs[h][...] / ls[h]).T.astype(o_ref.dtype)


def _dsa_prompt(hp, kb16, vt16, kiw, obuf, batch, seq):
    tq = ATT_BLOCK
    nq = seq // tq
    topk = min(TOPK_MAX, seq // 4)
    kern = functools.partial(_dsa_prompt_kernel, tq=tq, topk=topk)
    once = pl.Buffered(1)
    return pl.pallas_call(
        kern,
        grid=(batch, nq),
        in_specs=[
            pl.BlockSpec((tq, D_A), lambda b, i: (b * nq + i, COL_Q // D_A)),
            pl.BlockSpec((seq, D_A), lambda b, i: (b, 0), pipeline_mode=once),
            pl.BlockSpec((nq, A_HEADS, A_HEAD_DIM, tq), lambda b, i: (b, 0, 0, 0), pipeline_mode=once),
            pl.BlockSpec((tq, 512), lambda b, i: (b * nq + i, COL_QI // 512)),
            pl.BlockSpec((seq, LANE), lambda b, i: (b, KIW_KI // LANE), pipeline_mode=once),
            pl.BlockSpec((tq, LANE), lambda b, i: (b * nq + i, KIW_W // LANE)),
            pl.BlockSpec(memory_space=pl.ANY),
        ],
        out_specs=pl.BlockSpec((tq, D_A), lambda b, i: (b * nq + i, 0)),
        out_shape=jax.ShapeDtypeStruct(obuf.shape, obuf.dtype),
        input_output_aliases={6: 0},
        scratch_shapes=[pltpu.VMEM((nq, tq, tq), jnp.int32), pltpu.VMEM((nq, tq, tq), f32),
                        pltpu.VMEM((tq, D_A), bf16)]
                       + [pltpu.VMEM((A_HEAD_DIM, tq), f32) for _ in range(A_HEADS)],
        compiler_params=_cparams(("parallel", "arbitrary")),
        name="dsa_prompt",
    )(hp, kb16, vt16, hp, kiw, kiw, obuf)


def _dsa_sample_kernel(q_ref, kn_ref, vn_ref, qi_ref, kin_ref, w_ref, kc_ref, vc_ref, kic_ref, _o_in, o_ref,
                       biasc_ref, biasn_ref, *, topk):
    nq = q_ref.shape[0]
    past = kc_ref.shape[0]

    def _select():
        qi = qi_ref[...].astype(bf16)
        w = w_ref[...][:, :IDX_HEADS] * (IDX_HEADS ** -0.5 * IDX_DIM ** -0.5)
        key_c = _sort_key(_idx_scores(qi, w, kic_ref[...].astype(bf16)))
        key_n = _sort_key(_idx_scores(qi, w, kin_ref[...][:, :IDX_DIM].astype(bf16)))

        def count(pred):
            cc = jnp.sum(_fold_lanes(jnp.where(pred(key_c), 1.0, 0.0)), axis=-1, keepdims=True)
            return cc + jnp.sum(jnp.where(pred(key_n), 1.0, 0.0), axis=-1, keepdims=True)

        kk = jnp.full((nq, 1), float(topk), f32)
        thr = _kth_largest(lambda c: count(lambda k: k >= c), kk)
        n_ge = count(lambda k: k >= thr)
        has_tie = jnp.max(n_ge - kk) > 0.0

        @pl.when(jnp.logical_not(has_tie))
        def _():
            biasc_ref[...] = jnp.where(key_c >= thr, 0.0, NEG)
            biasn_ref[...] = jnp.where(key_n >= thr, 0.0, NEG)

        @pl.when(has_tie)
        def _():
            need = kk - count(lambda k: k > thr)
            tri = _tri_incl(LANE)
            seen = jnp.zeros((nq, 1), f32)
            for j in range(past // LANE):
                k = key_c[:, j * LANE:(j + 1) * LANE]
                eq = jnp.where(k == thr, 1.0, 0.0)
                rank = seen + jnp.dot(eq.astype(bf16), tri, preferred_element_type=f32)
                take = jnp.logical_or(k > thr, jnp.logical_and(k == thr, rank <= need))
                biasc_ref[:, j * LANE:(j + 1) * LANE] = jnp.where(take, 0.0, NEG)
                seen = seen + jnp.sum(eq, axis=-1, keepdims=True)
            eq = jnp.where(key_n == thr, 1.0, 0.0)
            rank = seen + jnp.dot(eq.astype(bf16), _tri_incl(nq), preferred_element_type=f32)
            take = jnp.logical_or(key_n > thr, jnp.logical_and(key_n == thr, rank <= need))
            biasn_ref[...] = jnp.where(take, 0.0, NEG)

    _select()

    hd = A_HEAD_DIM
    q = (q_ref[...] * (hd ** -0.5 * LOG2E)).astype(bf16)
    bias_c = biasc_ref[...]
    bias_n = biasn_ref[...]
    scs, sns = [], []
    for h in range(A_HEADS):
        cols = slice(h * hd, (h + 1) * hd)
        scs.append(lax.dot_general(q[:, cols], kc_ref[:, cols].astype(bf16), NT_DIMS, preferred_element_type=f32))
        sns.append(lax.dot_general(q[:, cols], kn_ref[:, cols].astype(bf16), NT_DIMS, preferred_element_type=f32))
    pcs, pns, ls = [], [], []
    for h in range(A_HEADS):
        s_c = scs[h] + bias_c
        s_n = sns[h] + bias_n
        m = jnp.maximum(jnp.max(s_c, axis=-1, keepdims=True), jnp.max(s_n, axis=-1, keepdims=True))
        p_c = jnp.exp2(s_c - m)
        p_n = jnp.exp2(s_n - m)
        ls.append(jnp.sum(p_c, axis=-1, keepdims=True) + jnp.sum(p_n, axis=-1, keepdims=True))
        pcs.append(p_c.astype(bf16))
        pns.append(p_n.astype(bf16))
    for h in range(A_HEADS):
        cols = slice(h * hd, (h + 1) * hd)
        acc = jnp.dot(pcs[h], vc_ref[:, cols].astype(bf16), preferred_element_type=f32)
        acc = acc + jnp.dot(pns[h], vn_ref[:, cols].astype(bf16), preferred_element_type=f32)
        o_ref[:, cols] = (acc / ls[h]).astype(o_ref.dtype)


def _dsa_sample(hp, ks, vs, kiw, obuf, layer, row0, nb, nq, cache_k, cache_v, cache_kidx):
    past = cache_k.shape[1]
    topk = min(TOPK_MAX, (past + nq) // 4)
    kc = cache_k.reshape(nb, past, D_A)
    vc = cache_v.reshape(nb, past, D_A)
    rb = row0 // nq
    kern = functools.partial(_dsa_sample_kernel, topk=topk)
    return pl.pallas_call(
        kern,
        grid=(nb,),
        in_specs=[
            pl.BlockSpec((nq, D_A), lambda b: (rb + b, COL_Q // D_A)),
            pl.BlockSpec((nq, D_A), lambda b: (layer * nb + b, 0)),
            pl.BlockSpec((nq, D_A), lambda b: (layer * nb + b, 0)),
            pl.BlockSpec((nq, 512), lambda b: (rb + b, COL_QI // 512)),
            pl.BlockSpec((nq, LANE), lambda b: (b, KIW_KI // LANE)),
            pl.BlockSpec((nq, LANE), lambda b: (b, KIW_W // LANE)),
            pl.BlockSpec((None, past, D_A), lambda b: (b, 0, 0)),
            pl.BlockSpec((None, past, D_A), lambda b: (b, 0, 0)),
            pl.BlockSpec((None, past, IDX_DIM), lambda b: (b, 0, 0)),
            pl.BlockSpec(memory_space=pl.ANY),
        ],
        out_specs=pl.BlockSpec((nq, D_A), lambda b: (rb + b, 0)),
        out_shape=jax.ShapeDtypeStruct(obuf.shape, obuf.dtype),
        input_output_aliases={9: 0},
        scratch_shapes=[pltpu.VMEM((nq, past), f32), pltpu.VMEM((nq, nq), f32)],
        compiler_params=_cparams(("parallel",)),
        name="dsa_sample",
    )(hp, ks, vs, hp, kiw, kiw, kc, vc, cache_kidx, obuf)


def _pool_kernel(prev_ref, u_ref, pw_ref, ps_ref, _o_in, o_ref, *, tt, first_pos, zero_first):
    tb = pl.program_id(1)
    halo = POOL_STATE + 1
    prev = prev_ref[...]
    if zero_first:
        prev = jnp.where(tb == 0, 0.0, prev)
    cur = u_ref[...]
    ext = jnp.concatenate([prev, cur], axis=0)
    pos = first_pos + tb * tt + lax.broadcasted_iota(jnp.int32, (tt, 1), 0)
    outs = []
    for gi, wdw in enumerate(POOL_WINDOWS):
        cols = slice(gi * POOL_GROUP_DIM, (gi + 1) * POOL_GROUP_DIM)
        a = ext[:, cols]
        n = 1
        while n < wdw:
            a = a[n:] + a[:-n]
            n *= 2
        s = a[halo + 1 - wdw: halo + 1 - wdw + tt]
        cnt = jnp.minimum(pos + 1, wdw).astype(f32)
        d = s / cnt - cur[:, cols]
        outs.append(jnp.dot(d.astype(bf16), pw_ref[gi].astype(bf16), preferred_element_type=f32))
    o = jnp.concatenate(outs, axis=-1) * ps_ref[...]
    o_ref[...] = o.astype(o_ref.dtype)


def _pool(hp, obuf, row0, nb, t, tt, prev, pool_w, pool_scale, first_pos):
    ntb = t // tt
    halo = POOL_STATE + 1
    rb = row0 // tt
    if prev is None:
        prev_arr = hp
        per = tt // halo
        prev_spec = pl.BlockSpec(
            (halo, D_B), lambda b, i: (jnp.maximum((row0 // halo) + (b * ntb + i) * per - 1, 0), COL_U // D_B))
    else:
        prev_arr = prev
        prev_spec = pl.BlockSpec((None, halo, D_B), lambda b, i: (b, 0, 0))
    kern = functools.partial(_pool_kernel, tt=tt, first_pos=first_pos, zero_first=prev is None)
    return pl.pallas_call(
        kern,
        grid=(nb, ntb),
        in_specs=[prev_spec,
                  pl.BlockSpec((tt, D_B), lambda b, i: (rb + b * ntb + i, COL_U // D_B)),
                  pl.BlockSpec((len(POOL_WINDOWS), POOL_GROUP_DIM, POOL_GROUP_DIM), lambda b, i: (0, 0, 0)),
                  pl.BlockSpec((1, D_B), lambda b, i: (0, 0)),
                  pl.BlockSpec(memory_space=pl.ANY)],
        out_specs=pl.BlockSpec((tt, D_B), lambda b, i: (rb + b * ntb + i, 0)),
        out_shape=jax.ShapeDtypeStruct(obuf.shape, obuf.dtype),
        input_output_aliases={4: 0},
        compiler_params=_cparams(("parallel", "arbitrary")),
        name="pool",
    )(prev_arr, hp, pool_w, pool_scale.reshape(1, D_B), obuf)


SAFE_LOG_RANGE = 80.0


def _hgrn_kernel(cq_ref, cf_ref, ci_ref, cg_ref, lb_ref, ng_ref, s0_ref, _o_in, o_ref, sn_ref,
                 st_ref, b_ref, q_ref, k_ref, osc_ref, *, tt, blk):
    tb = pl.program_id(1)
    nh = C_HEADS
    hd = C_HEAD_DIM

    @pl.when(tb == 0)
    def _init():
        for h in range(nh):
            st_ref[h] = s0_ref[h].T

    lb = lb_ref[...]
    f = jnp.maximum(lb, LB_FLOOR) + (1.0 - lb) * jax.nn.sigmoid(cf_ref[...])
    g = jnp.log(f)
    k_ref[...] = 1.0 - f
    q_ref[...] = _silu(cq_ref[...])
    r = lax.broadcasted_iota(jnp.int32, (tt, tt), 0)
    c = lax.broadcasted_iota(jnp.int32, (tt, tt), 1)
    tri = jnp.where(jnp.logical_and(r // blk == c // blk, c <= r), 1.0, 0.0).astype(bf16)
    g1 = g.astype(bf16)
    e1 = g - g1.astype(f32)
    g2 = e1.astype(bf16)
    g3 = (e1 - g2.astype(f32)).astype(bf16)
    b = (jnp.dot(tri, g1, preferred_element_type=f32) + jnp.dot(tri, g2, preferred_element_type=f32)
         + jnp.dot(tri, g3, preferred_element_type=f32))
    b_ref[...] = b
    safe = jnp.min(b) > -SAFE_LOG_RANGE

    trow = lax.broadcasted_iota(jnp.int32, (blk, 1), 0)

    def exact_chunk(ci):
        r0 = pl.multiple_of(ci * blk, blk)
        for h in range(nh):
            rows = pl.ds(r0, blk)
            cols = slice(h * hd, (h + 1) * hd)
            bb = b_ref[rows, cols]
            qh = q_ref[rows, cols]
            kh = k_ref[rows, cols]
            ih = ci_ref[rows, cols]
            bl = bb[blk - 1:blk, :]
            ke = (kh * jnp.exp(bl - bb)).astype(bf16)
            ib = ih.astype(bf16)
            st = st_ref[h]
            o = lax.dot_general((qh * jnp.exp(bb)).astype(bf16), st.astype(bf16), NT_DIMS,
                                preferred_element_type=f32)
            for s in range(blk):
                live = trow >= s
                dec = jnp.exp(jnp.where(live, bb - bb[s:s + 1, :], 0.0))
                a_col = jnp.sum(qh * kh[s:s + 1, :] * dec, axis=-1, keepdims=True)
                o = o + jnp.where(live, a_col, 0.0) * ih[s:s + 1, :]
            osc_ref[rows, cols] = o
            st_ref[h] = st * jnp.exp(bl) + lax.dot_general(ib, ke, TN_DIMS, preferred_element_type=f32)

    @pl.when(safe)
    def _():
        same_blk = r // blk == c // blk
        blk_causal = jnp.logical_and(same_blk, c <= r)
        ones_blk = jnp.where(same_blk, 1.0, 0.0).astype(bf16)
        bl_all = (jnp.dot(ones_blk, g1, preferred_element_type=f32) + jnp.dot(ones_blk, g2, preferred_element_type=f32)
                  + jnp.dot(ones_blk, g3, preferred_element_type=f32))
        for h in range(nh):
            cols = slice(h * hd, (h + 1) * hd)
            bb = b[:, cols]
            blh = bl_all[:, cols]
            qh = q_ref[:, cols]
            ib = ci_ref[:, cols].astype(bf16)
            ke = (k_ref[:, cols] * jnp.exp(blh - bb)).astype(bf16)
            qe = (qh * jnp.exp(bb - blh)).astype(bf16)
            qb = (qh * jnp.exp(bb)).astype(bf16)
            a = jnp.where(blk_causal, lax.dot_general(qe, ke, NT_DIMS, preferred_element_type=f32), 0.0)
            o_intra = jnp.dot(a.astype(bf16), ib, preferred_element_type=f32)
            incs = [lax.dot_general(ib[c0:c0 + blk], ke[c0:c0 + blk], TN_DIMS, preferred_element_type=f32)
                    for c0 in range(0, tt, blk)]
            dec = jnp.exp(blh)
            st = st_ref[h]
            o_inter = []
            for ci, c0 in enumerate(range(0, tt, blk)):
                o_inter.append(lax.dot_general(qb[c0:c0 + blk], st.astype(bf16), NT_DIMS,
                                               preferred_element_type=f32))
                st = st * dec[c0:c0 + 1, :] + incs[ci]
            st_ref[h] = st
            osc_ref[:, cols] = o_intra + jnp.concatenate(o_inter, axis=0)

    @pl.when(jnp.logical_not(safe))
    def _():
        def body(ci, c):
            exact_chunk(ci)
            return c
        lax.fori_loop(0, tt // blk, body, 0)

    outs = []
    for h in range(nh):
        cols = slice(h * hd, (h + 1) * hd)
        outs.append(_rms(osc_ref[:, cols], ng_ref[...]))
    o_ref[...] = (jnp.concatenate(outs, axis=-1) * _silu(cg_ref[...])).astype(o_ref.dtype)

    @pl.when(tb == pl.num_programs(1) - 1)
    def _fin():
        for h in range(nh):
            sn_ref[h] = st_ref[h].T


def _hgrn(hp, obuf, row0, nb, t, tt, lb, norm_g, s0):
    ntb = t // tt
    rb = row0 // tt

    def col(cidx):
        return pl.BlockSpec((tt, D_C), lambda b, i: (rb + b * ntb + i, cidx // D_C))

    st_spec = pl.BlockSpec((None, C_HEADS, C_HEAD_DIM, C_V_DIM), lambda b, i: (b, 0, 0, 0))
    kern = functools.partial(_hgrn_kernel, tt=tt, blk=HGRN_BLOCK)
    return pl.pallas_call(
        kern,
        grid=(nb, ntb),
        in_specs=[col(COL_CQ), col(COL_CF), col(COL_CI), col(COL_CG),
                  pl.BlockSpec((1, D_C), lambda b, i: (0, 0)),
                  pl.BlockSpec((1, C_V_DIM), lambda b, i: (0, 0)),
                  st_spec,
                  pl.BlockSpec(memory_space=pl.ANY)],
        out_specs=[pl.BlockSpec((tt, D_C), lambda b, i: (rb + b * ntb + i, 0)), st_spec],
        out_shape=[jax.ShapeDtypeStruct(obuf.shape, obuf.dtype),
                   jax.ShapeDtypeStruct((nb, C_HEADS, C_HEAD_DIM, C_V_DIM), f32)],
        input_output_aliases={7: 0},
        scratch_shapes=[pltpu.VMEM((C_HEADS, C_V_DIM, C_HEAD_DIM), f32),
                        pltpu.VMEM((tt, D_C), f32), pltpu.VMEM((tt, D_C), f32),
                        pltpu.VMEM((tt, D_C), f32), pltpu.VMEM((tt, D_C), f32)],
        compiler_params=_cparams(("parallel", "arbitrary")),
        name="hgrn",
    )(hp, hp, hp, hp, lb.reshape(1, D_C), norm_g.reshape(1, C_V_DIM), s0, obuf)


def _merge_kernel(x_ref, g_ref, oa_ref, ob_ref, oc_ref, wga_ref, wgb_ref, wgc_ref,
                  pa_ref, pb_ref, pc_ref, wo_ref, o_ref, xn_ref, acc_ref):
    j = pl.program_id(1)

    @pl.when(j == 0)
    def _():
        xn_ref[...] = _rms(x_ref[...], g_ref[...]).astype(bf16)
        acc_ref[...] = jnp.zeros_like(acc_ref)

    xn = xn_ref[...]

    def branch(wg_ref, o_ref_, p_ref):
        gate = jax.nn.sigmoid(jnp.dot(xn, wg_ref[...], preferred_element_type=f32))
        return gate * jnp.dot(o_ref_[...], p_ref[...], preferred_element_type=f32)

    merged = branch(wga_ref, oa_ref, pa_ref) + branch(wgb_ref, ob_ref, pb_ref) + branch(wgc_ref, oc_ref, pc_ref)
    acc_ref[...] += jnp.dot(merged.astype(bf16), wo_ref[...], preferred_element_type=f32)

    @pl.when(j == pl.num_programs(1) - 1)
    def _():
        o_ref[...] = x_ref[...] + acc_ref[...]


def _merge(h, g, oa, ob, oc, wg, pa, pb, pc, wo, tm, tj):
    m, d = h.shape
    nj = d // tj
    row = lambda i, j: (i, 0)
    return pl.pallas_call(
        _merge_kernel,
        grid=(m // tm, nj),
        in_specs=[pl.BlockSpec((tm, d), row),
                  pl.BlockSpec((1, d), lambda i, j: (0, 0)),
                  pl.BlockSpec((tm, D_A), row),
                  pl.BlockSpec((tm, D_B), row),
                  pl.BlockSpec((tm, D_C), row),
                  pl.BlockSpec((d, tj), lambda i, j: (0, j)),
                  pl.BlockSpec((d, tj), lambda i, j: (0, nj + j)),
                  pl.BlockSpec((d, tj), lambda i, j: (0, 2 * nj + j)),
                  pl.BlockSpec((D_A, tj), lambda i, j: (0, j)),
                  pl.BlockSpec((D_B, tj), lambda i, j: (0, j)),
                  pl.BlockSpec((D_C, tj), lambda i, j: (0, j)),
                  pl.BlockSpec((tj, d), lambda i, j: (j, 0))],
        out_specs=pl.BlockSpec((tm, d), row),
        out_shape=jax.ShapeDtypeStruct((m, d), f32),
        scratch_shapes=[pltpu.VMEM((tm, d), bf16), pltpu.VMEM((tm, d), f32)],
        compiler_params=_cparams(("parallel", "arbitrary")),
        name="merge",
    )(h, g.reshape(1, d), oa, ob, oc, wg, wg, wg, pa, pb, pc, wo)


def _router_kernel(x_ref, g_ref, r_ref, o_ref):
    xn = _rms(x_ref[...], g_ref[...]).astype(bf16)
    logits = jnp.dot(xn, r_ref[...], preferred_element_type=f32)
    lane = lax.broadcasted_iota(jnp.int32, logits.shape, 1)
    logits = jnp.where(lane < N_EXPERTS, logits, -jnp.inf)
    m1 = jnp.max(logits, axis=-1, keepdims=True)
    i1 = jnp.min(jnp.where(logits == m1, lane, LANE), axis=-1, keepdims=True)
    rest = jnp.where(lane == i1, -jnp.inf, logits)
    m2 = jnp.max(rest, axis=-1, keepdims=True)
    i2 = jnp.min(jnp.where(rest == m2, lane, LANE), axis=-1, keepdims=True)
    e2 = jnp.exp(m2 - m1)
    g1 = 1.0 / (1.0 + e2)
    g2 = e2 / (1.0 + e2)
    route = jnp.where(lane == 0, i1.astype(f32), jnp.where(lane == 1, i2.astype(f32),
                      jnp.where(lane == 2, g1, jnp.where(lane == 3, g2, 0.0))))
    o_ref[...] = route[:, :N_EXPERTS]


def _router(h, g, router, tm):
    m, d = h.shape
    rp = jnp.pad(router, ((0, 0), (0, LANE - N_EXPERTS))).astype(bf16)
    return pl.pallas_call(
        _router_kernel,
        grid=(m // tm,),
        in_specs=[pl.BlockSpec((tm, d), lambda i: (i, 0)),
                  pl.BlockSpec((1, d), lambda i: (0, 0)),
                  pl.BlockSpec((d, LANE), lambda i: (0, 0))],
        out_specs=pl.BlockSpec((tm, N_EXPERTS), lambda i: (i, 0)),
        out_shape=jax.ShapeDtypeStruct((m, N_EXPERTS), f32),
        compiler_params=_cparams(("parallel",)),
        name="router",
    )(h, g.reshape(1, d), rp)


def _ffn_kernel(x_ref, g_ref, wg_ref, wu_ref, wd_ref, o_ref, xn_ref):
    @pl.when(pl.program_id(1) == 0)
    def _():
        x = x_ref[...]
        xn_ref[...] = _rms(x, g_ref[...]).astype(bf16)
        o_ref[...] = x

    xn = xn_ref[...]
    a = jnp.dot(xn, wg_ref[...], preferred_element_type=f32)
    u = jnp.dot(xn, wu_ref[...], preferred_element_type=f32)
    act = (_silu(a) * u).astype(bf16)
    o_ref[...] += jnp.dot(act, wd_ref[...], preferred_element_type=f32)


def _ffn(h, g, wg, wu, wd, tm, tf):
    m, d = h.shape
    ff = wg.shape[1]
    return pl.pallas_call(
        _ffn_kernel,
        grid=(m // tm, ff // tf),
        in_specs=[pl.BlockSpec((tm, d), lambda i, f: (i, 0)),
                  pl.BlockSpec((1, d), lambda i, f: (0, 0)),
                  pl.BlockSpec((d, tf), lambda i, f: (0, f)),
                  pl.BlockSpec((d, tf), lambda i, f: (0, f)),
                  pl.BlockSpec((tf, d), lambda i, f: (f, 0))],
        out_specs=pl.BlockSpec((tm, d), lambda i, f: (i, 0)),
        out_shape=jax.ShapeDtypeStruct((m, d), f32),
        scratch_shapes=[pltpu.VMEM((tm, d), bf16)],
        compiler_params=_cparams(("parallel", "arbitrary")),
        name="ffn",
    )(h, g.reshape(1, d), wg, wu, wd)


MOE_ROWS = 1024
MOE_SUB = 256
DMA_UNROLL = 8


def _moe_plan(route, m):
    n = 2 * m
    n_groups = -(-n // MOE_ROWS) + N_EXPERTS
    ef = route[:, 0:2].astype(jnp.int32).T.reshape(n)
    row_dst = jnp.argsort(ef, stable=True).astype(jnp.int32)
    row_tok = jnp.where(row_dst >= m, row_dst - m, row_dst)
    counts = jnp.sum((ef[:, None] == jnp.arange(N_EXPERTS, dtype=jnp.int32)[None, :]).astype(jnp.int32), axis=0)
    first = jnp.cumsum(counts) - counts
    per_e = (counts + MOE_ROWS - 1) // MOE_ROWS
    size = (counts + jnp.maximum(per_e, 1) - 1) // jnp.maximum(per_e, 1)
    size = (size + DMA_UNROLL - 1) // DMA_UNROLL * DMA_UNROLL
    ends = jnp.cumsum(per_e)
    gi = jnp.arange(n_groups, dtype=jnp.int32)
    ge = jnp.minimum(jnp.sum((gi[:, None] >= ends[None, :]).astype(jnp.int32), axis=1), N_EXPERTS - 1)
    j = gi - (ends - per_e)[ge]
    g_start = first[ge] + j * size[ge]
    g_cnt = jnp.clip(counts[ge] - j * size[ge], 0, size[ge])
    g_cnt = jnp.where(gi < ends[-1], g_cnt, 0)
    last_e = ge[jnp.maximum(ends[-1] - 1, 0)]
    ge = jnp.where(gi < ends[-1], ge, last_e)
    return ge, g_start.astype(jnp.int32), g_cnt.astype(jnp.int32), row_tok, row_dst


def _moe_kernel(ge_ref, gs_ref, gc_ref, tok_ref, dst_ref, h_hbm, g_ref, wg_ref, wu_ref, wd_ref, y_hbm,
                xs_ref, acc_ref, stage_ref, spare_ref, gsem, ssem, *, nf):
    s = pl.program_id(0)
    fi = pl.program_id(1)
    cnt = gc_ref[s]
    start = gs_ref[s]
    n_rows = tok_ref.shape[0]
    nsub = (cnt + MOE_SUB - 1) // MOE_SUB
    chunk = stage_ref.shape[0] // nf

    def gather_chunk(first_row, base):
        for u in range(chunk):
            tok = tok_ref[jnp.minimum(first_row + base + u, n_rows - 1)]
            pltpu.make_async_copy(h_hbm.at[pl.ds(tok, 1)], stage_ref.at[pl.ds(base + u, 1)], gsem).start()

    def row_tile(j):
        rows = pl.ds(pl.multiple_of(j * MOE_SUB, MOE_SUB), MOE_SUB)
        x = xs_ref[rows, :]
        a = jnp.dot(x, wg_ref[...].astype(bf16), preferred_element_type=f32)
        u = jnp.dot(x, wu_ref[...].astype(bf16), preferred_element_type=f32)
        act = (_silu(a) * u).astype(bf16)
        acc_ref[rows, :] += jnp.dot(act, wd_ref[...].astype(bf16), preferred_element_type=f32)

    @pl.when(cnt > 0)
    def _active():
        @pl.when(fi == 0)
        def _stage_in():
            @pl.when(s == 0)
            def _():
                def issue(c, carry):
                    gather_chunk(start, c * chunk)
                    return carry
                lax.fori_loop(0, nf, issue, 0)

            def wait_chunk(c, carry):
                pltpu.make_async_copy(h_hbm.at[pl.ds(0, chunk)], stage_ref.at[pl.ds(0, chunk)], gsem).wait()
                return carry
            lax.fori_loop(0, nf, wait_chunk, 0)

            def norm(j, c):
                rows = pl.ds(pl.multiple_of(j * MOE_SUB, MOE_SUB), MOE_SUB)
                xs_ref[rows, :] = _rms(stage_ref[rows, :], g_ref[...]).astype(bf16)
                acc_ref[rows, :] = jnp.zeros((MOE_SUB, acc_ref.shape[1]), f32)
                return c
            lax.fori_loop(0, nsub, norm, 0)

        nxt = jnp.minimum(s + 1, pl.num_programs(0) - 1)
        prefetch = jnp.logical_and(s + 1 < pl.num_programs(0), gc_ref[nxt] > 0)

        @pl.when(prefetch)
        def _():
            gather_chunk(gs_ref[nxt], fi * chunk)
            row_tile(0)

        @pl.when(jnp.logical_not(prefetch))
        def _():
            row_tile(0)

        def body(j, c):
            row_tile(j)
            return c
        lax.fori_loop(1, nsub, body, 0)

        @pl.when(fi == nf - 1)
        def _scatter():
            def issue_one(r):
                pltpu.make_async_copy(acc_ref.at[pl.ds(r, 1)], y_hbm.at[pl.ds(dst_ref[start + r], 1)], ssem).start()

            def issue(r8, c):
                for u in range(DMA_UNROLL):
                    issue_one(r8 * DMA_UNROLL + u)
                return c
            n8 = cnt // DMA_UNROLL
            lax.fori_loop(0, n8, issue, 0)

            def issue_tail(r, c):
                issue_one(r)
                return c
            lax.fori_loop(n8 * DMA_UNROLL, cnt, issue_tail, 0)

            def issue_spare(r, c):
                pltpu.make_async_copy(acc_ref.at[pl.ds(r, 1)], spare_ref.at[pl.ds(r - cnt, 1)], ssem).start()
                return c
            lax.fori_loop(cnt, nsub * MOE_SUB, issue_spare, 0)

            def wait_tile(j, c):
                pltpu.make_async_copy(acc_ref.at[pl.ds(0, MOE_SUB)], y_hbm.at[pl.ds(0, MOE_SUB)], ssem).wait()
                return c
            lax.fori_loop(0, nsub, wait_tile, 0)


def _moe(h, g, route, wg, wu, wd, tf):
    m, d = h.shape
    ne, _, ff = wg.shape
    nf = ff // tf
    chunk = -(-MOE_ROWS // nf)
    chunk = -(-chunk // DMA_UNROLL) * DMA_UNROLL
    ge, gs, gc, row_tok, row_dst = _moe_plan(route, m)
    n_groups = ge.shape[0]

    def wcol(s, f, ge_r, gs_r, gc_r, tok_r, dst_r):
        return (ge_r[s], 0, jnp.where(gc_r[s] > 0, f, nf - 1))

    def wrow(s, f, ge_r, gs_r, gc_r, tok_r, dst_r):
        return (ge_r[s], jnp.where(gc_r[s] > 0, f, nf - 1), 0)

    grid_spec = pltpu.PrefetchScalarGridSpec(
        num_scalar_prefetch=5,
        grid=(n_groups, nf),
        in_specs=[pl.BlockSpec(memory_space=pl.ANY),
                  pl.BlockSpec((1, d), lambda s, f, *_: (0, 0)),
                  pl.BlockSpec((None, d, tf), wcol),
                  pl.BlockSpec((None, d, tf), wcol),
                  pl.BlockSpec((None, tf, d), wrow)],
        out_specs=pl.BlockSpec(memory_space=pl.ANY),
        scratch_shapes=[pltpu.VMEM((MOE_ROWS, d), bf16), pltpu.VMEM((MOE_ROWS, d), f32),
                        pltpu.VMEM((chunk * nf, d), f32), pltpu.VMEM((MOE_SUB, d), f32),
                        pltpu.SemaphoreType.DMA(()), pltpu.SemaphoreType.DMA(())],
    )
    return pl.pallas_call(
        functools.partial(_moe_kernel, nf=nf),
        grid_spec=grid_spec,
        out_shape=jax.ShapeDtypeStruct((2 * m, d), f32),
        compiler_params=_cparams(("arbitrary", "arbitrary")),
        name="moe",
    )(ge, gs, gc, row_tok, row_dst, h, g.reshape(1, d), wg, wu, wd)


def _combine_kernel(x_ref, r_ref, y1_ref, y2_ref, o_ref):
    r = r_ref[...]
    o_ref[...] = x_ref[...] + (r[:, 2:3] * y1_ref[...] + r[:, 3:4] * y2_ref[...])


def _combine(h, route, y, tm):
    m, d = h.shape
    nb = m // tm
    return pl.pallas_call(
        _combine_kernel,
        grid=(nb,),
        in_specs=[pl.BlockSpec((tm, d), lambda i: (i, 0)),
                  pl.BlockSpec((tm, N_EXPERTS), lambda i: (i, 0)),
                  pl.BlockSpec((tm, d), lambda i: (i, 0)),
                  pl.BlockSpec((tm, d), lambda i: (nb + i, 0))],
        out_specs=pl.BlockSpec((tm, d), lambda i: (i, 0)),
        out_shape=jax.ShapeDtypeStruct((m, d), f32),
        compiler_params=_cparams(("parallel",)),
        name="moe_combine",
    )(h, route, y, y)


def _combine_norm_kernel(x_ref, r_ref, y1_ref, y2_ref, g_ref, op_ref, os_ref, *, n_p):
    r = r_ref[...]
    v = _rms(x_ref[...] + (r[:, 2:3] * y1_ref[...] + r[:, 3:4] * y2_ref[...]), g_ref[...])

    @pl.when(pl.program_id(0) < n_p)
    def _():
        op_ref[...] = v

    @pl.when(pl.program_id(0) >= n_p)
    def _():
        os_ref[...] = v


def _combine_norm(h, route, y, g, mp, tm):
    m, d = h.shape
    nb = m // tm
    n_p = mp // tm
    return pl.pallas_call(
        functools.partial(_combine_norm_kernel, n_p=n_p),
        grid=(nb,),
        in_specs=[pl.BlockSpec((tm, d), lambda i: (i, 0)),
                  pl.BlockSpec((tm, N_EXPERTS), lambda i: (i, 0)),
                  pl.BlockSpec((tm, d), lambda i: (i, 0)),
                  pl.BlockSpec((tm, d), lambda i: (nb + i, 0)),
                  pl.BlockSpec((1, d), lambda i: (0, 0))],
        out_specs=[pl.BlockSpec((tm, d), lambda i: (jnp.minimum(i, n_p - 1), 0)),
                   pl.BlockSpec((tm, d), lambda i: (jnp.maximum(i - n_p, 0), 0))],
        out_shape=[jax.ShapeDtypeStruct((mp, d), f32), jax.ShapeDtypeStruct((m - mp, d), f32)],
        compiler_params=_cparams(("arbitrary",)),
        name="moe_combine_norm",
    )(h, route, y, y, g.reshape(1, d))


def _final_norm_kernel(x_ref, g_ref, o_ref):
    o_ref[...] = _rms(x_ref[...], g_ref[...])


def _final_norm(h, g, row0, rows, tm):
    d = h.shape[1]
    rb = row0 // tm
    return pl.pallas_call(
        _final_norm_kernel,
        grid=(rows // tm,),
        in_specs=[pl.BlockSpec((tm, d), lambda i: (rb + i, 0)),
                  pl.BlockSpec((1, d), lambda i: (0, 0))],
        out_specs=pl.BlockSpec((tm, d), lambda i: (i, 0)),
        out_shape=jax.ShapeDtypeStruct((rows, d), f32),
        compiler_params=_cparams(("parallel",)),
        name="final_norm",
    )(h, g.reshape(1, d))


def _cast_kernel(w_ref, o_ref):
    o_ref[...] = w_ref[...].astype(bf16)


def _cast_bf16(w, layer, tr):
    _, r, c = w.shape
    return pl.pallas_call(
        _cast_kernel,
        grid=(r // tr,),
        in_specs=[pl.BlockSpec((None, tr, c), lambda i: (layer, i, 0))],
        out_specs=pl.BlockSpec((tr, c), lambda i: (i, 0)),
        out_shape=jax.ShapeDtypeStruct((r, c), bf16),
        compiler_params=_cparams(("parallel",)),
        name="cast_bf16",
    )(w)


def _pack_kernel(w_ref, wp_ref, wk_ref, wv_ref, wg_ref):
    o = np.cumsum((0,) + IN_SIZES)
    w = w_ref[...]
    rows = w.shape[0]
    cut = lambda a, b: w[:, int(o[a]):int(o[b])].astype(bf16)
    pad = lambda a: jnp.concatenate([a, jnp.zeros((rows, LANE - a.shape[1]), bf16)], axis=1)
    wp_ref[...] = jnp.concatenate([cut(0, 1), cut(3, 4), cut(6, 7), cut(7, 11)], axis=1)
    wk_ref[...] = jnp.concatenate([cut(1, 2), pad(cut(4, 5)), pad(cut(5, 6))], axis=1)
    wv_ref[...] = cut(2, 3)
    wg_ref[...] = cut(11, 12)


def _pack_in_proj(w, layer, tr):
    _, d, n = w.shape
    widths = (N_PACK, D_A + 2 * LANE, D_A, N_BRANCH * D_MODEL)
    return pl.pallas_call(
        _pack_kernel,
        grid=(d // tr,),
        in_specs=[pl.BlockSpec((None, tr, n), lambda i: (layer, i, 0))],
        out_specs=[pl.BlockSpec((tr, c), lambda i: (i, 0)) for c in widths],
        out_shape=[jax.ShapeDtypeStruct((d, c), bf16) for c in widths],
        compiler_params=_cparams(("parallel",)),
        name="pack_w_in",
    )(w)


def kernel(x_prompt, x_sample, cache_k, cache_v, cache_kidx, cache_pool, state_hgrn, norm_mix_g, norm_ffn_g, final_norm_g, w_in, w_proj_a, w_proj_b, w_proj_c, w_out, pool_w, pool_scale, hgrn_lb_logits, hgrn_norm_g, ffn_w_gate, ffn_w_up, ffn_w_down, moe_router, moe_w_gate, moe_w_up, moe_w_down):
    batch, seq, d = x_prompt.shape
    nb, nq, _ = x_sample.shape
    depth = w_in.shape[0]
    past = cache_k.shape[2]
    mp = batch * seq
    ms = nb * nq
    m = mp + ms

    lb_soft = jax.nn.softmax(hgrn_lb_logits.astype(f32), axis=0)
    lb_all = jnp.cumsum(lb_soft, axis=0) - lb_soft[0:1]

    h = jnp.concatenate([x_prompt.reshape(mp, d), x_sample.reshape(ms, d)], axis=0)
    s0_prompt = jnp.zeros((batch, C_HEADS, C_HEAD_DIM, C_V_DIM), f32)
    tm = 768 if m % 768 == 0 else 256
    tmh = tm // 2

    kp = jnp.zeros((depth * mp, D_A), f32)
    vp = jnp.zeros((depth * mp, D_A), f32)
    ks = jnp.zeros((depth * ms, D_A), f32)
    vs = jnp.zeros((depth * ms, D_A), f32)
    tkv = ATT_BLOCK
    tkp = 1024 if mp % 1024 == 0 else ATT_BLOCK
    outs = [[] for _ in range(6)]
    for l in range(depth):
        wp, wk, wv, wgates = _pack_in_proj(w_in, l, 128)
        hp = _inproj(h, norm_mix_g[l], wp, tm, 1024)
        kp, kb16, kiw_p = _inproj_k(h, norm_mix_g[l], wk, kp, l, 0, mp, tkp)
        ks, _, kiw_s = _inproj_k(h, norm_mix_g[l], wk, ks, l, mp, ms, tkv)
        vp, vt16 = _inproj_v(h, norm_mix_g[l], wv, vp, l, 0, mp, tkp, True)
        vs, = _inproj_v(h, norm_mix_g[l], wv, vs, l, mp, ms, tkv, False)

        oa = _dsa_prompt(hp, kb16, vt16, kiw_p, jnp.zeros((m, D_A), bf16), batch, seq)
        oa = _dsa_sample(hp, ks, vs, kiw_s, oa, l, mp, nb, nq, cache_k[l], cache_v[l], cache_kidx[l])
        ob = _pool(hp, jnp.zeros((m, D_B), bf16), 0, batch, seq, 256, None, pool_w[l], pool_scale[l], 0)
        prev = jnp.pad(cache_pool[l], ((0, 0), (1, 0), (0, 0)))
        ob = _pool(hp, ob, mp, nb, nq, nq, prev, pool_w[l], pool_scale[l], past)
        oc, st_p = _hgrn(hp, jnp.zeros((m, D_C), bf16), 0, batch, seq, 256, lb_all[l], hgrn_norm_g[l], s0_prompt)
        oc, st_s = _hgrn(hp, oc, mp, nb, nq, nq, lb_all[l], hgrn_norm_g[l], state_hgrn[l])

        h = _merge(h, norm_mix_g[l], oa, ob, oc, wgates, _cast_bf16(w_proj_a, l, 256), _cast_bf16(w_proj_b, l, 256),
                   _cast_bf16(w_proj_c, l, 256), _cast_bf16(w_out, l, 256), tmh, 512)

        uu = hp[:, COL_U:COL_U + D_B]
        outs[0].append(kiw_p[:, KIW_KI:KIW_KI + IDX_DIM].reshape(batch, seq, IDX_DIM))
        outs[1].append(uu[:mp].reshape(batch, seq, D_B)[:, -POOL_STATE:])
        outs[2].append(st_p)
        outs[3].append(kiw_s[:, KIW_KI:KIW_KI + IDX_DIM].reshape(nb, nq, IDX_DIM))
        u_ext = jnp.concatenate([cache_pool[l], uu[mp:].reshape(nb, nq, D_B)], axis=1)
        outs[4].append(u_ext[:, -POOL_STATE:])
        outs[5].append(st_s)

        j = l // 2
        if l % 2 == 0:
            h = _ffn(h, norm_ffn_g[l], _cast_bf16(ffn_w_gate, j, 256), _cast_bf16(ffn_w_up, j, 256),
                     _cast_bf16(ffn_w_down, j, 512), tm, 512)
        else:
            route = _router(h, norm_ffn_g[l], moe_router[j], tm)
            y = _moe(h, norm_ffn_g[l], route, moe_w_gate[j], moe_w_up[j], moe_w_down[j], 512)
            if l == depth - 1:
                y_prompt, y_sample = _combine_norm(h, route, y, final_norm_g, mp, tkv)
                h = None
            else:
                h = _combine(h, route, y, tmh)

    if h is not None:
        y_prompt = _final_norm(h, final_norm_g, 0, mp, 256)
        y_sample = _final_norm(h, final_norm_g, mp, ms, 256)
    y_prompt = y_prompt.reshape(batch, seq, d)
    y_sample = y_sample.reshape(nb, nq, d)
    kidx_p, pool_p, st_p, kidx_s, pool_s, st_s = (jnp.stack(o) for o in outs)
    return (y_prompt, y_sample,
            kp.reshape(depth, batch, seq, A_HEADS, A_HEAD_DIM), vp.reshape(depth, batch, seq, A_HEADS, A_HEAD_DIM),
            kidx_p, pool_p, st_p,
            ks.reshape(depth, nb, nq, A_HEADS, A_HEAD_DIM), vs.reshape(depth, nb, nq, A_HEADS, A_HEAD_DIM),
            kidx_s, pool_s, st_s)
```

```python
import functools

import numpy as np
import jax
import jax.numpy as jnp
from jax import lax
from jax.experimental import pallas as pl
from jax.experimental.pallas import tpu as pltpu

f32 = jnp.float32
bf16 = jnp.bfloat16

D_MODEL = 2048
CHUNK = 64
A_HEADS = 8
A_HEAD_DIM = 128
D_A = A_HEADS * A_HEAD_DIM
IDX_HEADS = 8
IDX_DIM = 64
TOPK_MAX = 256
POOL_WINDOWS = (2, 4, 8, 16)
POOL_GROUP_DIM = 128
D_B = len(POOL_WINDOWS) * POOL_GROUP_DIM
POOL_STATE = max(POOL_WINDOWS) - 1
C_HEADS = 4
C_HEAD_DIM = 128
C_V_DIM = 128
D_C = C_HEADS * C_HEAD_DIM
HGRN_BLOCK = 16
N_BRANCH = 3
N_EXPERTS = 8
IN_SIZES = (D_A, D_A, D_A, IDX_HEADS * IDX_DIM, IDX_DIM, IDX_HEADS,
            D_B, D_C, D_C, C_HEADS * C_V_DIM, C_HEADS * C_V_DIM, N_BRANCH * D_MODEL)
EPS = 1e-6
NEG = -1e30
LB_FLOOR = 1e-30
INT_MIN = -2 ** 31

LOG2E = 1.4426950408889634
LANE = 128
COUNT_ROWS = 64
ATT_BLOCK = 256
COL_Q = 0
COL_QI = D_A
COL_U = COL_QI + 512
COL_CQ = COL_U + 512
COL_CF = COL_CQ + 512
COL_CI = COL_CF + 512
COL_CG = COL_CI + 512
N_PACK = COL_CG + 512
KIW_KI, KIW_W = 0, LANE

VMEM_LIMIT = 56 * 1024 * 1024

NT_DIMS = (((1,), (1,)), ((), ()))
TN_DIMS = (((0,), (0,)), ((), ()))


def _cparams(sem):
    return pltpu.CompilerParams(dimension_semantics=sem, vmem_limit_bytes=VMEM_LIMIT)


def _rms(x, g):
    ms = jnp.mean(x * x, axis=-1, keepdims=True)
    return x * lax.rsqrt(ms + EPS) * g


def _silu(x):
    return x * jax.nn.sigmoid(x)


def _sort_key(x):
    bits = pltpu.bitcast(x, jnp.int32)
    return bits ^ ((bits >> 31) & jnp.int32(0x7FFFFFFF))


def _kth_largest(count_ge, kk):
    zero = jnp.zeros(kk.shape, jnp.int32)
    t0 = jnp.where(count_ge(zero) >= kk, zero, jnp.int32(INT_MIN))

    def bit_body(i, t):
        c = t | lax.shift_left(jnp.int32(1), 30 - i)
        return jnp.where(count_ge(c) >= kk, c, t)

    return lax.fori_loop(0, 31, bit_body, t0)


def _inproj_kernel(x_ref, g_ref, w_ref, o_ref, xn_ref):
    @pl.when(pl.program_id(1) == 0)
    def _():
        xn_ref[...] = _rms(x_ref[...], g_ref[...]).astype(bf16)

    o_ref[...] = jnp.dot(xn_ref[...], w_ref[...], preferred_element_type=f32)


def _inproj(h, g, wp, tm, tn):
    m, d = h.shape
    n = wp.shape[1]
    return pl.pallas_call(
        _inproj_kernel,
        grid=(m // tm, n // tn),
        in_specs=[pl.BlockSpec((tm, d), lambda i, j: (i, 0)),
                  pl.BlockSpec((1, d), lambda i, j: (0, 0)),
                  pl.BlockSpec((d, tn), lambda i, j: (0, j))],
        out_specs=pl.BlockSpec((tm, tn), lambda i, j: (i, j)),
        out_shape=jax.ShapeDtypeStruct((m, n), f32),
        scratch_shapes=[pltpu.VMEM((tm, d), bf16)],
        compiler_params=_cparams(("parallel", "arbitrary")),
        name="inproj",
    )(h, g.reshape(1, d), wp)


def _inproj_k_kernel(x_ref, g_ref, w_ref, dst_in, k_ref, kb_ref, kiw_ref):
    del dst_in
    xn = _rms(x_ref[...], g_ref[...]).astype(bf16)
    r = jnp.dot(xn, w_ref[...], preferred_element_type=f32)
    for hh in range(A_HEADS):
        k_ref[:, hh, :] = r[:, hh * A_HEAD_DIM:(hh + 1) * A_HEAD_DIM]
    kb_ref[...] = r[:, :D_A].astype(bf16)
    kiw_ref[...] = r[:, D_A:]


def _inproj_k(h, g, wk, dst, layer, row0, rows, tm):
    d = h.shape[1]
    n = wk.shape[1]
    nt, rb = rows // tm, row0 // tm
    row = lambda i: (i, 0)
    return pl.pallas_call(
        _inproj_k_kernel,
        grid=(nt,),
        in_specs=[pl.BlockSpec((tm, d), lambda i: (rb + i, 0)),
                  pl.BlockSpec((1, d), lambda i: (0, 0)),
                  pl.BlockSpec((d, n), lambda i: (0, 0)),
                  pl.BlockSpec(memory_space=pl.ANY)],
        out_specs=[pl.BlockSpec((tm, A_HEADS, A_HEAD_DIM), lambda i: (layer * nt + i, 0, 0)),
                   pl.BlockSpec((tm, D_A), row), pl.BlockSpec((tm, n - D_A), row)],
        out_shape=[jax.ShapeDtypeStruct(dst.shape, f32),
                   jax.ShapeDtypeStruct((rows, D_A), bf16), jax.ShapeDtypeStruct((rows, n - D_A), f32)],
        input_output_aliases={3: 0},
        compiler_params=_cparams(("parallel",)),
        name="inproj_k",
    )(h, g.reshape(1, d), wk, dst)


def _inproj_v_kernel(x_ref, g_ref, w_ref, dst_in, v_ref, *vt_ref):
    del dst_in
    xn = _rms(x_ref[...], g_ref[...]).astype(bf16)
    r = jnp.dot(xn, w_ref[...], preferred_element_type=f32)
    for hh in range(A_HEADS):
        v_ref[:, hh, :] = r[:, hh * A_HEAD_DIM:(hh + 1) * A_HEAD_DIM]
    for vt in vt_ref:
        for c in range(vt.shape[0]):
            for hh in range(A_HEADS):
                blk = r[c * ATT_BLOCK:(c + 1) * ATT_BLOCK, hh * A_HEAD_DIM:(hh + 1) * A_HEAD_DIM]
                vt[c, hh] = blk.T.astype(bf16)


def _inproj_v(h, g, wv, dst, layer, row0, rows, tm, transposed):
    d = h.shape[1]
    nt, rb = rows // tm, row0 // tm
    out_specs = [pl.BlockSpec((tm, A_HEADS, A_HEAD_DIM), lambda i: (layer * nt + i, 0, 0))]
    out_shape = [jax.ShapeDtypeStruct(dst.shape, f32)]
    if transposed:
        out_specs.append(pl.BlockSpec((tm // ATT_BLOCK, A_HEADS, A_HEAD_DIM, ATT_BLOCK), lambda i: (i, 0, 0, 0)))
        out_shape.append(jax.ShapeDtypeStruct((rows // ATT_BLOCK, A_HEADS, A_HEAD_DIM, ATT_BLOCK), bf16))
    return pl.pallas_call(
        _inproj_v_kernel,
        grid=(nt,),
        in_specs=[pl.BlockSpec((tm, d), lambda i: (rb + i, 0)),
                  pl.BlockSpec((1, d), lambda i: (0, 0)),
                  pl.BlockSpec((d, D_A), lambda i: (0, 0)),
                  pl.BlockSpec(memory_space=pl.ANY)],
        out_specs=out_specs,
        out_shape=out_shape,
        input_output_aliases={3: 0},
        compiler_params=_cparams(("parallel",)),
        name="inproj_v",
    )(h, g.reshape(1, d), wv, dst)


def _idx_scores(qi, w, kib):
    acc = jnp.zeros((qi.shape[0], kib.shape[0]), f32)
    for hh in range(IDX_HEADS):
        d = lax.dot_general(qi[:, hh * IDX_DIM:(hh + 1) * IDX_DIM], kib, NT_DIMS,
                            preferred_element_type=f32)
        acc = acc + w[:, hh:hh + 1] * jnp.maximum(d, 0.0)
    return acc


def _fold_lanes(x):
    part = x[:, :LANE]
    for j in range(1, x.shape[1] // LANE):
        part = part + x[:, j * LANE:(j + 1) * LANE]
    return part


def _tri_incl(n):
    r = lax.broadcasted_iota(jnp.int32, (n, n), 0)
    c = lax.broadcasted_iota(jnp.int32, (n, n), 1)
    return jnp.where(r <= c, 1.0, 0.0).astype(bf16)


def _dsa_prompt_kernel(q_ref, k_ref, vt_ref, qi_ref, ki_ref, w_ref, _o_in, o_ref, keys_ref, bias_ref, qs_ref,
                       *acc_refs, tq, topk):
    qb = pl.program_id(1)
    nkb = qb + 1
    hd = A_HEAD_DIM

    qi = qi_ref[...].astype(bf16)
    wt = (w_ref[...] * (IDX_HEADS ** -0.5 * IDX_DIM ** -0.5)).T
    srow = lax.broadcasted_iota(jnp.int32, (tq, tq), 0)
    qcol = lax.broadcasted_iota(jnp.int32, (tq, tq), 1)
    diag_adm = (srow // CHUNK) <= (qcol // CHUNK)

    def score_body(kb, c):
        off = pl.multiple_of(kb * tq, tq)
        kib = ki_ref[pl.ds(off, tq), :][:, :IDX_DIM].astype(bf16)
        ds = [lax.dot_general(kib, qi[:, hh * IDX_DIM:(hh + 1) * IDX_DIM], NT_DIMS, preferred_element_type=f32)
              for hh in range(IDX_HEADS)]
        acc = jnp.zeros((tq, tq), f32)
        for hh in range(IDX_HEADS):
            acc = acc + wt[hh:hh + 1, :] * jnp.maximum(ds[hh], 0.0)
        adm = jnp.logical_or(kb < qb, diag_adm)
        keys_ref[kb] = jnp.where(adm, _sort_key(acc), jnp.int32(INT_MIN))
        return c

    lax.fori_loop(0, nkb, score_body, 0)

    def count(cmp, c):
        cb = jnp.broadcast_to(c, (COUNT_ROWS, tq))

        def body(kb, acc):
            for r0 in range(0, tq, COUNT_ROWS):
                acc = acc + jnp.where(cmp(keys_ref[kb, r0:r0 + COUNT_ROWS, :], cb), 1.0, 0.0)
            return acc
        acc = lax.fori_loop(0, nkb, body, jnp.zeros((COUNT_ROWS, tq), f32))
        return jnp.sum(acc, axis=0, keepdims=True)

    lane = lax.broadcasted_iota(jnp.int32, (1, tq), 1)
    n_adm = ((qb * tq + lane) // CHUNK + 1) * CHUNK
    kk = jnp.minimum(topk, n_adm).astype(f32)
    thr = _kth_largest(lambda c: count(lambda k, t: k >= t, c), kk)
    n_ge = count(lambda k, t: k >= t, thr)
    has_tie = jnp.max(n_ge - kk) > 0.0

    @pl.when(jnp.logical_not(has_tie))
    def _():
        def body(kb, c):
            bias_ref[kb] = jnp.where(keys_ref[kb] >= thr, 0.0, NEG)
            return c
        lax.fori_loop(0, nkb, body, 0)

    @pl.when(has_tie)
    def _():
        need = kk - count(lambda k, t: k > t, thr)
        tri = jnp.where(qcol <= srow, 1.0, 0.0).astype(bf16)

        def body(kb, seen):
            k = keys_ref[kb]
            eq = jnp.where(k == thr, 1.0, 0.0)
            rank = seen + jnp.dot(tri, eq.astype(bf16), preferred_element_type=f32)
            take = jnp.logical_or(k > thr, jnp.logical_and(k == thr, rank <= need))
            bias_ref[kb] = jnp.where(take, 0.0, NEG)
            return seen + jnp.sum(eq, axis=0, keepdims=True)
        lax.fori_loop(0, nkb, body, jnp.zeros((1, tq), f32))

    qs_ref[...] = (q_ref[...] * (hd ** -0.5 * LOG2E)).astype(bf16)
    for h in range(A_HEADS):
        acc_refs[h][...] = jnp.zeros((hd, tq), f32)

    def att_body(kb, carry):
        ms, ls = carry
        off = pl.multiple_of(kb * tq, tq)
        kblk = k_ref[pl.ds(off, tq), :]
        bias = bias_ref[kb]
        sts = []
        for h in range(A_HEADS):
            cols = slice(h * hd, (h + 1) * hd)
            sts.append(lax.dot_general(kblk[:, cols], qs_ref[:, cols], NT_DIMS, preferred_element_type=f32))
        new_ms, new_ls, ps, corrs = [], [], [], []
        for h in range(A_HEADS):
            st = sts[h] + bias
            m_new = jnp.maximum(ms[h], jnp.max(st, axis=0, keepdims=True))
            p = jnp.exp2(st - m_new)
            corr = jnp.exp2(ms[h] - m_new)
            new_ls.append(corr * ls[h] + jnp.sum(p, axis=0, keepdims=True))
            new_ms.append(m_new)
            ps.append(p.astype(bf16))
            corrs.append(corr)
        for h in range(A_HEADS):
            acc_refs[h][...] = corrs[h] * acc_refs[h][...] + jnp.dot(vt_ref[kb, h], ps[h],
                                                                     preferred_element_type=f32)
        return tuple(new_ms), tuple(new_ls)

    init = (tuple(jnp.full((1, tq), NEG, f32) for _ in range(A_HEADS)),
            tuple(jnp.zeros((1, tq), f32) for _ in range(A_HEADS)))
    _, ls = lax.fori_loop(0, nkb, att_body, init)
    for h in range(A_HEADS):
        o_ref[:, h * hd:(h + 1) * hd] = (acc_refs[h][...] / ls[h]).T.astype(o_ref.dtype)


def _dsa_prompt(hp, kb16, vt16, kiw, obuf, batch, seq):
    tq = ATT_BLOCK
    nq = seq // tq
    topk = min(TOPK_MAX, seq // 4)
    kern = functools.partial(_dsa_prompt_kernel, tq=tq, topk=topk)
    once = pl.Buffered(1)
    return pl.pallas_call(
        kern,
        grid=(batch, nq),
        in_specs=[
            pl.BlockSpec((tq, D_A), lambda b, i: (b * nq + i, COL_Q // D_A)),
            pl.BlockSpec((seq, D_A), lambda b, i: (b, 0), pipeline_mode=once),
            pl.BlockSpec((nq, A_HEADS, A_HEAD_DIM, tq), lambda b, i: (b, 0, 0, 0), pipeline_mode=once),
            pl.BlockSpec((tq, 512), lambda b, i: (b * nq + i, COL_QI // 512)),
            pl.BlockSpec((seq, LANE), lambda b, i: (b, KIW_KI // LANE), pipeline_mode=once),
            pl.BlockSpec((tq, LANE), lambda b, i: (b * nq + i, KIW_W // LANE)),
            pl.BlockSpec(memory_space=pl.ANY),
        ],
        out_specs=pl.BlockSpec((tq, D_A), lambda b, i: (b * nq + i, 0)),
        out_shape=jax.ShapeDtypeStruct(obuf.shape, obuf.dtype),
        input_output_aliases={6: 0},
        scratch_shapes=[pltpu.VMEM((nq, tq, tq), jnp.int32), pltpu.VMEM((nq, tq, tq), f32),
                        pltpu.VMEM((tq, D_A), bf16)]
                       + [pltpu.VMEM((A_HEAD_DIM, tq), f32) for _ in range(A_HEADS)],
        compiler_params=_cparams(("parallel", "arbitrary")),
        name="dsa_prompt",
    )(hp, kb16, vt16, hp, kiw, kiw, obuf)


def _dsa_sample_kernel(q_ref, kn_ref, vn_ref, qi_ref, kin_ref, w_ref, kc_ref, vc_ref, kic_ref, _o_in, o_ref,
                       biasc_ref, biasn_ref, *, topk):
    nq = q_ref.shape[0]
    past = kc_ref.shape[0]

    def _select():
        qi = qi_ref[...].astype(bf16)
        w = w_ref[...][:, :IDX_HEADS] * (IDX_HEADS ** -0.5 * IDX_DIM ** -0.5)
        key_c = _sort_key(_idx_scores(qi, w, kic_ref[...].astype(bf16)))
        key_n = _sort_key(_idx_scores(qi, w, kin_ref[...][:, :IDX_DIM].astype(bf16)))

        def count(pred):
            cc = jnp.sum(_fold_lanes(jnp.where(pred(key_c), 1.0, 0.0)), axis=-1, keepdims=True)
            return cc + jnp.sum(jnp.where(pred(key_n), 1.0, 0.0), axis=-1, keepdims=True)

        kk = jnp.full((nq, 1), float(topk), f32)
        thr = _kth_largest(lambda c: count(lambda k: k >= c), kk)
        n_ge = count(lambda k: k >= thr)
        has_tie = jnp.max(n_ge - kk) > 0.0

        @pl.when(jnp.logical_not(has_tie))
        def _():
            biasc_ref[...] = jnp.where(key_c >= thr, 0.0, NEG)
            biasn_ref[...] = jnp.where(key_n >= thr, 0.0, NEG)

        @pl.when(has_tie)
        def _():
            need = kk - count(lambda k: k > thr)
            tri = _tri_incl(LANE)
            seen = jnp.zeros((nq, 1), f32)
            for j in range(past // LANE):
                k = key_c[:, j * LANE:(j + 1) * LANE]
                eq = jnp.where(k == thr, 1.0, 0.0)
                rank = seen + jnp.dot(eq.astype(bf16), tri, preferred_element_type=f32)
                take = jnp.logical_or(k > thr, jnp.logical_and(k == thr, rank <= need))
                biasc_ref[:, j * LANE:(j + 1) * LANE] = jnp.where(take, 0.0, NEG)
                seen = seen + jnp.sum(eq, axis=-1, keepdims=True)
            eq = jnp.where(key_n == thr, 1.0, 0.0)
            rank = seen + jnp.dot(eq.astype(bf16), _tri_incl(nq), preferred_element_type=f32)
            take = jnp.logical_or(key_n > thr, jnp.logical_and(key_n == thr, rank <= need))
            biasn_ref[...] = jnp.where(take, 0.0, NEG)

    _select()

    hd = A_HEAD_DIM
    q = (q_ref[...] * (hd ** -0.5 * LOG2E)).astype(bf16)
    bias_c = biasc_ref[...]
    bias_n = biasn_ref[...]
    scs, sns = [], []
    for h in range(A_HEADS):
        cols = slice(h * hd, (h + 1) * hd)
        scs.append(lax.dot_general(q[:, cols], kc_ref[:, cols].astype(bf16), NT_DIMS, preferred_element_type=f32))
        sns.append(lax.dot_general(q[:, cols], kn_ref[:, h, :].astype(bf16), NT_DIMS, preferred_element_type=f32))
    pcs, pns, ls = [], [], []
    for h in range(A_HEADS):
        s_c = scs[h] + bias_c
        s_n = sns[h] + bias_n
        m = jnp.maximum(jnp.max(s_c, axis=-1, keepdims=True), jnp.max(s_n, axis=-1, keepdims=True))
        p_c = jnp.exp2(s_c - m)
        p_n = jnp.exp2(s_n - m)
        ls.append(jnp.sum(p_c, axis=-1, keepdims=True) + jnp.sum(p_n, axis=-1, keepdims=True))
        pcs.append(p_c.astype(bf16))
        pns.append(p_n.astype(bf16))
    for h in range(A_HEADS):
        cols = slice(h * hd, (h + 1) * hd)
        acc = jnp.dot(pcs[h], vc_ref[:, cols].astype(bf16), preferred_element_type=f32)
        acc = acc + jnp.dot(pns[h], vn_ref[:, h, :].astype(bf16), preferred_element_type=f32)
        o_ref[:, cols] = (acc / ls[h]).astype(o_ref.dtype)


def _dsa_sample(hp, ks, vs, kiw, obuf, layer, row0, nb, nq, cache_k, cache_v, cache_kidx):
    past = cache_k.shape[1]
    topk = min(TOPK_MAX, (past + nq) // 4)
    kc = cache_k.reshape(nb, past, D_A)
    vc = cache_v.reshape(nb, past, D_A)
    rb = row0 // nq
    kern = functools.partial(_dsa_sample_kernel, topk=topk)
    return pl.pallas_call(
        kern,
        grid=(nb,),
        in_specs=[
            pl.BlockSpec((nq, D_A), lambda b: (rb + b, COL_Q // D_A)),
            pl.BlockSpec((nq, A_HEADS, A_HEAD_DIM), lambda b: (layer * nb + b, 0, 0)),
            pl.BlockSpec((nq, A_HEADS, A_HEAD_DIM), lambda b: (layer * nb + b, 0, 0)),
            pl.BlockSpec((nq, 512), lambda b: (rb + b, COL_QI // 512)),
            pl.BlockSpec((nq, LANE), lambda b: (b, KIW_KI // LANE)),
            pl.BlockSpec((nq, LANE), lambda b: (b, KIW_W // LANE)),
            pl.BlockSpec((None, past, D_A), lambda b: (b, 0, 0)),
            pl.BlockSpec((None, past, D_A), lambda b: (b, 0, 0)),
            pl.BlockSpec((None, past, IDX_DIM), lambda b: (b, 0, 0)),
            pl.BlockSpec(memory_space=pl.ANY),
        ],
        out_specs=pl.BlockSpec((nq, D_A), lambda b: (rb + b, 0)),
        out_shape=jax.ShapeDtypeStruct(obuf.shape, obuf.dtype),
        input_output_aliases={9: 0},
        scratch_shapes=[pltpu.VMEM((nq, past), f32), pltpu.VMEM((nq, nq), f32)],
        compiler_params=_cparams(("parallel",)),
        name="dsa_sample",
    )(hp, ks, vs, hp, kiw, kiw, kc, vc, cache_kidx, obuf)


def _pool_kernel(prev_ref, u_ref, pw_ref, ps_ref, _o_in, o_ref, *, tt, first_pos, zero_first):
    tb = pl.program_id(1)
    halo = POOL_STATE + 1
    prev = prev_ref[...]
    if zero_first:
        prev = jnp.where(tb == 0, 0.0, prev)
    cur = u_ref[...]
    ext = jnp.concatenate([prev, cur], axis=0)
    pos = first_pos + tb * tt + lax.broadcasted_iota(jnp.int32, (tt, 1), 0)
    outs = []
    for gi, wdw in enumerate(POOL_WINDOWS):
        cols = slice(gi * POOL_GROUP_DIM, (gi + 1) * POOL_GROUP_DIM)
        a = ext[:, cols]
        n = 1
        while n < wdw:
            a = a[n:] + a[:-n]
            n *= 2
        s = a[halo + 1 - wdw: halo + 1 - wdw + tt]
        cnt = jnp.minimum(pos + 1, wdw).astype(f32)
        d = s / cnt - cur[:, cols]
        outs.append(jnp.dot(d.astype(bf16), pw_ref[gi].astype(bf16), preferred_element_type=f32))
    o = jnp.concatenate(outs, axis=-1) * ps_ref[...]
    o_ref[...] = o.astype(o_ref.dtype)


def _pool(hp, obuf, row0, nb, t, tt, prev, pool_w, pool_scale, first_pos):
    ntb = t // tt
    halo = POOL_STATE + 1
    rb = row0 // tt
    if prev is None:
        prev_arr = hp
        per = tt // halo
        prev_spec = pl.BlockSpec(
            (halo, D_B), lambda b, i: (jnp.maximum((row0 // halo) + (b * ntb + i) * per - 1, 0), COL_U // D_B))
    else:
        prev_arr = prev
        prev_spec = pl.BlockSpec((None, halo, D_B), lambda b, i: (b, 0, 0))
    kern = functools.partial(_pool_kernel, tt=tt, first_pos=first_pos, zero_first=prev is None)
    return pl.pallas_call(
        kern,
        grid=(nb, ntb),
        in_specs=[prev_spec,
                  pl.BlockSpec((tt, D_B), lambda b, i: (rb + b * ntb + i, COL_U // D_B)),
                  pl.BlockSpec((len(POOL_WINDOWS), POOL_GROUP_DIM, POOL_GROUP_DIM), lambda b, i: (0, 0, 0)),
                  pl.BlockSpec((1, D_B), lambda b, i: (0, 0)),
                  pl.BlockSpec(memory_space=pl.ANY)],
        out_specs=pl.BlockSpec((tt, D_B), lambda b, i: (rb + b * ntb + i, 0)),
        out_shape=jax.ShapeDtypeStruct(obuf.shape, obuf.dtype),
        input_output_aliases={4: 0},
        compiler_params=_cparams(("parallel", "arbitrary")),
        name="pool",
    )(prev_arr, hp, pool_w, pool_scale.reshape(1, D_B), obuf)


SAFE_LOG_RANGE = 80.0


def _hgrn_kernel(cq_ref, cf_ref, ci_ref, cg_ref, lb_ref, ng_ref, s0_ref, _o_in, o_ref, sn_ref,
                 st_ref, b_ref, q_ref, k_ref, osc_ref, *, tt, blk):
    tb = pl.program_id(1)
    nh = C_HEADS
    hd = C_HEAD_DIM

    @pl.when(tb == 0)
    def _init():
        for h in range(nh):
            st_ref[h] = s0_ref[h].T

    lb = lb_ref[...]
    f = jnp.maximum(lb, LB_FLOOR) + (1.0 - lb) * jax.nn.sigmoid(cf_ref[...])
    g = jnp.log(f)
    k_ref[...] = 1.0 - f
    q_ref[...] = _silu(cq_ref[...])
    r = lax.broadcasted_iota(jnp.int32, (tt, tt), 0)
    c = lax.broadcasted_iota(jnp.int32, (tt, tt), 1)
    tri = jnp.where(jnp.logical_and(r // blk == c // blk, c <= r), 1.0, 0.0).astype(bf16)
    g1 = g.astype(bf16)
    e1 = g - g1.astype(f32)
    g2 = e1.astype(bf16)
    g3 = (e1 - g2.astype(f32)).astype(bf16)
    b = (jnp.dot(tri, g1, preferred_element_type=f32) + jnp.dot(tri, g2, preferred_element_type=f32)
         + jnp.dot(tri, g3, preferred_element_type=f32))
    b_ref[...] = b
    safe = jnp.min(b) > -SAFE_LOG_RANGE

    trow = lax.broadcasted_iota(jnp.int32, (blk, 1), 0)

    def exact_chunk(ci):
        r0 = pl.multiple_of(ci * blk, blk)
        for h in range(nh):
            rows = pl.ds(r0, blk)
            cols = slice(h * hd, (h + 1) * hd)
            bb = b_ref[rows, cols]
            qh = q_ref[rows, cols]
            kh = k_ref[rows, cols]
            ih = ci_ref[rows, cols]
            bl = bb[blk - 1:blk, :]
            ke = (kh * jnp.exp(bl - bb)).astype(bf16)
            ib = ih.astype(bf16)
            st = st_ref[h]
            o = lax.dot_general((qh * jnp.exp(bb)).astype(bf16), st.astype(bf16), NT_DIMS,
                                preferred_element_type=f32)
            for s in range(blk):
                live = trow >= s
                dec = jnp.exp(jnp.where(live, bb - bb[s:s + 1, :], 0.0))
                a_col = jnp.sum(qh * kh[s:s + 1, :] * dec, axis=-1, keepdims=True)
                o = o + jnp.where(live, a_col, 0.0) * ih[s:s + 1, :]
            osc_ref[rows, cols] = o
            st_ref[h] = st * jnp.exp(bl) + lax.dot_general(ib, ke, TN_DIMS, preferred_element_type=f32)

    @pl.when(safe)
    def _():
        same_blk = r // blk == c // blk
        blk_causal = jnp.logical_and(same_blk, c <= r)
        ones_blk = jnp.where(same_blk, 1.0, 0.0).astype(bf16)
        bl_all = (jnp.dot(ones_blk, g1, preferred_element_type=f32) + jnp.dot(ones_blk, g2, preferred_element_type=f32)
                  + jnp.dot(ones_blk, g3, preferred_element_type=f32))
        for h in range(nh):
            cols = slice(h * hd, (h + 1) * hd)
            bb = b[:, cols]
            blh = bl_all[:, cols]
            qh = q_ref[:, cols]
            ib = ci_ref[:, cols].astype(bf16)
            ke = (k_ref[:, cols] * jnp.exp(blh - bb)).astype(bf16)
            qe = (qh * jnp.exp(bb - blh)).astype(bf16)
            qb = (qh * jnp.exp(bb)).astype(bf16)
            a = jnp.where(blk_causal, lax.dot_general(qe, ke, NT_DIMS, preferred_element_type=f32), 0.0)
            o_intra = jnp.dot(a.astype(bf16), ib, preferred_element_type=f32)
            incs = [lax.dot_general(ib[c0:c0 + blk], ke[c0:c0 + blk], TN_DIMS, preferred_element_type=f32)
                    for c0 in range(0, tt, blk)]
            dec = jnp.exp(blh)
            st = st_ref[h]
            o_inter = []
            for ci, c0 in enumerate(range(0, tt, blk)):
                o_inter.append(lax.dot_general(qb[c0:c0 + blk], st.astype(bf16), NT_DIMS,
                                               preferred_element_type=f32))
                st = st * dec[c0:c0 + 1, :] + incs[ci]
            st_ref[h] = st
            osc_ref[:, cols] = o_intra + jnp.concatenate(o_inter, axis=0)

    @pl.when(jnp.logical_not(safe))
    def _():
        def body(ci, c):
            exact_chunk(ci)
            return c
        lax.fori_loop(0, tt // blk, body, 0)

    outs = []
    for h in range(nh):
        cols = slice(h * hd, (h + 1) * hd)
        outs.append(_rms(osc_ref[:, cols], ng_ref[...]))
    o_ref[...] = (jnp.concatenate(outs, axis=-1) * _silu(cg_ref[...])).astype(o_ref.dtype)

    @pl.when(tb == pl.num_programs(1) - 1)
    def _fin():
        for h in range(nh):
            sn_ref[h] = st_ref[h].T


def _hgrn(hp, obuf, row0, nb, t, tt, lb, norm_g, s0):
    ntb = t // tt
    rb = row0 // tt

    def col(cidx):
        return pl.BlockSpec((tt, D_C), lambda b, i: (rb + b * ntb + i, cidx // D_C))

    st_spec = pl.BlockSpec((None, C_HEADS, C_HEAD_DIM, C_V_DIM), lambda b, i: (b, 0, 0, 0))
    kern = functools.partial(_hgrn_kernel, tt=tt, blk=HGRN_BLOCK)
    return pl.pallas_call(
        kern,
        grid=(nb, ntb),
        in_specs=[col(COL_CQ), col(COL_CF), col(COL_CI), col(COL_CG),
                  pl.BlockSpec((1, D_C), lambda b, i: (0, 0)),
                  pl.BlockSpec((1, C_V_DIM), lambda b, i: (0, 0)),
                  st_spec,
                  pl.BlockSpec(memory_space=pl.ANY)],
        out_specs=[pl.BlockSpec((tt, D_C), lambda b, i: (rb + b * ntb + i, 0)), st_spec],
        out_shape=[jax.ShapeDtypeStruct(obuf.shape, obuf.dtype),
                   jax.ShapeDtypeStruct((nb, C_HEADS, C_HEAD_DIM, C_V_DIM), f32)],
        input_output_aliases={7: 0},
        scratch_shapes=[pltpu.VMEM((C_HEADS, C_V_DIM, C_HEAD_DIM), f32),
                        pltpu.VMEM((tt, D_C), f32), pltpu.VMEM((tt, D_C), f32),
                        pltpu.VMEM((tt, D_C), f32), pltpu.VMEM((tt, D_C), f32)],
        compiler_params=_cparams(("parallel", "arbitrary")),
        name="hgrn",
    )(hp, hp, hp, hp, lb.reshape(1, D_C), norm_g.reshape(1, C_V_DIM), s0, obuf)


def _merge_kernel(x_ref, g_ref, oa_ref, ob_ref, oc_ref, wga_ref, wgb_ref, wgc_ref,
                  pa_ref, pb_ref, pc_ref, wo_ref, o_ref, xn_ref, acc_ref):
    j = pl.program_id(1)

    @pl.when(j == 0)
    def _():
        xn_ref[...] = _rms(x_ref[...], g_ref[...]).astype(bf16)
        acc_ref[...] = jnp.zeros_like(acc_ref)

    xn = xn_ref[...]

    def branch(wg_ref, o_ref_, p_ref):
        gate = jax.nn.sigmoid(jnp.dot(xn, wg_ref[...], preferred_element_type=f32))
        return gate * jnp.dot(o_ref_[...], p_ref[...], preferred_element_type=f32)

    merged = branch(wga_ref, oa_ref, pa_ref) + branch(wgb_ref, ob_ref, pb_ref) + branch(wgc_ref, oc_ref, pc_ref)
    acc_ref[...] += jnp.dot(merged.astype(bf16), wo_ref[...], preferred_element_type=f32)

    @pl.when(j == pl.num_programs(1) - 1)
    def _():
        o_ref[...] = x_ref[...] + acc_ref[...]


def _merge(h, g, oa, ob, oc, wg, pa, pb, pc, wo, tm, tj):
    m, d = h.shape
    nj = d // tj
    row = lambda i, j: (i, 0)
    return pl.pallas_call(
        _merge_kernel,
        grid=(m // tm, nj),
        in_specs=[pl.BlockSpec((tm, d), row),
                  pl.BlockSpec((1, d), lambda i, j: (0, 0)),
                  pl.BlockSpec((tm, D_A), row),
                  pl.BlockSpec((tm, D_B), row),
                  pl.BlockSpec((tm, D_C), row),
                  pl.BlockSpec((d, tj), lambda i, j: (0, j)),
                  pl.BlockSpec((d, tj), lambda i, j: (0, nj + j)),
                  pl.BlockSpec((d, tj), lambda i, j: (0, 2 * nj + j)),
                  pl.BlockSpec((D_A, tj), lambda i, j: (0, j)),
                  pl.BlockSpec((D_B, tj), lambda i, j: (0, j)),
                  pl.BlockSpec((D_C, tj), lambda i, j: (0, j)),
                  pl.BlockSpec((tj, d), lambda i, j: (j, 0))],
        out_specs=pl.BlockSpec((tm, d), row),
        out_shape=jax.ShapeDtypeStruct((m, d), f32),
        scratch_shapes=[pltpu.VMEM((tm, d), bf16), pltpu.VMEM((tm, d), f32)],
        compiler_params=_cparams(("parallel", "arbitrary")),
        name="merge",
    )(h, g.reshape(1, d), oa, ob, oc, wg, wg, wg, pa, pb, pc, wo)


def _router_kernel(x_ref, g_ref, r_ref, o_ref):
    xn = _rms(x_ref[...], g_ref[...]).astype(bf16)
    logits = jnp.dot(xn, r_ref[...], preferred_element_type=f32)
    lane = lax.broadcasted_iota(jnp.int32, logits.shape, 1)
    logits = jnp.where(lane < N_EXPERTS, logits, -jnp.inf)
    m1 = jnp.max(logits, axis=-1, keepdims=True)
    i1 = jnp.min(jnp.where(logits == m1, lane, LANE), axis=-1, keepdims=True)
    rest = jnp.where(lane == i1, -jnp.inf, logits)
    m2 = jnp.max(rest, axis=-1, keepdims=True)
    i2 = jnp.min(jnp.where(rest == m2, lane, LANE), axis=-1, keepdims=True)
    e2 = jnp.exp(m2 - m1)
    g1 = 1.0 / (1.0 + e2)
    g2 = e2 / (1.0 + e2)
    route = jnp.where(lane == 0, i1.astype(f32), jnp.where(lane == 1, i2.astype(f32),
                      jnp.where(lane == 2, g1, jnp.where(lane == 3, g2, 0.0))))
    o_ref[...] = route[:, :N_EXPERTS]


def _router(h, g, router, tm):
    m, d = h.shape
    rp = jnp.pad(router, ((0, 0), (0, LANE - N_EXPERTS))).astype(bf16)
    return pl.pallas_call(
        _router_kernel,
        grid=(m // tm,),
        in_specs=[pl.BlockSpec((tm, d), lambda i: (i, 0)),
                  pl.BlockSpec((1, d), lambda i: (0, 0)),
                  pl.BlockSpec((d, LANE), lambda i: (0, 0))],
        out_specs=pl.BlockSpec((tm, N_EXPERTS), lambda i: (i, 0)),
        out_shape=jax.ShapeDtypeStruct((m, N_EXPERTS), f32),
        compiler_params=_cparams(("parallel",)),
        name="router",
    )(h, g.reshape(1, d), rp)


def _ffn_kernel(x_ref, g_ref, wg_ref, wu_ref, wd_ref, o_ref, xn_ref):
    @pl.when(pl.program_id(1) == 0)
    def _():
        x = x_ref[...]
        xn_ref[...] = _rms(x, g_ref[...]).astype(bf16)
        o_ref[...] = x

    xn = xn_ref[...]
    a = jnp.dot(xn, wg_ref[...], preferred_element_type=f32)
    u = jnp.dot(xn, wu_ref[...], preferred_element_type=f32)
    act = (_silu(a) * u).astype(bf16)
    o_ref[...] += jnp.dot(act, wd_ref[...], preferred_element_type=f32)


def _ffn(h, g, wg, wu, wd, tm, tf):
    m, d = h.shape
    ff = wg.shape[1]
    return pl.pallas_call(
        _ffn_kernel,
        grid=(m // tm, ff // tf),
        in_specs=[pl.BlockSpec((tm, d), lambda i, f: (i, 0)),
                  pl.BlockSpec((1, d), lambda i, f: (0, 0)),
                  pl.BlockSpec((d, tf), lambda i, f: (0, f)),
                  pl.BlockSpec((d, tf), lambda i, f: (0, f)),
                  pl.BlockSpec((tf, d), lambda i, f: (f, 0))],
        out_specs=pl.BlockSpec((tm, d), lambda i, f: (i, 0)),
        out_shape=jax.ShapeDtypeStruct((m, d), f32),
        scratch_shapes=[pltpu.VMEM((tm, d), bf16)],
        compiler_params=_cparams(("parallel", "arbitrary")),
        name="ffn",
    )(h, g.reshape(1, d), wg, wu, wd)


MOE_ROWS = 1024
MOE_SUB = 256
DMA_UNROLL = 8


def _moe_plan(route, m):
    n = 2 * m
    n_groups = -(-n // MOE_ROWS) + N_EXPERTS
    ef = route[:, 0:2].astype(jnp.int32).T.reshape(n)
    row_dst = jnp.argsort(ef, stable=True).astype(jnp.int32)
    row_tok = jnp.where(row_dst >= m, row_dst - m, row_dst)
    counts = jnp.sum((ef[:, None] == jnp.arange(N_EXPERTS, dtype=jnp.int32)[None, :]).astype(jnp.int32), axis=0)
    first = jnp.cumsum(counts) - counts
    per_e = (counts + MOE_ROWS - 1) // MOE_ROWS
    size = (counts + jnp.maximum(per_e, 1) - 1) // jnp.maximum(per_e, 1)
    size = (size + DMA_UNROLL - 1) // DMA_UNROLL * DMA_UNROLL
    ends = jnp.cumsum(per_e)
    gi = jnp.arange(n_groups, dtype=jnp.int32)
    ge = jnp.minimum(jnp.sum((gi[:, None] >= ends[None, :]).astype(jnp.int32), axis=1), N_EXPERTS - 1)
    j = gi - (ends - per_e)[ge]
    g_start = first[ge] + j * size[ge]
    g_cnt = jnp.clip(counts[ge] - j * size[ge], 0, size[ge])
    g_cnt = jnp.where(gi < ends[-1], g_cnt, 0)
    last_e = ge[jnp.maximum(ends[-1] - 1, 0)]
    ge = jnp.where(gi < ends[-1], ge, last_e)
    return ge, g_start.astype(jnp.int32), g_cnt.astype(jnp.int32), row_tok, row_dst


def _moe_kernel(ge_ref, gs_ref, gc_ref, tok_ref, dst_ref, h_hbm, g_ref, wg_ref, wu_ref, wd_ref, y_hbm,
                xs_ref, acc_ref, stage_ref, spare_ref, gsem, ssem, *, nf):
    s = pl.program_id(0)
    fi = pl.program_id(1)
    cnt = gc_ref[s]
    start = gs_ref[s]
    n_rows = tok_ref.shape[0]
    nsub = (cnt + MOE_SUB - 1) // MOE_SUB
    chunk = stage_ref.shape[0] // nf

    def gather_chunk(first_row, base):
        for u in range(chunk):
            tok = tok_ref[jnp.minimum(first_row + base + u, n_rows - 1)]
            pltpu.make_async_copy(h_hbm.at[pl.ds(tok, 1)], stage_ref.at[pl.ds(base + u, 1)], gsem).start()

    def row_tile(j):
        rows = pl.ds(pl.multiple_of(j * MOE_SUB, MOE_SUB), MOE_SUB)
        x = xs_ref[rows, :]
        a = jnp.dot(x, wg_ref[...].astype(bf16), preferred_element_type=f32)
        u = jnp.dot(x, wu_ref[...].astype(bf16), preferred_element_type=f32)
        act = (_silu(a) * u).astype(bf16)
        acc_ref[rows, :] += jnp.dot(act, wd_ref[...].astype(bf16), preferred_element_type=f32)

    @pl.when(cnt > 0)
    def _active():
        @pl.when(fi == 0)
        def _stage_in():
            @pl.when(s == 0)
            def _():
                def issue(c, carry):
                    gather_chunk(start, c * chunk)
                    return carry
                lax.fori_loop(0, nf, issue, 0)

            def wait_chunk(c, carry):
                pltpu.make_async_copy(h_hbm.at[pl.ds(0, chunk)], stage_ref.at[pl.ds(0, chunk)], gsem).wait()
                return carry
            lax.fori_loop(0, nf, wait_chunk, 0)

            def norm(j, c):
                rows = pl.ds(pl.multiple_of(j * MOE_SUB, MOE_SUB), MOE_SUB)
                xs_ref[rows, :] = _rms(stage_ref[rows, :], g_ref[...]).astype(bf16)
                acc_ref[rows, :] = jnp.zeros((MOE_SUB, acc_ref.shape[1]), f32)
                return c
            lax.fori_loop(0, nsub, norm, 0)

        nxt = jnp.minimum(s + 1, pl.num_programs(0) - 1)
        prefetch = jnp.logical_and(s + 1 < pl.num_programs(0), gc_ref[nxt] > 0)

        @pl.when(prefetch)
        def _():
            gather_chunk(gs_ref[nxt], fi * chunk)
            row_tile(0)

        @pl.when(jnp.logical_not(prefetch))
        def _():
            row_tile(0)

        def body(j, c):
            row_tile(j)
            return c
        lax.fori_loop(1, nsub, body, 0)

        @pl.when(fi == nf - 1)
        def _scatter():
            def issue_one(r):
                pltpu.make_async_copy(acc_ref.at[pl.ds(r, 1)], y_hbm.at[pl.ds(dst_ref[start + r], 1)], ssem).start()

            def issue(r8, c):
                for u in range(DMA_UNROLL):
                    issue_one(r8 * DMA_UNROLL + u)
                return c
            n8 = cnt // DMA_UNROLL
            lax.fori_loop(0, n8, issue, 0)

            def issue_tail(r, c):
                issue_one(r)
                return c
            lax.fori_loop(n8 * DMA_UNROLL, cnt, issue_tail, 0)

            def issue_spare(r, c):
                pltpu.make_async_copy(acc_ref.at[pl.ds(r, 1)], spare_ref.at[pl.ds(r - cnt, 1)], ssem).start()
                return c
            lax.fori_loop(cnt, nsub * MOE_SUB, issue_spare, 0)

            def wait_tile(j, c):
                pltpu.make_async_copy(acc_ref.at[pl.ds(0, MOE_SUB)], y_hbm.at[pl.ds(0, MOE_SUB)], ssem).wait()
                return c
            lax.fori_loop(0, nsub, wait_tile, 0)


def _moe(h, g, route, wg, wu, wd, tf):
    m, d = h.shape
    ne, _, ff = wg.shape
    nf = ff // tf
    chunk = -(-MOE_ROWS // nf)
    chunk = -(-chunk // DMA_UNROLL) * DMA_UNROLL
    ge, gs, gc, row_tok, row_dst = _moe_plan(route, m)
    n_groups = ge.shape[0]

    def wcol(s, f, ge_r, gs_r, gc_r, tok_r, dst_r):
        return (ge_r[s], 0, jnp.where(gc_r[s] > 0, f, nf - 1))

    def wrow(s, f, ge_r, gs_r, gc_r, tok_r, dst_r):
        return (ge_r[s], jnp.where(gc_r[s] > 0, f, nf - 1), 0)

    grid_spec = pltpu.PrefetchScalarGridSpec(
        num_scalar_prefetch=5,
        grid=(n_groups, nf),
        in_specs=[pl.BlockSpec(memory_space=pl.ANY),
                  pl.BlockSpec((1, d), lambda s, f, *_: (0, 0)),
                  pl.BlockSpec((None, d, tf), wcol),
                  pl.BlockSpec((None, d, tf), wcol),
                  pl.BlockSpec((None, tf, d), wrow)],
        out_specs=pl.BlockSpec(memory_space=pl.ANY),
        scratch_shapes=[pltpu.VMEM((MOE_ROWS, d), bf16), pltpu.VMEM((MOE_ROWS, d), f32),
                        pltpu.VMEM((chunk * nf, d), f32), pltpu.VMEM((MOE_SUB, d), f32),
                        pltpu.SemaphoreType.DMA(()), pltpu.SemaphoreType.DMA(())],
    )
    return pl.pallas_call(
        functools.partial(_moe_kernel, nf=nf),
        grid_spec=grid_spec,
        out_shape=jax.ShapeDtypeStruct((2 * m, d), f32),
        compiler_params=_cparams(("arbitrary", "arbitrary")),
        name="moe",
    )(ge, gs, gc, row_tok, row_dst, h, g.reshape(1, d), wg, wu, wd)


def _combine_kernel(x_ref, r_ref, y1_ref, y2_ref, o_ref):
    r = r_ref[...]
    o_ref[...] = x_ref[...] + (r[:, 2:3] * y1_ref[...] + r[:, 3:4] * y2_ref[...])


def _combine(h, route, y, tm):
    m, d = h.shape
    nb = m // tm
    return pl.pallas_call(
        _combine_kernel,
        grid=(nb,),
        in_specs=[pl.BlockSpec((tm, d), lambda i: (i, 0)),
                  pl.BlockSpec((tm, N_EXPERTS), lambda i: (i, 0)),
                  pl.BlockSpec((tm, d), lambda i: (i, 0)),
                  pl.BlockSpec((tm, d), lambda i: (nb + i, 0))],
        out_specs=pl.BlockSpec((tm, d), lambda i: (i, 0)),
        out_shape=jax.ShapeDtypeStruct((m, d), f32),
        compiler_params=_cparams(("parallel",)),
        name="moe_combine",
    )(h, route, y, y)


def _combine_norm_kernel(x_ref, r_ref, y1_ref, y2_ref, g_ref, op_ref, os_ref, *, n_p):
    r = r_ref[...]
    v = _rms(x_ref[...] + (r[:, 2:3] * y1_ref[...] + r[:, 3:4] * y2_ref[...]), g_ref[...])

    @pl.when(pl.program_id(0) < n_p)
    def _():
        op_ref[...] = v

    @pl.when(pl.program_id(0) >= n_p)
    def _():
        os_ref[...] = v


def _combine_norm(h, route, y, g, mp, tm):
    m, d = h.shape
    nb = m // tm
    n_p = mp // tm
    return pl.pallas_call(
        functools.partial(_combine_norm_kernel, n_p=n_p),
        grid=(nb,),
        in_specs=[pl.BlockSpec((tm, d), lambda i: (i, 0)),
                  pl.BlockSpec((tm, N_EXPERTS), lambda i: (i, 0)),
                  pl.BlockSpec((tm, d), lambda i: (i, 0)),
                  pl.BlockSpec((tm, d), lambda i: (nb + i, 0)),
                  pl.BlockSpec((1, d), lambda i: (0, 0))],
        out_specs=[pl.BlockSpec((tm, d), lambda i: (jnp.minimum(i, n_p - 1), 0)),
                   pl.BlockSpec((tm, d), lambda i: (jnp.maximum(i - n_p, 0), 0))],
        out_shape=[jax.ShapeDtypeStruct((mp, d), f32), jax.ShapeDtypeStruct((m - mp, d), f32)],
        compiler_params=_cparams(("arbitrary",)),
        name="moe_combine_norm",
    )(h, route, y, y, g.reshape(1, d))


def _final_norm_kernel(x_ref, g_ref, o_ref):
    o_ref[...] = _rms(x_ref[...], g_ref[...])


def _final_norm(h, g, row0, rows, tm):
    d = h.shape[1]
    rb = row0 // tm
    return pl.pallas_call(
        _final_norm_kernel,
        grid=(rows // tm,),
        in_specs=[pl.BlockSpec((tm, d), lambda i: (rb + i, 0)),
                  pl.BlockSpec((1, d), lambda i: (0, 0))],
        out_specs=pl.BlockSpec((tm, d), lambda i: (i, 0)),
        out_shape=jax.ShapeDtypeStruct((rows, d), f32),
        compiler_params=_cparams(("parallel",)),
        name="final_norm",
    )(h, g.reshape(1, d))


def _cast_kernel(w_ref, o_ref):
    o_ref[...] = w_ref[...].astype(bf16)


def _cast_bf16(w, layer, tr):
    _, r, c = w.shape
    return pl.pallas_call(
        _cast_kernel,
        grid=(r // tr,),
        in_specs=[pl.BlockSpec((None, tr, c), lambda i: (layer, i, 0))],
        out_specs=pl.BlockSpec((tr, c), lambda i: (i, 0)),
        out_shape=jax.ShapeDtypeStruct((r, c), bf16),
        compiler_params=_cparams(("parallel",)),
        name="cast_bf16",
    )(w)


def _pack_kernel(w_ref, wp_ref, wk_ref, wv_ref, wg_ref):
    o = np.cumsum((0,) + IN_SIZES)
    w = w_ref[...]
    rows = w.shape[0]
    cut = lambda a, b: w[:, int(o[a]):int(o[b])].astype(bf16)
    pad = lambda a: jnp.concatenate([a, jnp.zeros((rows, LANE - a.shape[1]), bf16)], axis=1)
    wp_ref[...] = jnp.concatenate([cut(0, 1), cut(3, 4), cut(6, 7), cut(7, 11)], axis=1)
    wk_ref[...] = jnp.concatenate([cut(1, 2), pad(cut(4, 5)), pad(cut(5, 6))], axis=1)
    wv_ref[...] = cut(2, 3)
    wg_ref[...] = cut(11, 12)


def _pack_in_proj(w, layer, tr):
    _, d, n = w.shape
    widths = (N_PACK, D_A + 2 * LANE, D_A, N_BRANCH * D_MODEL)
    return pl.pallas_call(
        _pack_kernel,
        grid=(d // tr,),
        in_specs=[pl.BlockSpec((None, tr, n), lambda i: (layer, i, 0))],
        out_specs=[pl.BlockSpec((tr, c), lambda i: (i, 0)) for c in widths],
        out_shape=[jax.ShapeDtypeStruct((d, c), bf16) for c in widths],
        compiler_params=_cparams(("parallel",)),
        name="pack_w_in",
    )(w)


def kernel(x_prompt, x_sample, cache_k, cache_v, cache_kidx, cache_pool, state_hgrn, norm_mix_g, norm_ffn_g, final_norm_g, w_in, w_proj_a, w_proj_b, w_proj_c, w_out, pool_w, pool_scale, hgrn_lb_logits, hgrn_norm_g, ffn_w_gate, ffn_w_up, ffn_w_down, moe_router, moe_w_gate, moe_w_up, moe_w_down):
    batch, seq, d = x_prompt.shape
    nb, nq, _ = x_sample.shape
    depth = w_in.shape[0]
    past = cache_k.shape[2]
    mp = batch * seq
    ms = nb * nq
    m = mp + ms

    lb_soft = jax.nn.softmax(hgrn_lb_logits.astype(f32), axis=0)
    lb_all = jnp.cumsum(lb_soft, axis=0) - lb_soft[0:1]

    h = jnp.concatenate([x_prompt.reshape(mp, d), x_sample.reshape(ms, d)], axis=0)
    s0_prompt = jnp.zeros((batch, C_HEADS, C_HEAD_DIM, C_V_DIM), f32)
    tm = 768 if m % 768 == 0 else 256
    tmh = tm // 2

    kp = jnp.zeros((depth * mp, A_HEADS, A_HEAD_DIM), f32)
    vp = jnp.zeros((depth * mp, A_HEADS, A_HEAD_DIM), f32)
    ks = jnp.zeros((depth * ms, A_HEADS, A_HEAD_DIM), f32)
    vs = jnp.zeros((depth * ms, A_HEADS, A_HEAD_DIM), f32)
    tkv = ATT_BLOCK
    tkp = 1024 if mp % 1024 == 0 else ATT_BLOCK
    outs = [[] for _ in range(6)]
    for l in range(depth):
        wp, wk, wv, wgates = _pack_in_proj(w_in, l, 128)
        hp = _inproj(h, norm_mix_g[l], wp, tm, 1024)
        kp, kb16, kiw_p = _inproj_k(h, norm_mix_g[l], wk, kp, l, 0, mp, tkp)
        ks, _, kiw_s = _inproj_k(h, norm_mix_g[l], wk, ks, l, mp, ms, tkv)
        vp, vt16 = _inproj_v(h, norm_mix_g[l], wv, vp, l, 0, mp, tkp, True)
        vs, = _inproj_v(h, norm_mix_g[l], wv, vs, l, mp, ms, tkv, False)

        oa = _dsa_prompt(hp, kb16, vt16, kiw_p, jnp.zeros((m, D_A), bf16), batch, seq)
        oa = _dsa_sample(hp, ks, vs, kiw_s, oa, l, mp, nb, nq, cache_k[l], cache_v[l], cache_kidx[l])
        ob = _pool(hp, jnp.zeros((m, D_B), bf16), 0, batch, seq, 256, None, pool_w[l], pool_scale[l], 0)
        prev = jnp.pad(cache_pool[l], ((0, 0), (1, 0), (0, 0)))
        ob = _pool(hp, ob, mp, nb, nq, nq, prev, pool_w[l], pool_scale[l], past)
        oc, st_p = _hgrn(hp, jnp.zeros((m, D_C), bf16), 0, batch, seq, 256, lb_all[l], hgrn_norm_g[l], s0_prompt)
        oc, st_s = _hgrn(hp, oc, mp, nb, nq, nq, lb_all[l], hgrn_norm_g[l], state_hgrn[l])

        h = _merge(h, norm_mix_g[l], oa, ob, oc, wgates, _cast_bf16(w_proj_a, l, 256), _cast_bf16(w_proj_b, l, 256),
                   _cast_bf16(w_proj_c, l, 256), _cast_bf16(w_out, l, 256), tmh, 512)

        uu = hp[:, COL_U:COL_U + D_B]
        outs[0].append(kiw_p[:, KIW_KI:KIW_KI + IDX_DIM].reshape(batch, seq, IDX_DIM))
        outs[1].append(uu[:mp].reshape(batch, seq, D_B)[:, -POOL_STATE:])
        outs[2].append(st_p)
        outs[3].append(kiw_s[:, KIW_KI:KIW_KI + IDX_DIM].reshape(nb, nq, IDX_DIM))
        u_ext = jnp.concatenate([cache_pool[l], uu[mp:].reshape(nb, nq, D_B)], axis=1)
        outs[4].append(u_ext[:, -POOL_STATE:])
        outs[5].append(st_s)

        j = l // 2
        if l % 2 == 0:
            h = _ffn(h, norm_ffn_g[l], _cast_bf16(ffn_w_gate, j, 256), _cast_bf16(ffn_w_up, j, 256),
                     _cast_bf16(ffn_w_down, j, 512), tm, 512)
        else:
            route = _router(h, norm_ffn_g[l], moe_router[j], tm)
            y = _moe(h, norm_ffn_g[l], route, moe_w_gate[j], moe_w_up[j], moe_w_down[j], 512)
            if l == depth - 1:
                y_prompt, y_sample = _combine_norm(h, route, y, final_norm_g, mp, tkv)
                h = None
            else:
                h = _combine(h, route, y, tmh)

    if h is not None:
        y_prompt = _final_norm(h, final_norm_g, 0, mp, 256)
        y_sample = _final_norm(h, final_norm_g, mp, ms, 256)
    y_prompt = y_prompt.reshape(batch, seq, d)
    y_sample = y_sample.reshape(nb, nq, d)
    kidx_p, pool_p, st_p, kidx_s, pool_s, st_s = (jnp.stack(o) for o in outs)
    return (y_prompt, y_sample,
            kp.reshape(depth, batch, seq, A_HEADS, A_HEAD_DIM), vp.reshape(depth, batch, seq, A_HEADS, A_HEAD_DIM),
            kidx_p, pool_p, st_p,
            ks.reshape(depth, nb, nq, A_HEADS, A_HEAD_DIM), vs.reshape(depth, nb, nq, A_HEADS, A_HEAD_DIM),
            kidx_s, pool_s, st_s)
```

```python
import functools

import numpy as np
import jax
import jax.numpy as jnp
from jax import lax
from jax.experimental import pallas as pl
from jax.experimental.pallas import tpu as pltpu

f32 = jnp.float32
bf16 = jnp.bfloat16

D_MODEL = 2048
CHUNK = 64
A_HEADS = 8
A_HEAD_DIM = 128
D_A = A_HEADS * A_HEAD_DIM
IDX_HEADS = 8
IDX_DIM = 64
TOPK_MAX = 256
POOL_WINDOWS = (2, 4, 8, 16)
POOL_GROUP_DIM = 128
D_B = len(POOL_WINDOWS) * POOL_GROUP_DIM
POOL_STATE = max(POOL_WINDOWS) - 1
C_HEADS = 4
C_HEAD_DIM = 128
C_V_DIM = 128
D_C = C_HEADS * C_HEAD_DIM
HGRN_BLOCK = 16
N_BRANCH = 3
N_EXPERTS = 8
IN_SIZES = (D_A, D_A, D_A, IDX_HEADS * IDX_DIM, IDX_DIM, IDX_HEADS,
            D_B, D_C, D_C, C_HEADS * C_V_DIM, C_HEADS * C_V_DIM, N_BRANCH * D_MODEL)
EPS = 1e-6
NEG = -1e30
LB_FLOOR = 1e-30
INT_MIN = -2 ** 31

LOG2E = 1.4426950408889634
LANE = 128
COUNT_ROWS = 64
ATT_BLOCK = 256
COL_Q = 0
COL_QI = D_A
COL_U = COL_QI + 512
COL_CQ = COL_U + 512
COL_CF = COL_CQ + 512
COL_CI = COL_CF + 512
COL_CG = COL_CI + 512
N_PACK = COL_CG + 512
KIW_KI, KIW_W = 0, LANE

VMEM_LIMIT = 56 * 1024 * 1024

NT_DIMS = (((1,), (1,)), ((), ()))
TN_DIMS = (((0,), (0,)), ((), ()))


def _cparams(sem):
    return pltpu.CompilerParams(dimension_semantics=sem, vmem_limit_bytes=VMEM_LIMIT)


def _rms(x, g):
    ms = jnp.mean(x * x, axis=-1, keepdims=True)
    return x * lax.rsqrt(ms + EPS) * g


def _silu(x):
    return x * jax.nn.sigmoid(x)


def _sort_key(x):
    bits = pltpu.bitcast(x, jnp.int32)
    return bits ^ ((bits >> 31) & jnp.int32(0x7FFFFFFF))


def _kth_largest(count_ge, kk):
    zero = jnp.zeros(kk.shape, jnp.int32)
    t0 = jnp.where(count_ge(zero) >= kk, zero, jnp.int32(INT_MIN))

    def bit_body(i, t):
        c = t | lax.shift_left(jnp.int32(1), 30 - i)
        return jnp.where(count_ge(c) >= kk, c, t)

    return lax.fori_loop(0, 31, bit_body, t0)


def _inproj_kernel(x_ref, g_ref, w_ref, o_ref, xn_ref):
    @pl.when(pl.program_id(1) == 0)
    def _():
        xn_ref[...] = _rms(x_ref[...], g_ref[...]).astype(bf16)

    o_ref[...] = jnp.dot(xn_ref[...], w_ref[...], preferred_element_type=f32)


def _inproj(h, g, wp, tm, tn):
    m, d = h.shape
    n = wp.shape[1]
    return pl.pallas_call(
        _inproj_kernel,
        grid=(m // tm, n // tn),
        in_specs=[pl.BlockSpec((tm, d), lambda i, j: (i, 0)),
                  pl.BlockSpec((1, d), lambda i, j: (0, 0)),
                  pl.BlockSpec((d, tn), lambda i, j: (0, j))],
        out_specs=pl.BlockSpec((tm, tn), lambda i, j: (i, j)),
        out_shape=jax.ShapeDtypeStruct((m, n), f32),
        scratch_shapes=[pltpu.VMEM((tm, d), bf16)],
        compiler_params=_cparams(("parallel", "arbitrary")),
        name="inproj",
    )(h, g.reshape(1, d), wp)


def _inproj_k_kernel(x_ref, g_ref, w_ref, dst_in, k_ref, kb_ref, kiw_ref):
    del dst_in
    xn = _rms(x_ref[...], g_ref[...]).astype(bf16)
    r = jnp.dot(xn, w_ref[...], preferred_element_type=f32)
    for hh in range(A_HEADS):
        k_ref[:, hh, :] = r[:, hh * A_HEAD_DIM:(hh + 1) * A_HEAD_DIM]
    kb_ref[...] = r[:, :D_A].astype(bf16)
    kiw_ref[...] = r[:, D_A:]


def _inproj_k(h, g, wk, dst, layer, row0, rows, tm):
    d = h.shape[1]
    n = wk.shape[1]
    nt, rb = rows // tm, row0 // tm
    row = lambda i: (i, 0)
    return pl.pallas_call(
        _inproj_k_kernel,
        grid=(nt,),
        in_specs=[pl.BlockSpec((tm, d), lambda i: (rb + i, 0)),
                  pl.BlockSpec((1, d), lambda i: (0, 0)),
                  pl.BlockSpec((d, n), lambda i: (0, 0)),
                  pl.BlockSpec(memory_space=pl.ANY)],
        out_specs=[pl.BlockSpec((tm, A_HEADS, A_HEAD_DIM), lambda i: (layer * nt + i, 0, 0)),
                   pl.BlockSpec((tm, D_A), row), pl.BlockSpec((tm, n - D_A), row)],
        out_shape=[jax.ShapeDtypeStruct(dst.shape, f32),
                   jax.ShapeDtypeStruct((rows, D_A), bf16), jax.ShapeDtypeStruct((rows, n - D_A), f32)],
        input_output_aliases={3: 0},
        compiler_params=_cparams(("parallel",)),
        name="inproj_k",
    )(h, g.reshape(1, d), wk, dst)


def _inproj_v_kernel(x_ref, g_ref, w_ref, dst_in, v_ref, *vt_ref):
    del dst_in
    xn = _rms(x_ref[...], g_ref[...]).astype(bf16)
    r = jnp.dot(xn, w_ref[...], preferred_element_type=f32)
    for hh in range(A_HEADS):
        v_ref[:, hh, :] = r[:, hh * A_HEAD_DIM:(hh + 1) * A_HEAD_DIM]
    for vt in vt_ref:
        for c in range(vt.shape[0]):
            for hh in range(A_HEADS):
                blk = r[c * ATT_BLOCK:(c + 1) * ATT_BLOCK, hh * A_HEAD_DIM:(hh + 1) * A_HEAD_DIM]
                vt[c, hh] = blk.T.astype(bf16)


def _inproj_v(h, g, wv, dst, layer, row0, rows, tm, transposed):
    d = h.shape[1]
    nt, rb = rows // tm, row0 // tm
    out_specs = [pl.BlockSpec((tm, A_HEADS, A_HEAD_DIM), lambda i: (layer * nt + i, 0, 0))]
    out_shape = [jax.ShapeDtypeStruct(dst.shape, f32)]
    if transposed:
        out_specs.append(pl.BlockSpec((tm // ATT_BLOCK, A_HEADS, A_HEAD_DIM, ATT_BLOCK), lambda i: (i, 0, 0, 0)))
        out_shape.append(jax.ShapeDtypeStruct((rows // ATT_BLOCK, A_HEADS, A_HEAD_DIM, ATT_BLOCK), bf16))
    return pl.pallas_call(
        _inproj_v_kernel,
        grid=(nt,),
        in_specs=[pl.BlockSpec((tm, d), lambda i: (rb + i, 0)),
                  pl.BlockSpec((1, d), lambda i: (0, 0)),
                  pl.BlockSpec((d, D_A), lambda i: (0, 0)),
                  pl.BlockSpec(memory_space=pl.ANY)],
        out_specs=out_specs,
        out_shape=out_shape,
        input_output_aliases={3: 0},
        compiler_params=_cparams(("parallel",)),
        name="inproj_v",
    )(h, g.reshape(1, d), wv, dst)


def _idx_scores(qi, w, kib):
    acc = jnp.zeros((qi.shape[0], kib.shape[0]), f32)
    for hh in range(IDX_HEADS):
        d = lax.dot_general(qi[:, hh * IDX_DIM:(hh + 1) * IDX_DIM], kib, NT_DIMS,
                            preferred_element_type=f32)
        acc = acc + w[:, hh:hh + 1] * jnp.maximum(d, 0.0)
    return acc


def _fold_lanes(x):
    part = x[:, :LANE]
    for j in range(1, x.shape[1] // LANE):
        part = part + x[:, j * LANE:(j + 1) * LANE]
    return part


def _tri_incl(n):
    r = lax.broadcasted_iota(jnp.int32, (n, n), 0)
    c = lax.broadcasted_iota(jnp.int32, (n, n), 1)
    return jnp.where(r <= c, 1.0, 0.0).astype(bf16)


def _dsa_prompt_kernel(q_ref, k_ref, vt_ref, qi_ref, ki_ref, w_ref, _o_in, o_ref, keys_ref, bias_ref, qs_ref,
                       *acc_refs, tq, topk):
    qb = pl.program_id(1)
    nkb = qb + 1
    hd = A_HEAD_DIM

    qi = qi_ref[...].astype(bf16)
    wt = (w_ref[...] * (IDX_HEADS ** -0.5 * IDX_DIM ** -0.5)).T
    srow = lax.broadcasted_iota(jnp.int32, (tq, tq), 0)
    qcol = lax.broadcasted_iota(jnp.int32, (tq, tq), 1)
    diag_adm = (srow // CHUNK) <= (qcol // CHUNK)

    def score_body(kb, c):
        off = pl.multiple_of(kb * tq, tq)
        kib = ki_ref[pl.ds(off, tq), :][:, :IDX_DIM].astype(bf16)
        ds = [lax.dot_general(kib, qi[:, hh * IDX_DIM:(hh + 1) * IDX_DIM], NT_DIMS, preferred_element_type=f32)
              for hh in range(IDX_HEADS)]
        acc = jnp.zeros((tq, tq), f32)
        for hh in range(IDX_HEADS):
            acc = acc + wt[hh:hh + 1, :] * jnp.maximum(ds[hh], 0.0)
        adm = jnp.logical_or(kb < qb, diag_adm)
        keys_ref[kb] = jnp.where(adm, _sort_key(acc), jnp.int32(INT_MIN))
        return c

    lax.fori_loop(0, nkb, score_body, 0)

    def count(cmp, c):
        cb = jnp.broadcast_to(c, (COUNT_ROWS, tq))

        def body(kb, acc):
            for r0 in range(0, tq, COUNT_ROWS):
                acc = acc + jnp.where(cmp(keys_ref[kb, r0:r0 + COUNT_ROWS, :], cb), 1.0, 0.0)
            return acc
        acc = lax.fori_loop(0, nkb, body, jnp.zeros((COUNT_ROWS, tq), f32))
        return jnp.sum(acc, axis=0, keepdims=True)

    lane = lax.broadcasted_iota(jnp.int32, (1, tq), 1)
    n_adm = ((qb * tq + lane) // CHUNK + 1) * CHUNK
    kk = jnp.minimum(topk, n_adm).astype(f32)
    thr = _kth_largest(lambda c: count(lambda k, t: k >= t, c), kk)
    n_ge = count(lambda k, t: k >= t, thr)
    has_tie = jnp.max(n_ge - kk) > 0.0

    @pl.when(jnp.logical_not(has_tie))
    def _():
        def body(kb, c):
            bias_ref[kb] = jnp.where(keys_ref[kb] >= thr, 0.0, NEG)
            return c
        lax.fori_loop(0, nkb, body, 0)

    @pl.when(has_tie)
    def _():
        need = kk - count(lambda k, t: k > t, thr)
        tri = jnp.where(qcol <= srow, 1.0, 0.0).astype(bf16)

        def body(kb, seen):
            k = keys_ref[kb]
            eq = jnp.where(k == thr, 1.0, 0.0)
            rank = seen + jnp.dot(tri, eq.astype(bf16), preferred_element_type=f32)
            take = jnp.logical_or(k > thr, jnp.logical_and(k == thr, rank <= need))
            bias_ref[kb] = jnp.where(take, 0.0, NEG)
            return seen + jnp.sum(eq, axis=0, keepdims=True)
        lax.fori_loop(0, nkb, body, jnp.zeros((1, tq), f32))

    qs_ref[...] = (q_ref[...] * (hd ** -0.5 * LOG2E)).astype(bf16)
    for h in range(A_HEADS):
        acc_refs[h][...] = jnp.zeros((hd, tq), f32)

    def att_body(kb, carry):
        ms, ls = carry
        off = pl.multiple_of(kb * tq, tq)
        kblk = k_ref[pl.ds(off, tq), :]
        bias = bias_ref[kb]
        sts = []
        for h in range(A_HEADS):
            cols = slice(h * hd, (h + 1) * hd)
            sts.append(lax.dot_general(kblk[:, cols], qs_ref[:, cols], NT_DIMS, preferred_element_type=f32))
        new_ms, new_ls, ps, corrs = [], [], [], []
        for h in range(A_HEADS):
            st = sts[h] + bias
            m_new = jnp.maximum(ms[h], jnp.max(st, axis=0, keepdims=True))
            p = jnp.exp2(st - m_new)
            corr = jnp.exp2(ms[h] - m_new)
            new_ls.append(corr * ls[h] + jnp.sum(p, axis=0, keepdims=True))
            new_ms.append(m_new)
            ps.append(p.astype(bf16))
            corrs.append(corr)
        for h in range(A_HEADS):
            acc_refs[h][...] = corrs[h] * acc_refs[h][...] + jnp.dot(vt_ref[kb, h], ps[h],
                                                                     preferred_element_type=f32)
        return tuple(new_ms), tuple(new_ls)

    init = (tuple(jnp.full((1, tq), NEG, f32) for _ in range(A_HEADS)),
            tuple(jnp.zeros((1, tq), f32) for _ in range(A_HEADS)))
    _, ls = lax.fori_loop(0, nkb, att_body, init)
    for h in range(A_HEADS):
        o_ref[:, h * hd:(h + 1) * hd] = (acc_refs[h][...] / ls[h]).T.astype(o_ref.dtype)


def _dsa_prompt(hp, kb16, vt16, kiw, obuf, batch, seq):
    tq = ATT_BLOCK
    nq = seq // tq
    topk = min(TOPK_MAX, seq // 4)
    kern = functools.partial(_dsa_prompt_kernel, tq=tq, topk=topk)
    once = pl.Buffered(1)
    return pl.pallas_call(
        kern,
        grid=(batch, nq),
        in_specs=[
            pl.BlockSpec((tq, D_A), lambda b, i: (b * nq + i, COL_Q // D_A)),
            pl.BlockSpec((seq, D_A), lambda b, i: (b, 0), pipeline_mode=once),
            pl.BlockSpec((nq, A_HEADS, A_HEAD_DIM, tq), lambda b, i: (b, 0, 0, 0), pipeline_mode=once),
            pl.BlockSpec((tq, 512), lambda b, i: (b * nq + i, COL_QI // 512)),
            pl.BlockSpec((seq, LANE), lambda b, i: (b, KIW_KI // LANE), pipeline_mode=once),
            pl.BlockSpec((tq, LANE), lambda b, i: (b * nq + i, KIW_W // LANE)),
            pl.BlockSpec(memory_space=pl.ANY),
        ],
        out_specs=pl.BlockSpec((tq, D_A), lambda b, i: (b * nq + i, 0)),
        out_shape=jax.ShapeDtypeStruct(obuf.shape, obuf.dtype),
        input_output_aliases={6: 0},
        scratch_shapes=[pltpu.VMEM((nq, tq, tq), jnp.int32), pltpu.VMEM((nq, tq, tq), f32),
                        pltpu.VMEM((tq, D_A), bf16)]
                       + [pltpu.VMEM((A_HEAD_DIM, tq), f32) for _ in range(A_HEADS)],
        compiler_params=_cparams(("parallel", "arbitrary")),
        name="dsa_prompt",
    )(hp, kb16, vt16, hp, kiw, kiw, obuf)


def _dsa_sample_kernel(q_ref, kn_ref, vn_ref, qi_ref, kin_ref, w_ref, kc_ref, vc_ref, kic_ref, _o_in, o_ref,
                       biasc_ref, biasn_ref, *, topk):
    nq = q_ref.shape[0]
    past = kc_ref.shape[0]

    def _select():
        qi = qi_ref[...].astype(bf16)
        w = w_ref[...][:, :IDX_HEADS] * (IDX_HEADS ** -0.5 * IDX_DIM ** -0.5)
        key_c = _sort_key(_idx_scores(qi, w, kic_ref[...].astype(bf16)))
        key_n = _sort_key(_idx_scores(qi, w, kin_ref[...][:, :IDX_DIM].astype(bf16)))

        def count(pred):
            cc = jnp.sum(_fold_lanes(jnp.where(pred(key_c), 1.0, 0.0)), axis=-1, keepdims=True)
            return cc + jnp.sum(jnp.where(pred(key_n), 1.0, 0.0), axis=-1, keepdims=True)

        kk = jnp.full((nq, 1), float(topk), f32)
        thr = _kth_largest(lambda c: count(lambda k: k >= c), kk)
        n_ge = count(lambda k: k >= thr)
        has_tie = jnp.max(n_ge - kk) > 0.0

        @pl.when(jnp.logical_not(has_tie))
        def _():
            biasc_ref[...] = jnp.where(key_c >= thr, 0.0, NEG)
            biasn_ref[...] = jnp.where(key_n >= thr, 0.0, NEG)

        @pl.when(has_tie)
        def _():
            need = kk - count(lambda k: k > thr)
            tri = _tri_incl(LANE)
            seen = jnp.zeros((nq, 1), f32)
            for j in range(past // LANE):
                k = key_c[:, j * LANE:(j + 1) * LANE]
                eq = jnp.where(k == thr, 1.0, 0.0)
                rank = seen + jnp.dot(eq.astype(bf16), tri, preferred_element_type=f32)
                take = jnp.logical_or(k > thr, jnp.logical_and(k == thr, rank <= need))
                biasc_ref[:, j * LANE:(j + 1) * LANE] = jnp.where(take, 0.0, NEG)
                seen = seen + jnp.sum(eq, axis=-1, keepdims=True)
            eq = jnp.where(key_n == thr, 1.0, 0.0)
            rank = seen + jnp.dot(eq.astype(bf16), _tri_incl(nq), preferred_element_type=f32)
            take = jnp.logical_or(key_n > thr, jnp.logical_and(key_n == thr, rank <= need))
            biasn_ref[...] = jnp.where(take, 0.0, NEG)

    _select()

    hd = A_HEAD_DIM
    q = (q_ref[...] * (hd ** -0.5 * LOG2E)).astype(bf16)
    bias_c = biasc_ref[...]
    bias_n = biasn_ref[...]
    scs, sns = [], []
    for h in range(A_HEADS):
        cols = slice(h * hd, (h + 1) * hd)
        scs.append(lax.dot_general(q[:, cols], kc_ref[:, cols].astype(bf16), NT_DIMS, preferred_element_type=f32))
        sns.append(lax.dot_general(q[:, cols], kn_ref[:, h, :].astype(bf16), NT_DIMS, preferred_element_type=f32))
    pcs, pns, ls = [], [], []
    for h in range(A_HEADS):
        s_c = scs[h] + bias_c
        s_n = sns[h] + bias_n
        m = jnp.maximum(jnp.max(s_c, axis=-1, keepdims=True), jnp.max(s_n, axis=-1, keepdims=True))
        p_c = jnp.exp2(s_c - m)
        p_n = jnp.exp2(s_n - m)
        ls.append(jnp.sum(p_c, axis=-1, keepdims=True) + jnp.sum(p_n, axis=-1, keepdims=True))
        pcs.append(p_c.astype(bf16))
        pns.append(p_n.astype(bf16))
    for h in range(A_HEADS):
        cols = slice(h * hd, (h + 1) * hd)
        acc = jnp.dot(pcs[h], vc_ref[:, cols].astype(bf16), preferred_element_type=f32)
        acc = acc + jnp.dot(pns[h], vn_ref[:, h, :].astype(bf16), preferred_element_type=f32)
        o_ref[:, cols] = (acc / ls[h]).astype(o_ref.dtype)


def _dsa_sample(hp, ks, vs, kiw, obuf, layer, row0, nb, nq, cache_k, cache_v, cache_kidx):
    past = cache_k.shape[2]
    topk = min(TOPK_MAX, (past + nq) // 4)
    kc, vc = cache_k, cache_v
    rb = row0 // nq
    kern = functools.partial(_dsa_sample_kernel, topk=topk)
    return pl.pallas_call(
        kern,
        grid=(nb,),
        in_specs=[
            pl.BlockSpec((nq, D_A), lambda b: (rb + b, COL_Q // D_A)),
            pl.BlockSpec((nq, A_HEADS, A_HEAD_DIM), lambda b: (layer * nb + b, 0, 0)),
            pl.BlockSpec((nq, A_HEADS, A_HEAD_DIM), lambda b: (layer * nb + b, 0, 0)),
            pl.BlockSpec((nq, 512), lambda b: (rb + b, COL_QI // 512)),
            pl.BlockSpec((nq, LANE), lambda b: (b, KIW_KI // LANE)),
            pl.BlockSpec((nq, LANE), lambda b: (b, KIW_W // LANE)),
            pl.BlockSpec((None, None, past, D_A), lambda b: (layer, b, 0, 0)),
            pl.BlockSpec((None, None, past, D_A), lambda b: (layer, b, 0, 0)),
            pl.BlockSpec((None, None, past, IDX_DIM), lambda b: (layer, b, 0, 0)),
            pl.BlockSpec(memory_space=pl.ANY),
        ],
        out_specs=pl.BlockSpec((nq, D_A), lambda b: (rb + b, 0)),
        out_shape=jax.ShapeDtypeStruct(obuf.shape, obuf.dtype),
        input_output_aliases={9: 0},
        scratch_shapes=[pltpu.VMEM((nq, past), f32), pltpu.VMEM((nq, nq), f32)],
        compiler_params=_cparams(("parallel",)),
        name="dsa_sample",
    )(hp, ks, vs, hp, kiw, kiw, kc, vc, cache_kidx, obuf)


def _pool_kernel(prev_ref, u_ref, pw_ref, ps_ref, _o_in, o_ref, *, tt, first_pos, zero_first):
    tb = pl.program_id(1)
    halo = POOL_STATE + 1
    prev = prev_ref[...]
    if zero_first:
        prev = jnp.where(tb == 0, 0.0, prev)
    cur = u_ref[...]
    ext = jnp.concatenate([prev, cur], axis=0)
    pos = first_pos + tb * tt + lax.broadcasted_iota(jnp.int32, (tt, 1), 0)
    outs = []
    for gi, wdw in enumerate(POOL_WINDOWS):
        cols = slice(gi * POOL_GROUP_DIM, (gi + 1) * POOL_GROUP_DIM)
        a = ext[:, cols]
        n = 1
        while n < wdw:
            a = a[n:] + a[:-n]
            n *= 2
        s = a[halo + 1 - wdw: halo + 1 - wdw + tt]
        cnt = jnp.minimum(pos + 1, wdw).astype(f32)
        d = s / cnt - cur[:, cols]
        outs.append(jnp.dot(d.astype(bf16), pw_ref[gi].astype(bf16), preferred_element_type=f32))
    o = jnp.concatenate(outs, axis=-1) * ps_ref[...]
    o_ref[...] = o.astype(o_ref.dtype)


def _pool(hp, obuf, row0, nb, t, tt, prev, pool_w, pool_scale, first_pos):
    ntb = t // tt
    halo = POOL_STATE + 1
    rb = row0 // tt
    if prev is None:
        prev_arr = hp
        per = tt // halo
        prev_spec = pl.BlockSpec(
            (halo, D_B), lambda b, i: (jnp.maximum((row0 // halo) + (b * ntb + i) * per - 1, 0), COL_U // D_B))
    else:
        prev_arr = prev
        prev_spec = pl.BlockSpec((None, halo, D_B), lambda b, i: (b, 0, 0))
    kern = functools.partial(_pool_kernel, tt=tt, first_pos=first_pos, zero_first=prev is None)
    return pl.pallas_call(
        kern,
        grid=(nb, ntb),
        in_specs=[prev_spec,
                  pl.BlockSpec((tt, D_B), lambda b, i: (rb + b * ntb + i, COL_U // D_B)),
                  pl.BlockSpec((len(POOL_WINDOWS), POOL_GROUP_DIM, POOL_GROUP_DIM), lambda b, i: (0, 0, 0)),
                  pl.BlockSpec((1, D_B), lambda b, i: (0, 0)),
                  pl.BlockSpec(memory_space=pl.ANY)],
        out_specs=pl.BlockSpec((tt, D_B), lambda b, i: (rb + b * ntb + i, 0)),
        out_shape=jax.ShapeDtypeStruct(obuf.shape, obuf.dtype),
        input_output_aliases={4: 0},
        compiler_params=_cparams(("parallel", "arbitrary")),
        name="pool",
    )(prev_arr, hp, pool_w, pool_scale.reshape(1, D_B), obuf)


SAFE_LOG_RANGE = 80.0


def _hgrn_kernel(cq_ref, cf_ref, ci_ref, cg_ref, lb_ref, ng_ref, s0_ref, _o_in, o_ref, sn_ref,
                 st_ref, b_ref, q_ref, k_ref, osc_ref, *, tt, blk):
    tb = pl.program_id(1)
    nh = C_HEADS
    hd = C_HEAD_DIM

    @pl.when(tb == 0)
    def _init():
        for h in range(nh):
            st_ref[h] = s0_ref[h].T

    lb = lb_ref[...]
    f = jnp.maximum(lb, LB_FLOOR) + (1.0 - lb) * jax.nn.sigmoid(cf_ref[...])
    g = jnp.log(f)
    k_ref[...] = 1.0 - f
    q_ref[...] = _silu(cq_ref[...])
    r = lax.broadcasted_iota(jnp.int32, (tt, tt), 0)
    c = lax.broadcasted_iota(jnp.int32, (tt, tt), 1)
    tri = jnp.where(jnp.logical_and(r // blk == c // blk, c <= r), 1.0, 0.0).astype(bf16)
    g1 = g.astype(bf16)
    e1 = g - g1.astype(f32)
    g2 = e1.astype(bf16)
    g3 = (e1 - g2.astype(f32)).astype(bf16)
    b = (jnp.dot(tri, g1, preferred_element_type=f32) + jnp.dot(tri, g2, preferred_element_type=f32)
         + jnp.dot(tri, g3, preferred_element_type=f32))
    b_ref[...] = b
    safe = jnp.min(b) > -SAFE_LOG_RANGE

    trow = lax.broadcasted_iota(jnp.int32, (blk, 1), 0)

    def exact_chunk(ci):
        r0 = pl.multiple_of(ci * blk, blk)
        for h in range(nh):
            rows = pl.ds(r0, blk)
            cols = slice(h * hd, (h + 1) * hd)
            bb = b_ref[rows, cols]
            qh = q_ref[rows, cols]
            kh = k_ref[rows, cols]
            ih = ci_ref[rows, cols]
            bl = bb[blk - 1:blk, :]
            ke = (kh * jnp.exp(bl - bb)).astype(bf16)
            ib = ih.astype(bf16)
            st = st_ref[h]
            o = lax.dot_general((qh * jnp.exp(bb)).astype(bf16), st.astype(bf16), NT_DIMS,
                                preferred_element_type=f32)
            for s in range(blk):
                live = trow >= s
                dec = jnp.exp(jnp.where(live, bb - bb[s:s + 1, :], 0.0))
                a_col = jnp.sum(qh * kh[s:s + 1, :] * dec, axis=-1, keepdims=True)
                o = o + jnp.where(live, a_col, 0.0) * ih[s:s + 1, :]
            osc_ref[rows, cols] = o
            st_ref[h] = st * jnp.exp(bl) + lax.dot_general(ib, ke, TN_DIMS, preferred_element_type=f32)

    @pl.when(safe)
    def _():
        same_blk = r // blk == c // blk
        blk_causal = jnp.logical_and(same_blk, c <= r)
        ones_blk = jnp.where(same_blk, 1.0, 0.0).astype(bf16)
        bl_all = (jnp.dot(ones_blk, g1, preferred_element_type=f32) + jnp.dot(ones_blk, g2, preferred_element_type=f32)
                  + jnp.dot(ones_blk, g3, preferred_element_type=f32))
        for h in range(nh):
            cols = slice(h * hd, (h + 1) * hd)
            bb = b[:, cols]
            blh = bl_all[:, cols]
            qh = q_ref[:, cols]
            ib = ci_ref[:, cols].astype(bf16)
            ke = (k_ref[:, cols] * jnp.exp(blh - bb)).astype(bf16)
            qe = (qh * jnp.exp(bb - blh)).astype(bf16)
            qb = (qh * jnp.exp(bb)).astype(bf16)
            a = jnp.where(blk_causal, lax.dot_general(qe, ke, NT_DIMS, preferred_element_type=f32), 0.0)
            o_intra = jnp.dot(a.astype(bf16), ib, preferred_element_type=f32)
            incs = [lax.dot_general(ib[c0:c0 + blk], ke[c0:c0 + blk], TN_DIMS, preferred_element_type=f32)
                    for c0 in range(0, tt, blk)]
            dec = jnp.exp(blh)
            st = st_ref[h]
            o_inter = []
            for ci, c0 in enumerate(range(0, tt, blk)):
                o_inter.append(lax.dot_general(qb[c0:c0 + blk], st.astype(bf16), NT_DIMS,
                                               preferred_element_type=f32))
                st = st * dec[c0:c0 + 1, :] + incs[ci]
            st_ref[h] = st
            osc_ref[:, cols] = o_intra + jnp.concatenate(o_inter, axis=0)

    @pl.when(jnp.logical_not(safe))
    def _():
        def body(ci, c):
            exact_chunk(ci)
            return c
        lax.fori_loop(0, tt // blk, body, 0)

    outs = []
    for h in range(nh):
        cols = slice(h * hd, (h + 1) * hd)
        outs.append(_rms(osc_ref[:, cols], ng_ref[...]))
    o_ref[...] = (jnp.concatenate(outs, axis=-1) * _silu(cg_ref[...])).astype(o_ref.dtype)

    @pl.when(tb == pl.num_programs(1) - 1)
    def _fin():
        for h in range(nh):
            sn_ref[h] = st_ref[h].T


def _hgrn(hp, obuf, row0, nb, t, tt, lb, norm_g, s0):
    ntb = t // tt
    rb = row0 // tt

    def col(cidx):
        return pl.BlockSpec((tt, D_C), lambda b, i: (rb + b * ntb + i, cidx // D_C))

    st_spec = pl.BlockSpec((None, C_HEADS, C_HEAD_DIM, C_V_DIM), lambda b, i: (b, 0, 0, 0))
    kern = functools.partial(_hgrn_kernel, tt=tt, blk=HGRN_BLOCK)
    return pl.pallas_call(
        kern,
        grid=(nb, ntb),
        in_specs=[col(COL_CQ), col(COL_CF), col(COL_CI), col(COL_CG),
                  pl.BlockSpec((1, D_C), lambda b, i: (0, 0)),
                  pl.BlockSpec((1, C_V_DIM), lambda b, i: (0, 0)),
                  st_spec,
                  pl.BlockSpec(memory_space=pl.ANY)],
        out_specs=[pl.BlockSpec((tt, D_C), lambda b, i: (rb + b * ntb + i, 0)), st_spec],
        out_shape=[jax.ShapeDtypeStruct(obuf.shape, obuf.dtype),
                   jax.ShapeDtypeStruct((nb, C_HEADS, C_HEAD_DIM, C_V_DIM), f32)],
        input_output_aliases={7: 0},
        scratch_shapes=[pltpu.VMEM((C_HEADS, C_V_DIM, C_HEAD_DIM), f32),
                        pltpu.VMEM((tt, D_C), f32), pltpu.VMEM((tt, D_C), f32),
                        pltpu.VMEM((tt, D_C), f32), pltpu.VMEM((tt, D_C), f32)],
        compiler_params=_cparams(("parallel", "arbitrary")),
        name="hgrn",
    )(hp, hp, hp, hp, lb.reshape(1, D_C), norm_g.reshape(1, C_V_DIM), s0, obuf)


def _merge_kernel(x_ref, g_ref, oa_ref, ob_ref, oc_ref, wga_ref, wgb_ref, wgc_ref,
                  pa_ref, pb_ref, pc_ref, wo_ref, o_ref, xn_ref):
    j = pl.program_id(1)

    @pl.when(j == 0)
    def _():
        x = x_ref[...]
        xn_ref[...] = _rms(x, g_ref[...]).astype(bf16)
        o_ref[...] = x

    xn = xn_ref[...]

    def branch(wg_ref, o_ref_, p_ref):
        gate = jax.nn.sigmoid(jnp.dot(xn, wg_ref[...], preferred_element_type=f32))
        return gate * jnp.dot(o_ref_[...], p_ref[...], preferred_element_type=f32)

    merged = branch(wga_ref, oa_ref, pa_ref) + branch(wgb_ref, ob_ref, pb_ref) + branch(wgc_ref, oc_ref, pc_ref)
    o_ref[...] += jnp.dot(merged.astype(bf16), wo_ref[...], preferred_element_type=f32)


def _merge(h, g, oa, ob, oc, wg, pa, pb, pc, wo, tm, tj):
    m, d = h.shape
    nj = d // tj
    row = lambda i, j: (i, 0)
    return pl.pallas_call(
        _merge_kernel,
        grid=(m // tm, nj),
        in_specs=[pl.BlockSpec((tm, d), row),
                  pl.BlockSpec((1, d), lambda i, j: (0, 0)),
                  pl.BlockSpec((tm, D_A), row),
                  pl.BlockSpec((tm, D_B), row),
                  pl.BlockSpec((tm, D_C), row),
                  pl.BlockSpec((d, tj), lambda i, j: (0, j)),
                  pl.BlockSpec((d, tj), lambda i, j: (0, nj + j)),
                  pl.BlockSpec((d, tj), lambda i, j: (0, 2 * nj + j)),
                  pl.BlockSpec((D_A, tj), lambda i, j: (0, j)),
                  pl.BlockSpec((D_B, tj), lambda i, j: (0, j)),
                  pl.BlockSpec((D_C, tj), lambda i, j: (0, j)),
                  pl.BlockSpec((tj, d), lambda i, j: (j, 0))],
        out_specs=pl.BlockSpec((tm, d), row),
        out_shape=jax.ShapeDtypeStruct((m, d), f32),
        scratch_shapes=[pltpu.VMEM((tm, d), bf16)],
        compiler_params=_cparams(("parallel", "arbitrary")),
        name="merge",
    )(h, g.reshape(1, d), oa, ob, oc, wg, wg, wg, pa, pb, pc, wo)


def _router_kernel(x_ref, g_ref, r_ref, o_ref):
    xn = _rms(x_ref[...], g_ref[...]).astype(bf16)
    logits = jnp.dot(xn, r_ref[...], preferred_element_type=f32)
    lane = lax.broadcasted_iota(jnp.int32, logits.shape, 1)
    logits = jnp.where(lane < N_EXPERTS, logits, -jnp.inf)
    m1 = jnp.max(logits, axis=-1, keepdims=True)
    i1 = jnp.min(jnp.where(logits == m1, lane, LANE), axis=-1, keepdims=True)
    rest = jnp.where(lane == i1, -jnp.inf, logits)
    m2 = jnp.max(rest, axis=-1, keepdims=True)
    i2 = jnp.min(jnp.where(rest == m2, lane, LANE), axis=-1, keepdims=True)
    e2 = jnp.exp(m2 - m1)
    g1 = 1.0 / (1.0 + e2)
    g2 = e2 / (1.0 + e2)
    route = jnp.where(lane == 0, i1.astype(f32), jnp.where(lane == 1, i2.astype(f32),
                      jnp.where(lane == 2, g1, jnp.where(lane == 3, g2, 0.0))))
    o_ref[...] = route[:, :N_EXPERTS]


def _router(h, g, router, tm):
    m, d = h.shape
    rp = jnp.pad(router, ((0, 0), (0, LANE - N_EXPERTS))).astype(bf16)
    return pl.pallas_call(
        _router_kernel,
        grid=(m // tm,),
        in_specs=[pl.BlockSpec((tm, d), lambda i: (i, 0)),
                  pl.BlockSpec((1, d), lambda i: (0, 0)),
                  pl.BlockSpec((d, LANE), lambda i: (0, 0))],
        out_specs=pl.BlockSpec((tm, N_EXPERTS), lambda i: (i, 0)),
        out_shape=jax.ShapeDtypeStruct((m, N_EXPERTS), f32),
        compiler_params=_cparams(("parallel",)),
        name="router",
    )(h, g.reshape(1, d), rp)


def _ffn_kernel(x_ref, g_ref, wg_ref, wu_ref, wd_ref, o_ref, xn_ref):
    @pl.when(pl.program_id(1) == 0)
    def _():
        x = x_ref[...]
        xn_ref[...] = _rms(x, g_ref[...]).astype(bf16)
        o_ref[...] = x

    xn = xn_ref[...]
    a = jnp.dot(xn, wg_ref[...], preferred_element_type=f32)
    u = jnp.dot(xn, wu_ref[...], preferred_element_type=f32)
    act = (_silu(a) * u).astype(bf16)
    o_ref[...] += jnp.dot(act, wd_ref[...], preferred_element_type=f32)


def _ffn(h, g, wg, wu, wd, tm, tf):
    m, d = h.shape
    ff = wg.shape[1]
    return pl.pallas_call(
        _ffn_kernel,
        grid=(m // tm, ff // tf),
        in_specs=[pl.BlockSpec((tm, d), lambda i, f: (i, 0)),
                  pl.BlockSpec((1, d), lambda i, f: (0, 0)),
                  pl.BlockSpec((d, tf), lambda i, f: (0, f)),
                  pl.BlockSpec((d, tf), lambda i, f: (0, f)),
                  pl.BlockSpec((tf, d), lambda i, f: (f, 0))],
        out_specs=pl.BlockSpec((tm, d), lambda i, f: (i, 0)),
        out_shape=jax.ShapeDtypeStruct((m, d), f32),
        scratch_shapes=[pltpu.VMEM((tm, d), bf16)],
        compiler_params=_cparams(("parallel", "arbitrary")),
        name="ffn",
    )(h, g.reshape(1, d), wg, wu, wd)


MOE_ROWS = 1024
MOE_SUB = 256
DMA_UNROLL = 8


def _moe_plan(route, m):
    n = 2 * m
    n_groups = -(-n // MOE_ROWS) + N_EXPERTS
    ef = route[:, 0:2].astype(jnp.int32).T.reshape(n)
    row_dst = jnp.argsort(ef, stable=True).astype(jnp.int32)
    row_tok = jnp.where(row_dst >= m, row_dst - m, row_dst)
    counts = jnp.sum((ef[:, None] == jnp.arange(N_EXPERTS, dtype=jnp.int32)[None, :]).astype(jnp.int32), axis=0)
    first = jnp.cumsum(counts) - counts
    per_e = (counts + MOE_ROWS - 1) // MOE_ROWS
    size = (counts + jnp.maximum(per_e, 1) - 1) // jnp.maximum(per_e, 1)
    size = (size + DMA_UNROLL - 1) // DMA_UNROLL * DMA_UNROLL
    ends = jnp.cumsum(per_e)
    gi = jnp.arange(n_groups, dtype=jnp.int32)
    ge = jnp.minimum(jnp.sum((gi[:, None] >= ends[None, :]).astype(jnp.int32), axis=1), N_EXPERTS - 1)
    j = gi - (ends - per_e)[ge]
    g_start = first[ge] + j * size[ge]
    g_cnt = jnp.clip(counts[ge] - j * size[ge], 0, size[ge])
    g_cnt = jnp.where(gi < ends[-1], g_cnt, 0)
    last_e = ge[jnp.maximum(ends[-1] - 1, 0)]
    ge = jnp.where(gi < ends[-1], ge, last_e)
    return ge, g_start.astype(jnp.int32), g_cnt.astype(jnp.int32), row_tok, row_dst


def _moe_kernel(ge_ref, gs_ref, gc_ref, tok_ref, dst_ref, h_hbm, g_ref, wg_ref, wu_ref, wd_ref, y_hbm,
                xs_ref, acc_ref, stage_ref, spare_ref, gsem, ssem, *, nf):
    s = pl.program_id(0)
    fi = pl.program_id(1)
    cnt = gc_ref[s]
    start = gs_ref[s]
    n_rows = tok_ref.shape[0]
    nsub = (cnt + MOE_SUB - 1) // MOE_SUB
    chunk = stage_ref.shape[0] // nf

    def gather_chunk(first_row, base):
        for u in range(chunk):
            tok = tok_ref[jnp.minimum(first_row + base + u, n_rows - 1)]
            pltpu.make_async_copy(h_hbm.at[pl.ds(tok, 1)], stage_ref.at[pl.ds(base + u, 1)], gsem).start()

    def row_tile(j):
        rows = pl.ds(pl.multiple_of(j * MOE_SUB, MOE_SUB), MOE_SUB)
        x = xs_ref[rows, :]
        a = jnp.dot(x, wg_ref[...].astype(bf16), preferred_element_type=f32)
        u = jnp.dot(x, wu_ref[...].astype(bf16), preferred_element_type=f32)
        act = (_silu(a) * u).astype(bf16)
        acc_ref[rows, :] += jnp.dot(act, wd_ref[...].astype(bf16), preferred_element_type=f32)

    @pl.when(cnt > 0)
    def _active():
        @pl.when(fi == 0)
        def _stage_in():
            @pl.when(s == 0)
            def _():
                def issue(c, carry):
                    gather_chunk(start, c * chunk)
                    return carry
                lax.fori_loop(0, nf, issue, 0)

            def wait_chunk(c, carry):
                pltpu.make_async_copy(h_hbm.at[pl.ds(0, chunk)], stage_ref.at[pl.ds(0, chunk)], gsem).wait()
                return carry
            lax.fori_loop(0, nf, wait_chunk, 0)

            def norm(j, c):
                rows = pl.ds(pl.multiple_of(j * MOE_SUB, MOE_SUB), MOE_SUB)
                xs_ref[rows, :] = _rms(stage_ref[rows, :], g_ref[...]).astype(bf16)
                acc_ref[rows, :] = jnp.zeros((MOE_SUB, acc_ref.shape[1]), f32)
                return c
            lax.fori_loop(0, nsub, norm, 0)

        nxt = jnp.minimum(s + 1, pl.num_programs(0) - 1)
        prefetch = jnp.logical_and(s + 1 < pl.num_programs(0), gc_ref[nxt] > 0)

        @pl.when(prefetch)
        def _():
            gather_chunk(gs_ref[nxt], fi * chunk)
            row_tile(0)

        @pl.when(jnp.logical_not(prefetch))
        def _():
            row_tile(0)

        def body(j, c):
            row_tile(j)
            return c
        lax.fori_loop(1, nsub, body, 0)

        @pl.when(fi == nf - 1)
        def _scatter():
            def issue_one(r):
                pltpu.make_async_copy(acc_ref.at[pl.ds(r, 1)], y_hbm.at[pl.ds(dst_ref[start + r], 1)], ssem).start()

            def issue(r8, c):
                for u in range(DMA_UNROLL):
                    issue_one(r8 * DMA_UNROLL + u)
                return c
            n8 = cnt // DMA_UNROLL
            lax.fori_loop(0, n8, issue, 0)

            def issue_tail(r, c):
                issue_one(r)
                return c
            lax.fori_loop(n8 * DMA_UNROLL, cnt, issue_tail, 0)

            def issue_spare(r, c):
                pltpu.make_async_copy(acc_ref.at[pl.ds(r, 1)], spare_ref.at[pl.ds(r - cnt, 1)], ssem).start()
                return c
            lax.fori_loop(cnt, nsub * MOE_SUB, issue_spare, 0)

            def wait_tile(j, c):
                pltpu.make_async_copy(acc_ref.at[pl.ds(0, MOE_SUB)], y_hbm.at[pl.ds(0, MOE_SUB)], ssem).wait()
                return c
            lax.fori_loop(0, nsub, wait_tile, 0)


def _moe(h, g, route, wg, wu, wd, tf):
    m, d = h.shape
    ne, _, ff = wg.shape
    nf = ff // tf
    chunk = -(-MOE_ROWS // nf)
    chunk = -(-chunk // DMA_UNROLL) * DMA_UNROLL
    ge, gs, gc, row_tok, row_dst = _moe_plan(route, m)
    n_groups = ge.shape[0]

    def wcol(s, f, ge_r, gs_r, gc_r, tok_r, dst_r):
        return (ge_r[s], 0, jnp.where(gc_r[s] > 0, f, nf - 1))

    def wrow(s, f, ge_r, gs_r, gc_r, tok_r, dst_r):
        return (ge_r[s], jnp.where(gc_r[s] > 0, f, nf - 1), 0)

    grid_spec = pltpu.PrefetchScalarGridSpec(
        num_scalar_prefetch=5,
        grid=(n_groups, nf),
        in_specs=[pl.BlockSpec(memory_space=pl.ANY),
                  pl.BlockSpec((1, d), lambda s, f, *_: (0, 0)),
                  pl.BlockSpec((None, d, tf), wcol),
                  pl.BlockSpec((None, d, tf), wcol),
                  pl.BlockSpec((None, tf, d), wrow)],
        out_specs=pl.BlockSpec(memory_space=pl.ANY),
        scratch_shapes=[pltpu.VMEM((MOE_ROWS, d), bf16), pltpu.VMEM((MOE_ROWS, d), f32),
                        pltpu.VMEM((chunk * nf, d), f32), pltpu.VMEM((MOE_SUB, d), f32),
                        pltpu.SemaphoreType.DMA(()), pltpu.SemaphoreType.DMA(())],
    )
    return pl.pallas_call(
        functools.partial(_moe_kernel, nf=nf),
        grid_spec=grid_spec,
        out_shape=jax.ShapeDtypeStruct((2 * m, d), f32),
        compiler_params=_cparams(("arbitrary", "arbitrary")),
        name="moe",
    )(ge, gs, gc, row_tok, row_dst, h, g.reshape(1, d), wg, wu, wd)


def _combine_kernel(x_ref, r_ref, y1_ref, y2_ref, o_ref):
    r = r_ref[...]
    o_ref[...] = x_ref[...] + (r[:, 2:3] * y1_ref[...] + r[:, 3:4] * y2_ref[...])


def _combine(h, route, y, tm):
    m, d = h.shape
    nb = m // tm
    return pl.pallas_call(
        _combine_kernel,
        grid=(nb,),
        in_specs=[pl.BlockSpec((tm, d), lambda i: (i, 0)),
                  pl.BlockSpec((tm, N_EXPERTS), lambda i: (i, 0)),
                  pl.BlockSpec((tm, d), lambda i: (i, 0)),
                  pl.BlockSpec((tm, d), lambda i: (nb + i, 0))],
        out_specs=pl.BlockSpec((tm, d), lambda i: (i, 0)),
        out_shape=jax.ShapeDtypeStruct((m, d), f32),
        compiler_params=_cparams(("parallel",)),
        name="moe_combine",
    )(h, route, y, y)


def _combine_norm_kernel(x_ref, r_ref, y1_ref, y2_ref, g_ref, op_ref, os_ref, *, n_p):
    r = r_ref[...]
    v = _rms(x_ref[...] + (r[:, 2:3] * y1_ref[...] + r[:, 3:4] * y2_ref[...]), g_ref[...])

    @pl.when(pl.program_id(0) < n_p)
    def _():
        op_ref[...] = v

    @pl.when(pl.program_id(0) >= n_p)
    def _():
        os_ref[...] = v


def _combine_norm(h, route, y, g, mp, tm):
    m, d = h.shape
    nb = m // tm
    n_p = mp // tm
    return pl.pallas_call(
        functools.partial(_combine_norm_kernel, n_p=n_p),
        grid=(nb,),
        in_specs=[pl.BlockSpec((tm, d), lambda i: (i, 0)),
                  pl.BlockSpec((tm, N_EXPERTS), lambda i: (i, 0)),
                  pl.BlockSpec((tm, d), lambda i: (i, 0)),
                  pl.BlockSpec((tm, d), lambda i: (nb + i, 0)),
                  pl.BlockSpec((1, d), lambda i: (0, 0))],
        out_specs=[pl.BlockSpec((tm, d), lambda i: (jnp.minimum(i, n_p - 1), 0)),
                   pl.BlockSpec((tm, d), lambda i: (jnp.maximum(i - n_p, 0), 0))],
        out_shape=[jax.ShapeDtypeStruct((mp, d), f32), jax.ShapeDtypeStruct((m - mp, d), f32)],
        compiler_params=_cparams(("arbitrary",)),
        name="moe_combine_norm",
    )(h, route, y, y, g.reshape(1, d))


def _final_norm_kernel(x_ref, g_ref, o_ref):
    o_ref[...] = _rms(x_ref[...], g_ref[...])


def _final_norm(h, g, row0, rows, tm):
    d = h.shape[1]
    rb = row0 // tm
    return pl.pallas_call(
        _final_norm_kernel,
        grid=(rows // tm,),
        in_specs=[pl.BlockSpec((tm, d), lambda i: (rb + i, 0)),
                  pl.BlockSpec((1, d), lambda i: (0, 0))],
        out_specs=pl.BlockSpec((tm, d), lambda i: (i, 0)),
        out_shape=jax.ShapeDtypeStruct((rows, d), f32),
        compiler_params=_cparams(("parallel",)),
        name="final_norm",
    )(h, g.reshape(1, d))


def _cast_kernel(w_ref, o_ref):
    o_ref[...] = w_ref[...].astype(bf16)


def _cast_bf16(w, layer, tr):
    _, r, c = w.shape
    return pl.pallas_call(
        _cast_kernel,
        grid=(r // tr,),
        in_specs=[pl.BlockSpec((None, tr, c), lambda i: (layer, i, 0))],
        out_specs=pl.BlockSpec((tr, c), lambda i: (i, 0)),
        out_shape=jax.ShapeDtypeStruct((r, c), bf16),
        compiler_params=_cparams(("parallel",)),
        name="cast_bf16",
    )(w)


def _pack_kernel(w_ref, wp_ref, wk_ref, wv_ref, wg_ref):
    o = np.cumsum((0,) + IN_SIZES)
    w = w_ref[...]
    tc = w.shape[1]
    cut = lambda a, b: w[int(o[a]):int(o[b]), :]
    pad = lambda a: jnp.concatenate([a, jnp.zeros((LANE - a.shape[0], tc), f32)], axis=0)
    t = lambda a: a.T.astype(bf16)
    wp_ref[...] = jnp.concatenate([t(cut(0, 1)), t(cut(3, 4)), t(cut(6, 11))], axis=1)
    wk_ref[...] = jnp.concatenate([t(cut(1, 2)), t(pad(cut(4, 5))), t(pad(cut(5, 6)))], axis=1)
    wv_ref[...] = t(cut(2, 3))
    wg_ref[...] = t(cut(11, 12))


def _pack_in_proj(wt, layer, tc):
    _, n, d = wt.shape
    widths = (N_PACK, D_A + 2 * LANE, D_A, N_BRANCH * D_MODEL)
    return pl.pallas_call(
        _pack_kernel,
        grid=(d // tc,),
        in_specs=[pl.BlockSpec((None, n, tc), lambda i: (layer, 0, i))],
        out_specs=[pl.BlockSpec((tc, c), lambda i: (i, 0)) for c in widths],
        out_shape=[jax.ShapeDtypeStruct((d, c), bf16) for c in widths],
        compiler_params=_cparams(("parallel",)),
        name="pack_w_in",
    )(wt)


def kernel(x_prompt, x_sample, cache_k, cache_v, cache_kidx, cache_pool, state_hgrn, norm_mix_g, norm_ffn_g, final_norm_g, w_in, w_proj_a, w_proj_b, w_proj_c, w_out, pool_w, pool_scale, hgrn_lb_logits, hgrn_norm_g, ffn_w_gate, ffn_w_up, ffn_w_down, moe_router, moe_w_gate, moe_w_up, moe_w_down):
    batch, seq, d = x_prompt.shape
    nb, nq, _ = x_sample.shape
    depth = w_in.shape[0]
    past = cache_k.shape[2]
    mp = batch * seq
    ms = nb * nq
    m = mp + ms

    lb_soft = jax.nn.softmax(hgrn_lb_logits.astype(f32), axis=0)
    lb_all = jnp.cumsum(lb_soft, axis=0) - lb_soft[0:1]

    h = jnp.concatenate([x_prompt.reshape(mp, d), x_sample.reshape(ms, d)], axis=0)
    s0_prompt = jnp.zeros((batch, C_HEADS, C_HEAD_DIM, C_V_DIM), f32)
    tm = 768 if m % 768 == 0 else 256
    tmh = tm // 2

    kp = jnp.zeros((depth * mp, A_HEADS, A_HEAD_DIM), f32)
    vp = jnp.zeros((depth * mp, A_HEADS, A_HEAD_DIM), f32)
    ks = jnp.zeros((depth * ms, A_HEADS, A_HEAD_DIM), f32)
    vs = jnp.zeros((depth * ms, A_HEADS, A_HEAD_DIM), f32)
    w_in_t = jnp.swapaxes(w_in, 1, 2)
    kc_all = cache_k.reshape(depth, nb, past, D_A)
    vc_all = cache_v.reshape(depth, nb, past, D_A)
    tkv = ATT_BLOCK
    tkp = 1024 if mp % 1024 == 0 else ATT_BLOCK
    outs = [[] for _ in range(6)]
    for l in range(depth):
        wp, wk, wv, wgates = _pack_in_proj(w_in_t, l, LANE)
        hp = _inproj(h, norm_mix_g[l], wp, tm, 1024)
        kp, kb16, kiw_p = _inproj_k(h, norm_mix_g[l], wk, kp, l, 0, mp, tkp)
        ks, _, kiw_s = _inproj_k(h, norm_mix_g[l], wk, ks, l, mp, ms, tkv)
        vp, vt16 = _inproj_v(h, norm_mix_g[l], wv, vp, l, 0, mp, tkp, True)
        vs, = _inproj_v(h, norm_mix_g[l], wv, vs, l, mp, ms, tkv, False)

        oa = _dsa_prompt(hp, kb16, vt16, kiw_p, jnp.zeros((m, D_A), bf16), batch, seq)
        oa = _dsa_sample(hp, ks, vs, kiw_s, oa, l, mp, nb, nq, kc_all, vc_all, cache_kidx)
        ob = _pool(hp, jnp.zeros((m, D_B), bf16), 0, batch, seq, 256, None, pool_w[l], pool_scale[l], 0)
        prev = jnp.pad(cache_pool[l], ((0, 0), (1, 0), (0, 0)))
        ob = _pool(hp, ob, mp, nb, nq, nq, prev, pool_w[l], pool_scale[l], past)
        oc, st_p = _hgrn(hp, jnp.zeros((m, D_C), bf16), 0, batch, seq, 256, lb_all[l], hgrn_norm_g[l], s0_prompt)
        oc, st_s = _hgrn(hp, oc, mp, nb, nq, nq, lb_all[l], hgrn_norm_g[l], state_hgrn[l])

        h = _merge(h, norm_mix_g[l], oa, ob, oc, wgates, _cast_bf16(w_proj_a, l, 256), _cast_bf16(w_proj_b, l, 256),
                   _cast_bf16(w_proj_c, l, 256), _cast_bf16(w_out, l, 256), tmh, 512)

        uu = hp[:, COL_U:COL_U + D_B]
        outs[0].append(kiw_p[:, KIW_KI:KIW_KI + IDX_DIM].reshape(batch, seq, IDX_DIM))
        outs[1].append(uu[:mp].reshape(batch, seq, D_B)[:, -POOL_STATE:])
        outs[2].append(st_p)
        outs[3].append(kiw_s[:, KIW_KI:KIW_KI + IDX_DIM].reshape(nb, nq, IDX_DIM))
        u_ext = jnp.concatenate([cache_pool[l], uu[mp:].reshape(nb, nq, D_B)], axis=1)
        outs[4].append(u_ext[:, -POOL_STATE:])
        outs[5].append(st_s)

        j = l // 2
        if l % 2 == 0:
            h = _ffn(h, norm_ffn_g[l], _cast_bf16(ffn_w_gate, j, 256), _cast_bf16(ffn_w_up, j, 256),
                     _cast_bf16(ffn_w_down, j, 512), tm, 512)
        else:
            route = _router(h, norm_ffn_g[l], moe_router[j], tm)
            y = _moe(h, norm_ffn_g[l], route, moe_w_gate[j], moe_w_up[j], moe_w_down[j], 512)
            if l == depth - 1:
                y_prompt, y_sample = _combine_norm(h, route, y, final_norm_g, mp, tkv)
                h = None
            else:
                h = _combine(h, route, y, tmh)

    if h is not None:
        y_prompt = _final_norm(h, final_norm_g, 0, mp, 256)
        y_sample = _final_norm(h, final_norm_g, mp, ms, 256)
    y_prompt = y_prompt.reshape(batch, seq, d)
    y_sample = y_sample.reshape(nb, nq, d)
    kidx_p, pool_p, st_p, kidx_s, pool_s, st_s = (jnp.stack(o) for o in outs)
    return (y_prompt, y_sample,
            kp.reshape(depth, batch, seq, A_HEADS, A_HEAD_DIM), vp.reshape(depth, batch, seq, A_HEADS, A_HEAD_DIM),
            kidx_p, pool_p, st_p,
            ks.reshape(depth, nb, nq, A_HEADS, A_HEAD_DIM), vs.reshape(depth, nb, nq, A_HEADS, A_HEAD_DIM),
            kidx_s, pool_s, st_s)
```

```python
import functools

import numpy as np
import jax
import jax.numpy as jnp
from jax import lax
from jax.experimental import pallas as pl
from jax.experimental.pallas import tpu as pltpu

f32 = jnp.float32
bf16 = jnp.bfloat16

D_MODEL = 2048
CHUNK = 64
A_HEADS = 8
A_HEAD_DIM = 128
D_A = A_HEADS * A_HEAD_DIM
IDX_HEADS = 8
IDX_DIM = 64
TOPK_MAX = 256
POOL_WINDOWS = (2, 4, 8, 16)
POOL_GROUP_DIM = 128
D_B = len(POOL_WINDOWS) * POOL_GROUP_DIM
POOL_STATE = max(POOL_WINDOWS) - 1
C_HEADS = 4
C_HEAD_DIM = 128
C_V_DIM = 128
D_C = C_HEADS * C_HEAD_DIM
HGRN_BLOCK = 16
N_BRANCH = 3
N_EXPERTS = 8
IN_SIZES = (D_A, D_A, D_A, IDX_HEADS * IDX_DIM, IDX_DIM, IDX_HEADS,
            D_B, D_C, D_C, C_HEADS * C_V_DIM, C_HEADS * C_V_DIM, N_BRANCH * D_MODEL)
EPS = 1e-6
NEG = -1e30
LB_FLOOR = 1e-30
INT_MIN = -2 ** 31

LOG2E = 1.4426950408889634
LANE = 128
COUNT_ROWS = 64
ATT_BLOCK = 256
ATT_HEAD_GROUP = 8
COL_Q = 0
COL_QI = D_A
COL_U = COL_QI + 512
COL_CQ = COL_U + 512
COL_CF = COL_CQ + 512
COL_CI = COL_CF + 512
COL_CG = COL_CI + 512
N_PACK = COL_CG + 512
KIW_KI, KIW_W = 0, LANE

VMEM_LIMIT = 56 * 1024 * 1024

NT_DIMS = (((1,), (1,)), ((), ()))
TN_DIMS = (((0,), (0,)), ((), ()))


def _cparams(sem):
    return pltpu.CompilerParams(dimension_semantics=sem, vmem_limit_bytes=VMEM_LIMIT)


def _rms(x, g):
    ms = jnp.mean(x * x, axis=-1, keepdims=True)
    return x * lax.rsqrt(ms + EPS) * g


def _silu(x):
    return x * jax.nn.sigmoid(x)


def _sort_key(x):
    bits = pltpu.bitcast(x, jnp.int32)
    return bits ^ ((bits >> 31) & jnp.int32(0x7FFFFFFF))


def _kth_largest(count_ge, kk):
    zero = jnp.zeros(kk.shape, jnp.int32)
    t0 = jnp.where(count_ge(zero) >= kk, zero, jnp.int32(INT_MIN))

    def bit_body(i, t):
        c = t | lax.shift_left(jnp.int32(1), 30 - i)
        return jnp.where(count_ge(c) >= kk, c, t)

    return lax.fori_loop(0, 31, bit_body, t0)


def _inproj_kernel(x_ref, g_ref, w_ref, o_ref, xn_ref):
    @pl.when(pl.program_id(1) == 0)
    def _():
        xn_ref[...] = _rms(x_ref[...], g_ref[...]).astype(bf16)

    o_ref[...] = jnp.dot(xn_ref[...], w_ref[...], preferred_element_type=f32)


def _inproj(h, g, wp, tm, tn):
    m, d = h.shape
    n = wp.shape[1]
    return pl.pallas_call(
        _inproj_kernel,
        grid=(m // tm, n // tn),
        in_specs=[pl.BlockSpec((tm, d), lambda i, j: (i, 0)),
                  pl.BlockSpec((1, d), lambda i, j: (0, 0)),
                  pl.BlockSpec((d, tn), lambda i, j: (0, j))],
        out_specs=pl.BlockSpec((tm, tn), lambda i, j: (i, j)),
        out_shape=jax.ShapeDtypeStruct((m, n), f32),
        scratch_shapes=[pltpu.VMEM((tm, d), bf16)],
        compiler_params=_cparams(("parallel", "arbitrary")),
        name="inproj",
    )(h, g.reshape(1, d), wp)


def _inproj_k_kernel(x_ref, g_ref, w_ref, dst_in, k_ref, kb_ref, kiw_ref):
    del dst_in
    xn = _rms(x_ref[...], g_ref[...]).astype(bf16)
    r = jnp.dot(xn, w_ref[...], preferred_element_type=f32)
    for hh in range(A_HEADS):
        k_ref[:, hh, :] = r[:, hh * A_HEAD_DIM:(hh + 1) * A_HEAD_DIM]
    kb_ref[...] = r[:, :D_A].astype(bf16)
    kiw_ref[...] = r[:, D_A:]


def _inproj_k(h, g, wk, dst, layer, row0, rows, tm):
    d = h.shape[1]
    n = wk.shape[1]
    nt, rb = rows // tm, row0 // tm
    row = lambda i: (i, 0)
    return pl.pallas_call(
        _inproj_k_kernel,
        grid=(nt,),
        in_specs=[pl.BlockSpec((tm, d), lambda i: (rb + i, 0)),
                  pl.BlockSpec((1, d), lambda i: (0, 0)),
                  pl.BlockSpec((d, n), lambda i: (0, 0)),
                  pl.BlockSpec(memory_space=pl.ANY)],
        out_specs=[pl.BlockSpec((tm, A_HEADS, A_HEAD_DIM), lambda i: (layer * nt + i, 0, 0)),
                   pl.BlockSpec((tm, D_A), row), pl.BlockSpec((tm, n - D_A), row)],
        out_shape=[jax.ShapeDtypeStruct(dst.shape, f32),
                   jax.ShapeDtypeStruct((rows, D_A), bf16), jax.ShapeDtypeStruct((rows, n - D_A), f32)],
        input_output_aliases={3: 0},
        compiler_params=_cparams(("parallel",)),
        name="inproj_k",
    )(h, g.reshape(1, d), wk, dst)


def _inproj_v_kernel(x_ref, g_ref, w_ref, dst_in, v_ref, *vt_ref):
    del dst_in
    xn = _rms(x_ref[...], g_ref[...]).astype(bf16)
    r = jnp.dot(xn, w_ref[...], preferred_element_type=f32)
    for hh in range(A_HEADS):
        v_ref[:, hh, :] = r[:, hh * A_HEAD_DIM:(hh + 1) * A_HEAD_DIM]
    for vt in vt_ref:
        for c in range(vt.shape[0]):
            for hh in range(A_HEADS):
                blk = r[c * ATT_BLOCK:(c + 1) * ATT_BLOCK, hh * A_HEAD_DIM:(hh + 1) * A_HEAD_DIM]
                vt[c, hh] = blk.T.astype(bf16)


def _inproj_v(h, g, wv, dst, layer, row0, rows, tm, transposed):
    d = h.shape[1]
    nt, rb = rows // tm, row0 // tm
    out_specs = [pl.BlockSpec((tm, A_HEADS, A_HEAD_DIM), lambda i: (layer * nt + i, 0, 0))]
    out_shape = [jax.ShapeDtypeStruct(dst.shape, f32)]
    if transposed:
        out_specs.append(pl.BlockSpec((tm // ATT_BLOCK, A_HEADS, A_HEAD_DIM, ATT_BLOCK), lambda i: (i, 0, 0, 0)))
        out_shape.append(jax.ShapeDtypeStruct((rows // ATT_BLOCK, A_HEADS, A_HEAD_DIM, ATT_BLOCK), bf16))
    return pl.pallas_call(
        _inproj_v_kernel,
        grid=(nt,),
        in_specs=[pl.BlockSpec((tm, d), lambda i: (rb + i, 0)),
                  pl.BlockSpec((1, d), lambda i: (0, 0)),
                  pl.BlockSpec((d, D_A), lambda i: (0, 0)),
                  pl.BlockSpec(memory_space=pl.ANY)],
        out_specs=out_specs,
        out_shape=out_shape,
        input_output_aliases={3: 0},
        compiler_params=_cparams(("parallel",)),
        name="inproj_v",
    )(h, g.reshape(1, d), wv, dst)


def _idx_scores(qi, w, kib):
    acc = jnp.zeros((qi.shape[0], kib.shape[0]), f32)
    for hh in range(IDX_HEADS):
        d = lax.dot_general(qi[:, hh * IDX_DIM:(hh + 1) * IDX_DIM], kib, NT_DIMS,
                            preferred_element_type=f32)
        acc = acc + w[:, hh:hh + 1] * jnp.maximum(d, 0.0)
    return acc


def _fold_lanes(x):
    part = x[:, :LANE]
    for j in range(1, x.shape[1] // LANE):
        part = part + x[:, j * LANE:(j + 1) * LANE]
    return part


def _tri_incl(n):
    r = lax.broadcasted_iota(jnp.int32, (n, n), 0)
    c = lax.broadcasted_iota(jnp.int32, (n, n), 1)
    return jnp.where(r <= c, 1.0, 0.0).astype(bf16)


def _dsa_prompt_kernel(q_ref, k_ref, vt_ref, qi_ref, ki_ref, w_ref, _o_in, o_ref, keys_ref, bias_ref, qs_ref,
                       *acc_refs, tq, topk):
    qb = pl.program_id(1)
    nkb = qb + 1
    hd = A_HEAD_DIM

    qi = qi_ref[...].astype(bf16)
    wt = (w_ref[...] * (IDX_HEADS ** -0.5 * IDX_DIM ** -0.5)).T
    srow = lax.broadcasted_iota(jnp.int32, (tq, tq), 0)
    qcol = lax.broadcasted_iota(jnp.int32, (tq, tq), 1)
    diag_adm = (srow // CHUNK) <= (qcol // CHUNK)

    def score_body(kb, c):
        off = pl.multiple_of(kb * tq, tq)
        kib = ki_ref[pl.ds(off, tq), :][:, :IDX_DIM].astype(bf16)
        ds = [lax.dot_general(kib, qi[:, hh * IDX_DIM:(hh + 1) * IDX_DIM], NT_DIMS, preferred_element_type=f32)
              for hh in range(IDX_HEADS)]
        acc = jnp.zeros((tq, tq), f32)
        for hh in range(IDX_HEADS):
            acc = acc + wt[hh:hh + 1, :] * jnp.maximum(ds[hh], 0.0)
        adm = jnp.logical_or(kb < qb, diag_adm)
        keys_ref[kb] = jnp.where(adm, _sort_key(acc), jnp.int32(INT_MIN))
        return c

    lax.fori_loop(0, nkb, score_body, 0)

    def count(cmp, c):
        cb = jnp.broadcast_to(c, (COUNT_ROWS, tq))

        def body(kb, acc):
            for r0 in range(0, tq, COUNT_ROWS):
                acc = acc + jnp.where(cmp(keys_ref[kb, r0:r0 + COUNT_ROWS, :], cb), 1.0, 0.0)
            return acc
        acc = lax.fori_loop(0, nkb, body, jnp.zeros((COUNT_ROWS, tq), f32))
        return jnp.sum(acc, axis=0, keepdims=True)

    lane = lax.broadcasted_iota(jnp.int32, (1, tq), 1)
    n_adm = ((qb * tq + lane) // CHUNK + 1) * CHUNK
    kk = jnp.minimum(topk, n_adm).astype(f32)
    thr = _kth_largest(lambda c: count(lambda k, t: k >= t, c), kk)
    n_ge = count(lambda k, t: k >= t, thr)
    has_tie = jnp.max(n_ge - kk) > 0.0

    @pl.when(jnp.logical_not(has_tie))
    def _():
        def body(kb, c):
            bias_ref[kb] = jnp.where(keys_ref[kb] >= thr, 0.0, NEG)
            return c
        lax.fori_loop(0, nkb, body, 0)

    @pl.when(has_tie)
    def _():
        need = kk - count(lambda k, t: k > t, thr)
        tri = jnp.where(qcol <= srow, 1.0, 0.0).astype(bf16)

        def body(kb, seen):
            k = keys_ref[kb]
            eq = jnp.where(k == thr, 1.0, 0.0)
            rank = seen + jnp.dot(tri, eq.astype(bf16), preferred_element_type=f32)
            take = jnp.logical_or(k > thr, jnp.logical_and(k == thr, rank <= need))
            bias_ref[kb] = jnp.where(take, 0.0, NEG)
            return seen + jnp.sum(eq, axis=0, keepdims=True)
        lax.fori_loop(0, nkb, body, jnp.zeros((1, tq), f32))

    qs_ref[...] = (q_ref[...] * (hd ** -0.5 * LOG2E)).astype(bf16)
    for h in range(A_HEADS):
        acc_refs[h][...] = jnp.zeros((hd, tq), f32)

    def att_body(kb, carry):
        ms, ls = carry
        off = pl.multiple_of(kb * tq, tq)
        kblk = k_ref[pl.ds(off, tq), :]
        bias = bias_ref[kb]
        new_ms, new_ls = [], []
        for g0 in range(0, A_HEADS, ATT_HEAD_GROUP):
            heads = range(g0, g0 + ATT_HEAD_GROUP)
            sts, ps, corrs = {}, {}, {}
            for h in heads:
                cols = slice(h * hd, (h + 1) * hd)
                sts[h] = lax.dot_general(kblk[:, cols], qs_ref[:, cols], NT_DIMS, preferred_element_type=f32)
            for h in heads:
                st = sts[h] + bias
                m_new = jnp.maximum(ms[h], jnp.max(st, axis=0, keepdims=True))
                p = jnp.exp2(st - m_new)
                corrs[h] = jnp.exp2(ms[h] - m_new)
                new_ls.append(corrs[h] * ls[h] + jnp.sum(p, axis=0, keepdims=True))
                new_ms.append(m_new)
                ps[h] = p.astype(bf16)
            for h in heads:
                acc_refs[h][...] = corrs[h] * acc_refs[h][...] + jnp.dot(vt_ref[kb, h], ps[h],
                                                                         preferred_element_type=f32)
        return tuple(new_ms), tuple(new_ls)

    init = (tuple(jnp.full((1, tq), NEG, f32) for _ in range(A_HEADS)),
            tuple(jnp.zeros((1, tq), f32) for _ in range(A_HEADS)))
    _, ls = lax.fori_loop(0, nkb, att_body, init)
    for h in range(A_HEADS):
        o_ref[:, h * hd:(h + 1) * hd] = (acc_refs[h][...] / ls[h]).T.astype(o_ref.dtype)


def _dsa_prompt(hp, kb16, vt16, kiw, obuf, batch, seq):
    tq = ATT_BLOCK
    nq = seq // tq
    topk = min(TOPK_MAX, seq // 4)
    kern = functools.partial(_dsa_prompt_kernel, tq=tq, topk=topk)
    once = pl.Buffered(1)
    return pl.pallas_call(
        kern,
        grid=(batch, nq),
        in_specs=[
            pl.BlockSpec((tq, D_A), lambda b, i: (b * nq + i, COL_Q // D_A)),
            pl.BlockSpec((seq, D_A), lambda b, i: (b, 0), pipeline_mode=once),
            pl.BlockSpec((nq, A_HEADS, A_HEAD_DIM, tq), lambda b, i: (b, 0, 0, 0), pipeline_mode=once),
            pl.BlockSpec((tq, 512), lambda b, i: (b * nq + i, COL_QI // 512)),
            pl.BlockSpec((seq, LANE), lambda b, i: (b, KIW_KI // LANE), pipeline_mode=once),
            pl.BlockSpec((tq, LANE), lambda b, i: (b * nq + i, KIW_W // LANE)),
            pl.BlockSpec(memory_space=pl.ANY),
        ],
        out_specs=pl.BlockSpec((tq, D_A), lambda b, i: (b * nq + i, 0)),
        out_shape=jax.ShapeDtypeStruct(obuf.shape, obuf.dtype),
        input_output_aliases={6: 0},
        scratch_shapes=[pltpu.VMEM((nq, tq, tq), jnp.int32), pltpu.VMEM((nq, tq, tq), f32),
                        pltpu.VMEM((tq, D_A), bf16)]
                       + [pltpu.VMEM((A_HEAD_DIM, tq), f32) for _ in range(A_HEADS)],
        compiler_params=_cparams(("parallel", "arbitrary")),
        name="dsa_prompt",
    )(hp, kb16, vt16, hp, kiw, kiw, obuf)


def _dsa_sample_kernel(q_ref, kn_ref, vn_ref, qi_ref, kin_ref, w_ref, kc_ref, vc_ref, kic_ref, _o_in, o_ref,
                       biasc_ref, biasn_ref, *, topk):
    nq = q_ref.shape[0]
    past = kic_ref.shape[0]

    def _select():
        qi = qi_ref[...].astype(bf16)
        w = w_ref[...][:, :IDX_HEADS] * (IDX_HEADS ** -0.5 * IDX_DIM ** -0.5)
        key_c = _sort_key(_idx_scores(qi, w, kic_ref[...].astype(bf16)))
        key_n = _sort_key(_idx_scores(qi, w, kin_ref[...][:, :IDX_DIM].astype(bf16)))

        def count(pred):
            cc = jnp.sum(_fold_lanes(jnp.where(pred(key_c), 1.0, 0.0)), axis=-1, keepdims=True)
            return cc + jnp.sum(jnp.where(pred(key_n), 1.0, 0.0), axis=-1, keepdims=True)

        kk = jnp.full((nq, 1), float(topk), f32)
        thr = _kth_largest(lambda c: count(lambda k: k >= c), kk)
        n_ge = count(lambda k: k >= thr)
        has_tie = jnp.max(n_ge - kk) > 0.0

        @pl.when(jnp.logical_not(has_tie))
        def _():
            biasc_ref[...] = jnp.where(key_c >= thr, 0.0, NEG)
            biasn_ref[...] = jnp.where(key_n >= thr, 0.0, NEG)

        @pl.when(has_tie)
        def _():
            need = kk - count(lambda k: k > thr)
            tri = _tri_incl(LANE)
            seen = jnp.zeros((nq, 1), f32)
            for j in range(past // LANE):
                k = key_c[:, j * LANE:(j + 1) * LANE]
                eq = jnp.where(k == thr, 1.0, 0.0)
                rank = seen + jnp.dot(eq.astype(bf16), tri, preferred_element_type=f32)
                take = jnp.logical_or(k > thr, jnp.logical_and(k == thr, rank <= need))
                biasc_ref[:, j * LANE:(j + 1) * LANE] = jnp.where(take, 0.0, NEG)
                seen = seen + jnp.sum(eq, axis=-1, keepdims=True)
            eq = jnp.where(key_n == thr, 1.0, 0.0)
            rank = seen + jnp.dot(eq.astype(bf16), _tri_incl(nq), preferred_element_type=f32)
            take = jnp.logical_or(key_n > thr, jnp.logical_and(key_n == thr, rank <= need))
            biasn_ref[...] = jnp.where(take, 0.0, NEG)

    _select()

    hd = A_HEAD_DIM
    q = (q_ref[...] * (hd ** -0.5 * LOG2E)).astype(bf16)
    bias_c = biasc_ref[...]
    bias_n = biasn_ref[...]
    scs, sns = [], []
    for h in range(A_HEADS):
        cols = slice(h * hd, (h + 1) * hd)
        k_h = kc_ref[pl.ds(h, past, stride=A_HEADS), :]
        scs.append(lax.dot_general(q[:, cols], k_h.astype(bf16), NT_DIMS, preferred_element_type=f32))
        sns.append(lax.dot_general(q[:, cols], kn_ref[:, h, :].astype(bf16), NT_DIMS, preferred_element_type=f32))
    pcs, pns, ls = [], [], []
    for h in range(A_HEADS):
        s_c = scs[h] + bias_c
        s_n = sns[h] + bias_n
        m = jnp.maximum(jnp.max(s_c, axis=-1, keepdims=True), jnp.max(s_n, axis=-1, keepdims=True))
        p_c = jnp.exp2(s_c - m)
        p_n = jnp.exp2(s_n - m)
        ls.append(jnp.sum(p_c, axis=-1, keepdims=True) + jnp.sum(p_n, axis=-1, keepdims=True))
        pcs.append(p_c.astype(bf16))
        pns.append(p_n.astype(bf16))
    for h in range(A_HEADS):
        cols = slice(h * hd, (h + 1) * hd)
        v_h = vc_ref[pl.ds(h, past, stride=A_HEADS), :]
        acc = jnp.dot(pcs[h], v_h.astype(bf16), preferred_element_type=f32)
        acc = acc + jnp.dot(pns[h], vn_ref[:, h, :].astype(bf16), preferred_element_type=f32)
        o_ref[:, cols] = (acc / ls[h]).astype(o_ref.dtype)


def _dsa_sample(hp, ks, vs, kiw, obuf, layer, row0, nb, nq, cache_k, cache_v, cache_kidx):
    depth, _, past = cache_k.shape[:3]
    topk = min(TOPK_MAX, (past + nq) // 4)
    kc = cache_k.reshape(depth, nb, past * A_HEADS, A_HEAD_DIM)
    vc = cache_v.reshape(depth, nb, past * A_HEADS, A_HEAD_DIM)
    rb = row0 // nq
    kern = functools.partial(_dsa_sample_kernel, topk=topk)
    return pl.pallas_call(
        kern,
        grid=(nb,),
        in_specs=[
            pl.BlockSpec((nq, D_A), lambda b: (rb + b, COL_Q // D_A)),
            pl.BlockSpec((nq, A_HEADS, A_HEAD_DIM), lambda b: (layer * nb + b, 0, 0)),
            pl.BlockSpec((nq, A_HEADS, A_HEAD_DIM), lambda b: (layer * nb + b, 0, 0)),
            pl.BlockSpec((nq, 512), lambda b: (rb + b, COL_QI // 512)),
            pl.BlockSpec((nq, LANE), lambda b: (b, KIW_KI // LANE)),
            pl.BlockSpec((nq, LANE), lambda b: (b, KIW_W // LANE)),
            pl.BlockSpec((None, None, past * A_HEADS, A_HEAD_DIM), lambda b: (layer, b, 0, 0)),
            pl.BlockSpec((None, None, past * A_HEADS, A_HEAD_DIM), lambda b: (layer, b, 0, 0)),
            pl.BlockSpec((None, None, past, IDX_DIM), lambda b: (layer, b, 0, 0)),
            pl.BlockSpec(memory_space=pl.ANY),
        ],
        out_specs=pl.BlockSpec((nq, D_A), lambda b: (rb + b, 0)),
        out_shape=jax.ShapeDtypeStruct(obuf.shape, obuf.dtype),
        input_output_aliases={9: 0},
        scratch_shapes=[pltpu.VMEM((nq, past), f32), pltpu.VMEM((nq, nq), f32)],
        compiler_params=_cparams(("parallel",)),
        name="dsa_sample",
    )(hp, ks, vs, hp, kiw, kiw, kc, vc, cache_kidx, obuf)


def _pool_kernel(prev_ref, u_ref, pw_ref, ps_ref, _o_in, o_ref, *, tt, first_pos, zero_first):
    tb = pl.program_id(1)
    halo = POOL_STATE + 1
    prev = prev_ref[...]
    if zero_first:
        prev = jnp.where(tb == 0, 0.0, prev)
    cur = u_ref[...]
    ext = jnp.concatenate([prev, cur], axis=0)
    pos = first_pos + tb * tt + lax.broadcasted_iota(jnp.int32, (tt, 1), 0)
    outs = []
    for gi, wdw in enumerate(POOL_WINDOWS):
        cols = slice(gi * POOL_GROUP_DIM, (gi + 1) * POOL_GROUP_DIM)
        a = ext[:, cols]
        n = 1
        while n < wdw:
            a = a[n:] + a[:-n]
            n *= 2
        s = a[halo + 1 - wdw: halo + 1 - wdw + tt]
        cnt = jnp.minimum(pos + 1, wdw).astype(f32)
        d = s / cnt - cur[:, cols]
        outs.append(jnp.dot(d.astype(bf16), pw_ref[gi].astype(bf16), preferred_element_type=f32))
    o = jnp.concatenate(outs, axis=-1) * ps_ref[...]
    o_ref[...] = o.astype(o_ref.dtype)


def _pool(hp, obuf, row0, nb, t, tt, prev, pool_w, pool_scale, first_pos):
    ntb = t // tt
    halo = POOL_STATE + 1
    rb = row0 // tt
    if prev is None:
        prev_arr = hp
        per = tt // halo
        prev_spec = pl.BlockSpec(
            (halo, D_B), lambda b, i: (jnp.maximum((row0 // halo) + (b * ntb + i) * per - 1, 0), COL_U // D_B))
    else:
        prev_arr = prev
        prev_spec = pl.BlockSpec((None, halo, D_B), lambda b, i: (b, 0, 0))
    kern = functools.partial(_pool_kernel, tt=tt, first_pos=first_pos, zero_first=prev is None)
    return pl.pallas_call(
        kern,
        grid=(nb, ntb),
        in_specs=[prev_spec,
                  pl.BlockSpec((tt, D_B), lambda b, i: (rb + b * ntb + i, COL_U // D_B)),
                  pl.BlockSpec((len(POOL_WINDOWS), POOL_GROUP_DIM, POOL_GROUP_DIM), lambda b, i: (0, 0, 0)),
                  pl.BlockSpec((1, D_B), lambda b, i: (0, 0)),
                  pl.BlockSpec(memory_space=pl.ANY)],
        out_specs=pl.BlockSpec((tt, D_B), lambda b, i: (rb + b * ntb + i, 0)),
        out_shape=jax.ShapeDtypeStruct(obuf.shape, obuf.dtype),
        input_output_aliases={4: 0},
        compiler_params=_cparams(("parallel", "arbitrary")),
        name="pool",
    )(prev_arr, hp, pool_w, pool_scale.reshape(1, D_B), obuf)


SAFE_LOG_RANGE = 80.0


def _hgrn_kernel(cq_ref, cf_ref, ci_ref, cg_ref, lb_ref, ng_ref, s0_ref, _o_in, o_ref, sn_ref,
                 st_ref, b_ref, q_ref, k_ref, osc_ref, *, tt, blk):
    tb = pl.program_id(1)
    nh = C_HEADS
    hd = C_HEAD_DIM

    @pl.when(tb == 0)
    def _init():
        for h in range(nh):
            st_ref[h] = s0_ref[h].T

    lb = lb_ref[...]
    f = jnp.maximum(lb, LB_FLOOR) + (1.0 - lb) * jax.nn.sigmoid(cf_ref[...])
    g = jnp.log(f)
    k_ref[...] = 1.0 - f
    q_ref[...] = _silu(cq_ref[...])
    r = lax.broadcasted_iota(jnp.int32, (tt, tt), 0)
    c = lax.broadcasted_iota(jnp.int32, (tt, tt), 1)
    tri = jnp.where(jnp.logical_and(r // blk == c // blk, c <= r), 1.0, 0.0).astype(bf16)
    g1 = g.astype(bf16)
    e1 = g - g1.astype(f32)
    g2 = e1.astype(bf16)
    g3 = (e1 - g2.astype(f32)).astype(bf16)
    b = (jnp.dot(tri, g1, preferred_element_type=f32) + jnp.dot(tri, g2, preferred_element_type=f32)
         + jnp.dot(tri, g3, preferred_element_type=f32))
    b_ref[...] = b
    safe = jnp.min(b) > -SAFE_LOG_RANGE

    trow = lax.broadcasted_iota(jnp.int32, (blk, 1), 0)

    def exact_chunk(ci):
        r0 = pl.multiple_of(ci * blk, blk)
        for h in range(nh):
            rows = pl.ds(r0, blk)
            cols = slice(h * hd, (h + 1) * hd)
            bb = b_ref[rows, cols]
            qh = q_ref[rows, cols]
            kh = k_ref[rows, cols]
            ih = ci_ref[rows, cols]
            bl = bb[blk - 1:blk, :]
            ke = (kh * jnp.exp(bl - bb)).astype(bf16)
            ib = ih.astype(bf16)
            st = st_ref[h]
            o = lax.dot_general((qh * jnp.exp(bb)).astype(bf16), st.astype(bf16), NT_DIMS,
                                preferred_element_type=f32)
            for s in range(blk):
                live = trow >= s
                dec = jnp.exp(jnp.where(live, bb - bb[s:s + 1, :], 0.0))
                a_col = jnp.sum(qh * kh[s:s + 1, :] * dec, axis=-1, keepdims=True)
                o = o + jnp.where(live, a_col, 0.0) * ih[s:s + 1, :]
            osc_ref[rows, cols] = o
            st_ref[h] = st * jnp.exp(bl) + lax.dot_general(ib, ke, TN_DIMS, preferred_element_type=f32)

    @pl.when(safe)
    def _():
        same_blk = r // blk == c // blk
        blk_causal = jnp.logical_and(same_blk, c <= r)
        ones_blk = jnp.where(same_blk, 1.0, 0.0).astype(bf16)
        bl_all = (jnp.dot(ones_blk, g1, preferred_element_type=f32) + jnp.dot(ones_blk, g2, preferred_element_type=f32)
                  + jnp.dot(ones_blk, g3, preferred_element_type=f32))
        for h in range(nh):
            cols = slice(h * hd, (h + 1) * hd)
            bb = b[:, cols]
            blh = bl_all[:, cols]
            qh = q_ref[:, cols]
            ib = ci_ref[:, cols].astype(bf16)
            ke = (k_ref[:, cols] * jnp.exp(blh - bb)).astype(bf16)
            qe = (qh * jnp.exp(bb - blh)).astype(bf16)
            qb = (qh * jnp.exp(bb)).astype(bf16)
            a = jnp.where(blk_causal, lax.dot_general(qe, ke, NT_DIMS, preferred_element_type=f32), 0.0)
            o_intra = jnp.dot(a.astype(bf16), ib, preferred_element_type=f32)
            incs = [lax.dot_general(ib[c0:c0 + blk], ke[c0:c0 + blk], TN_DIMS, preferred_element_type=f32)
                    for c0 in range(0, tt, blk)]
            dec = jnp.exp(blh)
            st = st_ref[h]
            o_inter = []
            for ci, c0 in enumerate(range(0, tt, blk)):
                o_inter.append(lax.dot_general(qb[c0:c0 + blk], st.astype(bf16), NT_DIMS,
                                               preferred_element_type=f32))
                st = st * dec[c0:c0 + 1, :] + incs[ci]
            st_ref[h] = st
            osc_ref[:, cols] = o_intra + jnp.concatenate(o_inter, axis=0)

    @pl.when(jnp.logical_not(safe))
    def _():
        def body(ci, c):
            exact_chunk(ci)
            return c
        lax.fori_loop(0, tt // blk, body, 0)

    outs = []
    for h in range(nh):
        cols = slice(h * hd, (h + 1) * hd)
        outs.append(_rms(osc_ref[:, cols], ng_ref[...]))
    o_ref[...] = (jnp.concatenate(outs, axis=-1) * _silu(cg_ref[...])).astype(o_ref.dtype)

    @pl.when(tb == pl.num_programs(1) - 1)
    def _fin():
        for h in range(nh):
            sn_ref[h] = st_ref[h].T


def _hgrn(hp, obuf, row0, nb, t, tt, lb, norm_g, s0):
    ntb = t // tt
    rb = row0 // tt

    def col(cidx):
        return pl.BlockSpec((tt, D_C), lambda b, i: (rb + b * ntb + i, cidx // D_C))

    st_spec = pl.BlockSpec((None, C_HEADS, C_HEAD_DIM, C_V_DIM), lambda b, i: (b, 0, 0, 0))
    kern = functools.partial(_hgrn_kernel, tt=tt, blk=HGRN_BLOCK)
    return pl.pallas_call(
        kern,
        grid=(nb, ntb),
        in_specs=[col(COL_CQ), col(COL_CF), col(COL_CI), col(COL_CG),
                  pl.BlockSpec((1, D_C), lambda b, i: (0, 0)),
                  pl.BlockSpec((1, C_V_DIM), lambda b, i: (0, 0)),
                  st_spec,
                  pl.BlockSpec(memory_space=pl.ANY)],
        out_specs=[pl.BlockSpec((tt, D_C), lambda b, i: (rb + b * ntb + i, 0)), st_spec],
        out_shape=[jax.ShapeDtypeStruct(obuf.shape, obuf.dtype),
                   jax.ShapeDtypeStruct((nb, C_HEADS, C_HEAD_DIM, C_V_DIM), f32)],
        input_output_aliases={7: 0},
        scratch_shapes=[pltpu.VMEM((C_HEADS, C_V_DIM, C_HEAD_DIM), f32),
                        pltpu.VMEM((tt, D_C), f32), pltpu.VMEM((tt, D_C), f32),
                        pltpu.VMEM((tt, D_C), f32), pltpu.VMEM((tt, D_C), f32)],
        compiler_params=_cparams(("parallel", "arbitrary")),
        name="hgrn",
    )(hp, hp, hp, hp, lb.reshape(1, D_C), norm_g.reshape(1, C_V_DIM), s0, obuf)


def _merge_kernel(x_ref, g_ref, oa_ref, ob_ref, oc_ref, wga_ref, wgb_ref, wgc_ref,
                  pa_ref, pb_ref, pc_ref, wo_ref, o_ref, xn_ref):
    j = pl.program_id(1)

    @pl.when(j == 0)
    def _():
        x = x_ref[...]
        xn_ref[...] = _rms(x, g_ref[...]).astype(bf16)
        o_ref[...] = x

    xn = xn_ref[...]

    def branch(wg_ref, o_ref_, p_ref):
        gate = jax.nn.sigmoid(jnp.dot(xn, wg_ref[...], preferred_element_type=f32))
        return gate * jnp.dot(o_ref_[...], p_ref[...], preferred_element_type=f32)

    merged = branch(wga_ref, oa_ref, pa_ref) + branch(wgb_ref, ob_ref, pb_ref) + branch(wgc_ref, oc_ref, pc_ref)
    o_ref[...] += jnp.dot(merged.astype(bf16), wo_ref[...], preferred_element_type=f32)


def _merge(h, g, oa, ob, oc, wg, pa, pb, pc, wo, tm, tj):
    m, d = h.shape
    nj = d // tj
    row = lambda i, j: (i, 0)
    return pl.pallas_call(
        _merge_kernel,
        grid=(m // tm, nj),
        in_specs=[pl.BlockSpec((tm, d), row),
                  pl.BlockSpec((1, d), lambda i, j: (0, 0)),
                  pl.BlockSpec((tm, D_A), row),
                  pl.BlockSpec((tm, D_B), row),
                  pl.BlockSpec((tm, D_C), row),
                  pl.BlockSpec((d, tj), lambda i, j: (0, j)),
                  pl.BlockSpec((d, tj), lambda i, j: (0, nj + j)),
                  pl.BlockSpec((d, tj), lambda i, j: (0, 2 * nj + j)),
                  pl.BlockSpec((D_A, tj), lambda i, j: (0, j)),
                  pl.BlockSpec((D_B, tj), lambda i, j: (0, j)),
                  pl.BlockSpec((D_C, tj), lambda i, j: (0, j)),
                  pl.BlockSpec((tj, d), lambda i, j: (j, 0))],
        out_specs=pl.BlockSpec((tm, d), row),
        out_shape=jax.ShapeDtypeStruct((m, d), f32),
        scratch_shapes=[pltpu.VMEM((tm, d), bf16)],
        compiler_params=_cparams(("parallel", "arbitrary")),
        name="merge",
    )(h, g.reshape(1, d), oa, ob, oc, wg, wg, wg, pa, pb, pc, wo)


def _router_kernel(x_ref, g_ref, r_ref, o_ref):
    xn = _rms(x_ref[...], g_ref[...]).astype(bf16)
    logits = jnp.dot(xn, r_ref[...], preferred_element_type=f32)
    lane = lax.broadcasted_iota(jnp.int32, logits.shape, 1)
    logits = jnp.where(lane < N_EXPERTS, logits, -jnp.inf)
    m1 = jnp.max(logits, axis=-1, keepdims=True)
    i1 = jnp.min(jnp.where(logits == m1, lane, LANE), axis=-1, keepdims=True)
    rest = jnp.where(lane == i1, -jnp.inf, logits)
    m2 = jnp.max(rest, axis=-1, keepdims=True)
    i2 = jnp.min(jnp.where(rest == m2, lane, LANE), axis=-1, keepdims=True)
    e2 = jnp.exp(m2 - m1)
    g1 = 1.0 / (1.0 + e2)
    g2 = e2 / (1.0 + e2)
    route = jnp.where(lane == 0, i1.astype(f32), jnp.where(lane == 1, i2.astype(f32),
                      jnp.where(lane == 2, g1, jnp.where(lane == 3, g2, 0.0))))
    o_ref[...] = route[:, :N_EXPERTS]


def _router(h, g, router, tm):
    m, d = h.shape
    rp = jnp.pad(router, ((0, 0), (0, LANE - N_EXPERTS))).astype(bf16)
    return pl.pallas_call(
        _router_kernel,
        grid=(m // tm,),
        in_specs=[pl.BlockSpec((tm, d), lambda i: (i, 0)),
                  pl.BlockSpec((1, d), lambda i: (0, 0)),
                  pl.BlockSpec((d, LANE), lambda i: (0, 0))],
        out_specs=pl.BlockSpec((tm, N_EXPERTS), lambda i: (i, 0)),
        out_shape=jax.ShapeDtypeStruct((m, N_EXPERTS), f32),
        compiler_params=_cparams(("parallel",)),
        name="router",
    )(h, g.reshape(1, d), rp)


def _ffn_kernel(x_ref, g_ref, wg_ref, wu_ref, wd_ref, o_ref, xn_ref):
    @pl.when(pl.program_id(1) == 0)
    def _():
        x = x_ref[...]
        xn_ref[...] = _rms(x, g_ref[...]).astype(bf16)
        o_ref[...] = x

    xn = xn_ref[...]
    a = jnp.dot(xn, wg_ref[...], preferred_element_type=f32)
    u = jnp.dot(xn, wu_ref[...], preferred_element_type=f32)
    act = (_silu(a) * u).astype(bf16)
    o_ref[...] += jnp.dot(act, wd_ref[...], preferred_element_type=f32)


def _ffn(h, g, wg, wu, wd, tm, tf):
    m, d = h.shape
    ff = wg.shape[1]
    return pl.pallas_call(
        _ffn_kernel,
        grid=(m // tm, ff // tf),
        in_specs=[pl.BlockSpec((tm, d), lambda i, f: (i, 0)),
                  pl.BlockSpec((1, d), lambda i, f: (0, 0)),
                  pl.BlockSpec((d, tf), lambda i, f: (0, f)),
                  pl.BlockSpec((d, tf), lambda i, f: (0, f)),
                  pl.BlockSpec((tf, d), lambda i, f: (f, 0))],
        out_specs=pl.BlockSpec((tm, d), lambda i, f: (i, 0)),
        out_shape=jax.ShapeDtypeStruct((m, d), f32),
        scratch_shapes=[pltpu.VMEM((tm, d), bf16)],
        compiler_params=_cparams(("parallel", "arbitrary")),
        name="ffn",
    )(h, g.reshape(1, d), wg, wu, wd)


MOE_ROWS = 1024
MOE_SUB = 256
DMA_UNROLL = 8


def _moe_plan(route, m):
    n = 2 * m
    n_groups = -(-n // MOE_ROWS) + N_EXPERTS
    ef = route[:, 0:2].astype(jnp.int32).T.reshape(n)
    row_dst = jnp.argsort(ef, stable=True).astype(jnp.int32)
    row_tok = jnp.where(row_dst >= m, row_dst - m, row_dst)
    counts = jnp.sum((ef[:, None] == jnp.arange(N_EXPERTS, dtype=jnp.int32)[None, :]).astype(jnp.int32), axis=0)
    first = jnp.cumsum(counts) - counts
    per_e = (counts + MOE_ROWS - 1) // MOE_ROWS
    size = (counts + jnp.maximum(per_e, 1) - 1) // jnp.maximum(per_e, 1)
    size = (size + DMA_UNROLL - 1) // DMA_UNROLL * DMA_UNROLL
    ends = jnp.cumsum(per_e)
    gi = jnp.arange(n_groups, dtype=jnp.int32)
    ge = jnp.minimum(jnp.sum((gi[:, None] >= ends[None, :]).astype(jnp.int32), axis=1), N_EXPERTS - 1)
    j = gi - (ends - per_e)[ge]
    g_start = first[ge] + j * size[ge]
    g_cnt = jnp.clip(counts[ge] - j * size[ge], 0, size[ge])
    g_cnt = jnp.where(gi < ends[-1], g_cnt, 0)
    last_e = ge[jnp.maximum(ends[-1] - 1, 0)]
    ge = jnp.where(gi < ends[-1], ge, last_e)
    return ge, g_start.astype(jnp.int32), g_cnt.astype(jnp.int32), row_tok, row_dst


def _moe_kernel(ge_ref, gs_ref, gc_ref, tok_ref, dst_ref, h_hbm, g_ref, wg_ref, wu_ref, wd_ref, y_hbm,
                xs_ref, acc_ref, stage_ref, spare_ref, gsem, ssem, *, nf):
    s = pl.program_id(0)
    fi = pl.program_id(1)
    cnt = gc_ref[s]
    start = gs_ref[s]
    n_rows = tok_ref.shape[0]
    nsub = (cnt + MOE_SUB - 1) // MOE_SUB
    chunk = stage_ref.shape[0] // nf

    def gather_chunk(first_row, base):
        for u in range(chunk):
            tok = tok_ref[jnp.minimum(first_row + base + u, n_rows - 1)]
            pltpu.make_async_copy(h_hbm.at[pl.ds(tok, 1)], stage_ref.at[pl.ds(base + u, 1)], gsem).start()

    def row_tile(j):
        rows = pl.ds(pl.multiple_of(j * MOE_SUB, MOE_SUB), MOE_SUB)
        x = xs_ref[rows, :]
        a = jnp.dot(x, wg_ref[...].astype(bf16), preferred_element_type=f32)
        u = jnp.dot(x, wu_ref[...].astype(bf16), preferred_element_type=f32)
        act = (_silu(a) * u).astype(bf16)
        acc_ref[rows, :] += jnp.dot(act, wd_ref[...].astype(bf16), preferred_element_type=f32)

    @pl.when(cnt > 0)
    def _active():
        @pl.when(fi == 0)
        def _stage_in():
            @pl.when(s == 0)
            def _():
                def issue(c, carry):
                    gather_chunk(start, c * chunk)
                    return carry
                lax.fori_loop(0, nf, issue, 0)

            def wait_chunk(c, carry):
                pltpu.make_async_copy(h_hbm.at[pl.ds(0, chunk)], stage_ref.at[pl.ds(0, chunk)], gsem).wait()
                return carry
            lax.fori_loop(0, nf, wait_chunk, 0)

            def norm(j, c):
                rows = pl.ds(pl.multiple_of(j * MOE_SUB, MOE_SUB), MOE_SUB)
                xs_ref[rows, :] = _rms(stage_ref[rows, :], g_ref[...]).astype(bf16)
                acc_ref[rows, :] = jnp.zeros((MOE_SUB, acc_ref.shape[1]), f32)
                return c
            lax.fori_loop(0, nsub, norm, 0)

        nxt = jnp.minimum(s + 1, pl.num_programs(0) - 1)
        prefetch = jnp.logical_and(s + 1 < pl.num_programs(0), gc_ref[nxt] > 0)

        @pl.when(prefetch)
        def _():
            gather_chunk(gs_ref[nxt], fi * chunk)
            row_tile(0)

        @pl.when(jnp.logical_not(prefetch))
        def _():
            row_tile(0)

        def body(j, c):
            row_tile(j)
            return c
        lax.fori_loop(1, nsub, body, 0)

        @pl.when(fi == nf - 1)
        def _scatter():
            def issue_one(r):
                pltpu.make_async_copy(acc_ref.at[pl.ds(r, 1)], y_hbm.at[pl.ds(dst_ref[start + r], 1)], ssem).start()

            def issue(r8, c):
                for u in range(DMA_UNROLL):
                    issue_one(r8 * DMA_UNROLL + u)
                return c
            n8 = cnt // DMA_UNROLL
            lax.fori_loop(0, n8, issue, 0)

            def issue_tail(r, c):
                issue_one(r)
                return c
            lax.fori_loop(n8 * DMA_UNROLL, cnt, issue_tail, 0)

            def issue_spare(r, c):
                pltpu.make_async_copy(acc_ref.at[pl.ds(r, 1)], spare_ref.at[pl.ds(r - cnt, 1)], ssem).start()
                return c
            lax.fori_loop(cnt, nsub * MOE_SUB, issue_spare, 0)

            def wait_tile(j, c):
                pltpu.make_async_copy(acc_ref.at[pl.ds(0, MOE_SUB)], y_hbm.at[pl.ds(0, MOE_SUB)], ssem).wait()
                return c
            lax.fori_loop(0, nsub, wait_tile, 0)


def _moe(h, g, route, wg, wu, wd, tf):
    m, d = h.shape
    ne, _, ff = wg.shape
    nf = ff // tf
    chunk = -(-MOE_ROWS // nf)
    chunk = -(-chunk // DMA_UNROLL) * DMA_UNROLL
    ge, gs, gc, row_tok, row_dst = _moe_plan(route, m)
    n_groups = ge.shape[0]

    def wcol(s, f, ge_r, gs_r, gc_r, tok_r, dst_r):
        return (ge_r[s], 0, jnp.where(gc_r[s] > 0, f, nf - 1))

    def wrow(s, f, ge_r, gs_r, gc_r, tok_r, dst_r):
        return (ge_r[s], jnp.where(gc_r[s] > 0, f, nf - 1), 0)

    grid_spec = pltpu.PrefetchScalarGridSpec(
        num_scalar_prefetch=5,
        grid=(n_groups, nf),
        in_specs=[pl.BlockSpec(memory_space=pl.ANY),
                  pl.BlockSpec((1, d), lambda s, f, *_: (0, 0)),
                  pl.BlockSpec((None, d, tf), wcol),
                  pl.BlockSpec((None, d, tf), wcol),
                  pl.BlockSpec((None, tf, d), wrow)],
        out_specs=pl.BlockSpec(memory_space=pl.ANY),
        scratch_shapes=[pltpu.VMEM((MOE_ROWS, d), bf16), pltpu.VMEM((MOE_ROWS, d), f32),
                        pltpu.VMEM((chunk * nf, d), f32), pltpu.VMEM((MOE_SUB, d), f32),
                        pltpu.SemaphoreType.DMA(()), pltpu.SemaphoreType.DMA(())],
    )
    return pl.pallas_call(
        functools.partial(_moe_kernel, nf=nf),
        grid_spec=grid_spec,
        out_shape=jax.ShapeDtypeStruct((2 * m, d), f32),
        compiler_params=_cparams(("arbitrary", "arbitrary")),
        name="moe",
    )(ge, gs, gc, row_tok, row_dst, h, g.reshape(1, d), wg, wu, wd)


def _combine_kernel(x_ref, r_ref, y1_ref, y2_ref, o_ref):
    r = r_ref[...]
    o_ref[...] = x_ref[...] + (r[:, 2:3] * y1_ref[...] + r[:, 3:4] * y2_ref[...])


def _combine(h, route, y, tm):
    m, d = h.shape
    nb = m // tm
    return pl.pallas_call(
        _combine_kernel,
        grid=(nb,),
        in_specs=[pl.BlockSpec((tm, d), lambda i: (i, 0)),
                  pl.BlockSpec((tm, N_EXPERTS), lambda i: (i, 0)),
                  pl.BlockSpec((tm, d), lambda i: (i, 0)),
                  pl.BlockSpec((tm, d), lambda i: (nb + i, 0))],
        out_specs=pl.BlockSpec((tm, d), lambda i: (i, 0)),
        out_shape=jax.ShapeDtypeStruct((m, d), f32),
        compiler_params=_cparams(("parallel",)),
        name="moe_combine",
    )(h, route, y, y)


def _combine_norm_kernel(x_ref, r_ref, y1_ref, y2_ref, g_ref, op_ref, os_ref, *, n_p):
    r = r_ref[...]
    v = _rms(x_ref[...] + (r[:, 2:3] * y1_ref[...] + r[:, 3:4] * y2_ref[...]), g_ref[...])

    @pl.when(pl.program_id(0) < n_p)
    def _():
        op_ref[...] = v

    @pl.when(pl.program_id(0) >= n_p)
    def _():
        os_ref[...] = v


def _combine_norm(h, route, y, g, mp, tm):
    m, d = h.shape
    nb = m // tm
    n_p = mp // tm
    return pl.pallas_call(
        functools.partial(_combine_norm_kernel, n_p=n_p),
        grid=(nb,),
        in_specs=[pl.BlockSpec((tm, d), lambda i: (i, 0)),
                  pl.BlockSpec((tm, N_EXPERTS), lambda i: (i, 0)),
                  pl.BlockSpec((tm, d), lambda i: (i, 0)),
                  pl.BlockSpec((tm, d), lambda i: (nb + i, 0)),
                  pl.BlockSpec((1, d), lambda i: (0, 0))],
        out_specs=[pl.BlockSpec((tm, d), lambda i: (jnp.minimum(i, n_p - 1), 0)),
                   pl.BlockSpec((tm, d), lambda i: (jnp.maximum(i - n_p, 0), 0))],
        out_shape=[jax.ShapeDtypeStruct((mp, d), f32), jax.ShapeDtypeStruct((m - mp, d), f32)],
        compiler_params=_cparams(("arbitrary",)),
        name="moe_combine_norm",
    )(h, route, y, y, g.reshape(1, d))


def _final_norm_kernel(x_ref, g_ref, o_ref):
    o_ref[...] = _rms(x_ref[...], g_ref[...])


def _final_norm(h, g, row0, rows, tm):
    d = h.shape[1]
    rb = row0 // tm
    return pl.pallas_call(
        _final_norm_kernel,
        grid=(rows // tm,),
        in_specs=[pl.BlockSpec((tm, d), lambda i: (rb + i, 0)),
                  pl.BlockSpec((1, d), lambda i: (0, 0))],
        out_specs=pl.BlockSpec((tm, d), lambda i: (i, 0)),
        out_shape=jax.ShapeDtypeStruct((rows, d), f32),
        compiler_params=_cparams(("parallel",)),
        name="final_norm",
    )(h, g.reshape(1, d))


def _cast_kernel(w_ref, o_ref):
    o_ref[...] = w_ref[...].astype(bf16)


def _cast_bf16(w, layer, tr):
    _, r, c = w.shape
    return pl.pallas_call(
        _cast_kernel,
        grid=(r // tr,),
        in_specs=[pl.BlockSpec((None, tr, c), lambda i: (layer, i, 0))],
        out_specs=pl.BlockSpec((tr, c), lambda i: (i, 0)),
        out_shape=jax.ShapeDtypeStruct((r, c), bf16),
        compiler_params=_cparams(("parallel",)),
        name="cast_bf16",
    )(w)


def _pack_kernel(w_ref, wp_ref, wk_ref, wv_ref, wg_ref):
    o = np.cumsum((0,) + IN_SIZES)
    w = w_ref[...]
    tc = w.shape[1]
    cut = lambda a, b: w[int(o[a]):int(o[b]), :]
    pad = lambda a: jnp.concatenate([a, jnp.zeros((LANE - a.shape[0], tc), f32)], axis=0)
    t = lambda a: a.T.astype(bf16)
    wp_ref[...] = jnp.concatenate([t(cut(0, 1)), t(cut(3, 4)), t(cut(6, 11))], axis=1)
    wk_ref[...] = jnp.concatenate([t(cut(1, 2)), t(pad(cut(4, 5))), t(pad(cut(5, 6)))], axis=1)
    wv_ref[...] = t(cut(2, 3))
    wg_ref[...] = t(cut(11, 12))


def _pack_in_proj(wt, layer, tc):
    _, n, d = wt.shape
    widths = (N_PACK, D_A + 2 * LANE, D_A, N_BRANCH * D_MODEL)
    return pl.pallas_call(
        _pack_kernel,
        grid=(d // tc,),
        in_specs=[pl.BlockSpec((None, n, tc), lambda i: (layer, 0, i))],
        out_specs=[pl.BlockSpec((tc, c), lambda i: (i, 0)) for c in widths],
        out_shape=[jax.ShapeDtypeStruct((d, c), bf16) for c in widths],
        compiler_params=_cparams(("parallel",)),
        name="pack_w_in",
    )(wt)


def kernel(x_prompt, x_sample, cache_k, cache_v, cache_kidx, cache_pool, state_hgrn, norm_mix_g, norm_ffn_g, final_norm_g, w_in, w_proj_a, w_proj_b, w_proj_c, w_out, pool_w, pool_scale, hgrn_lb_logits, hgrn_norm_g, ffn_w_gate, ffn_w_up, ffn_w_down, moe_router, moe_w_gate, moe_w_up, moe_w_down):
    batch, seq, d = x_prompt.shape
    nb, nq, _ = x_sample.shape
    depth = w_in.shape[0]
    past = cache_k.shape[2]
    mp = batch * seq
    ms = nb * nq
    m = mp + ms

    lb_soft = jax.nn.softmax(hgrn_lb_logits.astype(f32), axis=0)
    lb_all = jnp.cumsum(lb_soft, axis=0) - lb_soft[0:1]

    h = jnp.concatenate([x_prompt.reshape(mp, d), x_sample.reshape(ms, d)], axis=0)
    s0_prompt = jnp.zeros((batch, C_HEADS, C_HEAD_DIM, C_V_DIM), f32)
    tm = 768 if m % 768 == 0 else 256
    tmh = tm // 2

    kp = jnp.zeros((depth * mp, A_HEADS, A_HEAD_DIM), f32)
    vp = jnp.zeros((depth * mp, A_HEADS, A_HEAD_DIM), f32)
    ks = jnp.zeros((depth * ms, A_HEADS, A_HEAD_DIM), f32)
    vs = jnp.zeros((depth * ms, A_HEADS, A_HEAD_DIM), f32)
    w_in_t = jnp.swapaxes(w_in, 1, 2)
    tkv = ATT_BLOCK
    tkp = 1024 if mp % 1024 == 0 else ATT_BLOCK
    outs = [[] for _ in range(6)]
    for l in range(depth):
        wp, wk, wv, wgates = _pack_in_proj(w_in_t, l, LANE)
        hp = _inproj(h, norm_mix_g[l], wp, tm, 1024)
        kp, kb16, kiw_p = _inproj_k(h, norm_mix_g[l], wk, kp, l, 0, mp, tkp)
        ks, _, kiw_s = _inproj_k(h, norm_mix_g[l], wk, ks, l, mp, ms, tkv)
        vp, vt16 = _inproj_v(h, norm_mix_g[l], wv, vp, l, 0, mp, tkp, True)
        vs, = _inproj_v(h, norm_mix_g[l], wv, vs, l, mp, ms, tkv, False)

        oa = _dsa_prompt(hp, kb16, vt16, kiw_p, jnp.zeros((m, D_A), bf16), batch, seq)
        oa = _dsa_sample(hp, ks, vs, kiw_s, oa, l, mp, nb, nq, cache_k, cache_v, cache_kidx)
        ob = _pool(hp, jnp.zeros((m, D_B), bf16), 0, batch, seq, 256, None, pool_w[l], pool_scale[l], 0)
        prev = jnp.pad(cache_pool[l], ((0, 0), (1, 0), (0, 0)))
        ob = _pool(hp, ob, mp, nb, nq, nq, prev, pool_w[l], pool_scale[l], past)
        oc, st_p = _hgrn(hp, jnp.zeros((m, D_C), bf16), 0, batch, seq, 256, lb_all[l], hgrn_norm_g[l], s0_prompt)
        oc, st_s = _hgrn(hp, oc, mp, nb, nq, nq, lb_all[l], hgrn_norm_g[l], state_hgrn[l])

        h = _merge(h, norm_mix_g[l], oa, ob, oc, wgates, _cast_bf16(w_proj_a, l, 256), _cast_bf16(w_proj_b, l, 256),
                   _cast_bf16(w_proj_c, l, 256), _cast_bf16(w_out, l, 256), tmh, 512)

        uu = hp[:, COL_U:COL_U + D_B]
        outs[0].append(kiw_p[:, KIW_KI:KIW_KI + IDX_DIM].reshape(batch, seq, IDX_DIM))
        outs[1].append(uu[:mp].reshape(batch, seq, D_B)[:, -POOL_STATE:])
        outs[2].append(st_p)
        outs[3].append(kiw_s[:, KIW_KI:KIW_KI + IDX_DIM].reshape(nb, nq, IDX_DIM))
        u_ext = jnp.concatenate([cache_pool[l], uu[mp:].reshape(nb, nq, D_B)], axis=1)
        outs[4].append(u_ext[:, -POOL_STATE:])
        outs[5].append(st_s)

        j = l // 2
        if l % 2 == 0:
            h = _ffn(h, norm_ffn_g[l], _cast_bf16(ffn_w_gate, j, 256), _cast_bf16(ffn_w_up, j, 256),
                     _cast_bf16(ffn_w_down, j, 512), tm, 512)
        else:
            route = _router(h, norm_ffn_g[l], moe_router[j], tm)
            y = _moe(h, norm_ffn_g[l], route, moe_w_gate[j], moe_w_up[j], moe_w_down[j], 512)
            if l == depth - 1:
                y_prompt, y_sample = _combine_norm(h, route, y, final_norm_g, mp, tkv)
                h = None
            else:
                h = _combine(h, route, y, tmh)

    if h is not None:
        y_prompt = _final_norm(h, final_norm_g, 0, mp, 256)
        y_sample = _final_norm(h, final_norm_g, mp, ms, 256)
    y_prompt = y_prompt.reshape(batch, seq, d)
    y_sample = y_sample.reshape(nb, nq, d)
    kidx_p, pool_p, st_p, kidx_s, pool_s, st_s = (jnp.stack(o) for o in outs)
    return (y_prompt, y_sample,
            kp.reshape(depth, batch, seq, A_HEADS, A_HEAD_DIM), vp.reshape(depth, batch, seq, A_HEADS, A_HEAD_DIM),
            kidx_p, pool_p, st_p,
            ks.reshape(depth, nb, nq, A_HEADS, A_HEAD_DIM), vs.reshape(depth, nb, nq, A_HEADS, A_HEAD_DIM),
            kidx_s, pool_s, st_s)
```

```python
import functools

import numpy as np
import jax
import jax.numpy as jnp
from jax import lax
from jax.experimental import pallas as pl
from jax.experimental.pallas import tpu as pltpu

f32 = jnp.float32
bf16 = jnp.bfloat16

D_MODEL = 2048
CHUNK = 64
A_HEADS = 8
A_HEAD_DIM = 128
D_A = A_HEADS * A_HEAD_DIM
IDX_HEADS = 8
IDX_DIM = 64
TOPK_MAX = 256
POOL_WINDOWS = (2, 4, 8, 16)
POOL_GROUP_DIM = 128
D_B = len(POOL_WINDOWS) * POOL_GROUP_DIM
POOL_STATE = max(POOL_WINDOWS) - 1
C_HEADS = 4
C_HEAD_DIM = 128
C_V_DIM = 128
D_C = C_HEADS * C_HEAD_DIM
HGRN_BLOCK = 16
N_BRANCH = 3
N_EXPERTS = 8
IN_SIZES = (D_A, D_A, D_A, IDX_HEADS * IDX_DIM, IDX_DIM, IDX_HEADS,
            D_B, D_C, D_C, C_HEADS * C_V_DIM, C_HEADS * C_V_DIM, N_BRANCH * D_MODEL)
EPS = 1e-6
NEG = -1e30
LB_FLOOR = 1e-30
INT_MIN = -2 ** 31

LOG2E = 1.4426950408889634
LANE = 128
COUNT_ROWS = 64
ATT_BLOCK = 256
ATT_HEAD_GROUP = 8
COL_Q = 0
COL_QI = D_A
COL_U = COL_QI + 512
COL_CQ = COL_U + 512
COL_CF = COL_CQ + 512
COL_CI = COL_CF + 512
COL_CG = COL_CI + 512
N_PACK = COL_CG + 512
KIW_KI, KIW_W = 0, LANE

VMEM_LIMIT = 56 * 1024 * 1024

NT_DIMS = (((1,), (1,)), ((), ()))
TN_DIMS = (((0,), (0,)), ((), ()))


def _cparams(sem):
    return pltpu.CompilerParams(dimension_semantics=sem, vmem_limit_bytes=VMEM_LIMIT)


def _rms(x, g):
    ms = jnp.mean(x * x, axis=-1, keepdims=True)
    return x * lax.rsqrt(ms + EPS) * g


def _silu(x):
    return x * jax.nn.sigmoid(x)


def _sort_key(x):
    bits = pltpu.bitcast(x, jnp.int32)
    return bits ^ ((bits >> 31) & jnp.int32(0x7FFFFFFF))


def _kth_largest(count_ge, kk):
    zero = jnp.zeros(kk.shape, jnp.int32)
    t0 = jnp.where(count_ge(zero) >= kk, zero, jnp.int32(INT_MIN))

    def bit_body(i, t):
        c = t | lax.shift_left(jnp.int32(1), 30 - i)
        return jnp.where(count_ge(c) >= kk, c, t)

    return lax.fori_loop(0, 31, bit_body, t0)


def _inproj_kernel(x_ref, g_ref, w_ref, o_ref, xn_ref):
    @pl.when(pl.program_id(1) == 0)
    def _():
        xn_ref[...] = _rms(x_ref[...], g_ref[...]).astype(bf16)

    o_ref[...] = jnp.dot(xn_ref[...], w_ref[...], preferred_element_type=f32)


def _inproj(h, g, wp, tm, tn):
    m, d = h.shape
    n = wp.shape[1]
    return pl.pallas_call(
        _inproj_kernel,
        grid=(m // tm, n // tn),
        in_specs=[pl.BlockSpec((tm, d), lambda i, j: (i, 0)),
                  pl.BlockSpec((1, d), lambda i, j: (0, 0)),
                  pl.BlockSpec((d, tn), lambda i, j: (0, j))],
        out_specs=pl.BlockSpec((tm, tn), lambda i, j: (i, j)),
        out_shape=jax.ShapeDtypeStruct((m, n), f32),
        scratch_shapes=[pltpu.VMEM((tm, d), bf16)],
        compiler_params=_cparams(("parallel", "arbitrary")),
        name="inproj",
    )(h, g.reshape(1, d), wp)


def _inproj_k_kernel(x_ref, g_ref, w_ref, dst_in, k_ref, kb_ref, kiw_ref):
    del dst_in
    xn = _rms(x_ref[...], g_ref[...]).astype(bf16)
    r = jnp.dot(xn, w_ref[...], preferred_element_type=f32)
    for hh in range(A_HEADS):
        k_ref[:, hh, :] = r[:, hh * A_HEAD_DIM:(hh + 1) * A_HEAD_DIM]
    kb_ref[...] = r[:, :D_A].astype(bf16)
    kiw_ref[...] = r[:, D_A:]


def _inproj_k(h, g, wk, dst, layer, row0, rows, tm):
    d = h.shape[1]
    n = wk.shape[1]
    nt, rb = rows // tm, row0 // tm
    row = lambda i: (i, 0)
    return pl.pallas_call(
        _inproj_k_kernel,
        grid=(nt,),
        in_specs=[pl.BlockSpec((tm, d), lambda i: (rb + i, 0)),
                  pl.BlockSpec((1, d), lambda i: (0, 0)),
                  pl.BlockSpec((d, n), lambda i: (0, 0)),
                  pl.BlockSpec(memory_space=pl.ANY)],
        out_specs=[pl.BlockSpec((tm, A_HEADS, A_HEAD_DIM), lambda i: (layer * nt + i, 0, 0)),
                   pl.BlockSpec((tm, D_A), row), pl.BlockSpec((tm, n - D_A), row)],
        out_shape=[jax.ShapeDtypeStruct(dst.shape, f32),
                   jax.ShapeDtypeStruct((rows, D_A), bf16), jax.ShapeDtypeStruct((rows, n - D_A), f32)],
        input_output_aliases={3: 0},
        compiler_params=_cparams(("parallel",)),
        name="inproj_k",
    )(h, g.reshape(1, d), wk, dst)


def _inproj_v_kernel(x_ref, g_ref, w_ref, dst_in, v_ref, *vt_ref):
    del dst_in
    xn = _rms(x_ref[...], g_ref[...]).astype(bf16)
    r = jnp.dot(xn, w_ref[...], preferred_element_type=f32)
    for hh in range(A_HEADS):
        v_ref[:, hh, :] = r[:, hh * A_HEAD_DIM:(hh + 1) * A_HEAD_DIM]
    for vt in vt_ref:
        for c in range(vt.shape[0]):
            for hh in range(A_HEADS):
                blk = r[c * ATT_BLOCK:(c + 1) * ATT_BLOCK, hh * A_HEAD_DIM:(hh + 1) * A_HEAD_DIM]
                vt[c, hh] = blk.T.astype(bf16)


def _inproj_v(h, g, wv, dst, layer, row0, rows, tm, transposed):
    d = h.shape[1]
    nt, rb = rows // tm, row0 // tm
    out_specs = [pl.BlockSpec((tm, A_HEADS, A_HEAD_DIM), lambda i: (layer * nt + i, 0, 0))]
    out_shape = [jax.ShapeDtypeStruct(dst.shape, f32)]
    if transposed:
        out_specs.append(pl.BlockSpec((tm // ATT_BLOCK, A_HEADS, A_HEAD_DIM, ATT_BLOCK), lambda i: (i, 0, 0, 0)))
        out_shape.append(jax.ShapeDtypeStruct((rows // ATT_BLOCK, A_HEADS, A_HEAD_DIM, ATT_BLOCK), bf16))
    return pl.pallas_call(
        _inproj_v_kernel,
        grid=(nt,),
        in_specs=[pl.BlockSpec((tm, d), lambda i: (rb + i, 0)),
                  pl.BlockSpec((1, d), lambda i: (0, 0)),
                  pl.BlockSpec((d, D_A), lambda i: (0, 0)),
                  pl.BlockSpec(memory_space=pl.ANY)],
        out_specs=out_specs,
        out_shape=out_shape,
        input_output_aliases={3: 0},
        compiler_params=_cparams(("parallel",)),
        name="inproj_v",
    )(h, g.reshape(1, d), wv, dst)


def _idx_scores(qi, w, kib):
    acc = jnp.zeros((qi.shape[0], kib.shape[0]), f32)
    for hh in range(IDX_HEADS):
        d = lax.dot_general(qi[:, hh * IDX_DIM:(hh + 1) * IDX_DIM], kib, NT_DIMS,
                            preferred_element_type=f32)
        acc = acc + w[:, hh:hh + 1] * jnp.maximum(d, 0.0)
    return acc


def _fold_lanes(x):
    part = x[:, :LANE]
    for j in range(1, x.shape[1] // LANE):
        part = part + x[:, j * LANE:(j + 1) * LANE]
    return part


def _tri_incl(n):
    r = lax.broadcasted_iota(jnp.int32, (n, n), 0)
    c = lax.broadcasted_iota(jnp.int32, (n, n), 1)
    return jnp.where(r <= c, 1.0, 0.0).astype(bf16)


def _dsa_prompt_kernel(q_ref, k_ref, vt_ref, qi_ref, ki_ref, w_ref, _o_in, o_ref, keys_ref, bias_ref, qs_ref,
                       *acc_refs, tq, topk):
    qb = pl.program_id(1)
    nkb = qb + 1
    hd = A_HEAD_DIM

    qi = qi_ref[...].astype(bf16)
    wt = (w_ref[...] * (IDX_HEADS ** -0.5 * IDX_DIM ** -0.5)).T
    srow = lax.broadcasted_iota(jnp.int32, (tq, tq), 0)
    qcol = lax.broadcasted_iota(jnp.int32, (tq, tq), 1)
    diag_adm = (srow // CHUNK) <= (qcol // CHUNK)

    def score_block(kb):
        off = pl.multiple_of(kb * tq, tq)
        kib = ki_ref[pl.ds(off, tq), :][:, :IDX_DIM].astype(bf16)
        ds = [lax.dot_general(kib, qi[:, hh * IDX_DIM:(hh + 1) * IDX_DIM], NT_DIMS, preferred_element_type=f32)
              for hh in range(IDX_HEADS)]
        acc = jnp.zeros((tq, tq), f32)
        for hh in range(IDX_HEADS):
            acc = acc + wt[hh:hh + 1, :] * jnp.maximum(ds[hh], 0.0)
        adm = jnp.logical_or(kb < qb, diag_adm)
        keys_ref[kb] = jnp.where(adm, _sort_key(acc), jnp.int32(INT_MIN))

    def score_pair(kp, c):
        score_block(2 * kp)
        score_block(2 * kp + 1)
        return c

    lax.fori_loop(0, nkb // 2, score_pair, 0)

    @pl.when(nkb % 2 == 1)
    def _():
        score_block(nkb - 1)

    def count(cmp, c):
        cb = jnp.broadcast_to(c, (COUNT_ROWS, tq))

        def body(kb, acc):
            for r0 in range(0, tq, COUNT_ROWS):
                acc = acc + jnp.where(cmp(keys_ref[kb, r0:r0 + COUNT_ROWS, :], cb), 1.0, 0.0)
            return acc
        acc = lax.fori_loop(0, nkb, body, jnp.zeros((COUNT_ROWS, tq), f32))
        return jnp.sum(acc, axis=0, keepdims=True)

    lane = lax.broadcasted_iota(jnp.int32, (1, tq), 1)
    n_adm = ((qb * tq + lane) // CHUNK + 1) * CHUNK
    kk = jnp.minimum(topk, n_adm).astype(f32)
    thr = _kth_largest(lambda c: count(lambda k, t: k >= t, c), kk)
    n_ge = count(lambda k, t: k >= t, thr)
    has_tie = jnp.max(n_ge - kk) > 0.0

    @pl.when(jnp.logical_not(has_tie))
    def _():
        def body(kb, c):
            bias_ref[kb] = jnp.where(keys_ref[kb] >= thr, 0.0, NEG)
            return c
        lax.fori_loop(0, nkb, body, 0)

    @pl.when(has_tie)
    def _():
        need = kk - count(lambda k, t: k > t, thr)
        tri = jnp.where(qcol <= srow, 1.0, 0.0).astype(bf16)

        def body(kb, seen):
            k = keys_ref[kb]
            eq = jnp.where(k == thr, 1.0, 0.0)
            rank = seen + jnp.dot(tri, eq.astype(bf16), preferred_element_type=f32)
            take = jnp.logical_or(k > thr, jnp.logical_and(k == thr, rank <= need))
            bias_ref[kb] = jnp.where(take, 0.0, NEG)
            return seen + jnp.sum(eq, axis=0, keepdims=True)
        lax.fori_loop(0, nkb, body, jnp.zeros((1, tq), f32))

    qs_ref[...] = (q_ref[...] * (hd ** -0.5 * LOG2E)).astype(bf16)
    for h in range(A_HEADS):
        acc_refs[h][...] = jnp.zeros((hd, tq), f32)

    def att_blocks(kbs, carry):
        ms, ls = carry
        kblks = [k_ref[pl.ds(pl.multiple_of(kb * tq, tq), tq), :] for kb in kbs]
        biases = [bias_ref[kb] for kb in kbs]
        sts = [[lax.dot_general(kblk[:, h * hd:(h + 1) * hd], qs_ref[:, h * hd:(h + 1) * hd], NT_DIMS,
                                preferred_element_type=f32) for kblk in kblks] for h in range(A_HEADS)]
        new_ms, new_ls, ps, corrs = [], [], [], []
        for h in range(A_HEADS):
            st = [s + b for s, b in zip(sts[h], biases)]
            m_new = ms[h]
            for s in st:
                m_new = jnp.maximum(m_new, jnp.max(s, axis=0, keepdims=True))
            p = [jnp.exp2(s - m_new) for s in st]
            corr = jnp.exp2(ms[h] - m_new)
            l_new = corr * ls[h]
            for pp in p:
                l_new = l_new + jnp.sum(pp, axis=0, keepdims=True)
            new_ls.append(l_new)
            new_ms.append(m_new)
            ps.append([pp.astype(bf16) for pp in p])
            corrs.append(corr)
        for h in range(A_HEADS):
            upd = corrs[h] * acc_refs[h][...]
            for kb, pp in zip(kbs, ps[h]):
                upd = upd + jnp.dot(vt_ref[kb, h], pp, preferred_element_type=f32)
            acc_refs[h][...] = upd
        return tuple(new_ms), tuple(new_ls)

    init = (tuple(jnp.full((1, tq), NEG, f32) for _ in range(A_HEADS)),
            tuple(jnp.zeros((1, tq), f32) for _ in range(A_HEADS)))
    carry = lax.fori_loop(0, nkb // 2, lambda kp, c: att_blocks((2 * kp, 2 * kp + 1), c), init)
    _, ls = lax.cond(nkb % 2 == 1, lambda c: att_blocks((nkb - 1,), c), lambda c: c, carry)
    for h in range(A_HEADS):
        o_ref[:, h * hd:(h + 1) * hd] = (acc_refs[h][...] / ls[h]).T.astype(o_ref.dtype)


def _dsa_prompt(hp, kb16, vt16, kiw, obuf, batch, seq):
    tq = ATT_BLOCK
    nq = seq // tq
    topk = min(TOPK_MAX, seq // 4)
    kern = functools.partial(_dsa_prompt_kernel, tq=tq, topk=topk)
    once = pl.Buffered(1)
    return pl.pallas_call(
        kern,
        grid=(batch, nq),
        in_specs=[
            pl.BlockSpec((tq, D_A), lambda b, i: (b * nq + i, COL_Q // D_A)),
            pl.BlockSpec((seq, D_A), lambda b, i: (b, 0), pipeline_mode=once),
            pl.BlockSpec((nq, A_HEADS, A_HEAD_DIM, tq), lambda b, i: (b, 0, 0, 0), pipeline_mode=once),
            pl.BlockSpec((tq, 512), lambda b, i: (b * nq + i, COL_QI // 512)),
            pl.BlockSpec((seq, LANE), lambda b, i: (b, KIW_KI // LANE), pipeline_mode=once),
            pl.BlockSpec((tq, LANE), lambda b, i: (b * nq + i, KIW_W // LANE)),
            pl.BlockSpec(memory_space=pl.ANY),
        ],
        out_specs=pl.BlockSpec((tq, D_A), lambda b, i: (b * nq + i, 0)),
        out_shape=jax.ShapeDtypeStruct(obuf.shape, obuf.dtype),
        input_output_aliases={6: 0},
        scratch_shapes=[pltpu.VMEM((nq, tq, tq), jnp.int32), pltpu.VMEM((nq, tq, tq), f32),
                        pltpu.VMEM((tq, D_A), bf16)]
                       + [pltpu.VMEM((A_HEAD_DIM, tq), f32) for _ in range(A_HEADS)],
        compiler_params=_cparams(("parallel", "arbitrary")),
        name="dsa_prompt",
    )(hp, kb16, vt16, hp, kiw, kiw, obuf)


def _dsa_sample_kernel(q_ref, kn_ref, vn_ref, qi_ref, kin_ref, w_ref, kc_ref, vc_ref, kic_ref, _o_in, o_ref,
                       biasc_ref, biasn_ref, *, topk):
    nq = q_ref.shape[0]
    past = kic_ref.shape[0]

    def _select():
        qi = qi_ref[...].astype(bf16)
        w = w_ref[...][:, :IDX_HEADS] * (IDX_HEADS ** -0.5 * IDX_DIM ** -0.5)
        key_c = _sort_key(_idx_scores(qi, w, kic_ref[...].astype(bf16)))
        key_n = _sort_key(_idx_scores(qi, w, kin_ref[...][:, :IDX_DIM].astype(bf16)))

        def count(pred):
            cc = jnp.sum(_fold_lanes(jnp.where(pred(key_c), 1.0, 0.0)), axis=-1, keepdims=True)
            return cc + jnp.sum(jnp.where(pred(key_n), 1.0, 0.0), axis=-1, keepdims=True)

        kk = jnp.full((nq, 1), float(topk), f32)
        thr = _kth_largest(lambda c: count(lambda k: k >= c), kk)
        n_ge = count(lambda k: k >= thr)
        has_tie = jnp.max(n_ge - kk) > 0.0

        @pl.when(jnp.logical_not(has_tie))
        def _():
            biasc_ref[...] = jnp.where(key_c >= thr, 0.0, NEG)
            biasn_ref[...] = jnp.where(key_n >= thr, 0.0, NEG)

        @pl.when(has_tie)
        def _():
            need = kk - count(lambda k: k > thr)
            tri = _tri_incl(LANE)
            seen = jnp.zeros((nq, 1), f32)
            for j in range(past // LANE):
                k = key_c[:, j * LANE:(j + 1) * LANE]
                eq = jnp.where(k == thr, 1.0, 0.0)
                rank = seen + jnp.dot(eq.astype(bf16), tri, preferred_element_type=f32)
                take = jnp.logical_or(k > thr, jnp.logical_and(k == thr, rank <= need))
                biasc_ref[:, j * LANE:(j + 1) * LANE] = jnp.where(take, 0.0, NEG)
                seen = seen + jnp.sum(eq, axis=-1, keepdims=True)
            eq = jnp.where(key_n == thr, 1.0, 0.0)
            rank = seen + jnp.dot(eq.astype(bf16), _tri_incl(nq), preferred_element_type=f32)
            take = jnp.logical_or(key_n > thr, jnp.logical_and(key_n == thr, rank <= need))
            biasn_ref[...] = jnp.where(take, 0.0, NEG)

    _select()

    hd = A_HEAD_DIM
    q = (q_ref[...] * (hd ** -0.5 * LOG2E)).astype(bf16)
    bias_c = biasc_ref[...]
    bias_n = biasn_ref[...]
    scs, sns = [], []
    for h in range(A_HEADS):
        cols = slice(h * hd, (h + 1) * hd)
        k_h = kc_ref[pl.ds(h, past, stride=A_HEADS), :]
        scs.append(lax.dot_general(q[:, cols], k_h.astype(bf16), NT_DIMS, preferred_element_type=f32))
        sns.append(lax.dot_general(q[:, cols], kn_ref[:, h, :].astype(bf16), NT_DIMS, preferred_element_type=f32))
    pcs, pns, ls = [], [], []
    for h in range(A_HEADS):
        s_c = scs[h] + bias_c
        s_n = sns[h] + bias_n
        m = jnp.maximum(jnp.max(s_c, axis=-1, keepdims=True), jnp.max(s_n, axis=-1, keepdims=True))
        p_c = jnp.exp2(s_c - m)
        p_n = jnp.exp2(s_n - m)
        ls.append(jnp.sum(p_c, axis=-1, keepdims=True) + jnp.sum(p_n, axis=-1, keepdims=True))
        pcs.append(p_c.astype(bf16))
        pns.append(p_n.astype(bf16))
    for h in range(A_HEADS):
        cols = slice(h * hd, (h + 1) * hd)
        v_h = vc_ref[pl.ds(h, past, stride=A_HEADS), :]
        acc = jnp.dot(pcs[h], v_h.astype(bf16), preferred_element_type=f32)
        acc = acc + jnp.dot(pns[h], vn_ref[:, h, :].astype(bf16), preferred_element_type=f32)
        o_ref[:, cols] = (acc / ls[h]).astype(o_ref.dtype)


def _dsa_sample(hp, ks, vs, kiw, obuf, layer, row0, nb, nq, cache_k, cache_v, cache_kidx):
    depth, _, past = cache_k.shape[:3]
    topk = min(TOPK_MAX, (past + nq) // 4)
    kc = cache_k.reshape(depth, nb, past * A_HEADS, A_HEAD_DIM)
    vc = cache_v.reshape(depth, nb, past * A_HEADS, A_HEAD_DIM)
    rb = row0 // nq
    kern = functools.partial(_dsa_sample_kernel, topk=topk)
    return pl.pallas_call(
        kern,
        grid=(nb,),
        in_specs=[
            pl.BlockSpec((nq, D_A), lambda b: (rb + b, COL_Q // D_A)),
            pl.BlockSpec((nq, A_HEADS, A_HEAD_DIM), lambda b: (layer * nb + b, 0, 0)),
            pl.BlockSpec((nq, A_HEADS, A_HEAD_DIM), lambda b: (layer * nb + b, 0, 0)),
            pl.BlockSpec((nq, 512), lambda b: (rb + b, COL_QI // 512)),
            pl.BlockSpec((nq, LANE), lambda b: (b, KIW_KI // LANE)),
            pl.BlockSpec((nq, LANE), lambda b: (b, KIW_W // LANE)),
            pl.BlockSpec((None, None, past * A_HEADS, A_HEAD_DIM), lambda b: (layer, b, 0, 0)),
            pl.BlockSpec((None, None, past * A_HEADS, A_HEAD_DIM), lambda b: (layer, b, 0, 0)),
            pl.BlockSpec((None, None, past, IDX_DIM), lambda b: (layer, b, 0, 0)),
            pl.BlockSpec(memory_space=pl.ANY),
        ],
        out_specs=pl.BlockSpec((nq, D_A), lambda b: (rb + b, 0)),
        out_shape=jax.ShapeDtypeStruct(obuf.shape, obuf.dtype),
        input_output_aliases={9: 0},
        scratch_shapes=[pltpu.VMEM((nq, past), f32), pltpu.VMEM((nq, nq), f32)],
        compiler_params=_cparams(("parallel",)),
        name="dsa_sample",
    )(hp, ks, vs, hp, kiw, kiw, kc, vc, cache_kidx, obuf)


def _pool_kernel(prev_ref, u_ref, pw_ref, ps_ref, _o_in, o_ref, *, tt, first_pos, zero_first):
    tb = pl.program_id(1)
    halo = POOL_STATE + 1
    prev = prev_ref[...]
    if zero_first:
        prev = jnp.where(tb == 0, 0.0, prev)
    cur = u_ref[...]
    ext = jnp.concatenate([prev, cur], axis=0)
    pos = first_pos + tb * tt + lax.broadcasted_iota(jnp.int32, (tt, 1), 0)
    outs = []
    for gi, wdw in enumerate(POOL_WINDOWS):
        cols = slice(gi * POOL_GROUP_DIM, (gi + 1) * POOL_GROUP_DIM)
        a = ext[:, cols]
        n = 1
        while n < wdw:
            a = a[n:] + a[:-n]
            n *= 2
        s = a[halo + 1 - wdw: halo + 1 - wdw + tt]
        cnt = jnp.minimum(pos + 1, wdw).astype(f32)
        d = s / cnt - cur[:, cols]
        outs.append(jnp.dot(d.astype(bf16), pw_ref[gi].astype(bf16), preferred_element_type=f32))
    o = jnp.concatenate(outs, axis=-1) * ps_ref[...]
    o_ref[...] = o.astype(o_ref.dtype)


def _pool(hp, obuf, row0, nb, t, tt, prev, pool_w, pool_scale, first_pos):
    ntb = t // tt
    halo = POOL_STATE + 1
    rb = row0 // tt
    if prev is None:
        prev_arr = hp
        per = tt // halo
        prev_spec = pl.BlockSpec(
            (halo, D_B), lambda b, i: (jnp.maximum((row0 // halo) + (b * ntb + i) * per - 1, 0), COL_U // D_B))
    else:
        prev_arr = prev
        prev_spec = pl.BlockSpec((None, halo, D_B), lambda b, i: (b, 0, 0))
    kern = functools.partial(_pool_kernel, tt=tt, first_pos=first_pos, zero_first=prev is None)
    return pl.pallas_call(
        kern,
        grid=(nb, ntb),
        in_specs=[prev_spec,
                  pl.BlockSpec((tt, D_B), lambda b, i: (rb + b * ntb + i, COL_U // D_B)),
                  pl.BlockSpec((len(POOL_WINDOWS), POOL_GROUP_DIM, POOL_GROUP_DIM), lambda b, i: (0, 0, 0)),
                  pl.BlockSpec((1, D_B), lambda b, i: (0, 0)),
                  pl.BlockSpec(memory_space=pl.ANY)],
        out_specs=pl.BlockSpec((tt, D_B), lambda b, i: (rb + b * ntb + i, 0)),
        out_shape=jax.ShapeDtypeStruct(obuf.shape, obuf.dtype),
        input_output_aliases={4: 0},
        compiler_params=_cparams(("parallel", "arbitrary")),
        name="pool",
    )(prev_arr, hp, pool_w, pool_scale.reshape(1, D_B), obuf)


SAFE_LOG_RANGE = 80.0


def _hgrn_kernel(cq_ref, cf_ref, ci_ref, cg_ref, lb_ref, ng_ref, s0_ref, _o_in, o_ref, sn_ref,
                 st_ref, b_ref, q_ref, k_ref, osc_ref, *, tt, blk):
    tb = pl.program_id(1)
    nh = C_HEADS
    hd = C_HEAD_DIM

    @pl.when(tb == 0)
    def _init():
        for h in range(nh):
            st_ref[h] = s0_ref[h].T

    lb = lb_ref[...]
    f = jnp.maximum(lb, LB_FLOOR) + (1.0 - lb) * jax.nn.sigmoid(cf_ref[...])
    g = jnp.log(f)
    k_ref[...] = 1.0 - f
    q_ref[...] = _silu(cq_ref[...])
    r = lax.broadcasted_iota(jnp.int32, (tt, tt), 0)
    c = lax.broadcasted_iota(jnp.int32, (tt, tt), 1)
    tri = jnp.where(jnp.logical_and(r // blk == c // blk, c <= r), 1.0, 0.0).astype(bf16)
    g1 = g.astype(bf16)
    e1 = g - g1.astype(f32)
    g2 = e1.astype(bf16)
    g3 = (e1 - g2.astype(f32)).astype(bf16)
    b = (jnp.dot(tri, g1, preferred_element_type=f32) + jnp.dot(tri, g2, preferred_element_type=f32)
         + jnp.dot(tri, g3, preferred_element_type=f32))
    b_ref[...] = b
    safe = jnp.min(b) > -SAFE_LOG_RANGE

    trow = lax.broadcasted_iota(jnp.int32, (blk, 1), 0)

    def exact_chunk(ci):
        r0 = pl.multiple_of(ci * blk, blk)
        for h in range(nh):
            rows = pl.ds(r0, blk)
            cols = slice(h * hd, (h + 1) * hd)
            bb = b_ref[rows, cols]
            qh = q_ref[rows, cols]
            kh = k_ref[rows, cols]
            ih = ci_ref[rows, cols]
            bl = bb[blk - 1:blk, :]
            ke = (kh * jnp.exp(bl - bb)).astype(bf16)
            ib = ih.astype(bf16)
            st = st_ref[h]
            o = lax.dot_general((qh * jnp.exp(bb)).astype(bf16), st.astype(bf16), NT_DIMS,
                                preferred_element_type=f32)
            for s in range(blk):
                live = trow >= s
                dec = jnp.exp(jnp.where(live, bb - bb[s:s + 1, :], 0.0))
                a_col = jnp.sum(qh * kh[s:s + 1, :] * dec, axis=-1, keepdims=True)
                o = o + jnp.where(live, a_col, 0.0) * ih[s:s + 1, :]
            osc_ref[rows, cols] = o
            st_ref[h] = st * jnp.exp(bl) + lax.dot_general(ib, ke, TN_DIMS, preferred_element_type=f32)

    @pl.when(safe)
    def _():
        same_blk = r // blk == c // blk
        blk_causal = jnp.logical_and(same_blk, c <= r)
        ones_blk = jnp.where(same_blk, 1.0, 0.0).astype(bf16)
        bl_all = (jnp.dot(ones_blk, g1, preferred_element_type=f32) + jnp.dot(ones_blk, g2, preferred_element_type=f32)
                  + jnp.dot(ones_blk, g3, preferred_element_type=f32))
        for h in range(nh):
            cols = slice(h * hd, (h + 1) * hd)
            bb = b[:, cols]
            blh = bl_all[:, cols]
            qh = q_ref[:, cols]
            ib = ci_ref[:, cols].astype(bf16)
            ke = (k_ref[:, cols] * jnp.exp(blh - bb)).astype(bf16)
            qe = (qh * jnp.exp(bb - blh)).astype(bf16)
            qb = (qh * jnp.exp(bb)).astype(bf16)
            a = jnp.where(blk_causal, lax.dot_general(qe, ke, NT_DIMS, preferred_element_type=f32), 0.0)
            o_intra = jnp.dot(a.astype(bf16), ib, preferred_element_type=f32)
            incs = [lax.dot_general(ib[c0:c0 + blk], ke[c0:c0 + blk], TN_DIMS, preferred_element_type=f32)
                    for c0 in range(0, tt, blk)]
            dec = jnp.exp(blh)
            st = st_ref[h]
            o_inter = []
            for ci, c0 in enumerate(range(0, tt, blk)):
                o_inter.append(lax.dot_general(qb[c0:c0 + blk], st.astype(bf16), NT_DIMS,
                                               preferred_element_type=f32))
                st = st * dec[c0:c0 + 1, :] + incs[ci]
            st_ref[h] = st
            osc_ref[:, cols] = o_intra + jnp.concatenate(o_inter, axis=0)

    @pl.when(jnp.logical_not(safe))
    def _():
        def body(ci, c):
            exact_chunk(ci)
            return c
        lax.fori_loop(0, tt // blk, body, 0)

    outs = []
    for h in range(nh):
        cols = slice(h * hd, (h + 1) * hd)
        outs.append(_rms(osc_ref[:, cols], ng_ref[...]))
    o_ref[...] = (jnp.concatenate(outs, axis=-1) * _silu(cg_ref[...])).astype(o_ref.dtype)

    @pl.when(tb == pl.num_programs(1) - 1)
    def _fin():
        for h in range(nh):
            sn_ref[h] = st_ref[h].T


def _hgrn(hp, obuf, row0, nb, t, tt, lb, norm_g, s0):
    ntb = t // tt
    rb = row0 // tt

    def col(cidx):
        return pl.BlockSpec((tt, D_C), lambda b, i: (rb + b * ntb + i, cidx // D_C))

    st_spec = pl.BlockSpec((None, C_HEADS, C_HEAD_DIM, C_V_DIM), lambda b, i: (b, 0, 0, 0))
    kern = functools.partial(_hgrn_kernel, tt=tt, blk=HGRN_BLOCK)
    return pl.pallas_call(
        kern,
        grid=(nb, ntb),
        in_specs=[col(COL_CQ), col(COL_CF), col(COL_CI), col(COL_CG),
                  pl.BlockSpec((1, D_C), lambda b, i: (0, 0)),
                  pl.BlockSpec((1, C_V_DIM), lambda b, i: (0, 0)),
                  st_spec,
                  pl.BlockSpec(memory_space=pl.ANY)],
        out_specs=[pl.BlockSpec((tt, D_C), lambda b, i: (rb + b * ntb + i, 0)), st_spec],
        out_shape=[jax.ShapeDtypeStruct(obuf.shape, obuf.dtype),
                   jax.ShapeDtypeStruct((nb, C_HEADS, C_HEAD_DIM, C_V_DIM), f32)],
        input_output_aliases={7: 0},
        scratch_shapes=[pltpu.VMEM((C_HEADS, C_V_DIM, C_HEAD_DIM), f32),
                        pltpu.VMEM((tt, D_C), f32), pltpu.VMEM((tt, D_C), f32),
                        pltpu.VMEM((tt, D_C), f32), pltpu.VMEM((tt, D_C), f32)],
        compiler_params=_cparams(("parallel", "arbitrary")),
        name="hgrn",
    )(hp, hp, hp, hp, lb.reshape(1, D_C), norm_g.reshape(1, C_V_DIM), s0, obuf)


def _merge_kernel(x_ref, g_ref, oa_ref, ob_ref, oc_ref, wga_ref, wgb_ref, wgc_ref,
                  pa_ref, pb_ref, pc_ref, wo_ref, o_ref, xn_ref):
    j = pl.program_id(1)

    @pl.when(j == 0)
    def _():
        x = x_ref[...]
        xn_ref[...] = _rms(x, g_ref[...]).astype(bf16)
        o_ref[...] = x

    xn = xn_ref[...]

    def branch(wg_ref, o_ref_, p_ref):
        gate = jax.nn.sigmoid(jnp.dot(xn, wg_ref[...], preferred_element_type=f32))
        return gate * jnp.dot(o_ref_[...], p_ref[...], preferred_element_type=f32)

    merged = branch(wga_ref, oa_ref, pa_ref) + branch(wgb_ref, ob_ref, pb_ref) + branch(wgc_ref, oc_ref, pc_ref)
    o_ref[...] += jnp.dot(merged.astype(bf16), wo_ref[...], preferred_element_type=f32)


def _merge(h, g, oa, ob, oc, wg, pa, pb, pc, wo, tm, tj):
    m, d = h.shape
    nj = d // tj
    row = lambda i, j: (i, 0)
    return pl.pallas_call(
        _merge_kernel,
        grid=(m // tm, nj),
        in_specs=[pl.BlockSpec((tm, d), row),
                  pl.BlockSpec((1, d), lambda i, j: (0, 0)),
                  pl.BlockSpec((tm, D_A), row),
                  pl.BlockSpec((tm, D_B), row),
                  pl.BlockSpec((tm, D_C), row),
                  pl.BlockSpec((d, tj), lambda i, j: (0, j)),
                  pl.BlockSpec((d, tj), lambda i, j: (0, nj + j)),
                  pl.BlockSpec((d, tj), lambda i, j: (0, 2 * nj + j)),
                  pl.BlockSpec((D_A, tj), lambda i, j: (0, j)),
                  pl.BlockSpec((D_B, tj), lambda i, j: (0, j)),
                  pl.BlockSpec((D_C, tj), lambda i, j: (0, j)),
                  pl.BlockSpec((tj, d), lambda i, j: (j, 0))],
        out_specs=pl.BlockSpec((tm, d), row),
        out_shape=jax.ShapeDtypeStruct((m, d), f32),
        scratch_shapes=[pltpu.VMEM((tm, d), bf16)],
        compiler_params=_cparams(("parallel", "arbitrary")),
        name="merge",
    )(h, g.reshape(1, d), oa, ob, oc, wg, wg, wg, pa, pb, pc, wo)


def _router_kernel(x_ref, g_ref, r_ref, o_ref):
    xn = _rms(x_ref[...], g_ref[...]).astype(bf16)
    logits = jnp.dot(xn, r_ref[...], preferred_element_type=f32)
    lane = lax.broadcasted_iota(jnp.int32, logits.shape, 1)
    logits = jnp.where(lane < N_EXPERTS, logits, -jnp.inf)
    m1 = jnp.max(logits, axis=-1, keepdims=True)
    i1 = jnp.min(jnp.where(logits == m1, lane, LANE), axis=-1, keepdims=True)
    rest = jnp.where(lane == i1, -jnp.inf, logits)
    m2 = jnp.max(rest, axis=-1, keepdims=True)
    i2 = jnp.min(jnp.where(rest == m2, lane, LANE), axis=-1, keepdims=True)
    e2 = jnp.exp(m2 - m1)
    g1 = 1.0 / (1.0 + e2)
    g2 = e2 / (1.0 + e2)
    route = jnp.where(lane == 0, i1.astype(f32), jnp.where(lane == 1, i2.astype(f32),
                      jnp.where(lane == 2, g1, jnp.where(lane == 3, g2, 0.0))))
    o_ref[...] = route[:, :N_EXPERTS]


def _router(h, g, router, tm):
    m, d = h.shape
    rp = jnp.pad(router, ((0, 0), (0, LANE - N_EXPERTS))).astype(bf16)
    return pl.pallas_call(
        _router_kernel,
        grid=(m // tm,),
        in_specs=[pl.BlockSpec((tm, d), lambda i: (i, 0)),
                  pl.BlockSpec((1, d), lambda i: (0, 0)),
                  pl.BlockSpec((d, LANE), lambda i: (0, 0))],
        out_specs=pl.BlockSpec((tm, N_EXPERTS), lambda i: (i, 0)),
        out_shape=jax.ShapeDtypeStruct((m, N_EXPERTS), f32),
        compiler_params=_cparams(("parallel",)),
        name="router",
    )(h, g.reshape(1, d), rp)


def _ffn_kernel(x_ref, g_ref, wg_ref, wu_ref, wd_ref, o_ref, xn_ref):
    @pl.when(pl.program_id(1) == 0)
    def _():
        x = x_ref[...]
        xn_ref[...] = _rms(x, g_ref[...]).astype(bf16)
        o_ref[...] = x

    xn = xn_ref[...]
    a = jnp.dot(xn, wg_ref[...], preferred_element_type=f32)
    u = jnp.dot(xn, wu_ref[...], preferred_element_type=f32)
    act = (_silu(a) * u).astype(bf16)
    o_ref[...] += jnp.dot(act, wd_ref[...], preferred_element_type=f32)


def _ffn(h, g, wg, wu, wd, tm, tf):
    m, d = h.shape
    ff = wg.shape[1]
    return pl.pallas_call(
        _ffn_kernel,
        grid=(m // tm, ff // tf),
        in_specs=[pl.BlockSpec((tm, d), lambda i, f: (i, 0)),
                  pl.BlockSpec((1, d), lambda i, f: (0, 0)),
                  pl.BlockSpec((d, tf), lambda i, f: (0, f)),
                  pl.BlockSpec((d, tf), lambda i, f: (0, f)),
                  pl.BlockSpec((tf, d), lambda i, f: (f, 0))],
        out_specs=pl.BlockSpec((tm, d), lambda i, f: (i, 0)),
        out_shape=jax.ShapeDtypeStruct((m, d), f32),
        scratch_shapes=[pltpu.VMEM((tm, d), bf16)],
        compiler_params=_cparams(("parallel", "arbitrary")),
        name="ffn",
    )(h, g.reshape(1, d), wg, wu, wd)


MOE_ROWS = 1024
MOE_SUB = 256
DMA_UNROLL = 8


def _moe_plan(route, m):
    n = 2 * m
    n_groups = -(-n // MOE_ROWS) + N_EXPERTS
    ef = route[:, 0:2].astype(jnp.int32).T.reshape(n)
    row_dst = jnp.argsort(ef, stable=True).astype(jnp.int32)
    row_tok = jnp.where(row_dst >= m, row_dst - m, row_dst)
    counts = jnp.sum((ef[:, None] == jnp.arange(N_EXPERTS, dtype=jnp.int32)[None, :]).astype(jnp.int32), axis=0)
    first = jnp.cumsum(counts) - counts
    per_e = (counts + MOE_ROWS - 1) // MOE_ROWS
    size = (counts + jnp.maximum(per_e, 1) - 1) // jnp.maximum(per_e, 1)
    size = (size + DMA_UNROLL - 1) // DMA_UNROLL * DMA_UNROLL
    ends = jnp.cumsum(per_e)
    gi = jnp.arange(n_groups, dtype=jnp.int32)
    ge = jnp.minimum(jnp.sum((gi[:, None] >= ends[None, :]).astype(jnp.int32), axis=1), N_EXPERTS - 1)
    j = gi - (ends - per_e)[ge]
    g_start = first[ge] + j * size[ge]
    g_cnt = jnp.clip(counts[ge] - j * size[ge], 0, size[ge])
    g_cnt = jnp.where(gi < ends[-1], g_cnt, 0)
    last_e = ge[jnp.maximum(ends[-1] - 1, 0)]
    ge = jnp.where(gi < ends[-1], ge, last_e)
    return ge, g_start.astype(jnp.int32), g_cnt.astype(jnp.int32), row_tok, row_dst


def _moe_kernel(ge_ref, gs_ref, gc_ref, tok_ref, dst_ref, h_hbm, g_ref, wg_ref, wu_ref, wd_ref, y_hbm,
                xs_ref, acc_ref, stage_ref, spare_ref, gsem, ssem, *, nf):
    s = pl.program_id(0)
    fi = pl.program_id(1)
    cnt = gc_ref[s]
    start = gs_ref[s]
    n_rows = tok_ref.shape[0]
    nsub = (cnt + MOE_SUB - 1) // MOE_SUB
    chunk = stage_ref.shape[0] // nf

    def gather_chunk(first_row, base):
        for u in range(chunk):
            tok = tok_ref[jnp.minimum(first_row + base + u, n_rows - 1)]
            pltpu.make_async_copy(h_hbm.at[pl.ds(tok, 1)], stage_ref.at[pl.ds(base + u, 1)], gsem).start()

    def row_tile(j):
        rows = pl.ds(pl.multiple_of(j * MOE_SUB, MOE_SUB), MOE_SUB)
        x = xs_ref[rows, :]
        a = jnp.dot(x, wg_ref[...].astype(bf16), preferred_element_type=f32)
        u = jnp.dot(x, wu_ref[...].astype(bf16), preferred_element_type=f32)
        act = (_silu(a) * u).astype(bf16)
        acc_ref[rows, :] += jnp.dot(act, wd_ref[...].astype(bf16), preferred_element_type=f32)

    @pl.when(cnt > 0)
    def _active():
        @pl.when(fi == 0)
        def _stage_in():
            @pl.when(s == 0)
            def _():
                def issue(c, carry):
                    gather_chunk(start, c * chunk)
                    return carry
                lax.fori_loop(0, nf, issue, 0)

            def wait_chunk(c, carry):
                pltpu.make_async_copy(h_hbm.at[pl.ds(0, chunk)], stage_ref.at[pl.ds(0, chunk)], gsem).wait()
                return carry
            lax.fori_loop(0, nf, wait_chunk, 0)

            def norm(j, c):
                rows = pl.ds(pl.multiple_of(j * MOE_SUB, MOE_SUB), MOE_SUB)
                xs_ref[rows, :] = _rms(stage_ref[rows, :], g_ref[...]).astype(bf16)
                acc_ref[rows, :] = jnp.zeros((MOE_SUB, acc_ref.shape[1]), f32)
                return c
            lax.fori_loop(0, nsub, norm, 0)

        nxt = jnp.minimum(s + 1, pl.num_programs(0) - 1)
        prefetch = jnp.logical_and(s + 1 < pl.num_programs(0), gc_ref[nxt] > 0)

        @pl.when(prefetch)
        def _():
            gather_chunk(gs_ref[nxt], fi * chunk)
            row_tile(0)

        @pl.when(jnp.logical_not(prefetch))
        def _():
            row_tile(0)

        def body(j, c):
            row_tile(j)
            return c
        lax.fori_loop(1, nsub, body, 0)

        @pl.when(fi == nf - 1)
        def _scatter():
            def issue_one(r):
                pltpu.make_async_copy(acc_ref.at[pl.ds(r, 1)], y_hbm.at[pl.ds(dst_ref[start + r], 1)], ssem).start()

            def issue(r8, c):
                for u in range(DMA_UNROLL):
                    issue_one(r8 * DMA_UNROLL + u)
                return c
            n8 = cnt // DMA_UNROLL
            lax.fori_loop(0, n8, issue, 0)

            def issue_tail(r, c):
                issue_one(r)
                return c
            lax.fori_loop(n8 * DMA_UNROLL, cnt, issue_tail, 0)

            def issue_spare(r, c):
                pltpu.make_async_copy(acc_ref.at[pl.ds(r, 1)], spare_ref.at[pl.ds(r - cnt, 1)], ssem).start()
                return c
            lax.fori_loop(cnt, nsub * MOE_SUB, issue_spare, 0)

            def wait_tile(j, c):
                pltpu.make_async_copy(acc_ref.at[pl.ds(0, MOE_SUB)], y_hbm.at[pl.ds(0, MOE_SUB)], ssem).wait()
                return c
            lax.fori_loop(0, nsub, wait_tile, 0)


def _moe(h, g, route, wg, wu, wd, tf):
    m, d = h.shape
    ne, _, ff = wg.shape
    nf = ff // tf
    chunk = -(-MOE_ROWS // nf)
    chunk = -(-chunk // DMA_UNROLL) * DMA_UNROLL
    ge, gs, gc, row_tok, row_dst = _moe_plan(route, m)
    n_groups = ge.shape[0]

    def wcol(s, f, ge_r, gs_r, gc_r, tok_r, dst_r):
        return (ge_r[s], 0, jnp.where(gc_r[s] > 0, f, nf - 1))

    def wrow(s, f, ge_r, gs_r, gc_r, tok_r, dst_r):
        return (ge_r[s], jnp.where(gc_r[s] > 0, f, nf - 1), 0)

    grid_spec = pltpu.PrefetchScalarGridSpec(
        num_scalar_prefetch=5,
        grid=(n_groups, nf),
        in_specs=[pl.BlockSpec(memory_space=pl.ANY),
                  pl.BlockSpec((1, d), lambda s, f, *_: (0, 0)),
                  pl.BlockSpec((None, d, tf), wcol),
                  pl.BlockSpec((None, d, tf), wcol),
                  pl.BlockSpec((None, tf, d), wrow)],
        out_specs=pl.BlockSpec(memory_space=pl.ANY),
        scratch_shapes=[pltpu.VMEM((MOE_ROWS, d), bf16), pltpu.VMEM((MOE_ROWS, d), f32),
                        pltpu.VMEM((chunk * nf, d), f32), pltpu.VMEM((MOE_SUB, d), f32),
                        pltpu.SemaphoreType.DMA(()), pltpu.SemaphoreType.DMA(())],
    )
    return pl.pallas_call(
        functools.partial(_moe_kernel, nf=nf),
        grid_spec=grid_spec,
        out_shape=jax.ShapeDtypeStruct((2 * m, d), f32),
        compiler_params=_cparams(("arbitrary", "arbitrary")),
        name="moe",
    )(ge, gs, gc, row_tok, row_dst, h, g.reshape(1, d), wg, wu, wd)


def _combine_kernel(x_ref, r_ref, y1_ref, y2_ref, o_ref):
    r = r_ref[...]
    o_ref[...] = x_ref[...] + (r[:, 2:3] * y1_ref[...] + r[:, 3:4] * y2_ref[...])


def _combine(h, route, y, tm):
    m, d = h.shape
    nb = m // tm
    return pl.pallas_call(
        _combine_kernel,
        grid=(nb,),
        in_specs=[pl.BlockSpec((tm, d), lambda i: (i, 0)),
                  pl.BlockSpec((tm, N_EXPERTS), lambda i: (i, 0)),
                  pl.BlockSpec((tm, d), lambda i: (i, 0)),
                  pl.BlockSpec((tm, d), lambda i: (nb + i, 0))],
        out_specs=pl.BlockSpec((tm, d), lambda i: (i, 0)),
        out_shape=jax.ShapeDtypeStruct((m, d), f32),
        compiler_params=_cparams(("parallel",)),
        name="moe_combine",
    )(h, route, y, y)


def _combine_norm_kernel(x_ref, r_ref, y1_ref, y2_ref, g_ref, op_ref, os_ref, *, n_p):
    r = r_ref[...]
    v = _rms(x_ref[...] + (r[:, 2:3] * y1_ref[...] + r[:, 3:4] * y2_ref[...]), g_ref[...])

    @pl.when(pl.program_id(0) < n_p)
    def _():
        op_ref[...] = v

    @pl.when(pl.program_id(0) >= n_p)
    def _():
        os_ref[...] = v


def _combine_norm(h, route, y, g, mp, tm):
    m, d = h.shape
    nb = m // tm
    n_p = mp // tm
    return pl.pallas_call(
        functools.partial(_combine_norm_kernel, n_p=n_p),
        grid=(nb,),
        in_specs=[pl.BlockSpec((tm, d), lambda i: (i, 0)),
                  pl.BlockSpec((tm, N_EXPERTS), lambda i: (i, 0)),
                  pl.BlockSpec((tm, d), lambda i: (i, 0)),
                  pl.BlockSpec((tm, d), lambda i: (nb + i, 0)),
                  pl.BlockSpec((1, d), lambda i: (0, 0))],
        out_specs=[pl.BlockSpec((tm, d), lambda i: (jnp.minimum(i, n_p - 1), 0)),
                   pl.BlockSpec((tm, d), lambda i: (jnp.maximum(i - n_p, 0), 0))],
        out_shape=[jax.ShapeDtypeStruct((mp, d), f32), jax.ShapeDtypeStruct((m - mp, d), f32)],
        compiler_params=_cparams(("arbitrary",)),
        name="moe_combine_norm",
    )(h, route, y, y, g.reshape(1, d))


def _final_norm_kernel(x_ref, g_ref, o_ref):
    o_ref[...] = _rms(x_ref[...], g_ref[...])


def _final_norm(h, g, row0, rows, tm):
    d = h.shape[1]
    rb = row0 // tm
    return pl.pallas_call(
        _final_norm_kernel,
        grid=(rows // tm,),
        in_specs=[pl.BlockSpec((tm, d), lambda i: (rb + i, 0)),
                  pl.BlockSpec((1, d), lambda i: (0, 0))],
        out_specs=pl.BlockSpec((tm, d), lambda i: (i, 0)),
        out_shape=jax.ShapeDtypeStruct((rows, d), f32),
        compiler_params=_cparams(("parallel",)),
        name="final_norm",
    )(h, g.reshape(1, d))


def _cast_kernel(w_ref, o_ref):
    o_ref[...] = w_ref[...].astype(bf16)


def _cast_bf16(w, layer, tr):
    _, r, c = w.shape
    return pl.pallas_call(
        _cast_kernel,
        grid=(r // tr,),
        in_specs=[pl.BlockSpec((None, tr, c), lambda i: (layer, i, 0))],
        out_specs=pl.BlockSpec((tr, c), lambda i: (i, 0)),
        out_shape=jax.ShapeDtypeStruct((r, c), bf16),
        compiler_params=_cparams(("parallel",)),
        name="cast_bf16",
    )(w)


def _pack_kernel(w_ref, wp_ref, wk_ref, wv_ref, wg_ref):
    o = np.cumsum((0,) + IN_SIZES)
    w = w_ref[...]
    tc = w.shape[1]
    cut = lambda a, b: w[int(o[a]):int(o[b]), :]
    pad = lambda a: jnp.concatenate([a, jnp.zeros((LANE - a.shape[0], tc), f32)], axis=0)
    t = lambda a: a.T.astype(bf16)
    wp_ref[...] = jnp.concatenate([t(cut(0, 1)), t(cut(3, 4)), t(cut(6, 11))], axis=1)
    wk_ref[...] = jnp.concatenate([t(cut(1, 2)), t(pad(cut(4, 5))), t(pad(cut(5, 6)))], axis=1)
    wv_ref[...] = t(cut(2, 3))
    wg_ref[...] = t(cut(11, 12))


def _pack_in_proj(wt, layer, tc):
    _, n, d = wt.shape
    widths = (N_PACK, D_A + 2 * LANE, D_A, N_BRANCH * D_MODEL)
    return pl.pallas_call(
        _pack_kernel,
        grid=(d // tc,),
        in_specs=[pl.BlockSpec((None, n, tc), lambda i: (layer, 0, i))],
        out_specs=[pl.BlockSpec((tc, c), lambda i: (i, 0)) for c in widths],
        out_shape=[jax.ShapeDtypeStruct((d, c), bf16) for c in widths],
        compiler_params=_cparams(("parallel",)),
        name="pack_w_in",
    )(wt)


def kernel(x_prompt, x_sample, cache_k, cache_v, cache_kidx, cache_pool, state_hgrn, norm_mix_g, norm_ffn_g, final_norm_g, w_in, w_proj_a, w_proj_b, w_proj_c, w_out, pool_w, pool_scale, hgrn_lb_logits, hgrn_norm_g, ffn_w_gate, ffn_w_up, ffn_w_down, moe_router, moe_w_gate, moe_w_up, moe_w_down):
    batch, seq, d = x_prompt.shape
    nb, nq, _ = x_sample.shape
    depth = w_in.shape[0]
    past = cache_k.shape[2]
    mp = batch * seq
    ms = nb * nq
    m = mp + ms

    lb_soft = jax.nn.softmax(hgrn_lb_logits.astype(f32), axis=0)
    lb_all = jnp.cumsum(lb_soft, axis=0) - lb_soft[0:1]

    h = jnp.concatenate([x_prompt.reshape(mp, d), x_sample.reshape(ms, d)], axis=0)
    s0_prompt = jnp.zeros((batch, C_HEADS, C_HEAD_DIM, C_V_DIM), f32)
    tm = 768 if m % 768 == 0 else 256
    tmh = tm // 2

    kp = jnp.zeros((depth * mp, A_HEADS, A_HEAD_DIM), f32)
    vp = jnp.zeros((depth * mp, A_HEADS, A_HEAD_DIM), f32)
    ks = jnp.zeros((depth * ms, A_HEADS, A_HEAD_DIM), f32)
    vs = jnp.zeros((depth * ms, A_HEADS, A_HEAD_DIM), f32)
    w_in_t = jnp.swapaxes(w_in, 1, 2)
    tkv = ATT_BLOCK
    tkp = 1024 if mp % 1024 == 0 else ATT_BLOCK
    outs = [[] for _ in range(6)]
    for l in range(depth):
        wp, wk, wv, wgates = _pack_in_proj(w_in_t, l, LANE)
        hp = _inproj(h, norm_mix_g[l], wp, tm, 1024)
        kp, kb16, kiw_p = _inproj_k(h, norm_mix_g[l], wk, kp, l, 0, mp, tkp)
        ks, _, kiw_s = _inproj_k(h, norm_mix_g[l], wk, ks, l, mp, ms, tkv)
        vp, vt16 = _inproj_v(h, norm_mix_g[l], wv, vp, l, 0, mp, tkp, True)
        vs, = _inproj_v(h, norm_mix_g[l], wv, vs, l, mp, ms, tkv, False)

        oa = _dsa_prompt(hp, kb16, vt16, kiw_p, jnp.zeros((m, D_A), bf16), batch, seq)
        oa = _dsa_sample(hp, ks, vs, kiw_s, oa, l, mp, nb, nq, cache_k, cache_v, cache_kidx)
        ob = _pool(hp, jnp.zeros((m, D_B), bf16), 0, batch, seq, 256, None, pool_w[l], pool_scale[l], 0)
        prev = jnp.pad(cache_pool[l], ((0, 0), (1, 0), (0, 0)))
        ob = _pool(hp, ob, mp, nb, nq, nq, prev, pool_w[l], pool_scale[l], past)
        oc, st_p = _hgrn(hp, jnp.zeros((m, D_C), bf16), 0, batch, seq, 256, lb_all[l], hgrn_norm_g[l], s0_prompt)
        oc, st_s = _hgrn(hp, oc, mp, nb, nq, nq, lb_all[l], hgrn_norm_g[l], state_hgrn[l])

        h = _merge(h, norm_mix_g[l], oa, ob, oc, wgates, _cast_bf16(w_proj_a, l, 256), _cast_bf16(w_proj_b, l, 256),
                   _cast_bf16(w_proj_c, l, 256), _cast_bf16(w_out, l, 256), tmh, 512)

        uu = hp[:, COL_U:COL_U + D_B]
        outs[0].append(kiw_p[:, KIW_KI:KIW_KI + IDX_DIM].reshape(batch, seq, IDX_DIM))
        outs[1].append(uu[:mp].reshape(batch, seq, D_B)[:, -POOL_STATE:])
        outs[2].append(st_p)
        outs[3].append(kiw_s[:, KIW_KI:KIW_KI + IDX_DIM].reshape(nb, nq, IDX_DIM))
        u_ext = jnp.concatenate([cache_pool[l], uu[mp:].reshape(nb, nq, D_B)], axis=1)
        outs[4].append(u_ext[:, -POOL_STATE:])
        outs[5].append(st_s)

        j = l // 2
        if l % 2 == 0:
            h = _ffn(h, norm_ffn_g[l], _cast_bf16(ffn_w_gate, j, 256), _cast_bf16(ffn_w_up, j, 256),
                     _cast_bf16(ffn_w_down, j, 512), tm, 512)
        else:
            route = _router(h, norm_ffn_g[l], moe_router[j], tm)
            y = _moe(h, norm_ffn_g[l], route, moe_w_gate[j], moe_w_up[j], moe_w_down[j], 512)
            if l == depth - 1:
                y_prompt, y_sample = _combine_norm(h, route, y, final_norm_g, mp, tkv)
                h = None
            else:
                h = _combine(h, route, y, tmh)

    if h is not None:
        y_prompt = _final_norm(h, final_norm_g, 0, mp, 256)
        y_sample = _final_norm(h, final_norm_g, mp, ms, 256)
    y_prompt = y_prompt.reshape(batch, seq, d)
    y_sample = y_sample.reshape(nb, nq, d)
    kidx_p, pool_p, st_p, kidx_s, pool_s, st_s = (jnp.stack(o) for o in outs)
    return (y_prompt, y_sample,
            kp.reshape(depth, batch, seq, A_HEADS, A_HEAD_DIM), vp.reshape(depth, batch, seq, A_HEADS, A_HEAD_DIM),
            kidx_p, pool_p, st_p,
            ks.reshape(depth, nb, nq, A_HEADS, A_HEAD_DIM), vs.reshape(depth, nb, nq, A_HEADS, A_HEAD_DIM),
            kidx_s, pool_s, st_s)
```

```python
import functools

import numpy as np
import jax
import jax.numpy as jnp
from jax import lax
from jax.experimental import pallas as pl
from jax.experimental.pallas import tpu as pltpu

f32 = jnp.float32
bf16 = jnp.bfloat16

D_MODEL = 2048
CHUNK = 64
A_HEADS = 8
A_HEAD_DIM = 128
D_A = A_HEADS * A_HEAD_DIM
IDX_HEADS = 8
IDX_DIM = 64
TOPK_MAX = 256
POOL_WINDOWS = (2, 4, 8, 16)
POOL_GROUP_DIM = 128
D_B = len(POOL_WINDOWS) * POOL_GROUP_DIM
POOL_STATE = max(POOL_WINDOWS) - 1
C_HEADS = 4
C_HEAD_DIM = 128
C_V_DIM = 128
D_C = C_HEADS * C_HEAD_DIM
HGRN_BLOCK = 16
N_BRANCH = 3
N_EXPERTS = 8
IN_SIZES = (D_A, D_A, D_A, IDX_HEADS * IDX_DIM, IDX_DIM, IDX_HEADS,
            D_B, D_C, D_C, C_HEADS * C_V_DIM, C_HEADS * C_V_DIM, N_BRANCH * D_MODEL)
EPS = 1e-6
NEG = -1e30
LB_FLOOR = 1e-30
INT_MIN = -2 ** 31

LOG2E = 1.4426950408889634
LANE = 128
COUNT_ROWS = 64
ATT_BLOCK = 256
ATT_HEAD_GROUP = 8
COL_Q = 0
COL_QI = D_A
COL_U = COL_QI + 512
COL_CQ = COL_U + 512
COL_CF = COL_CQ + 512
COL_CI = COL_CF + 512
COL_CG = COL_CI + 512
N_PACK = COL_CG + 512
KIW_KI, KIW_W = 0, LANE

VMEM_LIMIT = 56 * 1024 * 1024

NT_DIMS = (((1,), (1,)), ((), ()))
TN_DIMS = (((0,), (0,)), ((), ()))


def _cparams(sem):
    return pltpu.CompilerParams(dimension_semantics=sem, vmem_limit_bytes=VMEM_LIMIT)


def _rms(x, g):
    ms = jnp.mean(x * x, axis=-1, keepdims=True)
    return x * lax.rsqrt(ms + EPS) * g


def _silu(x):
    return x * jax.nn.sigmoid(x)


def _sort_key(x):
    bits = pltpu.bitcast(x, jnp.int32)
    return bits ^ ((bits >> 31) & jnp.int32(0x7FFFFFFF))


def _kth_largest(count_ge, kk):
    zero = jnp.zeros(kk.shape, jnp.int32)
    t0 = jnp.where(count_ge(zero) >= kk, zero, jnp.int32(INT_MIN))

    def bit_body(i, t):
        c = t | lax.shift_left(jnp.int32(1), 30 - i)
        return jnp.where(count_ge(c) >= kk, c, t)

    return lax.fori_loop(0, 31, bit_body, t0)


def _inproj_kernel(x_ref, g_ref, w_ref, o_ref, xn_ref):
    @pl.when(pl.program_id(1) == 0)
    def _():
        xn_ref[...] = _rms(x_ref[...], g_ref[...]).astype(bf16)

    o_ref[...] = jnp.dot(xn_ref[...], w_ref[...], preferred_element_type=f32)


def _inproj(h, g, wp, tm, tn):
    m, d = h.shape
    n = wp.shape[1]
    return pl.pallas_call(
        _inproj_kernel,
        grid=(m // tm, n // tn),
        in_specs=[pl.BlockSpec((tm, d), lambda i, j: (i, 0)),
                  pl.BlockSpec((1, d), lambda i, j: (0, 0)),
                  pl.BlockSpec((d, tn), lambda i, j: (0, j))],
        out_specs=pl.BlockSpec((tm, tn), lambda i, j: (i, j)),
        out_shape=jax.ShapeDtypeStruct((m, n), f32),
        scratch_shapes=[pltpu.VMEM((tm, d), bf16)],
        compiler_params=_cparams(("parallel", "arbitrary")),
        name="inproj",
    )(h, g.reshape(1, d), wp)


def _inproj_k_kernel(x_ref, g_ref, w_ref, dst_in, k_ref, kb_ref, kiw_ref):
    del dst_in
    xn = _rms(x_ref[...], g_ref[...]).astype(bf16)
    r = jnp.dot(xn, w_ref[...], preferred_element_type=f32)
    for hh in range(A_HEADS):
        k_ref[:, hh, :] = r[:, hh * A_HEAD_DIM:(hh + 1) * A_HEAD_DIM]
    kb_ref[...] = r[:, :D_A].astype(bf16)
    kiw_ref[...] = r[:, D_A:]


def _inproj_k(h, g, wk, dst, layer, row0, rows, tm):
    d = h.shape[1]
    n = wk.shape[1]
    nt, rb = rows // tm, row0 // tm
    row = lambda i: (i, 0)
    return pl.pallas_call(
        _inproj_k_kernel,
        grid=(nt,),
        in_specs=[pl.BlockSpec((tm, d), lambda i: (rb + i, 0)),
                  pl.BlockSpec((1, d), lambda i: (0, 0)),
                  pl.BlockSpec((d, n), lambda i: (0, 0)),
                  pl.BlockSpec(memory_space=pl.ANY)],
        out_specs=[pl.BlockSpec((tm, A_HEADS, A_HEAD_DIM), lambda i: (layer * nt + i, 0, 0)),
                   pl.BlockSpec((tm, D_A), row), pl.BlockSpec((tm, n - D_A), row)],
        out_shape=[jax.ShapeDtypeStruct(dst.shape, f32),
                   jax.ShapeDtypeStruct((rows, D_A), bf16), jax.ShapeDtypeStruct((rows, n - D_A), f32)],
        input_output_aliases={3: 0},
        compiler_params=_cparams(("parallel",)),
        name="inproj_k",
    )(h, g.reshape(1, d), wk, dst)


def _inproj_v_kernel(x_ref, g_ref, w_ref, dst_in, v_ref, *vt_ref):
    del dst_in
    xn = _rms(x_ref[...], g_ref[...]).astype(bf16)
    r = jnp.dot(xn, w_ref[...], preferred_element_type=f32)
    for hh in range(A_HEADS):
        v_ref[:, hh, :] = r[:, hh * A_HEAD_DIM:(hh + 1) * A_HEAD_DIM]
    for vt in vt_ref:
        for c in range(vt.shape[0]):
            for hh in range(A_HEADS):
                blk = r[c * ATT_BLOCK:(c + 1) * ATT_BLOCK, hh * A_HEAD_DIM:(hh + 1) * A_HEAD_DIM]
                vt[c, hh] = blk.T.astype(bf16)


def _inproj_v(h, g, wv, dst, layer, row0, rows, tm, transposed):
    d = h.shape[1]
    nt, rb = rows // tm, row0 // tm
    out_specs = [pl.BlockSpec((tm, A_HEADS, A_HEAD_DIM), lambda i: (layer * nt + i, 0, 0))]
    out_shape = [jax.ShapeDtypeStruct(dst.shape, f32)]
    if transposed:
        out_specs.append(pl.BlockSpec((tm // ATT_BLOCK, A_HEADS, A_HEAD_DIM, ATT_BLOCK), lambda i: (i, 0, 0, 0)))
        out_shape.append(jax.ShapeDtypeStruct((rows // ATT_BLOCK, A_HEADS, A_HEAD_DIM, ATT_BLOCK), bf16))
    return pl.pallas_call(
        _inproj_v_kernel,
        grid=(nt,),
        in_specs=[pl.BlockSpec((tm, d), lambda i: (rb + i, 0)),
                  pl.BlockSpec((1, d), lambda i: (0, 0)),
                  pl.BlockSpec((d, D_A), lambda i: (0, 0)),
                  pl.BlockSpec(memory_space=pl.ANY)],
        out_specs=out_specs,
        out_shape=out_shape,
        input_output_aliases={3: 0},
        compiler_params=_cparams(("parallel",)),
        name="inproj_v",
    )(h, g.reshape(1, d), wv, dst)


def _idx_scores(qi, w, kib):
    acc = jnp.zeros((qi.shape[0], kib.shape[0]), f32)
    for hh in range(IDX_HEADS):
        d = lax.dot_general(qi[:, hh * IDX_DIM:(hh + 1) * IDX_DIM], kib, NT_DIMS,
                            preferred_element_type=f32)
        acc = acc + w[:, hh:hh + 1] * jnp.maximum(d, 0.0)
    return acc


def _fold_lanes(x):
    part = x[:, :LANE]
    for j in range(1, x.shape[1] // LANE):
        part = part + x[:, j * LANE:(j + 1) * LANE]
    return part


def _tri_incl(n):
    r = lax.broadcasted_iota(jnp.int32, (n, n), 0)
    c = lax.broadcasted_iota(jnp.int32, (n, n), 1)
    return jnp.where(r <= c, 1.0, 0.0).astype(bf16)


def _dsa_prompt_kernel(q_ref, k_ref, vt_ref, qi_ref, ki_ref, w_ref, _o_in, o_ref, keys_ref, bias_ref, qs_ref,
                       *acc_refs, tq, topk):
    qb = pl.program_id(1)
    nkb = qb + 1
    hd = A_HEAD_DIM

    qi = qi_ref[...].astype(bf16)
    wt = (w_ref[...] * (IDX_HEADS ** -0.5 * IDX_DIM ** -0.5)).T
    srow = lax.broadcasted_iota(jnp.int32, (tq, tq), 0)
    qcol = lax.broadcasted_iota(jnp.int32, (tq, tq), 1)
    diag_adm = (srow // CHUNK) <= (qcol // CHUNK)

    def score_block(kb):
        off = pl.multiple_of(kb * tq, tq)
        kib = ki_ref[pl.ds(off, tq), :][:, :IDX_DIM].astype(bf16)
        ds = [lax.dot_general(kib, qi[:, hh * IDX_DIM:(hh + 1) * IDX_DIM], NT_DIMS, preferred_element_type=f32)
              for hh in range(IDX_HEADS)]
        acc = jnp.zeros((tq, tq), f32)
        for hh in range(IDX_HEADS):
            acc = acc + wt[hh:hh + 1, :] * jnp.maximum(ds[hh], 0.0)
        adm = jnp.logical_or(kb < qb, diag_adm)
        keys_ref[kb] = jnp.where(adm, _sort_key(acc), jnp.int32(INT_MIN))

    def score_pair(kp, c):
        score_block(2 * kp)
        score_block(2 * kp + 1)
        return c

    lax.fori_loop(0, nkb // 2, score_pair, 0)

    @pl.when(nkb % 2 == 1)
    def _():
        score_block(nkb - 1)

    def count(cmp, c):
        cb = jnp.broadcast_to(c, (COUNT_ROWS, tq))

        def body(kb, acc):
            for r0 in range(0, tq, COUNT_ROWS):
                acc = acc + jnp.where(cmp(keys_ref[kb, r0:r0 + COUNT_ROWS, :], cb), 1.0, 0.0)
            return acc
        acc = lax.fori_loop(0, nkb, body, jnp.zeros((COUNT_ROWS, tq), f32))
        return jnp.sum(acc, axis=0, keepdims=True)

    lane = lax.broadcasted_iota(jnp.int32, (1, tq), 1)
    n_adm = ((qb * tq + lane) // CHUNK + 1) * CHUNK
    kk = jnp.minimum(topk, n_adm).astype(f32)
    thr = _kth_largest(lambda c: count(lambda k, t: k >= t, c), kk)
    n_ge = count(lambda k, t: k >= t, thr)
    has_tie = jnp.max(n_ge - kk) > 0.0

    @pl.when(jnp.logical_not(has_tie))
    def _():
        def body(kb, c):
            bias_ref[kb] = jnp.where(keys_ref[kb] >= thr, 0.0, NEG)
            return c
        lax.fori_loop(0, nkb, body, 0)

    @pl.when(has_tie)
    def _():
        need = kk - count(lambda k, t: k > t, thr)
        tri = jnp.where(qcol <= srow, 1.0, 0.0).astype(bf16)

        def body(kb, seen):
            k = keys_ref[kb]
            eq = jnp.where(k == thr, 1.0, 0.0)
            rank = seen + jnp.dot(tri, eq.astype(bf16), preferred_element_type=f32)
            take = jnp.logical_or(k > thr, jnp.logical_and(k == thr, rank <= need))
            bias_ref[kb] = jnp.where(take, 0.0, NEG)
            return seen + jnp.sum(eq, axis=0, keepdims=True)
        lax.fori_loop(0, nkb, body, jnp.zeros((1, tq), f32))

    qs_ref[...] = (q_ref[...] * (hd ** -0.5 * LOG2E)).astype(bf16)
    for h in range(A_HEADS):
        acc_refs[h][...] = jnp.zeros((hd, tq), f32)

    def att_blocks(kbs, carry):
        ms, ls = carry
        kblks = [k_ref[pl.ds(pl.multiple_of(kb * tq, tq), tq), :] for kb in kbs]
        biases = [bias_ref[kb] for kb in kbs]
        sts = [[lax.dot_general(kblk[:, h * hd:(h + 1) * hd], qs_ref[:, h * hd:(h + 1) * hd], NT_DIMS,
                                preferred_element_type=f32) for kblk in kblks] for h in range(A_HEADS)]
        new_ms, new_ls, ps, corrs = [], [], [], []
        for h in range(A_HEADS):
            st = [s + b for s, b in zip(sts[h], biases)]
            m_new = ms[h]
            for s in st:
                m_new = jnp.maximum(m_new, jnp.max(s, axis=0, keepdims=True))
            p = [jnp.exp2(s - m_new) for s in st]
            corr = jnp.exp2(ms[h] - m_new)
            l_new = corr * ls[h]
            for pp in p:
                l_new = l_new + jnp.sum(pp, axis=0, keepdims=True)
            new_ls.append(l_new)
            new_ms.append(m_new)
            ps.append([pp.astype(bf16) for pp in p])
            corrs.append(corr)
        for h in range(A_HEADS):
            upd = corrs[h] * acc_refs[h][...]
            for kb, pp in zip(kbs, ps[h]):
                upd = upd + jnp.dot(vt_ref[kb, h], pp, preferred_element_type=f32)
            acc_refs[h][...] = upd
        return tuple(new_ms), tuple(new_ls)

    init = (tuple(jnp.full((1, tq), NEG, f32) for _ in range(A_HEADS)),
            tuple(jnp.zeros((1, tq), f32) for _ in range(A_HEADS)))
    carry = lax.fori_loop(0, nkb // 2, lambda kp, c: att_blocks((2 * kp, 2 * kp + 1), c), init)
    _, ls = lax.cond(nkb % 2 == 1, lambda c: att_blocks((nkb - 1,), c), lambda c: c, carry)
    for h in range(A_HEADS):
        o_ref[:, h * hd:(h + 1) * hd] = (acc_refs[h][...] / ls[h]).T.astype(o_ref.dtype)


def _dsa_prompt(hp, kb16, vt16, kiw, obuf, batch, seq):
    tq = ATT_BLOCK
    nq = seq // tq
    topk = min(TOPK_MAX, seq // 4)
    kern = functools.partial(_dsa_prompt_kernel, tq=tq, topk=topk)
    once = pl.Buffered(1)
    return pl.pallas_call(
        kern,
        grid=(batch, nq),
        in_specs=[
            pl.BlockSpec((tq, D_A), lambda b, i: (b * nq + i, COL_Q // D_A)),
            pl.BlockSpec((seq, D_A), lambda b, i: (b, 0), pipeline_mode=once),
            pl.BlockSpec((nq, A_HEADS, A_HEAD_DIM, tq), lambda b, i: (b, 0, 0, 0), pipeline_mode=once),
            pl.BlockSpec((tq, 512), lambda b, i: (b * nq + i, COL_QI // 512)),
            pl.BlockSpec((seq, LANE), lambda b, i: (b, KIW_KI // LANE), pipeline_mode=once),
            pl.BlockSpec((tq, LANE), lambda b, i: (b * nq + i, KIW_W // LANE)),
            pl.BlockSpec(memory_space=pl.ANY),
        ],
        out_specs=pl.BlockSpec((tq, D_A), lambda b, i: (b * nq + i, 0)),
        out_shape=jax.ShapeDtypeStruct(obuf.shape, obuf.dtype),
        input_output_aliases={6: 0},
        scratch_shapes=[pltpu.VMEM((nq, tq, tq), jnp.int32), pltpu.VMEM((nq, tq, tq), f32),
                        pltpu.VMEM((tq, D_A), bf16)]
                       + [pltpu.VMEM((A_HEAD_DIM, tq), f32) for _ in range(A_HEADS)],
        compiler_params=_cparams(("parallel", "arbitrary")),
        name="dsa_prompt",
    )(hp, kb16, vt16, hp, kiw, kiw, obuf)


def _dsa_sample_kernel(q_ref, kn_ref, vn_ref, qi_ref, kin_ref, w_ref, kc_ref, vc_ref, kic_ref, _o_in, o_ref,
                       biasc_ref, biasn_ref, *, topk):
    nq = q_ref.shape[0]
    past = kic_ref.shape[0]

    def _select():
        qi = qi_ref[...].astype(bf16)
        w = w_ref[...][:, :IDX_HEADS] * (IDX_HEADS ** -0.5 * IDX_DIM ** -0.5)
        key_c = _sort_key(_idx_scores(qi, w, kic_ref[...].astype(bf16)))
        key_n = _sort_key(_idx_scores(qi, w, kin_ref[...][:, :IDX_DIM].astype(bf16)))

        def count(pred):
            cc = jnp.sum(_fold_lanes(jnp.where(pred(key_c), 1.0, 0.0)), axis=-1, keepdims=True)
            return cc + jnp.sum(jnp.where(pred(key_n), 1.0, 0.0), axis=-1, keepdims=True)

        kk = jnp.full((nq, 1), float(topk), f32)
        thr = _kth_largest(lambda c: count(lambda k: k >= c), kk)
        n_ge = count(lambda k: k >= thr)
        has_tie = jnp.max(n_ge - kk) > 0.0

        @pl.when(jnp.logical_not(has_tie))
        def _():
            biasc_ref[...] = jnp.where(key_c >= thr, 0.0, NEG)
            biasn_ref[...] = jnp.where(key_n >= thr, 0.0, NEG)

        @pl.when(has_tie)
        def _():
            need = kk - count(lambda k: k > thr)
            tri = _tri_incl(LANE)
            seen = jnp.zeros((nq, 1), f32)
            for j in range(past // LANE):
                k = key_c[:, j * LANE:(j + 1) * LANE]
                eq = jnp.where(k == thr, 1.0, 0.0)
                rank = seen + jnp.dot(eq.astype(bf16), tri, preferred_element_type=f32)
                take = jnp.logical_or(k > thr, jnp.logical_and(k == thr, rank <= need))
                biasc_ref[:, j * LANE:(j + 1) * LANE] = jnp.where(take, 0.0, NEG)
                seen = seen + jnp.sum(eq, axis=-1, keepdims=True)
            eq = jnp.where(key_n == thr, 1.0, 0.0)
            rank = seen + jnp.dot(eq.astype(bf16), _tri_incl(nq), preferred_element_type=f32)
            take = jnp.logical_or(key_n > thr, jnp.logical_and(key_n == thr, rank <= need))
            biasn_ref[...] = jnp.where(take, 0.0, NEG)

    _select()

    hd = A_HEAD_DIM
    q = (q_ref[...] * (hd ** -0.5 * LOG2E)).astype(bf16)
    bias_c = biasc_ref[...]
    bias_n = biasn_ref[...]
    scs, sns = [], []
    for h in range(A_HEADS):
        cols = slice(h * hd, (h + 1) * hd)
        k_h = kc_ref[pl.ds(h, past, stride=A_HEADS), :]
        scs.append(lax.dot_general(q[:, cols], k_h.astype(bf16), NT_DIMS, preferred_element_type=f32))
        sns.append(lax.dot_general(q[:, cols], kn_ref[:, h, :].astype(bf16), NT_DIMS, preferred_element_type=f32))
    pcs, pns, ls = [], [], []
    for h in range(A_HEADS):
        s_c = scs[h] + bias_c
        s_n = sns[h] + bias_n
        m = jnp.maximum(jnp.max(s_c, axis=-1, keepdims=True), jnp.max(s_n, axis=-1, keepdims=True))
        p_c = jnp.exp2(s_c - m)
        p_n = jnp.exp2(s_n - m)
        ls.append(jnp.sum(p_c, axis=-1, keepdims=True) + jnp.sum(p_n, axis=-1, keepdims=True))
        pcs.append(p_c.astype(bf16))
        pns.append(p_n.astype(bf16))
    for h in range(A_HEADS):
        cols = slice(h * hd, (h + 1) * hd)
        v_h = vc_ref[pl.ds(h, past, stride=A_HEADS), :]
        acc = jnp.dot(pcs[h], v_h.astype(bf16), preferred_element_type=f32)
        acc = acc + jnp.dot(pns[h], vn_ref[:, h, :].astype(bf16), preferred_element_type=f32)
        o_ref[:, cols] = (acc / ls[h]).astype(o_ref.dtype)


def _dsa_sample(hp, ks, vs, kiw, obuf, layer, row0, nb, nq, cache_k, cache_v, cache_kidx):
    depth, _, past = cache_k.shape[:3]
    topk = min(TOPK_MAX, (past + nq) // 4)
    kc = cache_k.reshape(depth, nb, past * A_HEADS, A_HEAD_DIM)
    vc = cache_v.reshape(depth, nb, past * A_HEADS, A_HEAD_DIM)
    rb = row0 // nq
    kern = functools.partial(_dsa_sample_kernel, topk=topk)
    return pl.pallas_call(
        kern,
        grid=(nb,),
        in_specs=[
            pl.BlockSpec((nq, D_A), lambda b: (rb + b, COL_Q // D_A)),
            pl.BlockSpec((nq, A_HEADS, A_HEAD_DIM), lambda b: (layer * nb + b, 0, 0)),
            pl.BlockSpec((nq, A_HEADS, A_HEAD_DIM), lambda b: (layer * nb + b, 0, 0)),
            pl.BlockSpec((nq, 512), lambda b: (rb + b, COL_QI // 512)),
            pl.BlockSpec((nq, LANE), lambda b: (b, KIW_KI // LANE)),
            pl.BlockSpec((nq, LANE), lambda b: (b, KIW_W // LANE)),
            pl.BlockSpec((None, None, past * A_HEADS, A_HEAD_DIM), lambda b: (layer, b, 0, 0)),
            pl.BlockSpec((None, None, past * A_HEADS, A_HEAD_DIM), lambda b: (layer, b, 0, 0)),
            pl.BlockSpec((None, None, past, IDX_DIM), lambda b: (layer, b, 0, 0)),
            pl.BlockSpec(memory_space=pl.ANY),
        ],
        out_specs=pl.BlockSpec((nq, D_A), lambda b: (rb + b, 0)),
        out_shape=jax.ShapeDtypeStruct(obuf.shape, obuf.dtype),
        input_output_aliases={9: 0},
        scratch_shapes=[pltpu.VMEM((nq, past), f32), pltpu.VMEM((nq, nq), f32)],
        compiler_params=_cparams(("parallel",)),
        name="dsa_sample",
    )(hp, ks, vs, hp, kiw, kiw, kc, vc, cache_kidx, obuf)


def _pool_kernel(prev_ref, u_ref, pw_ref, ps_ref, _o_in, o_ref, *, tt, first_pos, zero_first):
    tb = pl.program_id(1)
    halo = POOL_STATE + 1
    prev = prev_ref[...]
    if zero_first:
        prev = jnp.where(tb == 0, 0.0, prev)
    cur = u_ref[...]
    ext = jnp.concatenate([prev, cur], axis=0)
    pos = first_pos + tb * tt + lax.broadcasted_iota(jnp.int32, (tt, 1), 0)
    outs = []
    for gi, wdw in enumerate(POOL_WINDOWS):
        cols = slice(gi * POOL_GROUP_DIM, (gi + 1) * POOL_GROUP_DIM)
        a = ext[:, cols]
        n = 1
        while n < wdw:
            a = a[n:] + a[:-n]
            n *= 2
        s = a[halo + 1 - wdw: halo + 1 - wdw + tt]
        cnt = jnp.minimum(pos + 1, wdw).astype(f32)
        d = s / cnt - cur[:, cols]
        outs.append(jnp.dot(d.astype(bf16), pw_ref[gi].astype(bf16), preferred_element_type=f32))
    o = jnp.concatenate(outs, axis=-1) * ps_ref[...]
    o_ref[...] = o.astype(o_ref.dtype)


def _pool(hp, obuf, row0, nb, t, tt, prev, pool_w, pool_scale, first_pos):
    ntb = t // tt
    halo = POOL_STATE + 1
    rb = row0 // tt
    if prev is None:
        prev_arr = hp
        per = tt // halo
        prev_spec = pl.BlockSpec(
            (halo, D_B), lambda b, i: (jnp.maximum((row0 // halo) + (b * ntb + i) * per - 1, 0), COL_U // D_B))
    else:
        prev_arr = prev
        prev_spec = pl.BlockSpec((None, halo, D_B), lambda b, i: (b, 0, 0))
    kern = functools.partial(_pool_kernel, tt=tt, first_pos=first_pos, zero_first=prev is None)
    return pl.pallas_call(
        kern,
        grid=(nb, ntb),
        in_specs=[prev_spec,
                  pl.BlockSpec((tt, D_B), lambda b, i: (rb + b * ntb + i, COL_U // D_B)),
                  pl.BlockSpec((len(POOL_WINDOWS), POOL_GROUP_DIM, POOL_GROUP_DIM), lambda b, i: (0, 0, 0)),
                  pl.BlockSpec((1, D_B), lambda b, i: (0, 0)),
                  pl.BlockSpec(memory_space=pl.ANY)],
        out_specs=pl.BlockSpec((tt, D_B), lambda b, i: (rb + b * ntb + i, 0)),
        out_shape=jax.ShapeDtypeStruct(obuf.shape, obuf.dtype),
        input_output_aliases={4: 0},
        compiler_params=_cparams(("parallel", "arbitrary")),
        name="pool",
    )(prev_arr, hp, pool_w, pool_scale.reshape(1, D_B), obuf)


SAFE_LOG_RANGE = 80.0


def _hgrn_kernel(cq_ref, cf_ref, ci_ref, cg_ref, lb_ref, ng_ref, s0_ref, _o_in, o_ref, sn_ref,
                 st_ref, b_ref, q_ref, k_ref, osc_ref, *, tt, blk):
    tb = pl.program_id(1)
    nh = C_HEADS
    hd = C_HEAD_DIM

    @pl.when(tb == 0)
    def _init():
        for h in range(nh):
            st_ref[h] = s0_ref[h].T

    lb = lb_ref[...]
    f = jnp.maximum(lb, LB_FLOOR) + (1.0 - lb) * jax.nn.sigmoid(cf_ref[...])
    g = jnp.log(f)
    k_ref[...] = 1.0 - f
    q_ref[...] = _silu(cq_ref[...])
    r = lax.broadcasted_iota(jnp.int32, (tt, tt), 0)
    c = lax.broadcasted_iota(jnp.int32, (tt, tt), 1)
    tri = jnp.where(jnp.logical_and(r // blk == c // blk, c <= r), 1.0, 0.0).astype(bf16)
    g1 = g.astype(bf16)
    e1 = g - g1.astype(f32)
    g2 = e1.astype(bf16)
    g3 = (e1 - g2.astype(f32)).astype(bf16)
    b = (jnp.dot(tri, g1, preferred_element_type=f32) + jnp.dot(tri, g2, preferred_element_type=f32)
         + jnp.dot(tri, g3, preferred_element_type=f32))
    b_ref[...] = b
    safe = jnp.min(b) > -SAFE_LOG_RANGE

    trow = lax.broadcasted_iota(jnp.int32, (blk, 1), 0)

    def exact_chunk(ci):
        r0 = pl.multiple_of(ci * blk, blk)
        for h in range(nh):
            rows = pl.ds(r0, blk)
            cols = slice(h * hd, (h + 1) * hd)
            bb = b_ref[rows, cols]
            qh = q_ref[rows, cols]
            kh = k_ref[rows, cols]
            ih = ci_ref[rows, cols]
            bl = bb[blk - 1:blk, :]
            ke = (kh * jnp.exp(bl - bb)).astype(bf16)
            ib = ih.astype(bf16)
            st = st_ref[h]
            o = lax.dot_general((qh * jnp.exp(bb)).astype(bf16), st.astype(bf16), NT_DIMS,
                                preferred_element_type=f32)
            for s in range(blk):
                live = trow >= s
                dec = jnp.exp(jnp.where(live, bb - bb[s:s + 1, :], 0.0))
                a_col = jnp.sum(qh * kh[s:s + 1, :] * dec, axis=-1, keepdims=True)
                o = o + jnp.where(live, a_col, 0.0) * ih[s:s + 1, :]
            osc_ref[rows, cols] = o
            st_ref[h] = st * jnp.exp(bl) + lax.dot_general(ib, ke, TN_DIMS, preferred_element_type=f32)

    @pl.when(safe)
    def _():
        same_blk = r // blk == c // blk
        blk_causal = jnp.logical_and(same_blk, c <= r)
        ones_blk = jnp.where(same_blk, 1.0, 0.0).astype(bf16)
        bl_all = (jnp.dot(ones_blk, g1, preferred_element_type=f32) + jnp.dot(ones_blk, g2, preferred_element_type=f32)
                  + jnp.dot(ones_blk, g3, preferred_element_type=f32))
        for h in range(nh):
            cols = slice(h * hd, (h + 1) * hd)
            bb = b[:, cols]
            blh = bl_all[:, cols]
            qh = q_ref[:, cols]
            ib = ci_ref[:, cols].astype(bf16)
            ke = (k_ref[:, cols] * jnp.exp(blh - bb)).astype(bf16)
            qe = (qh * jnp.exp(bb - blh)).astype(bf16)
            qb = (qh * jnp.exp(bb)).astype(bf16)
            a = jnp.where(blk_causal, lax.dot_general(qe, ke, NT_DIMS, preferred_element_type=f32), 0.0)
            o_intra = jnp.dot(a.astype(bf16), ib, preferred_element_type=f32)
            incs = [lax.dot_general(ib[c0:c0 + blk], ke[c0:c0 + blk], TN_DIMS, preferred_element_type=f32)
                    for c0 in range(0, tt, blk)]
            dec = jnp.exp(blh)
            st = st_ref[h]
            o_inter = []
            for ci, c0 in enumerate(range(0, tt, blk)):
                o_inter.append(lax.dot_general(qb[c0:c0 + blk], st.astype(bf16), NT_DIMS,
                                               preferred_element_type=f32))
                st = st * dec[c0:c0 + 1, :] + incs[ci]
            st_ref[h] = st
            osc_ref[:, cols] = o_intra + jnp.concatenate(o_inter, axis=0)

    @pl.when(jnp.logical_not(safe))
    def _():
        def body(ci, c):
            exact_chunk(ci)
            return c
        lax.fori_loop(0, tt // blk, body, 0)

    outs = []
    for h in range(nh):
        cols = slice(h * hd, (h + 1) * hd)
        outs.append(_rms(osc_ref[:, cols], ng_ref[...]))
    o_ref[...] = (jnp.concatenate(outs, axis=-1) * _silu(cg_ref[...])).astype(o_ref.dtype)

    @pl.when(tb == pl.num_programs(1) - 1)
    def _fin():
        for h in range(nh):
            sn_ref[h] = st_ref[h].T


def _hgrn(hp, obuf, row0, nb, t, tt, lb, norm_g, s0):
    ntb = t // tt
    rb = row0 // tt

    def col(cidx):
        return pl.BlockSpec((tt, D_C), lambda b, i: (rb + b * ntb + i, cidx // D_C))

    st_spec = pl.BlockSpec((None, C_HEADS, C_HEAD_DIM, C_V_DIM), lambda b, i: (b, 0, 0, 0))
    kern = functools.partial(_hgrn_kernel, tt=tt, blk=HGRN_BLOCK)
    return pl.pallas_call(
        kern,
        grid=(nb, ntb),
        in_specs=[col(COL_CQ), col(COL_CF), col(COL_CI), col(COL_CG),
                  pl.BlockSpec((1, D_C), lambda b, i: (0, 0)),
                  pl.BlockSpec((1, C_V_DIM), lambda b, i: (0, 0)),
                  st_spec,
                  pl.BlockSpec(memory_space=pl.ANY)],
        out_specs=[pl.BlockSpec((tt, D_C), lambda b, i: (rb + b * ntb + i, 0)), st_spec],
        out_shape=[jax.ShapeDtypeStruct(obuf.shape, obuf.dtype),
                   jax.ShapeDtypeStruct((nb, C_HEADS, C_HEAD_DIM, C_V_DIM), f32)],
        input_output_aliases={7: 0},
        scratch_shapes=[pltpu.VMEM((C_HEADS, C_V_DIM, C_HEAD_DIM), f32),
                        pltpu.VMEM((tt, D_C), f32), pltpu.VMEM((tt, D_C), f32),
                        pltpu.VMEM((tt, D_C), f32), pltpu.VMEM((tt, D_C), f32)],
        compiler_params=_cparams(("parallel", "arbitrary")),
        name="hgrn",
    )(hp, hp, hp, hp, lb.reshape(1, D_C), norm_g.reshape(1, C_V_DIM), s0, obuf)


def _merge_kernel(x_ref, g_ref, oa_ref, ob_ref, oc_ref, wga_ref, wgb_ref, wgc_ref,
                  pa_ref, pb_ref, pc_ref, wo_ref, o_ref, xn_ref):
    j = pl.program_id(1)

    @pl.when(j == 0)
    def _():
        x = x_ref[...]
        xn_ref[...] = _rms(x, g_ref[...]).astype(bf16)
        o_ref[...] = x

    xn = xn_ref[...]

    def branch(wg_ref, o_ref_, p_ref):
        gate = jax.nn.sigmoid(jnp.dot(xn, wg_ref[...], preferred_element_type=f32))
        return gate * jnp.dot(o_ref_[...], p_ref[...], preferred_element_type=f32)

    merged = branch(wga_ref, oa_ref, pa_ref) + branch(wgb_ref, ob_ref, pb_ref) + branch(wgc_ref, oc_ref, pc_ref)
    o_ref[...] += jnp.dot(merged.astype(bf16), wo_ref[...], preferred_element_type=f32)


def _merge(h, g, oa, ob, oc, wg, pa, pb, pc, wo, tm, tj):
    m, d = h.shape
    nj = d // tj
    row = lambda i, j: (i, 0)
    return pl.pallas_call(
        _merge_kernel,
        grid=(m // tm, nj),
        in_specs=[pl.BlockSpec((tm, d), row),
                  pl.BlockSpec((1, d), lambda i, j: (0, 0)),
                  pl.BlockSpec((tm, D_A), row),
                  pl.BlockSpec((tm, D_B), row),
                  pl.BlockSpec((tm, D_C), row),
                  pl.BlockSpec((d, tj), lambda i, j: (0, j)),
                  pl.BlockSpec((d, tj), lambda i, j: (0, nj + j)),
                  pl.BlockSpec((d, tj), lambda i, j: (0, 2 * nj + j)),
                  pl.BlockSpec((D_A, tj), lambda i, j: (0, j)),
                  pl.BlockSpec((D_B, tj), lambda i, j: (0, j)),
                  pl.BlockSpec((D_C, tj), lambda i, j: (0, j)),
                  pl.BlockSpec((tj, d), lambda i, j: (j, 0))],
        out_specs=pl.BlockSpec((tm, d), row),
        out_shape=jax.ShapeDtypeStruct((m, d), f32),
        scratch_shapes=[pltpu.VMEM((tm, d), bf16)],
        compiler_params=_cparams(("parallel", "arbitrary")),
        name="merge",
    )(h, g.reshape(1, d), oa, ob, oc, wg, wg, wg, pa, pb, pc, wo)


def _router_kernel(x_ref, g_ref, r_ref, o_ref):
    xn = _rms(x_ref[...], g_ref[...]).astype(bf16)
    logits = jnp.dot(xn, r_ref[...], preferred_element_type=f32)
    lane = lax.broadcasted_iota(jnp.int32, logits.shape, 1)
    logits = jnp.where(lane < N_EXPERTS, logits, -jnp.inf)
    m1 = jnp.max(logits, axis=-1, keepdims=True)
    i1 = jnp.min(jnp.where(logits == m1, lane, LANE), axis=-1, keepdims=True)
    rest = jnp.where(lane == i1, -jnp.inf, logits)
    m2 = jnp.max(rest, axis=-1, keepdims=True)
    i2 = jnp.min(jnp.where(rest == m2, lane, LANE), axis=-1, keepdims=True)
    e2 = jnp.exp(m2 - m1)
    g1 = 1.0 / (1.0 + e2)
    g2 = e2 / (1.0 + e2)
    route = jnp.where(lane == 0, i1.astype(f32), jnp.where(lane == 1, i2.astype(f32),
                      jnp.where(lane == 2, g1, jnp.where(lane == 3, g2, 0.0))))
    o_ref[...] = route[:, :N_EXPERTS]


def _router(h, g, router, tm):
    m, d = h.shape
    rp = jnp.pad(router, ((0, 0), (0, LANE - N_EXPERTS))).astype(bf16)
    return pl.pallas_call(
        _router_kernel,
        grid=(m // tm,),
        in_specs=[pl.BlockSpec((tm, d), lambda i: (i, 0)),
                  pl.BlockSpec((1, d), lambda i: (0, 0)),
                  pl.BlockSpec((d, LANE), lambda i: (0, 0))],
        out_specs=pl.BlockSpec((tm, N_EXPERTS), lambda i: (i, 0)),
        out_shape=jax.ShapeDtypeStruct((m, N_EXPERTS), f32),
        compiler_params=_cparams(("parallel",)),
        name="router",
    )(h, g.reshape(1, d), rp)


def _ffn_kernel(x_ref, g_ref, wg_ref, wu_ref, wd_ref, o_ref, xn_ref):
    @pl.when(pl.program_id(1) == 0)
    def _():
        x = x_ref[...]
        xn_ref[...] = _rms(x, g_ref[...]).astype(bf16)
        o_ref[...] = x

    xn = xn_ref[...]
    a = jnp.dot(xn, wg_ref[...], preferred_element_type=f32)
    u = jnp.dot(xn, wu_ref[...], preferred_element_type=f32)
    act = (_silu(a) * u).astype(bf16)
    o_ref[...] += jnp.dot(act, wd_ref[...], preferred_element_type=f32)


def _ffn(h, g, wg, wu, wd, tm, tf):
    m, d = h.shape
    ff = wg.shape[1]
    return pl.pallas_call(
        _ffn_kernel,
        grid=(m // tm, ff // tf),
        in_specs=[pl.BlockSpec((tm, d), lambda i, f: (i, 0)),
                  pl.BlockSpec((1, d), lambda i, f: (0, 0)),
                  pl.BlockSpec((d, tf), lambda i, f: (0, f)),
                  pl.BlockSpec((d, tf), lambda i, f: (0, f)),
                  pl.BlockSpec((tf, d), lambda i, f: (f, 0))],
        out_specs=pl.BlockSpec((tm, d), lambda i, f: (i, 0)),
        out_shape=jax.ShapeDtypeStruct((m, d), f32),
        scratch_shapes=[pltpu.VMEM((tm, d), bf16)],
        compiler_params=_cparams(("parallel", "arbitrary")),
        name="ffn",
    )(h, g.reshape(1, d), wg, wu, wd)


MOE_ROWS = 1024
MOE_SUB = 256
DMA_UNROLL = 8


def _moe_plan(route, m):
    n = 2 * m
    n_groups = -(-n // MOE_ROWS) + N_EXPERTS
    ef = route[:, 0:2].astype(jnp.int32).T.reshape(n)
    row_dst = jnp.argsort(ef, stable=True).astype(jnp.int32)
    row_tok = jnp.where(row_dst >= m, row_dst - m, row_dst)
    counts = jnp.sum((ef[:, None] == jnp.arange(N_EXPERTS, dtype=jnp.int32)[None, :]).astype(jnp.int32), axis=0)
    first = jnp.cumsum(counts) - counts
    per_e = (counts + MOE_ROWS - 1) // MOE_ROWS
    size = (counts + jnp.maximum(per_e, 1) - 1) // jnp.maximum(per_e, 1)
    size = (size + DMA_UNROLL - 1) // DMA_UNROLL * DMA_UNROLL
    ends = jnp.cumsum(per_e)
    gi = jnp.arange(n_groups, dtype=jnp.int32)
    ge = jnp.minimum(jnp.sum((gi[:, None] >= ends[None, :]).astype(jnp.int32), axis=1), N_EXPERTS - 1)
    j = gi - (ends - per_e)[ge]
    g_start = first[ge] + j * size[ge]
    g_cnt = jnp.clip(counts[ge] - j * size[ge], 0, size[ge])
    g_cnt = jnp.where(gi < ends[-1], g_cnt, 0)
    last_e = ge[jnp.maximum(ends[-1] - 1, 0)]
    ge = jnp.where(gi < ends[-1], ge, last_e)
    return ge, g_start.astype(jnp.int32), g_cnt.astype(jnp.int32), row_tok, row_dst


def _moe_kernel(ge_ref, gs_ref, gc_ref, tok_ref, dst_ref, h_hbm, g_ref, wg_ref, wu_ref, wd_ref, y_hbm,
                xs_ref, acc_ref, stage_ref, spare_ref, gsem, ssem, *, nf):
    s = pl.program_id(0)
    fi = pl.program_id(1)
    cnt = gc_ref[s]
    start = gs_ref[s]
    n_rows = tok_ref.shape[0]
    nsub = (cnt + MOE_SUB - 1) // MOE_SUB
    chunk = stage_ref.shape[0] // nf

    def gather_chunk(first_row, base):
        for u in range(chunk):
            tok = tok_ref[jnp.minimum(first_row + base + u, n_rows - 1)]
            pltpu.make_async_copy(h_hbm.at[pl.ds(tok, 1)], stage_ref.at[pl.ds(base + u, 1)], gsem).start()

    def row_tiles(js):
        wg = wg_ref[...].astype(bf16)
        wu = wu_ref[...].astype(bf16)
        wd = wd_ref[...].astype(bf16)
        for j in js:
            rows = pl.ds(pl.multiple_of(j * MOE_SUB, MOE_SUB), MOE_SUB)
            x = xs_ref[rows, :]
            a = jnp.dot(x, wg, preferred_element_type=f32)
            u = jnp.dot(x, wu, preferred_element_type=f32)
            act = (_silu(a) * u).astype(bf16)
            acc_ref[rows, :] += jnp.dot(act, wd, preferred_element_type=f32)

    def row_tile(j):
        row_tiles((j,))

    @pl.when(cnt > 0)
    def _active():
        @pl.when(fi == 0)
        def _stage_in():
            @pl.when(s == 0)
            def _():
                def issue(c, carry):
                    gather_chunk(start, c * chunk)
                    return carry
                lax.fori_loop(0, nf, issue, 0)

            def wait_chunk(c, carry):
                pltpu.make_async_copy(h_hbm.at[pl.ds(0, chunk)], stage_ref.at[pl.ds(0, chunk)], gsem).wait()
                return carry
            lax.fori_loop(0, nf, wait_chunk, 0)

            def norm(j, c):
                rows = pl.ds(pl.multiple_of(j * MOE_SUB, MOE_SUB), MOE_SUB)
                xs_ref[rows, :] = _rms(stage_ref[rows, :], g_ref[...]).astype(bf16)
                acc_ref[rows, :] = jnp.zeros((MOE_SUB, acc_ref.shape[1]), f32)
                return c
            lax.fori_loop(0, nsub, norm, 0)

        nxt = jnp.minimum(s + 1, pl.num_programs(0) - 1)
        prefetch = jnp.logical_and(s + 1 < pl.num_programs(0), gc_ref[nxt] > 0)

        @pl.when(prefetch)
        def _():
            gather_chunk(gs_ref[nxt], fi * chunk)
            row_tile(0)

        @pl.when(jnp.logical_not(prefetch))
        def _():
            row_tile(0)

        def body(jp, c):
            row_tiles((1 + 2 * jp, 2 + 2 * jp))
            return c
        lax.fori_loop(0, (nsub - 1) // 2, body, 0)

        @pl.when((nsub - 1) % 2 == 1)
        def _():
            row_tile(nsub - 1)

        @pl.when(fi == nf - 1)
        def _scatter():
            def issue_one(r):
                pltpu.make_async_copy(acc_ref.at[pl.ds(r, 1)], y_hbm.at[pl.ds(dst_ref[start + r], 1)], ssem).start()

            def issue(r8, c):
                for u in range(DMA_UNROLL):
                    issue_one(r8 * DMA_UNROLL + u)
                return c
            n8 = cnt // DMA_UNROLL
            lax.fori_loop(0, n8, issue, 0)

            def issue_tail(r, c):
                issue_one(r)
                return c
            lax.fori_loop(n8 * DMA_UNROLL, cnt, issue_tail, 0)

            def issue_spare(r, c):
                pltpu.make_async_copy(acc_ref.at[pl.ds(r, 1)], spare_ref.at[pl.ds(r - cnt, 1)], ssem).start()
                return c
            lax.fori_loop(cnt, nsub * MOE_SUB, issue_spare, 0)

            def wait_tile(j, c):
                pltpu.make_async_copy(acc_ref.at[pl.ds(0, MOE_SUB)], y_hbm.at[pl.ds(0, MOE_SUB)], ssem).wait()
                return c
            lax.fori_loop(0, nsub, wait_tile, 0)


def _moe(h, g, route, wg, wu, wd, tf):
    m, d = h.shape
    ne, _, ff = wg.shape
    nf = ff // tf
    chunk = -(-MOE_ROWS // nf)
    chunk = -(-chunk // DMA_UNROLL) * DMA_UNROLL
    ge, gs, gc, row_tok, row_dst = _moe_plan(route, m)
    n_groups = ge.shape[0]

    def wcol(s, f, ge_r, gs_r, gc_r, tok_r, dst_r):
        return (ge_r[s], 0, jnp.where(gc_r[s] > 0, f, nf - 1))

    def wrow(s, f, ge_r, gs_r, gc_r, tok_r, dst_r):
        return (ge_r[s], jnp.where(gc_r[s] > 0, f, nf - 1), 0)

    grid_spec = pltpu.PrefetchScalarGridSpec(
        num_scalar_prefetch=5,
        grid=(n_groups, nf),
        in_specs=[pl.BlockSpec(memory_space=pl.ANY),
                  pl.BlockSpec((1, d), lambda s, f, *_: (0, 0)),
                  pl.BlockSpec((None, d, tf), wcol),
                  pl.BlockSpec((None, d, tf), wcol),
                  pl.BlockSpec((None, tf, d), wrow)],
        out_specs=pl.BlockSpec(memory_space=pl.ANY),
        scratch_shapes=[pltpu.VMEM((MOE_ROWS, d), bf16), pltpu.VMEM((MOE_ROWS, d), f32),
                        pltpu.VMEM((chunk * nf, d), f32), pltpu.VMEM((MOE_SUB, d), f32),
                        pltpu.SemaphoreType.DMA(()), pltpu.SemaphoreType.DMA(())],
    )
    return pl.pallas_call(
        functools.partial(_moe_kernel, nf=nf),
        grid_spec=grid_spec,
        out_shape=jax.ShapeDtypeStruct((2 * m, d), f32),
        compiler_params=_cparams(("arbitrary", "arbitrary")),
        name="moe",
    )(ge, gs, gc, row_tok, row_dst, h, g.reshape(1, d), wg, wu, wd)


def _combine_kernel(x_ref, r_ref, y1_ref, y2_ref, o_ref):
    r = r_ref[...]
    o_ref[...] = x_ref[...] + (r[:, 2:3] * y1_ref[...] + r[:, 3:4] * y2_ref[...])


def _combine(h, route, y, tm):
    m, d = h.shape
    nb = m // tm
    return pl.pallas_call(
        _combine_kernel,
        grid=(nb,),
        in_specs=[pl.BlockSpec((tm, d), lambda i: (i, 0)),
                  pl.BlockSpec((tm, N_EXPERTS), lambda i: (i, 0)),
                  pl.BlockSpec((tm, d), lambda i: (i, 0)),
                  pl.BlockSpec((tm, d), lambda i: (nb + i, 0))],
        out_specs=pl.BlockSpec((tm, d), lambda i: (i, 0)),
        out_shape=jax.ShapeDtypeStruct((m, d), f32),
        compiler_params=_cparams(("parallel",)),
        name="moe_combine",
    )(h, route, y, y)


def _combine_norm_kernel(x_ref, r_ref, y1_ref, y2_ref, g_ref, op_ref, os_ref, *, n_p):
    r = r_ref[...]
    v = _rms(x_ref[...] + (r[:, 2:3] * y1_ref[...] + r[:, 3:4] * y2_ref[...]), g_ref[...])

    @pl.when(pl.program_id(0) < n_p)
    def _():
        op_ref[...] = v

    @pl.when(pl.program_id(0) >= n_p)
    def _():
        os_ref[...] = v


def _combine_norm(h, route, y, g, mp, tm):
    m, d = h.shape
    nb = m // tm
    n_p = mp // tm
    return pl.pallas_call(
        functools.partial(_combine_norm_kernel, n_p=n_p),
        grid=(nb,),
        in_specs=[pl.BlockSpec((tm, d), lambda i: (i, 0)),
                  pl.BlockSpec((tm, N_EXPERTS), lambda i: (i, 0)),
                  pl.BlockSpec((tm, d), lambda i: (i, 0)),
                  pl.BlockSpec((tm, d), lambda i: (nb + i, 0)),
                  pl.BlockSpec((1, d), lambda i: (0, 0))],
        out_specs=[pl.BlockSpec((tm, d), lambda i: (jnp.minimum(i, n_p - 1), 0)),
                   pl.BlockSpec((tm, d), lambda i: (jnp.maximum(i - n_p, 0), 0))],
        out_shape=[jax.ShapeDtypeStruct((mp, d), f32), jax.ShapeDtypeStruct((m - mp, d), f32)],
        compiler_params=_cparams(("arbitrary",)),
        name="moe_combine_norm",
    )(h, route, y, y, g.reshape(1, d))


def _final_norm_kernel(x_ref, g_ref, o_ref):
    o_ref[...] = _rms(x_ref[...], g_ref[...])


def _final_norm(h, g, row0, rows, tm):
    d = h.shape[1]
    rb = row0 // tm
    return pl.pallas_call(
        _final_norm_kernel,
        grid=(rows // tm,),
        in_specs=[pl.BlockSpec((tm, d), lambda i: (rb + i, 0)),
                  pl.BlockSpec((1, d), lambda i: (0, 0))],
        out_specs=pl.BlockSpec((tm, d), lambda i: (i, 0)),
        out_shape=jax.ShapeDtypeStruct((rows, d), f32),
        compiler_params=_cparams(("parallel",)),
        name="final_norm",
    )(h, g.reshape(1, d))


def _cast_kernel(w_ref, o_ref):
    o_ref[...] = w_ref[...].astype(bf16)


def _cast_bf16(w, layer, tr):
    _, r, c = w.shape
    return pl.pallas_call(
        _cast_kernel,
        grid=(r // tr,),
        in_specs=[pl.BlockSpec((None, tr, c), lambda i: (layer, i, 0))],
        out_specs=pl.BlockSpec((tr, c), lambda i: (i, 0)),
        out_shape=jax.ShapeDtypeStruct((r, c), bf16),
        compiler_params=_cparams(("parallel",)),
        name="cast_bf16",
    )(w)


def _pack_kernel(w_ref, wp_ref, wk_ref, wv_ref, wg_ref):
    o = np.cumsum((0,) + IN_SIZES)
    w = w_ref[...]
    tc = w.shape[1]
    cut = lambda a, b: w[int(o[a]):int(o[b]), :]
    pad = lambda a: jnp.concatenate([a, jnp.zeros((LANE - a.shape[0], tc), f32)], axis=0)
    t = lambda a: a.T.astype(bf16)
    wp_ref[...] = jnp.concatenate([t(cut(0, 1)), t(cut(3, 4)), t(cut(6, 11))], axis=1)
    wk_ref[...] = jnp.concatenate([t(cut(1, 2)), t(pad(cut(4, 5))), t(pad(cut(5, 6)))], axis=1)
    wv_ref[...] = t(cut(2, 3))
    wg_ref[...] = t(cut(11, 12))


def _pack_in_proj(wt, layer, tc):
    _, n, d = wt.shape
    widths = (N_PACK, D_A + 2 * LANE, D_A, N_BRANCH * D_MODEL)
    return pl.pallas_call(
        _pack_kernel,
        grid=(d // tc,),
        in_specs=[pl.BlockSpec((None, n, tc), lambda i: (layer, 0, i))],
        out_specs=[pl.BlockSpec((tc, c), lambda i: (i, 0)) for c in widths],
        out_shape=[jax.ShapeDtypeStruct((d, c), bf16) for c in widths],
        compiler_params=_cparams(("parallel",)),
        name="pack_w_in",
    )(wt)


def kernel(x_prompt, x_sample, cache_k, cache_v, cache_kidx, cache_pool, state_hgrn, norm_mix_g, norm_ffn_g, final_norm_g, w_in, w_proj_a, w_proj_b, w_proj_c, w_out, pool_w, pool_scale, hgrn_lb_logits, hgrn_norm_g, ffn_w_gate, ffn_w_up, ffn_w_down, moe_router, moe_w_gate, moe_w_up, moe_w_down):
    batch, seq, d = x_prompt.shape
    nb, nq, _ = x_sample.shape
    depth = w_in.shape[0]
    past = cache_k.shape[2]
    mp = batch * seq
    ms = nb * nq
    m = mp + ms

    lb_soft = jax.nn.softmax(hgrn_lb_logits.astype(f32), axis=0)
    lb_all = jnp.cumsum(lb_soft, axis=0) - lb_soft[0:1]

    h = jnp.concatenate([x_prompt.reshape(mp, d), x_sample.reshape(ms, d)], axis=0)
    s0_prompt = jnp.zeros((batch, C_HEADS, C_HEAD_DIM, C_V_DIM), f32)
    tm = 768 if m % 768 == 0 else 256
    tmh = tm // 2

    kp = jnp.zeros((depth * mp, A_HEADS, A_HEAD_DIM), f32)
    vp = jnp.zeros((depth * mp, A_HEADS, A_HEAD_DIM), f32)
    ks = jnp.zeros((depth * ms, A_HEADS, A_HEAD_DIM), f32)
    vs = jnp.zeros((depth * ms, A_HEADS, A_HEAD_DIM), f32)
    w_in_t = jnp.swapaxes(w_in, 1, 2)
    tkv = ATT_BLOCK
    tkp = 1024 if mp % 1024 == 0 else ATT_BLOCK
    outs = [[] for _ in range(6)]
    for l in range(depth):
        wp, wk, wv, wgates = _pack_in_proj(w_in_t, l, LANE)
        hp = _inproj(h, norm_mix_g[l], wp, tm, 1024)
        kp, kb16, kiw_p = _inproj_k(h, norm_mix_g[l], wk, kp, l, 0, mp, tkp)
        ks, _, kiw_s = _inproj_k(h, norm_mix_g[l], wk, ks, l, mp, ms, tkv)
        vp, vt16 = _inproj_v(h, norm_mix_g[l], wv, vp, l, 0, mp, tkp, True)
        vs, = _inproj_v(h, norm_mix_g[l], wv, vs, l, mp, ms, tkv, False)

        oa = _dsa_prompt(hp, kb16, vt16, kiw_p, jnp.zeros((m, D_A), bf16), batch, seq)
        oa = _dsa_sample(hp, ks, vs, kiw_s, oa, l, mp, nb, nq, cache_k, cache_v, cache_kidx)
        ob = _pool(hp, jnp.zeros((m, D_B), bf16), 0, batch, seq, 256, None, pool_w[l], pool_scale[l], 0)
        prev = jnp.pad(cache_pool[l], ((0, 0), (1, 0), (0, 0)))
        ob = _pool(hp, ob, mp, nb, nq, nq, prev, pool_w[l], pool_scale[l], past)
        oc, st_p = _hgrn(hp, jnp.zeros((m, D_C), bf16), 0, batch, seq, 256, lb_all[l], hgrn_norm_g[l], s0_prompt)
        oc, st_s = _hgrn(hp, oc, mp, nb, nq, nq, lb_all[l], hgrn_norm_g[l], state_hgrn[l])

        h = _merge(h, norm_mix_g[l], oa, ob, oc, wgates, _cast_bf16(w_proj_a, l, 256), _cast_bf16(w_proj_b, l, 256),
                   _cast_bf16(w_proj_c, l, 256), _cast_bf16(w_out, l, 256), tmh, 512)

        uu = hp[:, COL_U:COL_U + D_B]
        outs[0].append(kiw_p[:, KIW_KI:KIW_KI + IDX_DIM].reshape(batch, seq, IDX_DIM))
        outs[1].append(uu[:mp].reshape(batch, seq, D_B)[:, -POOL_STATE:])
        outs[2].append(st_p)
        outs[3].append(kiw_s[:, KIW_KI:KIW_KI + IDX_DIM].reshape(nb, nq, IDX_DIM))
        u_ext = jnp.concatenate([cache_pool[l], uu[mp:].reshape(nb, nq, D_B)], axis=1)
        outs[4].append(u_ext[:, -POOL_STATE:])
        outs[5].append(st_s)

        j = l // 2
        if l % 2 == 0:
            h = _ffn(h, norm_ffn_g[l], _cast_bf16(ffn_w_gate, j, 256), _cast_bf16(ffn_w_up, j, 256),
                     _cast_bf16(ffn_w_down, j, 512), tm, 512)
        else:
            route = _router(h, norm_ffn_g[l], moe_router[j], tm)
            y = _moe(h, norm_ffn_g[l], route, moe_w_gate[j], moe_w_up[j], moe_w_down[j], 512)
            if l == depth - 1:
                y_prompt, y_sample = _combine_norm(h, route, y, final_norm_g, mp, tkv)
                h = None
            else:
                h = _combine(h, route, y, tmh)

    if h is not None:
        y_prompt = _final_norm(h, final_norm_g, 0, mp, 256)
        y_sample = _final_norm(h, final_norm_g, mp, ms, 256)
    y_prompt = y_prompt.reshape(batch, seq, d)
    y_sample = y_sample.reshape(nb, nq, d)
    kidx_p, pool_p, st_p, kidx_s, pool_s, st_s = (jnp.stack(o) for o in outs)
    return (y_prompt, y_sample,
            kp.reshape(depth, batch, seq, A_HEADS, A_HEAD_DIM), vp.reshape(depth, batch, seq, A_HEADS, A_HEAD_DIM),
            kidx_p, pool_p, st_p,
            ks.reshape(depth, nb, nq, A_HEADS, A_HEAD_DIM), vs.reshape(depth, nb, nq, A_HEADS, A_HEAD_DIM),
            kidx_s, pool_s, st_s)
```

```python
import functools

import numpy as np
import jax
import jax.numpy as jnp
from jax import lax
from jax.experimental import pallas as pl
from jax.experimental.pallas import tpu as pltpu

f32 = jnp.float32
bf16 = jnp.bfloat16

D_MODEL = 2048
CHUNK = 64
A_HEADS = 8
A_HEAD_DIM = 128
D_A = A_HEADS * A_HEAD_DIM
IDX_HEADS = 8
IDX_DIM = 64
TOPK_MAX = 256
POOL_WINDOWS = (2, 4, 8, 16)
POOL_GROUP_DIM = 128
D_B = len(POOL_WINDOWS) * POOL_GROUP_DIM
POOL_STATE = max(POOL_WINDOWS) - 1
C_HEADS = 4
C_HEAD_DIM = 128
C_V_DIM = 128
D_C = C_HEADS * C_HEAD_DIM
HGRN_BLOCK = 16
N_BRANCH = 3
N_EXPERTS = 8
IN_SIZES = (D_A, D_A, D_A, IDX_HEADS * IDX_DIM, IDX_DIM, IDX_HEADS,
            D_B, D_C, D_C, C_HEADS * C_V_DIM, C_HEADS * C_V_DIM, N_BRANCH * D_MODEL)
EPS = 1e-6
NEG = -1e30
LB_FLOOR = 1e-30
INT_MIN = -2 ** 31

LOG2E = 1.4426950408889634
LANE = 128
COUNT_ROWS = 64
ATT_BLOCK = 256
ATT_HEAD_GROUP = 8
COL_Q = 0
COL_QI = D_A
COL_U = COL_QI + 512
COL_CQ = COL_U + 512
COL_CF = COL_CQ + 512
COL_CI = COL_CF + 512
COL_CG = COL_CI + 512
N_PACK = COL_CG + 512
KIW_KI, KIW_W = 0, LANE

VMEM_LIMIT = 56 * 1024 * 1024

NT_DIMS = (((1,), (1,)), ((), ()))
TN_DIMS = (((0,), (0,)), ((), ()))


def _cparams(sem):
    return pltpu.CompilerParams(dimension_semantics=sem, vmem_limit_bytes=VMEM_LIMIT)


def _rms(x, g):
    ms = jnp.mean(x * x, axis=-1, keepdims=True)
    return x * lax.rsqrt(ms + EPS) * g


def _silu(x):
    return x * jax.nn.sigmoid(x)


def _sort_key(x):
    bits = pltpu.bitcast(x, jnp.int32)
    return bits ^ ((bits >> 31) & jnp.int32(0x7FFFFFFF))


def _kth_largest(count_ge, kk):
    zero = jnp.zeros(kk.shape, jnp.int32)
    t0 = jnp.where(count_ge(zero) >= kk, zero, jnp.int32(INT_MIN))

    def bit_body(i, t):
        c = t | lax.shift_left(jnp.int32(1), 30 - i)
        return jnp.where(count_ge(c) >= kk, c, t)

    return lax.fori_loop(0, 31, bit_body, t0)


def _inproj_kernel(x_ref, g_ref, w_ref, o_ref, xn_ref):
    @pl.when(pl.program_id(1) == 0)
    def _():
        xn_ref[...] = _rms(x_ref[...], g_ref[...]).astype(bf16)

    o_ref[...] = jnp.dot(xn_ref[...], w_ref[...], preferred_element_type=f32)


def _inproj(h, g, wp, tm, tn):
    m, d = h.shape
    n = wp.shape[1]
    return pl.pallas_call(
        _inproj_kernel,
        grid=(m // tm, n // tn),
        in_specs=[pl.BlockSpec((tm, d), lambda i, j: (i, 0)),
                  pl.BlockSpec((1, d), lambda i, j: (0, 0)),
                  pl.BlockSpec((d, tn), lambda i, j: (0, j))],
        out_specs=pl.BlockSpec((tm, tn), lambda i, j: (i, j)),
        out_shape=jax.ShapeDtypeStruct((m, n), f32),
        scratch_shapes=[pltpu.VMEM((tm, d), bf16)],
        compiler_params=_cparams(("parallel", "arbitrary")),
        name="inproj",
    )(h, g.reshape(1, d), wp)


def _inproj_k_kernel(x_ref, g_ref, w_ref, dst_in, k_ref, kb_ref, kiw_ref):
    del dst_in
    xn = _rms(x_ref[...], g_ref[...]).astype(bf16)
    r = jnp.dot(xn, w_ref[...], preferred_element_type=f32)
    for hh in range(A_HEADS):
        k_ref[:, hh, :] = r[:, hh * A_HEAD_DIM:(hh + 1) * A_HEAD_DIM]
    kb_ref[...] = r[:, :D_A].astype(bf16)
    kiw_ref[...] = r[:, D_A:]


def _inproj_k(h, g, wk, dst, layer, row0, rows, tm):
    d = h.shape[1]
    n = wk.shape[1]
    nt, rb = rows // tm, row0 // tm
    row = lambda i: (i, 0)
    return pl.pallas_call(
        _inproj_k_kernel,
        grid=(nt,),
        in_specs=[pl.BlockSpec((tm, d), lambda i: (rb + i, 0)),
                  pl.BlockSpec((1, d), lambda i: (0, 0)),
                  pl.BlockSpec((d, n), lambda i: (0, 0)),
                  pl.BlockSpec(memory_space=pl.ANY)],
        out_specs=[pl.BlockSpec((tm, A_HEADS, A_HEAD_DIM), lambda i: (layer * nt + i, 0, 0)),
                   pl.BlockSpec((tm, D_A), row), pl.BlockSpec((tm, n - D_A), row)],
        out_shape=[jax.ShapeDtypeStruct(dst.shape, f32),
                   jax.ShapeDtypeStruct((rows, D_A), bf16), jax.ShapeDtypeStruct((rows, n - D_A), f32)],
        input_output_aliases={3: 0},
        compiler_params=_cparams(("parallel",)),
        name="inproj_k",
    )(h, g.reshape(1, d), wk, dst)


def _inproj_v_kernel(x_ref, g_ref, w_ref, dst_in, v_ref, *vt_ref):
    del dst_in
    xn = _rms(x_ref[...], g_ref[...]).astype(bf16)
    r = jnp.dot(xn, w_ref[...], preferred_element_type=f32)
    for hh in range(A_HEADS):
        v_ref[:, hh, :] = r[:, hh * A_HEAD_DIM:(hh + 1) * A_HEAD_DIM]
    for vt in vt_ref:
        for c in range(vt.shape[0]):
            for hh in range(A_HEADS):
                blk = r[c * ATT_BLOCK:(c + 1) * ATT_BLOCK, hh * A_HEAD_DIM:(hh + 1) * A_HEAD_DIM]
                vt[c, hh] = blk.T.astype(bf16)


def _inproj_v(h, g, wv, dst, layer, row0, rows, tm, transposed):
    d = h.shape[1]
    nt, rb = rows // tm, row0 // tm
    out_specs = [pl.BlockSpec((tm, A_HEADS, A_HEAD_DIM), lambda i: (layer * nt + i, 0, 0))]
    out_shape = [jax.ShapeDtypeStruct(dst.shape, f32)]
    if transposed:
        out_specs.append(pl.BlockSpec((tm // ATT_BLOCK, A_HEADS, A_HEAD_DIM, ATT_BLOCK), lambda i: (i, 0, 0, 0)))
        out_shape.append(jax.ShapeDtypeStruct((rows // ATT_BLOCK, A_HEADS, A_HEAD_DIM, ATT_BLOCK), bf16))
    return pl.pallas_call(
        _inproj_v_kernel,
        grid=(nt,),
        in_specs=[pl.BlockSpec((tm, d), lambda i: (rb + i, 0)),
                  pl.BlockSpec((1, d), lambda i: (0, 0)),
                  pl.BlockSpec((d, D_A), lambda i: (0, 0)),
                  pl.BlockSpec(memory_space=pl.ANY)],
        out_specs=out_specs,
        out_shape=out_shape,
        input_output_aliases={3: 0},
        compiler_params=_cparams(("parallel",)),
        name="inproj_v",
    )(h, g.reshape(1, d), wv, dst)


def _idx_scores(qi, w, kib):
    acc = jnp.zeros((qi.shape[0], kib.shape[0]), f32)
    for hh in range(IDX_HEADS):
        d = lax.dot_general(qi[:, hh * IDX_DIM:(hh + 1) * IDX_DIM], kib, NT_DIMS,
                            preferred_element_type=f32)
        acc = acc + w[:, hh:hh + 1] * jnp.maximum(d, 0.0)
    return acc


def _fold_lanes(x):
    part = x[:, :LANE]
    for j in range(1, x.shape[1] // LANE):
        part = part + x[:, j * LANE:(j + 1) * LANE]
    return part


def _tri_incl(n):
    r = lax.broadcasted_iota(jnp.int32, (n, n), 0)
    c = lax.broadcasted_iota(jnp.int32, (n, n), 1)
    return jnp.where(r <= c, 1.0, 0.0).astype(bf16)


def _dsa_prompt_kernel(q_ref, k_ref, vt_ref, qi_ref, ki_ref, w_ref, _o_in, o_ref, keys_ref, bias_ref, qs_ref,
                       *acc_refs, tq, topk):
    qb = pl.program_id(1)
    nkb = qb + 1
    hd = A_HEAD_DIM

    qi = qi_ref[...].astype(bf16)
    wt = (w_ref[...] * (IDX_HEADS ** -0.5 * IDX_DIM ** -0.5)).T
    srow = lax.broadcasted_iota(jnp.int32, (tq, tq), 0)
    qcol = lax.broadcasted_iota(jnp.int32, (tq, tq), 1)
    diag_adm = (srow // CHUNK) <= (qcol // CHUNK)

    def score_block(kb):
        off = pl.multiple_of(kb * tq, tq)
        kib = ki_ref[pl.ds(off, tq), :][:, :IDX_DIM].astype(bf16)
        ds = [lax.dot_general(kib, qi[:, hh * IDX_DIM:(hh + 1) * IDX_DIM], NT_DIMS, preferred_element_type=f32)
              for hh in range(IDX_HEADS)]
        acc = jnp.zeros((tq, tq), f32)
        for hh in range(IDX_HEADS):
            acc = acc + wt[hh:hh + 1, :] * jnp.maximum(ds[hh], 0.0)
        adm = jnp.logical_or(kb < qb, diag_adm)
        keys_ref[kb] = jnp.where(adm, _sort_key(acc), jnp.int32(INT_MIN))

    def score_pair(kp, c):
        score_block(2 * kp)
        score_block(2 * kp + 1)
        return c

    lax.fori_loop(0, nkb // 2, score_pair, 0)

    @pl.when(nkb % 2 == 1)
    def _():
        score_block(nkb - 1)

    def count(cmp, c):
        cb = jnp.broadcast_to(c, (COUNT_ROWS, tq))

        def body(kb, acc):
            for r0 in range(0, tq, COUNT_ROWS):
                acc = acc + jnp.where(cmp(keys_ref[kb, r0:r0 + COUNT_ROWS, :], cb), 1.0, 0.0)
            return acc
        acc = lax.fori_loop(0, nkb, body, jnp.zeros((COUNT_ROWS, tq), f32))
        return jnp.sum(acc, axis=0, keepdims=True)

    lane = lax.broadcasted_iota(jnp.int32, (1, tq), 1)
    n_adm = ((qb * tq + lane) // CHUNK + 1) * CHUNK
    kk = jnp.minimum(topk, n_adm).astype(f32)
    thr = _kth_largest(lambda c: count(lambda k, t: k >= t, c), kk)
    n_ge = count(lambda k, t: k >= t, thr)
    has_tie = jnp.max(n_ge - kk) > 0.0

    @pl.when(jnp.logical_not(has_tie))
    def _():
        def body(kb, c):
            bias_ref[kb] = jnp.where(keys_ref[kb] >= thr, 0.0, NEG)
            return c
        lax.fori_loop(0, nkb, body, 0)

    @pl.when(has_tie)
    def _():
        need = kk - count(lambda k, t: k > t, thr)
        tri = jnp.where(qcol <= srow, 1.0, 0.0).astype(bf16)

        def body(kb, seen):
            k = keys_ref[kb]
            eq = jnp.where(k == thr, 1.0, 0.0)
            rank = seen + jnp.dot(tri, eq.astype(bf16), preferred_element_type=f32)
            take = jnp.logical_or(k > thr, jnp.logical_and(k == thr, rank <= need))
            bias_ref[kb] = jnp.where(take, 0.0, NEG)
            return seen + jnp.sum(eq, axis=0, keepdims=True)
        lax.fori_loop(0, nkb, body, jnp.zeros((1, tq), f32))

    qs_ref[...] = (q_ref[...] * (hd ** -0.5 * LOG2E)).astype(bf16)
    for h in range(A_HEADS):
        acc_refs[h][...] = jnp.zeros((hd, tq), f32)

    def att_blocks(kbs, carry):
        ms, ls = carry
        kblks = [k_ref[pl.ds(pl.multiple_of(kb * tq, tq), tq), :] for kb in kbs]
        biases = [bias_ref[kb] for kb in kbs]
        sts = [[lax.dot_general(kblk[:, h * hd:(h + 1) * hd], qs_ref[:, h * hd:(h + 1) * hd], NT_DIMS,
                                preferred_element_type=f32) for kblk in kblks] for h in range(A_HEADS)]
        new_ms, new_ls, ps, corrs = [], [], [], []
        for h in range(A_HEADS):
            st = [s + b for s, b in zip(sts[h], biases)]
            m_new = ms[h]
            for s in st:
                m_new = jnp.maximum(m_new, jnp.max(s, axis=0, keepdims=True))
            p = [jnp.exp2(s - m_new) for s in st]
            corr = jnp.exp2(ms[h] - m_new)
            l_new = corr * ls[h]
            for pp in p:
                l_new = l_new + jnp.sum(pp, axis=0, keepdims=True)
            new_ls.append(l_new)
            new_ms.append(m_new)
            ps.append([pp.astype(bf16) for pp in p])
            corrs.append(corr)
        for h in range(A_HEADS):
            upd = corrs[h] * acc_refs[h][...]
            for kb, pp in zip(kbs, ps[h]):
                upd = upd + jnp.dot(vt_ref[kb, h], pp, preferred_element_type=f32)
            acc_refs[h][...] = upd
        return tuple(new_ms), tuple(new_ls)

    init = (tuple(jnp.full((1, tq), NEG, f32) for _ in range(A_HEADS)),
            tuple(jnp.zeros((1, tq), f32) for _ in range(A_HEADS)))
    carry = lax.fori_loop(0, nkb // 2, lambda kp, c: att_blocks((2 * kp, 2 * kp + 1), c), init)
    _, ls = lax.cond(nkb % 2 == 1, lambda c: att_blocks((nkb - 1,), c), lambda c: c, carry)
    for h in range(A_HEADS):
        o_ref[:, h * hd:(h + 1) * hd] = (acc_refs[h][...] / ls[h]).T.astype(o_ref.dtype)


def _dsa_prompt(hp, kb16, vt16, kiw, obuf, batch, seq):
    tq = ATT_BLOCK
    nq = seq // tq
    topk = min(TOPK_MAX, seq // 4)
    kern = functools.partial(_dsa_prompt_kernel, tq=tq, topk=topk)
    once = pl.Buffered(1)
    return pl.pallas_call(
        kern,
        grid=(batch, nq),
        in_specs=[
            pl.BlockSpec((tq, D_A), lambda b, i: (b * nq + i, COL_Q // D_A)),
            pl.BlockSpec((seq, D_A), lambda b, i: (b, 0), pipeline_mode=once),
            pl.BlockSpec((nq, A_HEADS, A_HEAD_DIM, tq), lambda b, i: (b, 0, 0, 0), pipeline_mode=once),
            pl.BlockSpec((tq, 512), lambda b, i: (b * nq + i, COL_QI // 512)),
            pl.BlockSpec((seq, LANE), lambda b, i: (b, KIW_KI // LANE), pipeline_mode=once),
            pl.BlockSpec((tq, LANE), lambda b, i: (b * nq + i, KIW_W // LANE)),
            pl.BlockSpec(memory_space=pl.ANY),
        ],
        out_specs=pl.BlockSpec((tq, D_A), lambda b, i: (b * nq + i, 0)),
        out_shape=jax.ShapeDtypeStruct(obuf.shape, obuf.dtype),
        input_output_aliases={6: 0},
        scratch_shapes=[pltpu.VMEM((nq, tq, tq), jnp.int32), pltpu.VMEM((nq, tq, tq), f32),
                        pltpu.VMEM((tq, D_A), bf16)]
                       + [pltpu.VMEM((A_HEAD_DIM, tq), f32) for _ in range(A_HEADS)],
        compiler_params=_cparams(("parallel", "arbitrary")),
        name="dsa_prompt",
    )(hp, kb16, vt16, hp, kiw, kiw, obuf)


def _dsa_sample_kernel(q_ref, kn_ref, vn_ref, qi_ref, kin_ref, w_ref, kc_ref, vc_ref, kic_ref, _o_in, o_ref,
                       biasc_ref, biasn_ref, *, topk):
    nq = q_ref.shape[0]
    past = kic_ref.shape[0]

    def _select():
        qi = qi_ref[...].astype(bf16)
        w = w_ref[...][:, :IDX_HEADS] * (IDX_HEADS ** -0.5 * IDX_DIM ** -0.5)
        key_c = _sort_key(_idx_scores(qi, w, kic_ref[...].astype(bf16)))
        key_n = _sort_key(_idx_scores(qi, w, kin_ref[...][:, :IDX_DIM].astype(bf16)))

        def count(pred):
            cc = jnp.sum(_fold_lanes(jnp.where(pred(key_c), 1.0, 0.0)), axis=-1, keepdims=True)
            return cc + jnp.sum(jnp.where(pred(key_n), 1.0, 0.0), axis=-1, keepdims=True)

        kk = jnp.full((nq, 1), float(topk), f32)
        thr = _kth_largest(lambda c: count(lambda k: k >= c), kk)
        n_ge = count(lambda k: k >= thr)
        has_tie = jnp.max(n_ge - kk) > 0.0

        @pl.when(jnp.logical_not(has_tie))
        def _():
            biasc_ref[...] = jnp.where(key_c >= thr, 0.0, NEG)
            biasn_ref[...] = jnp.where(key_n >= thr, 0.0, NEG)

        @pl.when(has_tie)
        def _():
            need = kk - count(lambda k: k > thr)
            tri = _tri_incl(LANE)
            seen = jnp.zeros((nq, 1), f32)
            for j in range(past // LANE):
                k = key_c[:, j * LANE:(j + 1) * LANE]
                eq = jnp.where(k == thr, 1.0, 0.0)
                rank = seen + jnp.dot(eq.astype(bf16), tri, preferred_element_type=f32)
                take = jnp.logical_or(k > thr, jnp.logical_and(k == thr, rank <= need))
                biasc_ref[:, j * LANE:(j + 1) * LANE] = jnp.where(take, 0.0, NEG)
                seen = seen + jnp.sum(eq, axis=-1, keepdims=True)
            eq = jnp.where(key_n == thr, 1.0, 0.0)
            rank = seen + jnp.dot(eq.astype(bf16), _tri_incl(nq), preferred_element_type=f32)
            take = jnp.logical_or(key_n > thr, jnp.logical_and(key_n == thr, rank <= need))
            biasn_ref[...] = jnp.where(take, 0.0, NEG)

    _select()

    hd = A_HEAD_DIM
    q = (q_ref[...] * (hd ** -0.5 * LOG2E)).astype(bf16)
    bias_c = biasc_ref[...]
    bias_n = biasn_ref[...]
    scs, sns = [], []
    for h in range(A_HEADS):
        cols = slice(h * hd, (h + 1) * hd)
        k_h = kc_ref[pl.ds(h, past, stride=A_HEADS), :]
        scs.append(lax.dot_general(q[:, cols], k_h.astype(bf16), NT_DIMS, preferred_element_type=f32))
        sns.append(lax.dot_general(q[:, cols], kn_ref[:, h, :].astype(bf16), NT_DIMS, preferred_element_type=f32))
    pcs, pns, ls = [], [], []
    for h in range(A_HEADS):
        s_c = scs[h] + bias_c
        s_n = sns[h] + bias_n
        m = jnp.maximum(jnp.max(s_c, axis=-1, keepdims=True), jnp.max(s_n, axis=-1, keepdims=True))
        p_c = jnp.exp2(s_c - m)
        p_n = jnp.exp2(s_n - m)
        ls.append(jnp.sum(p_c, axis=-1, keepdims=True) + jnp.sum(p_n, axis=-1, keepdims=True))
        pcs.append(p_c.astype(bf16))
        pns.append(p_n.astype(bf16))
    for h in range(A_HEADS):
        cols = slice(h * hd, (h + 1) * hd)
        v_h = vc_ref[pl.ds(h, past, stride=A_HEADS), :]
        acc = jnp.dot(pcs[h], v_h.astype(bf16), preferred_element_type=f32)
        acc = acc + jnp.dot(pns[h], vn_ref[:, h, :].astype(bf16), preferred_element_type=f32)
        o_ref[:, cols] = (acc / ls[h]).astype(o_ref.dtype)


def _dsa_sample(hp, ks, vs, kiw, obuf, layer, row0, nb, nq, cache_k, cache_v, cache_kidx):
    depth, _, past = cache_k.shape[:3]
    topk = min(TOPK_MAX, (past + nq) // 4)
    kc = cache_k.reshape(depth, nb, past * A_HEADS, A_HEAD_DIM)
    vc = cache_v.reshape(depth, nb, past * A_HEADS, A_HEAD_DIM)
    rb = row0 // nq
    kern = functools.partial(_dsa_sample_kernel, topk=topk)
    return pl.pallas_call(
        kern,
        grid=(nb,),
        in_specs=[
            pl.BlockSpec((nq, D_A), lambda b: (rb + b, COL_Q // D_A)),
            pl.BlockSpec((nq, A_HEADS, A_HEAD_DIM), lambda b: (layer * nb + b, 0, 0)),
            pl.BlockSpec((nq, A_HEADS, A_HEAD_DIM), lambda b: (layer * nb + b, 0, 0)),
            pl.BlockSpec((nq, 512), lambda b: (rb + b, COL_QI // 512)),
            pl.BlockSpec((nq, LANE), lambda b: (b, KIW_KI // LANE)),
            pl.BlockSpec((nq, LANE), lambda b: (b, KIW_W // LANE)),
            pl.BlockSpec((None, None, past * A_HEADS, A_HEAD_DIM), lambda b: (layer, b, 0, 0)),
            pl.BlockSpec((None, None, past * A_HEADS, A_HEAD_DIM), lambda b: (layer, b, 0, 0)),
            pl.BlockSpec((None, None, past, IDX_DIM), lambda b: (layer, b, 0, 0)),
            pl.BlockSpec(memory_space=pl.ANY),
        ],
        out_specs=pl.BlockSpec((nq, D_A), lambda b: (rb + b, 0)),
        out_shape=jax.ShapeDtypeStruct(obuf.shape, obuf.dtype),
        input_output_aliases={9: 0},
        scratch_shapes=[pltpu.VMEM((nq, past), f32), pltpu.VMEM((nq, nq), f32)],
        compiler_params=_cparams(("parallel",)),
        name="dsa_sample",
    )(hp, ks, vs, hp, kiw, kiw, kc, vc, cache_kidx, obuf)


def _pool_kernel(prev_ref, u_ref, pw_ref, ps_ref, _o_in, o_ref, *, tt, first_pos, zero_first):
    tb = pl.program_id(1)
    halo = POOL_STATE + 1
    prev = prev_ref[...]
    if zero_first:
        prev = jnp.where(tb == 0, 0.0, prev)
    cur = u_ref[...]
    ext = jnp.concatenate([prev, cur], axis=0)
    pos = first_pos + tb * tt + lax.broadcasted_iota(jnp.int32, (tt, 1), 0)
    outs = []
    for gi, wdw in enumerate(POOL_WINDOWS):
        cols = slice(gi * POOL_GROUP_DIM, (gi + 1) * POOL_GROUP_DIM)
        a = ext[:, cols]
        n = 1
        while n < wdw:
            a = a[n:] + a[:-n]
            n *= 2
        s = a[halo + 1 - wdw: halo + 1 - wdw + tt]
        cnt = jnp.minimum(pos + 1, wdw).astype(f32)
        d = s / cnt - cur[:, cols]
        outs.append(jnp.dot(d.astype(bf16), pw_ref[gi].astype(bf16), preferred_element_type=f32))
    o = jnp.concatenate(outs, axis=-1) * ps_ref[...]
    o_ref[...] = o.astype(o_ref.dtype)


def _pool(hp, obuf, row0, nb, t, tt, prev, pool_w, pool_scale, first_pos):
    ntb = t // tt
    halo = POOL_STATE + 1
    rb = row0 // tt
    if prev is None:
        prev_arr = hp
        per = tt // halo
        prev_spec = pl.BlockSpec(
            (halo, D_B), lambda b, i: (jnp.maximum((row0 // halo) + (b * ntb + i) * per - 1, 0), COL_U // D_B))
    else:
        prev_arr = prev
        prev_spec = pl.BlockSpec((None, halo, D_B), lambda b, i: (b, 0, 0))
    kern = functools.partial(_pool_kernel, tt=tt, first_pos=first_pos, zero_first=prev is None)
    return pl.pallas_call(
        kern,
        grid=(nb, ntb),
        in_specs=[prev_spec,
                  pl.BlockSpec((tt, D_B), lambda b, i: (rb + b * ntb + i, COL_U // D_B)),
                  pl.BlockSpec((len(POOL_WINDOWS), POOL_GROUP_DIM, POOL_GROUP_DIM), lambda b, i: (0, 0, 0)),
                  pl.BlockSpec((1, D_B), lambda b, i: (0, 0)),
                  pl.BlockSpec(memory_space=pl.ANY)],
        out_specs=pl.BlockSpec((tt, D_B), lambda b, i: (rb + b * ntb + i, 0)),
        out_shape=jax.ShapeDtypeStruct(obuf.shape, obuf.dtype),
        input_output_aliases={4: 0},
        compiler_params=_cparams(("parallel", "arbitrary")),
        name="pool",
    )(prev_arr, hp, pool_w, pool_scale.reshape(1, D_B), obuf)


SAFE_LOG_RANGE = 80.0


def _hgrn_kernel(cq_ref, cf_ref, ci_ref, cg_ref, lb_ref, ng_ref, s0_ref, _o_in, o_ref, sn_ref,
                 st_ref, b_ref, q_ref, k_ref, osc_ref, *, tt, blk):
    tb = pl.program_id(1)
    nh = C_HEADS
    hd = C_HEAD_DIM

    @pl.when(tb == 0)
    def _init():
        for h in range(nh):
            st_ref[h] = s0_ref[h].T

    lb = lb_ref[...]
    f = jnp.maximum(lb, LB_FLOOR) + (1.0 - lb) * jax.nn.sigmoid(cf_ref[...])
    g = jnp.log(f)
    k_ref[...] = 1.0 - f
    q_ref[...] = _silu(cq_ref[...])
    r = lax.broadcasted_iota(jnp.int32, (tt, tt), 0)
    c = lax.broadcasted_iota(jnp.int32, (tt, tt), 1)
    tri = jnp.where(jnp.logical_and(r // blk == c // blk, c <= r), 1.0, 0.0).astype(bf16)
    g1 = g.astype(bf16)
    e1 = g - g1.astype(f32)
    g2 = e1.astype(bf16)
    g3 = (e1 - g2.astype(f32)).astype(bf16)
    b = (jnp.dot(tri, g1, preferred_element_type=f32) + jnp.dot(tri, g2, preferred_element_type=f32)
         + jnp.dot(tri, g3, preferred_element_type=f32))
    b_ref[...] = b
    safe = jnp.min(b) > -SAFE_LOG_RANGE

    trow = lax.broadcasted_iota(jnp.int32, (blk, 1), 0)

    def exact_chunk(ci):
        r0 = pl.multiple_of(ci * blk, blk)
        for h in range(nh):
            rows = pl.ds(r0, blk)
            cols = slice(h * hd, (h + 1) * hd)
            bb = b_ref[rows, cols]
            qh = q_ref[rows, cols]
            kh = k_ref[rows, cols]
            ih = ci_ref[rows, cols]
            bl = bb[blk - 1:blk, :]
            ke = (kh * jnp.exp(bl - bb)).astype(bf16)
            ib = ih.astype(bf16)
            st = st_ref[h]
            o = lax.dot_general((qh * jnp.exp(bb)).astype(bf16), st.astype(bf16), NT_DIMS,
                                preferred_element_type=f32)
            for s in range(blk):
                live = trow >= s
                dec = jnp.exp(jnp.where(live, bb - bb[s:s + 1, :], 0.0))
                a_col = jnp.sum(qh * kh[s:s + 1, :] * dec, axis=-1, keepdims=True)
                o = o + jnp.where(live, a_col, 0.0) * ih[s:s + 1, :]
            osc_ref[rows, cols] = o
            st_ref[h] = st * jnp.exp(bl) + lax.dot_general(ib, ke, TN_DIMS, preferred_element_type=f32)

    @pl.when(safe)
    def _():
        same_blk = r // blk == c // blk
        blk_causal = jnp.logical_and(same_blk, c <= r)
        ones_blk = jnp.where(same_blk, 1.0, 0.0).astype(bf16)
        bl_all = (jnp.dot(ones_blk, g1, preferred_element_type=f32) + jnp.dot(ones_blk, g2, preferred_element_type=f32)
                  + jnp.dot(ones_blk, g3, preferred_element_type=f32))
        for h in range(nh):
            cols = slice(h * hd, (h + 1) * hd)
            bb = b[:, cols]
            blh = bl_all[:, cols]
            qh = q_ref[:, cols]
            ib = ci_ref[:, cols].astype(bf16)
            ke = (k_ref[:, cols] * jnp.exp(blh - bb)).astype(bf16)
            qe = (qh * jnp.exp(bb - blh)).astype(bf16)
            qb = (qh * jnp.exp(bb)).astype(bf16)
            a = jnp.where(blk_causal, lax.dot_general(qe, ke, NT_DIMS, preferred_element_type=f32), 0.0)
            o_intra = jnp.dot(a.astype(bf16), ib, preferred_element_type=f32)
            incs = [lax.dot_general(ib[c0:c0 + blk], ke[c0:c0 + blk], TN_DIMS, preferred_element_type=f32)
                    for c0 in range(0, tt, blk)]
            dec = jnp.exp(blh)
            st = st_ref[h]
            o_inter = []
            for ci, c0 in enumerate(range(0, tt, blk)):
                o_inter.append(lax.dot_general(qb[c0:c0 + blk], st.astype(bf16), NT_DIMS,
                                               preferred_element_type=f32))
                st = st * dec[c0:c0 + 1, :] + incs[ci]
            st_ref[h] = st
            osc_ref[:, cols] = o_intra + jnp.concatenate(o_inter, axis=0)

    @pl.when(jnp.logical_not(safe))
    def _():
        def body(ci, c):
            exact_chunk(ci)
            return c
        lax.fori_loop(0, tt // blk, body, 0)

    outs = []
    for h in range(nh):
        cols = slice(h * hd, (h + 1) * hd)
        outs.append(_rms(osc_ref[:, cols], ng_ref[...]))
    o_ref[...] = (jnp.concatenate(outs, axis=-1) * _silu(cg_ref[...])).astype(o_ref.dtype)

    @pl.when(tb == pl.num_programs(1) - 1)
    def _fin():
        for h in range(nh):
            sn_ref[h] = st_ref[h].T


def _hgrn(hp, obuf, row0, nb, t, tt, lb, norm_g, s0):
    ntb = t // tt
    rb = row0 // tt

    def col(cidx):
        return pl.BlockSpec((tt, D_C), lambda b, i: (rb + b * ntb + i, cidx // D_C))

    st_spec = pl.BlockSpec((None, C_HEADS, C_HEAD_DIM, C_V_DIM), lambda b, i: (b, 0, 0, 0))
    kern = functools.partial(_hgrn_kernel, tt=tt, blk=HGRN_BLOCK)
    return pl.pallas_call(
        kern,
        grid=(nb, ntb),
        in_specs=[col(COL_CQ), col(COL_CF), col(COL_CI), col(COL_CG),
                  pl.BlockSpec((1, D_C), lambda b, i: (0, 0)),
                  pl.BlockSpec((1, C_V_DIM), lambda b, i: (0, 0)),
                  st_spec,
                  pl.BlockSpec(memory_space=pl.ANY)],
        out_specs=[pl.BlockSpec((tt, D_C), lambda b, i: (rb + b * ntb + i, 0)), st_spec],
        out_shape=[jax.ShapeDtypeStruct(obuf.shape, obuf.dtype),
                   jax.ShapeDtypeStruct((nb, C_HEADS, C_HEAD_DIM, C_V_DIM), f32)],
        input_output_aliases={7: 0},
        scratch_shapes=[pltpu.VMEM((C_HEADS, C_V_DIM, C_HEAD_DIM), f32),
                        pltpu.VMEM((tt, D_C), f32), pltpu.VMEM((tt, D_C), f32),
                        pltpu.VMEM((tt, D_C), f32), pltpu.VMEM((tt, D_C), f32)],
        compiler_params=_cparams(("parallel", "arbitrary")),
        name="hgrn",
    )(hp, hp, hp, hp, lb.reshape(1, D_C), norm_g.reshape(1, C_V_DIM), s0, obuf)


def _merge_kernel(x_ref, g_ref, oa_ref, ob_ref, oc_ref, wga_ref, wgb_ref, wgc_ref,
                  pa_ref, pb_ref, pc_ref, wo_ref, o_ref, xn_ref):
    j = pl.program_id(1)

    @pl.when(j == 0)
    def _():
        x = x_ref[...]
        xn_ref[...] = _rms(x, g_ref[...]).astype(bf16)
        o_ref[...] = x

    xn = xn_ref[...]

    def branch(wg_ref, o_ref_, p_ref):
        gate = jax.nn.sigmoid(jnp.dot(xn, wg_ref[...], preferred_element_type=f32))
        return gate * jnp.dot(o_ref_[...], p_ref[...], preferred_element_type=f32)

    merged = branch(wga_ref, oa_ref, pa_ref) + branch(wgb_ref, ob_ref, pb_ref) + branch(wgc_ref, oc_ref, pc_ref)
    o_ref[...] += jnp.dot(merged.astype(bf16), wo_ref[...], preferred_element_type=f32)


def _merge(h, g, oa, ob, oc, wg, pa, pb, pc, wo, tm, tj):
    m, d = h.shape
    nj = d // tj
    row = lambda i, j: (i, 0)
    return pl.pallas_call(
        _merge_kernel,
        grid=(m // tm, nj),
        in_specs=[pl.BlockSpec((tm, d), row),
                  pl.BlockSpec((1, d), lambda i, j: (0, 0)),
                  pl.BlockSpec((tm, D_A), row),
                  pl.BlockSpec((tm, D_B), row),
                  pl.BlockSpec((tm, D_C), row),
                  pl.BlockSpec((d, tj), lambda i, j: (0, j)),
                  pl.BlockSpec((d, tj), lambda i, j: (0, nj + j)),
                  pl.BlockSpec((d, tj), lambda i, j: (0, 2 * nj + j)),
                  pl.BlockSpec((D_A, tj), lambda i, j: (0, j)),
                  pl.BlockSpec((D_B, tj), lambda i, j: (0, j)),
                  pl.BlockSpec((D_C, tj), lambda i, j: (0, j)),
                  pl.BlockSpec((tj, d), lambda i, j: (j, 0))],
        out_specs=pl.BlockSpec((tm, d), row),
        out_shape=jax.ShapeDtypeStruct((m, d), f32),
        scratch_shapes=[pltpu.VMEM((tm, d), bf16)],
        compiler_params=_cparams(("parallel", "arbitrary")),
        name="merge",
    )(h, g.reshape(1, d), oa, ob, oc, wg, wg, wg, pa, pb, pc, wo)


def _router_kernel(x_ref, g_ref, r_ref, o_ref):
    xn = _rms(x_ref[...], g_ref[...]).astype(bf16)
    logits = jnp.dot(xn, r_ref[...], preferred_element_type=f32)
    lane = lax.broadcasted_iota(jnp.int32, logits.shape, 1)
    logits = jnp.where(lane < N_EXPERTS, logits, -jnp.inf)
    m1 = jnp.max(logits, axis=-1, keepdims=True)
    i1 = jnp.min(jnp.where(logits == m1, lane, LANE), axis=-1, keepdims=True)
    rest = jnp.where(lane == i1, -jnp.inf, logits)
    m2 = jnp.max(rest, axis=-1, keepdims=True)
    i2 = jnp.min(jnp.where(rest == m2, lane, LANE), axis=-1, keepdims=True)
    e2 = jnp.exp(m2 - m1)
    g1 = 1.0 / (1.0 + e2)
    g2 = e2 / (1.0 + e2)
    route = jnp.where(lane == 0, i1.astype(f32), jnp.where(lane == 1, i2.astype(f32),
                      jnp.where(lane == 2, g1, jnp.where(lane == 3, g2, 0.0))))
    o_ref[...] = route[:, :N_EXPERTS]


def _router(h, g, router, tm):
    m, d = h.shape
    rp = jnp.pad(router, ((0, 0), (0, LANE - N_EXPERTS))).astype(bf16)
    return pl.pallas_call(
        _router_kernel,
        grid=(m // tm,),
        in_specs=[pl.BlockSpec((tm, d), lambda i: (i, 0)),
                  pl.BlockSpec((1, d), lambda i: (0, 0)),
                  pl.BlockSpec((d, LANE), lambda i: (0, 0))],
        out_specs=pl.BlockSpec((tm, N_EXPERTS), lambda i: (i, 0)),
        out_shape=jax.ShapeDtypeStruct((m, N_EXPERTS), f32),
        compiler_params=_cparams(("parallel",)),
        name="router",
    )(h, g.reshape(1, d), rp)


def _ffn_kernel(x_ref, g_ref, wg_ref, wu_ref, wd_ref, o_ref, xn_ref):
    @pl.when(pl.program_id(1) == 0)
    def _():
        x = x_ref[...]
        xn_ref[...] = _rms(x, g_ref[...]).astype(bf16)
        o_ref[...] = x

    xn = xn_ref[...]
    a = jnp.dot(xn, wg_ref[...], preferred_element_type=f32)
    u = jnp.dot(xn, wu_ref[...], preferred_element_type=f32)
    act = (_silu(a) * u).astype(bf16)
    o_ref[...] += jnp.dot(act, wd_ref[...], preferred_element_type=f32)


def _ffn(h, g, wg, wu, wd, tm, tf):
    m, d = h.shape
    ff = wg.shape[1]
    return pl.pallas_call(
        _ffn_kernel,
        grid=(m // tm, ff // tf),
        in_specs=[pl.BlockSpec((tm, d), lambda i, f: (i, 0)),
                  pl.BlockSpec((1, d), lambda i, f: (0, 0)),
                  pl.BlockSpec((d, tf), lambda i, f: (0, f)),
                  pl.BlockSpec((d, tf), lambda i, f: (0, f)),
                  pl.BlockSpec((tf, d), lambda i, f: (f, 0))],
        out_specs=pl.BlockSpec((tm, d), lambda i, f: (i, 0)),
        out_shape=jax.ShapeDtypeStruct((m, d), f32),
        scratch_shapes=[pltpu.VMEM((tm, d), bf16)],
        compiler_params=_cparams(("parallel", "arbitrary")),
        name="ffn",
    )(h, g.reshape(1, d), wg, wu, wd)


MOE_ROWS = 1024
MOE_SUB = 256
DMA_UNROLL = 8


def _moe_plan(route, m):
    n = 2 * m
    n_groups = -(-n // MOE_ROWS) + N_EXPERTS
    ef = route[:, 0:2].astype(jnp.int32).T.reshape(n)
    row_dst = jnp.argsort(ef, stable=True).astype(jnp.int32)
    row_tok = jnp.where(row_dst >= m, row_dst - m, row_dst)
    counts = jnp.sum((ef[:, None] == jnp.arange(N_EXPERTS, dtype=jnp.int32)[None, :]).astype(jnp.int32), axis=0)
    first = jnp.cumsum(counts) - counts
    per_e = (counts + MOE_ROWS - 1) // MOE_ROWS
    size = (counts + jnp.maximum(per_e, 1) - 1) // jnp.maximum(per_e, 1)
    size = (size + DMA_UNROLL - 1) // DMA_UNROLL * DMA_UNROLL
    ends = jnp.cumsum(per_e)
    gi = jnp.arange(n_groups, dtype=jnp.int32)
    ge = jnp.minimum(jnp.sum((gi[:, None] >= ends[None, :]).astype(jnp.int32), axis=1), N_EXPERTS - 1)
    j = gi - (ends - per_e)[ge]
    g_start = first[ge] + j * size[ge]
    g_cnt = jnp.clip(counts[ge] - j * size[ge], 0, size[ge])
    g_cnt = jnp.where(gi < ends[-1], g_cnt, 0)
    last_e = ge[jnp.maximum(ends[-1] - 1, 0)]
    ge = jnp.where(gi < ends[-1], ge, last_e)
    return ge, g_start.astype(jnp.int32), g_cnt.astype(jnp.int32), row_tok, row_dst


def _moe_kernel(ge_ref, gs_ref, gc_ref, tok_ref, dst_ref, h_hbm, g_ref, wg_ref, wu_ref, wd_ref, y_hbm,
                xs_ref, acc_ref, stage_ref, spare_ref, gsem, ssem, *, nf):
    s = pl.program_id(0)
    fi = pl.program_id(1)
    cnt = gc_ref[s]
    start = gs_ref[s]
    n_rows = tok_ref.shape[0]
    nsub = (cnt + MOE_SUB - 1) // MOE_SUB
    chunk = stage_ref.shape[0] // nf

    def gather_chunk(first_row, base):
        for u in range(chunk):
            tok = tok_ref[jnp.minimum(first_row + base + u, n_rows - 1)]
            pltpu.make_async_copy(h_hbm.at[pl.ds(tok, 1)], stage_ref.at[pl.ds(base + u, 1)], gsem).start()

    def row_tiles(js):
        wg = wg_ref[...].astype(bf16)
        wu = wu_ref[...].astype(bf16)
        wd = wd_ref[...].astype(bf16)
        for j in js:
            rows = pl.ds(pl.multiple_of(j * MOE_SUB, MOE_SUB), MOE_SUB)
            x = xs_ref[rows, :]
            a = jnp.dot(x, wg, preferred_element_type=f32)
            u = jnp.dot(x, wu, preferred_element_type=f32)
            act = (_silu(a) * u).astype(bf16)
            acc_ref[rows, :] += jnp.dot(act, wd, preferred_element_type=f32)

    def row_tile(j):
        row_tiles((j,))

    @pl.when(cnt > 0)
    def _active():
        @pl.when(fi == 0)
        def _stage_in():
            @pl.when(s == 0)
            def _():
                def issue(c, carry):
                    gather_chunk(start, c * chunk)
                    return carry
                lax.fori_loop(0, nf, issue, 0)

            def wait_chunk(c, carry):
                pltpu.make_async_copy(h_hbm.at[pl.ds(0, chunk)], stage_ref.at[pl.ds(0, chunk)], gsem).wait()
                return carry
            lax.fori_loop(0, nf, wait_chunk, 0)

            def norm(j, c):
                rows = pl.ds(pl.multiple_of(j * MOE_SUB, MOE_SUB), MOE_SUB)
                xs_ref[rows, :] = _rms(stage_ref[rows, :], g_ref[...]).astype(bf16)
                acc_ref[rows, :] = jnp.zeros((MOE_SUB, acc_ref.shape[1]), f32)
                return c
            lax.fori_loop(0, nsub, norm, 0)

        nxt = jnp.minimum(s + 1, pl.num_programs(0) - 1)
        prefetch = jnp.logical_and(s + 1 < pl.num_programs(0), gc_ref[nxt] > 0)

        @pl.when(prefetch)
        def _():
            gather_chunk(gs_ref[nxt], fi * chunk)
            row_tile(0)

        @pl.when(jnp.logical_not(prefetch))
        def _():
            row_tile(0)

        def body(jp, c):
            row_tiles((1 + 2 * jp, 2 + 2 * jp))
            return c
        lax.fori_loop(0, (nsub - 1) // 2, body, 0)

        @pl.when((nsub - 1) % 2 == 1)
        def _():
            row_tile(nsub - 1)

        @pl.when(fi == nf - 1)
        def _scatter():
            def issue_one(r, priority=0):
                pltpu.make_async_copy(acc_ref.at[pl.ds(r, 1)], y_hbm.at[pl.ds(dst_ref[start + r], 1)],
                                      ssem).start(priority=priority)

            def issue(r8, c):
                for u in range(DMA_UNROLL):
                    issue_one(r8 * DMA_UNROLL + u, u % 2)
                return c
            n8 = cnt // DMA_UNROLL
            lax.fori_loop(0, n8, issue, 0)

            def issue_tail(r, c):
                issue_one(r)
                return c
            lax.fori_loop(n8 * DMA_UNROLL, cnt, issue_tail, 0)

            def issue_spare(r, c):
                pltpu.make_async_copy(acc_ref.at[pl.ds(r, 1)], spare_ref.at[pl.ds(r - cnt, 1)], ssem).start()
                return c
            lax.fori_loop(cnt, nsub * MOE_SUB, issue_spare, 0)

            def wait_tile(j, c):
                pltpu.make_async_copy(acc_ref.at[pl.ds(0, MOE_SUB)], y_hbm.at[pl.ds(0, MOE_SUB)], ssem).wait()
                return c
            lax.fori_loop(0, nsub, wait_tile, 0)


def _moe(h, g, route, wg, wu, wd, tf):
    m, d = h.shape
    ne, _, ff = wg.shape
    nf = ff // tf
    chunk = -(-MOE_ROWS // nf)
    chunk = -(-chunk // DMA_UNROLL) * DMA_UNROLL
    ge, gs, gc, row_tok, row_dst = _moe_plan(route, m)
    n_groups = ge.shape[0]

    def wcol(s, f, ge_r, gs_r, gc_r, tok_r, dst_r):
        return (ge_r[s], 0, jnp.where(gc_r[s] > 0, f, nf - 1))

    def wrow(s, f, ge_r, gs_r, gc_r, tok_r, dst_r):
        return (ge_r[s], jnp.where(gc_r[s] > 0, f, nf - 1), 0)

    grid_spec = pltpu.PrefetchScalarGridSpec(
        num_scalar_prefetch=5,
        grid=(n_groups, nf),
        in_specs=[pl.BlockSpec(memory_space=pl.ANY),
                  pl.BlockSpec((1, d), lambda s, f, *_: (0, 0)),
                  pl.BlockSpec((None, d, tf), wcol),
                  pl.BlockSpec((None, d, tf), wcol),
                  pl.BlockSpec((None, tf, d), wrow)],
        out_specs=pl.BlockSpec(memory_space=pl.ANY),
        scratch_shapes=[pltpu.VMEM((MOE_ROWS, d), bf16), pltpu.VMEM((MOE_ROWS, d), f32),
                        pltpu.VMEM((chunk * nf, d), f32), pltpu.VMEM((MOE_SUB, d), f32),
                        pltpu.SemaphoreType.DMA(()), pltpu.SemaphoreType.DMA(())],
    )
    return pl.pallas_call(
        functools.partial(_moe_kernel, nf=nf),
        grid_spec=grid_spec,
        out_shape=jax.ShapeDtypeStruct((2 * m, d), f32),
        compiler_params=_cparams(("arbitrary", "arbitrary")),
        name="moe",
    )(ge, gs, gc, row_tok, row_dst, h, g.reshape(1, d), wg, wu, wd)


def _combine_kernel(x_ref, r_ref, y1_ref, y2_ref, o_ref):
    r = r_ref[...]
    o_ref[...] = x_ref[...] + (r[:, 2:3] * y1_ref[...] + r[:, 3:4] * y2_ref[...])


def _combine(h, route, y, tm):
    m, d = h.shape
    nb = m // tm
    return pl.pallas_call(
        _combine_kernel,
        grid=(nb,),
        in_specs=[pl.BlockSpec((tm, d), lambda i: (i, 0)),
                  pl.BlockSpec((tm, N_EXPERTS), lambda i: (i, 0)),
                  pl.BlockSpec((tm, d), lambda i: (i, 0)),
                  pl.BlockSpec((tm, d), lambda i: (nb + i, 0))],
        out_specs=pl.BlockSpec((tm, d), lambda i: (i, 0)),
        out_shape=jax.ShapeDtypeStruct((m, d), f32),
        compiler_params=_cparams(("parallel",)),
        name="moe_combine",
    )(h, route, y, y)


def _combine_norm_kernel(x_ref, r_ref, y1_ref, y2_ref, g_ref, op_ref, os_ref, *, n_p):
    r = r_ref[...]
    v = _rms(x_ref[...] + (r[:, 2:3] * y1_ref[...] + r[:, 3:4] * y2_ref[...]), g_ref[...])

    @pl.when(pl.program_id(0) < n_p)
    def _():
        op_ref[...] = v

    @pl.when(pl.program_id(0) >= n_p)
    def _():
        os_ref[...] = v


def _combine_norm(h, route, y, g, mp, tm):
    m, d = h.shape
    nb = m // tm
    n_p = mp // tm
    return pl.pallas_call(
        functools.partial(_combine_norm_kernel, n_p=n_p),
        grid=(nb,),
        in_specs=[pl.BlockSpec((tm, d), lambda i: (i, 0)),
                  pl.BlockSpec((tm, N_EXPERTS), lambda i: (i, 0)),
                  pl.BlockSpec((tm, d), lambda i: (i, 0)),
                  pl.BlockSpec((tm, d), lambda i: (nb + i, 0)),
                  pl.BlockSpec((1, d), lambda i: (0, 0))],
        out_specs=[pl.BlockSpec((tm, d), lambda i: (jnp.minimum(i, n_p - 1), 0)),
                   pl.BlockSpec((tm, d), lambda i: (jnp.maximum(i - n_p, 0), 0))],
        out_shape=[jax.ShapeDtypeStruct((mp, d), f32), jax.ShapeDtypeStruct((m - mp, d), f32)],
        compiler_params=_cparams(("arbitrary",)),
        name="moe_combine_norm",
    )(h, route, y, y, g.reshape(1, d))


def _final_norm_kernel(x_ref, g_ref, o_ref):
    o_ref[...] = _rms(x_ref[...], g_ref[...])


def _final_norm(h, g, row0, rows, tm):
    d = h.shape[1]
    rb = row0 // tm
    return pl.pallas_call(
        _final_norm_kernel,
        grid=(rows // tm,),
        in_specs=[pl.BlockSpec((tm, d), lambda i: (rb + i, 0)),
                  pl.BlockSpec((1, d), lambda i: (0, 0))],
        out_specs=pl.BlockSpec((tm, d), lambda i: (i, 0)),
        out_shape=jax.ShapeDtypeStruct((rows, d), f32),
        compiler_params=_cparams(("parallel",)),
        name="final_norm",
    )(h, g.reshape(1, d))


def _cast_kernel(w_ref, o_ref):
    o_ref[...] = w_ref[...].astype(bf16)


def _cast_bf16(w, layer, tr):
    _, r, c = w.shape
    return pl.pallas_call(
        _cast_kernel,
        grid=(r // tr,),
        in_specs=[pl.BlockSpec((None, tr, c), lambda i: (layer, i, 0))],
        out_specs=pl.BlockSpec((tr, c), lambda i: (i, 0)),
        out_shape=jax.ShapeDtypeStruct((r, c), bf16),
        compiler_params=_cparams(("parallel",)),
        name="cast_bf16",
    )(w)


def _pack_kernel(w_ref, wp_ref, wk_ref, wv_ref, wg_ref):
    o = np.cumsum((0,) + IN_SIZES)
    w = w_ref[...]
    tc = w.shape[1]
    cut = lambda a, b: w[int(o[a]):int(o[b]), :]
    pad = lambda a: jnp.concatenate([a, jnp.zeros((LANE - a.shape[0], tc), f32)], axis=0)
    t = lambda a: a.T.astype(bf16)
    wp_ref[...] = jnp.concatenate([t(cut(0, 1)), t(cut(3, 4)), t(cut(6, 11))], axis=1)
    wk_ref[...] = jnp.concatenate([t(cut(1, 2)), t(pad(cut(4, 5))), t(pad(cut(5, 6)))], axis=1)
    wv_ref[...] = t(cut(2, 3))
    wg_ref[...] = t(cut(11, 12))


def _pack_in_proj(wt, layer, tc):
    _, n, d = wt.shape
    widths = (N_PACK, D_A + 2 * LANE, D_A, N_BRANCH * D_MODEL)
    return pl.pallas_call(
        _pack_kernel,
        grid=(d // tc,),
        in_specs=[pl.BlockSpec((None, n, tc), lambda i: (layer, 0, i))],
        out_specs=[pl.BlockSpec((tc, c), lambda i: (i, 0)) for c in widths],
        out_shape=[jax.ShapeDtypeStruct((d, c), bf16) for c in widths],
        compiler_params=_cparams(("parallel",)),
        name="pack_w_in",
    )(wt)


def kernel(x_prompt, x_sample, cache_k, cache_v, cache_kidx, cache_pool, state_hgrn, norm_mix_g, norm_ffn_g, final_norm_g, w_in, w_proj_a, w_proj_b, w_proj_c, w_out, pool_w, pool_scale, hgrn_lb_logits, hgrn_norm_g, ffn_w_gate, ffn_w_up, ffn_w_down, moe_router, moe_w_gate, moe_w_up, moe_w_down):
    batch, seq, d = x_prompt.shape
    nb, nq, _ = x_sample.shape
    depth = w_in.shape[0]
    past = cache_k.shape[2]
    mp = batch * seq
    ms = nb * nq
    m = mp + ms

    lb_soft = jax.nn.softmax(hgrn_lb_logits.astype(f32), axis=0)
    lb_all = jnp.cumsum(lb_soft, axis=0) - lb_soft[0:1]

    h = jnp.concatenate([x_prompt.reshape(mp, d), x_sample.reshape(ms, d)], axis=0)
    s0_prompt = jnp.zeros((batch, C_HEADS, C_HEAD_DIM, C_V_DIM), f32)
    tm = 768 if m % 768 == 0 else 256
    tmh = tm // 2

    kp = jnp.zeros((depth * mp, A_HEADS, A_HEAD_DIM), f32)
    vp = jnp.zeros((depth * mp, A_HEADS, A_HEAD_DIM), f32)
    ks = jnp.zeros((depth * ms, A_HEADS, A_HEAD_DIM), f32)
    vs = jnp.zeros((depth * ms, A_HEADS, A_HEAD_DIM), f32)
    w_in_t = jnp.swapaxes(w_in, 1, 2)
    tkv = ATT_BLOCK
    tkp = 1024 if mp % 1024 == 0 else ATT_BLOCK
    outs = [[] for _ in range(6)]
    for l in range(depth):
        wp, wk, wv, wgates = _pack_in_proj(w_in_t, l, LANE)
        hp = _inproj(h, norm_mix_g[l], wp, tm, 1024)
        kp, kb16, kiw_p = _inproj_k(h, norm_mix_g[l], wk, kp, l, 0, mp, tkp)
        ks, _, kiw_s = _inproj_k(h, norm_mix_g[l], wk, ks, l, mp, ms, tkv)
        vp, vt16 = _inproj_v(h, norm_mix_g[l], wv, vp, l, 0, mp, tkp, True)
        vs, = _inproj_v(h, norm_mix_g[l], wv, vs, l, mp, ms, tkv, False)

        oa = _dsa_prompt(hp, kb16, vt16, kiw_p, jnp.zeros((m, D_A), bf16), batch, seq)
        oa = _dsa_sample(hp, ks, vs, kiw_s, oa, l, mp, nb, nq, cache_k, cache_v, cache_kidx)
        ob = _pool(hp, jnp.zeros((m, D_B), bf16), 0, batch, seq, 256, None, pool_w[l], pool_scale[l], 0)
        prev = jnp.pad(cache_pool[l], ((0, 0), (1, 0), (0, 0)))
        ob = _pool(hp, ob, mp, nb, nq, nq, prev, pool_w[l], pool_scale[l], past)
        oc, st_p = _hgrn(hp, jnp.zeros((m, D_C), bf16), 0, batch, seq, 256, lb_all[l], hgrn_norm_g[l], s0_prompt)
        oc, st_s = _hgrn(hp, oc, mp, nb, nq, nq, lb_all[l], hgrn_norm_g[l], state_hgrn[l])

        h = _merge(h, norm_mix_g[l], oa, ob, oc, wgates, _cast_bf16(w_proj_a, l, 256), _cast_bf16(w_proj_b, l, 256),
                   _cast_bf16(w_proj_c, l, 256), _cast_bf16(w_out, l, 256), tmh, 512)

        uu = hp[:, COL_U:COL_U + D_B]
        outs[0].append(kiw_p[:, KIW_KI:KIW_KI + IDX_DIM].reshape(batch, seq, IDX_DIM))
        outs[1].append(uu[:mp].reshape(batch, seq, D_B)[:, -POOL_STATE:])
        outs[2].append(st_p)
        outs[3].append(kiw_s[:, KIW_KI:KIW_KI + IDX_DIM].reshape(nb, nq, IDX_DIM))
        u_ext = jnp.concatenate([cache_pool[l], uu[mp:].reshape(nb, nq, D_B)], axis=1)
        outs[4].append(u_ext[:, -POOL_STATE:])
        outs[5].append(st_s)

        j = l // 2
        if l % 2 == 0:
            h = _ffn(h, norm_ffn_g[l], _cast_bf16(ffn_w_gate, j, 256), _cast_bf16(ffn_w_up, j, 256),
                     _cast_bf16(ffn_w_down, j, 512), tm, 512)
        else:
            route = _router(h, norm_ffn_g[l], moe_router[j], tm)
            y = _moe(h, norm_ffn_g[l], route, moe_w_gate[j], moe_w_up[j], moe_w_down[j], 512)
            if l == depth - 1:
                y_prompt, y_sample = _combine_norm(h, route, y, final_norm_g, mp, tkv)
                h = None
            else:
                h = _combine(h, route, y, tmh)

    if h is not None:
        y_prompt = _final_norm(h, final_norm_g, 0, mp, 256)
        y_sample = _final_norm(h, final_norm_g, mp, ms, 256)
    y_prompt = y_prompt.reshape(batch, seq, d)
    y_sample = y_sample.reshape(nb, nq, d)
    kidx_p, pool_p, st_p, kidx_s, pool_s, st_s = (jnp.stack(o) for o in outs)
    return (y_prompt, y_sample,
            kp.reshape(depth, batch, seq, A_HEADS, A_HEAD_DIM), vp.reshape(depth, batch, seq, A_HEADS, A_HEAD_DIM),
            kidx_p, pool_p, st_p,
            ks.reshape(depth, nb, nq, A_HEADS, A_HEAD_DIM), vs.reshape(depth, nb, nq, A_HEADS, A_HEAD_DIM),
            kidx_s, pool_s, st_s)
```
